```python
import jax, jax.numpy as jnp
from jax import lax
import numpy as np

D_MODEL = 1024
BATCH = 4
SEQ = 4096
DEPTH = 1

HEAD_DIM = 64
FOX_HEADS = 8
MOBA_HEADS = 8
FOX_WIDTH = FOX_HEADS * HEAD_DIM
MOBA_WIDTH = MOBA_HEADS * HEAD_DIM
ROPE_DIM = HEAD_DIM // 4
ROPE_THETA = 500000.0
FOX_Q_BLOCK = 128
FOX_FORGET_BIAS = 2.0
MOBA_BLOCK = 256
MOBA_TOPK = 3
MOBA_Q_CHUNK = 32
RMS_EPS = 1e-6
IN_SPLITS = (FOX_WIDTH, FOX_WIDTH, FOX_WIDTH, FOX_WIDTH,
             MOBA_WIDTH, MOBA_WIDTH, MOBA_WIDTH, MOBA_WIDTH,
             D_MODEL, D_MODEL, FOX_HEADS)
IN_WIDTH = 4 * FOX_WIDTH + 4 * MOBA_WIDTH + 2 * D_MODEL + FOX_HEADS

kernel_name = "fox_moba_gated_hybrid"


def rms_norm(x, g):
    xf = x.astype(jnp.float32)
    y = xf * lax.rsqrt(jnp.mean(xf * xf, axis=-1, keepdims=True) + RMS_EPS)
    return (y * g.astype(jnp.float32)).astype(x.dtype)


def split_cols(t, sizes):
    outs, off = [], 0
    for size in sizes:
        outs.append(t[..., off:off + size])
        off += size
    return outs


def to_heads(t, n_heads):
    b, s, _ = t.shape
    return t.reshape(b, s, n_heads, HEAD_DIM).transpose(0, 2, 1, 3)


def from_heads(t):
    b, h, s, d = t.shape
    return t.transpose(0, 2, 1, 3).reshape(b, s, h * d)


def partial_rope(t, positions):
    half = ROPE_DIM // 2
    inv_freq = ROPE_THETA ** (-jnp.arange(0, half, dtype=jnp.float32) * 2.0 / ROPE_DIM)
    ang = positions.astype(jnp.float32)[:, None] * inv_freq[None, :]
    cos, sin = jnp.cos(ang), jnp.sin(ang)
    tf = t.astype(jnp.float32)
    x1, x2, rest = tf[..., :half], tf[..., half:ROPE_DIM], tf[..., ROPE_DIM:]
    rot = jnp.concatenate([x1 * cos - x2 * sin, x2 * cos + x1 * sin, rest], axis=-1)
    return rot.astype(t.dtype)


def fox_attention(q, k, v, log_f):
    b, h, s, d = q.shape
    scale = d ** -0.5
    c = jnp.cumsum(log_f, axis=-1)
    kpos = jnp.arange(s)
    n_blocks = s // FOX_Q_BLOCK

    def one_block(i):
        start = i * FOX_Q_BLOCK
        qb = lax.dynamic_slice_in_dim(q, start, FOX_Q_BLOCK, axis=2)
        cb = lax.dynamic_slice_in_dim(c, start, FOX_Q_BLOCK, axis=2)
        logits = jnp.einsum("bhqd,bhkd->bhqk", qb, k).astype(jnp.float32) * scale
        logits = logits + cb[..., :, None] - c[..., None, :]
        qpos = start + jnp.arange(FOX_Q_BLOCK)
        logits = jnp.where(kpos[None, :] <= qpos[:, None], logits, -jnp.inf)
        p = jax.nn.softmax(logits, axis=-1)
        return jnp.einsum("bhqk,bhkd->bhqd", p.astype(v.dtype), v)

    out = lax.map(one_block, jnp.arange(n_blocks))
    return out.transpose(1, 2, 0, 3, 4).reshape(b, h, s, d)


def moba_attention(q, k, v):
    b, h, s, d = q.shape
    scale = d ** -0.5
    n_kb = -(-s // MOBA_BLOCK)
    s_pad = n_kb * MOBA_BLOCK
    pad = ((0, 0), (0, 0), (0, s_pad - s), (0, 0))
    q, k, v = jnp.pad(q, pad), jnp.pad(k, pad), jnp.pad(v, pad)
    kb = k.reshape(b, h, n_kb, MOBA_BLOCK, d)
    vb = v.reshape(b, h, n_kb, MOBA_BLOCK, d)
    k_mean = jnp.mean(kb.astype(jnp.float32), axis=3)
    n_sel = min(MOBA_TOPK, n_kb)
    blk_ids = jnp.arange(n_kb)
    b_ix = jnp.arange(b)[:, None, None, None]
    h_ix = jnp.arange(h)[None, :, None, None]
    n_chunks = s_pad // MOBA_Q_CHUNK

    def one_chunk(ci):
        start = ci * MOBA_Q_CHUNK
        own = start // MOBA_BLOCK
        qc = lax.dynamic_slice_in_dim(q, start, MOBA_Q_CHUNK, axis=2)
        gate = jnp.einsum("bhtd,bhnd->bhtn", qc.astype(jnp.float32), k_mean)
        gate = jnp.where(blk_ids < own, gate, -jnp.inf)
        _, sel = lax.top_k(gate, n_sel)
        valid = sel < own
        k_sel = kb[b_ix, h_ix, sel]
        v_sel = vb[b_ix, h_ix, sel]
        s_sel = jnp.einsum("bhtd,bhtnld->bhtnl", qc, k_sel).astype(jnp.float32) * scale
        s_sel = jnp.where(valid[..., None], s_sel, -jnp.inf)
        s_sel = s_sel.reshape(b, h, MOBA_Q_CHUNK, n_sel * MOBA_BLOCK)
        k_own = lax.dynamic_index_in_dim(kb, own, axis=2, keepdims=False)
        v_own = lax.dynamic_index_in_dim(vb, own, axis=2, keepdims=False)
        s_own = jnp.einsum("bhtd,bhld->bhtl", qc, k_own).astype(jnp.float32) * scale
        qpos = start + jnp.arange(MOBA_Q_CHUNK)
        kpos = own * MOBA_BLOCK + jnp.arange(MOBA_BLOCK)
        s_own = jnp.where(kpos[None, :] <= qpos[:, None], s_own, -jnp.inf)
        p = jax.nn.softmax(jnp.concatenate([s_sel, s_own], axis=-1), axis=-1).astype(v.dtype)
        p_sel = p[..., :n_sel * MOBA_BLOCK].reshape(b, h, MOBA_Q_CHUNK, n_sel, MOBA_BLOCK)
        p_own = p[..., n_sel * MOBA_BLOCK:]
        return (jnp.einsum("bhtnl,bhtnld->bhtd", p_sel, v_sel)
                + jnp.einsum("bhtl,bhld->bhtd", p_own, v_own))

    out = lax.map(one_chunk, jnp.arange(n_chunks))
    out = out.transpose(1, 2, 0, 3, 4).reshape(b, h, s_pad, d)
    return out[:, :, :s]


def setup_inputs(seed: int = 0) -> dict:
    key = jax.random.key(seed)
    ks = jax.random.split(key, 12)
    f32 = jnp.float32
    x = jax.random.normal(ks[0], (BATCH, SEQ, D_MODEL), f32)
    norm_g = 1.0 + 0.02 * jax.random.normal(ks[1], (DEPTH, D_MODEL), f32)
    w_in = jax.random.normal(ks[2], (DEPTH, D_MODEL, IN_WIDTH), f32) * D_MODEL ** -0.5
    b_f = FOX_FORGET_BIAS + 0.1 * jax.random.normal(ks[3], (DEPTH, FOX_HEADS), f32)
    b_gate = 0.02 * jax.random.normal(ks[4], (DEPTH, 2, D_MODEL), f32)
    fox_q_g = 1.0 + 0.02 * jax.random.normal(ks[5], (DEPTH, HEAD_DIM), f32)
    fox_k_g = 1.0 + 0.02 * jax.random.normal(ks[6], (DEPTH, HEAD_DIM), f32)
    moba_q_g = 1.0 + 0.02 * jax.random.normal(ks[7], (DEPTH, HEAD_DIM), f32)
    moba_k_g = 1.0 + 0.02 * jax.random.normal(ks[8], (DEPTH, HEAD_DIM), f32)
    w_fox = jax.random.normal(ks[9], (DEPTH, FOX_WIDTH, D_MODEL), f32) * FOX_WIDTH ** -0.5
    w_moba = jax.random.normal(ks[10], (DEPTH, MOBA_WIDTH, D_MODEL), f32) * MOBA_WIDTH ** -0.5
    w_out = jax.random.normal(ks[11], (DEPTH, D_MODEL, D_MODEL), f32) * D_MODEL ** -0.5
    return {"x": x, "norm_g": norm_g, "w_in": w_in, "b_f": b_f, "b_gate": b_gate,
            "fox_q_g": fox_q_g, "fox_k_g": fox_k_g, "moba_q_g": moba_q_g,
            "moba_k_g": moba_k_g, "w_fox": w_fox, "w_moba": w_moba, "w_out": w_out}


def reference(x, norm_g, w_in, b_f, b_gate, fox_q_g, fox_k_g, moba_q_g, moba_k_g,
              w_fox, w_moba, w_out):
    seq = x.shape[1]
    positions = jnp.arange(seq, dtype=jnp.int32)
    for layer in range(DEPTH):
        h = rms_norm(x, norm_g[layer])
        proj = h @ w_in[layer]
        fq, fk, fv, fz, mq, mk, mv, mz, ga, gb, fl = split_cols(proj, IN_SPLITS)
        q = rms_norm(to_heads(fq, FOX_HEADS), fox_q_g[layer])
        k = rms_norm(to_heads(fk, FOX_HEADS), fox_k_g[layer])
        v = to_heads(fv, FOX_HEADS)
        log_f = jax.nn.log_sigmoid((fl + b_f[layer]).astype(jnp.float32)).transpose(0, 2, 1)
        y_fox = from_heads(fox_attention(q, k, v, log_f)) * jax.nn.silu(fz)
        q = partial_rope(rms_norm(to_heads(mq, MOBA_HEADS), moba_q_g[layer]), positions)
        k = partial_rope(rms_norm(to_heads(mk, MOBA_HEADS), moba_k_g[layer]), positions)
        v = to_heads(mv, MOBA_HEADS)
        y_moba = from_heads(moba_attention(q, k, v)) * jax.nn.silu(mz)
        merged = (jax.nn.sigmoid(ga + b_gate[layer, 0]) * (y_fox @ w_fox[layer])
                  + jax.nn.sigmoid(gb + b_gate[layer, 1]) * (y_moba @ w_moba[layer]))
        x = x + merged @ w_out[layer]
    return x
```

```python
import functools

import numpy as np
import jax
import jax.numpy as jnp
from jax import lax
from jax.experimental import pallas as pl
from jax.experimental.pallas import tpu as pltpu

D_MODEL = 1024
HEAD_DIM = 64
N_HEADS = 8
WIDTH = N_HEADS * HEAD_DIM
ROPE_DIM = HEAD_DIM // 4
ROPE_HALF = ROPE_DIM // 2
ROPE_THETA = 500000.0
MOBA_BLOCK = 256
MOBA_TOPK = 3
RMS_EPS = 1e-6

LANES = 128
TILE = 256
GROUP = 16
N_AUG_HEADS = 2 * N_HEADS

C_FQ, C_FK, C_FV, C_FZ = 0, 512, 1024, 1536
C_MQ, C_MK, C_MV, C_MZ = 2048, 2560, 3072, 3584
C_GA, C_GB, C_FL = 4096, 5120, 6144
W_COLS = 6144 + LANES

VMEM_LIMIT = 52 * 1024 * 1024

F32 = jnp.float32
BF16 = jnp.bfloat16


def _dot(a, b):
    return jnp.dot(a, b, preferred_element_type=F32)


def _dot_nt(a, b):
    return lax.dot_general(a, b, (((1,), (1,)), ((), ())), preferred_element_type=F32)


def _split3(v):
    hi = v.astype(BF16).astype(F32)
    r = v - hi
    mid = r.astype(BF16).astype(F32)
    lo = r - mid
    return hi, mid, lo


def _extra_base(h):
    return h * LANES + (HEAD_DIM if h % 2 == 0 else 0)


def _routing_constants():
    sel_fq = np.zeros((LANES, N_HEADS * LANES), np.float32)
    sel_fk = np.zeros((LANES, N_HEADS * LANES), np.float32)
    sel_mq = np.zeros((LANES, N_HEADS * LANES), np.float32)
    for h in range(N_HEADS):
        base = _extra_base(h)
        for part in range(3):
            sel_fq[part * 8 + h, base + part] = 1.0
            sel_fq[24, base + 3 + part] = 1.0
            sel_fk[24, base + part] = 1.0
            sel_fk[part * 8 + h, base + 3 + part] = -1.0
        for n in range(GROUP):
            sel_mq[h * GROUP + n, base + n] = 1.0
    tri = np.tril(np.ones((TILE, TILE), np.float32))
    return sel_fq, sel_fk, sel_mq, tri


def _rope_tables(seq):
    inv_freq = ROPE_THETA ** (-np.arange(0, ROPE_HALF, dtype=np.float32) * 2.0 / ROPE_DIM)
    ang = np.arange(seq, dtype=np.float32)[:, None] * inv_freq[None, :].astype(np.float32)
    cos, sin = np.cos(ang).astype(np.float32), np.sin(ang).astype(np.float32)
    rc = np.ones((seq, LANES), np.float32)
    rsp = np.zeros((seq, LANES), np.float32)
    rsm = np.zeros((seq, LANES), np.float32)
    for off in (0, HEAD_DIM):
        rc[:, off:off + ROPE_HALF] = cos
        rc[:, off + ROPE_HALF:off + ROPE_DIM] = cos
        rsm[:, off:off + ROPE_HALF] = -sin
        rsp[:, off + ROPE_HALF:off + ROPE_DIM] = sin
    return rc, rsp, rsm


def _proj_kernel(x_ref, w_ref, ng_ref, bf_ref, bg_ref, gfq_ref, gfk_ref, gmq_ref, gmk_ref,
                 rc_ref, rsp_ref, rsm_ref, tri_ref, selfq_ref, selfk_ref, selmq_ref, big_ref,
                 q_out, k_out, v_out, gz_out, gate_out,
                 carry_ref, kmt_ref, *, tiles_per_seq):
    t = pl.program_id(0) % tiles_per_seq

    @pl.when(t == 0)
    def _():
        carry_ref[...] = jnp.zeros_like(carry_ref)
        kmt_ref[...] = jnp.zeros_like(kmt_ref)

    lane = lax.broadcasted_iota(jnp.int32, (TILE, LANES), 1)
    low_half = lane < HEAD_DIM

    x = x_ref[...]
    ms = jnp.mean(x * x, axis=-1, keepdims=True)
    h = (x * lax.rsqrt(ms + RMS_EPS) * ng_ref[...]).astype(BF16)

    def proj(c0, width):
        return _dot(h, w_ref[:, c0:c0 + width])

    def head_norm(a, g_ref):
        tiles = []
        for p in range(WIDTH // LANES):
            ap = a[:, p * LANES:(p + 1) * LANES]
            sq = ap * ap
            s_lo = jnp.sum(jnp.where(low_half, sq, 0.0), axis=-1, keepdims=True)
            s_hi = jnp.sum(jnp.where(low_half, 0.0, sq), axis=-1, keepdims=True)
            inv_lo = lax.rsqrt(s_lo * (1.0 / HEAD_DIM) + RMS_EPS)
            inv_hi = lax.rsqrt(s_hi * (1.0 / HEAD_DIM) + RMS_EPS)
            scale = jnp.where(low_half, inv_lo, inv_hi)
            tiles.append(ap * scale * g_ref[:, p * LANES:(p + 1) * LANES])
        return tiles

    def rope(y):
        return (y * rc_ref[...] + pltpu.roll(y, ROPE_HALF, 1) * rsp_ref[...]
                + pltpu.roll(y, LANES - ROPE_HALF, 1) * rsm_ref[...])

    def store_heads(out_ref, head0, tiles, extras):
        for p, y in enumerate(tiles):
            he, ho = 2 * p, 2 * p + 1
            out_ref[0, head0 + he] = jnp.where(low_half, y, extras(he)).astype(BF16)
            out_ref[0, head0 + ho] = jnp.where(low_half, extras(ho), y).astype(BF16)

    fl = proj(C_FL, LANES) + bf_ref[...]
    logf = jnp.minimum(fl, 0.0) - jnp.log(1.0 + jnp.exp(-jnp.abs(fl)))
    logf = jnp.where(lane < N_HEADS, logf, 0.0)
    l_hi, l_mid, l_lo = _split3(logf)
    packed = (l_hi + pltpu.roll(l_mid, 8, 1) + pltpu.roll(l_lo, 16, 1)).astype(BF16)
    cum = _dot(tri_ref[...], packed)
    c = cum + pltpu.roll(cum, LANES - 8, 1) + pltpu.roll(cum, LANES - 16, 1)
    c = jnp.where(lane < N_HEADS, c, 0.0) + carry_ref[...]
    carry_ref[...] = c[TILE - 1:TILE, :]
    c_hi, c_mid, c_lo = _split3(c)
    cparts = (c_hi + pltpu.roll(c_mid, 8, 1) + pltpu.roll(c_lo, 16, 1)
              + jnp.where(lane == 24, 1.0, 0.0)).astype(BF16)
    ex_fq = _dot(cparts, selfq_ref[...])
    ex_fk = _dot(cparts, selfk_ref[...])

    def ex_slice(ex):
        return lambda hh: ex[:, hh * LANES:(hh + 1) * LANES]

    store_heads(q_out, 0, head_norm(proj(C_FQ, WIDTH), gfq_ref), ex_slice(ex_fq))
    store_heads(k_out, 0, head_norm(proj(C_FK, WIDTH), gfk_ref), ex_slice(ex_fk))

    ones_lo = jnp.where(lane == 0, 1.0, 0.0)
    ones_hi = jnp.where(lane == HEAD_DIM, 1.0, 0.0)

    def v_extras(hh):
        return ones_hi if hh % 2 == 0 else ones_lo

    fv = proj(C_FV, WIDTH)
    store_heads(v_out, 0, [fv[:, p * LANES:(p + 1) * LANES] for p in range(4)], v_extras)

    fz = proj(C_FZ, WIDTH)
    gz_out[:, 0:WIDTH] = (fz * (1.0 / (1.0 + jnp.exp(-fz)))).astype(BF16)

    mk_tiles = [rope(y) for y in head_norm(proj(C_MK, WIDTH), gmk_ref)]
    row_id = lax.broadcasted_iota(jnp.int32, (LANES, LANES), 0)
    lane_sq = lax.broadcasted_iota(jnp.int32, (LANES, LANES), 1)
    for p, kr in enumerate(mk_tiles):
        km = jnp.mean(kr, axis=0, keepdims=True)
        hit = (((row_id == (2 * p) * GROUP + t) & (lane_sq < HEAD_DIM))
               | ((row_id == (2 * p + 1) * GROUP + t) & (lane_sq >= HEAD_DIM)))
        blk = kmt_ref[:, p * LANES:(p + 1) * LANES]
        kmt_ref[:, p * LANES:(p + 1) * LANES] = jnp.where(hit, km, blk)

    def mk_extras(hh):
        return jnp.where(lane == (HEAD_DIM if hh % 2 == 0 else 0) + t, 1.0, 0.0)

    store_heads(k_out, N_HEADS, mk_tiles, mk_extras)

    mq_tiles = [rope(y) for y in head_norm(proj(C_MQ, WIDTH), gmq_ref)]
    gate = jnp.zeros((TILE, LANES), F32)
    for p, qr in enumerate(mq_tiles):
        kmt = kmt_ref[:, p * LANES:(p + 1) * LANES]
        q_hi = qr.astype(BF16)
        q_lo = (qr - q_hi.astype(F32)).astype(BF16)
        k_hi = kmt.astype(BF16)
        k_lo = (kmt - k_hi.astype(F32)).astype(BF16)
        gate = gate + _dot_nt(q_hi, k_hi) + _dot_nt(q_hi, k_lo) + _dot_nt(q_lo, k_hi)

    blk_id = lane % GROUP
    past = blk_id < t
    g = jnp.where(past, gate, -jnp.inf)
    beaten = jnp.zeros((TILE, LANES), jnp.int32)
    for d in range(1, GROUP):
        lower = pltpu.roll(g, d, 1)
        beaten = beaten + jnp.where((blk_id >= d) & (lower >= g), 1, 0)
        upper = pltpu.roll(g, LANES - d, 1)
        beaten = beaten + jnp.where((blk_id < GROUP - d) & (upper > g), 1, 0)
    keep = (past & (beaten < MOBA_TOPK)) | (blk_id == t)
    maskvals = jnp.where(keep, 0.0, -big_ref[...]).astype(BF16)
    ex_mq = _dot(maskvals, selmq_ref[...])
    store_heads(q_out, N_HEADS, mq_tiles, ex_slice(ex_mq))

    mv = proj(C_MV, WIDTH)
    store_heads(v_out, N_HEADS, [mv[:, p * LANES:(p + 1) * LANES] for p in range(4)], v_extras)

    mz = proj(C_MZ, WIDTH)
    gz_out[:, WIDTH:2 * WIDTH] = (mz * (1.0 / (1.0 + jnp.exp(-mz)))).astype(BF16)

    for c0, o0 in ((C_GA, 0), (C_GB, D_MODEL)):
        for half in range(2):
            lo = half * WIDTH
            ga = proj(c0 + lo, WIDTH) + bg_ref[:, o0 + lo:o0 + lo + WIDTH]
            gate_out[:, o0 + lo:o0 + lo + WIDTH] = (1.0 / (1.0 + jnp.exp(-ga))).astype(BF16)


def _attn_kernel(q_ref, k_ref, v_ref, gz_ref, o_ref):
    i = pl.program_id(2)
    row = lax.broadcasted_iota(jnp.int32, (TILE, TILE), 0)
    col = lax.broadcasted_iota(jnp.int32, (TILE, TILE), 1)
    causal = col <= row
    start = pl.multiple_of(i * TILE, TILE)

    qs = [q_ref[0, hh] for hh in range(2)]

    init = []
    for hh in range(2):
        kd = k_ref[0, hh, pl.ds(start, TILE), :]
        vd = v_ref[0, hh, pl.ds(start, TILE), :]
        s = jnp.where(causal, _dot_nt(qs[hh], kd), -jnp.inf)
        m = jnp.max(s, axis=-1, keepdims=True)
        p = jnp.exp(s - m)
        init += [m, _dot(p.astype(BF16), vd)]

    def body(j, carry):
        off = pl.multiple_of(j * TILE, TILE)
        new = []
        for hh in range(2):
            m, acc = carry[2 * hh], carry[2 * hh + 1]
            kj = k_ref[0, hh, pl.ds(off, TILE), :]
            vj = v_ref[0, hh, pl.ds(off, TILE), :]
            s = _dot_nt(qs[hh], kj)
            m_new = jnp.maximum(m, jnp.max(s, axis=-1, keepdims=True))
            alpha = jnp.exp(m - m_new)
            p = jnp.exp(s - m_new)
            new += [m_new, alpha * acc + _dot(p.astype(BF16), vj)]
        return tuple(new)

    _, acc_e, _, acc_o = lax.fori_loop(0, i, body, tuple(init))

    lane = lax.broadcasted_iota(jnp.int32, (TILE, LANES), 1)
    o_e = acc_e * (1.0 / acc_e[:, HEAD_DIM:HEAD_DIM + 1])
    o_o = acc_o * (1.0 / acc_o[:, 0:1])
    o = jnp.where(lane < HEAD_DIM, o_e, o_o)
    o_ref[...] = (o * gz_ref[...].astype(F32)).astype(BF16)


def _out_kernel(y_ref, gate_ref, x_ref, wf_ref, wm_ref, wo_ref, o_ref):
    pf = _dot(y_ref[:, 0:WIDTH], wf_ref[...])
    pm = _dot(y_ref[:, WIDTH:2 * WIDTH], wm_ref[...])
    merged = (gate_ref[:, 0:D_MODEL].astype(F32) * pf
              + gate_ref[:, D_MODEL:2 * D_MODEL].astype(F32) * pm)
    o_ref[...] = x_ref[...] + _dot(merged.astype(BF16), wo_ref[...])


def _const_spec(shape):
    return pl.BlockSpec(shape, lambda *_: (0,) * len(shape))


def _layer(x2, batch, seq, norm_g, w_in, b_f, b_gate, fox_q_g, fox_k_g, moba_q_g, moba_k_g,
           w_fox, w_moba, w_out):
    rows = batch * seq
    tiles_per_seq = seq // TILE
    n_tiles = rows // TILE
    scale = HEAD_DIM ** -0.5

    w_fl = jnp.pad(w_in[:, 6144:], ((0, 0), (0, LANES - N_HEADS)))
    w = jnp.concatenate([w_in[:, :6144], w_fl], axis=1).astype(BF16)
    ng = norm_g.reshape(1, D_MODEL)
    bf = jnp.pad(b_f, (0, LANES - N_HEADS)).reshape(1, LANES)
    bg = b_gate.reshape(1, 2 * D_MODEL)
    gfq = jnp.tile(fox_q_g * scale, N_HEADS).reshape(1, WIDTH)
    gfk = jnp.tile(fox_k_g, N_HEADS).reshape(1, WIDTH)
    gmq = jnp.tile(moba_q_g * scale, N_HEADS).reshape(1, WIDTH)
    gmk = jnp.tile(moba_k_g, N_HEADS).reshape(1, WIDTH)
    bound = 8.0 * jnp.max(jnp.abs(moba_q_g)) * jnp.max(jnp.abs(moba_k_g))
    big = jnp.exp2(jnp.ceil(jnp.log2(2.1 * bound + 128.0)))
    big = jnp.full((1, LANES), 1.0, F32) * big

    sel_fq, sel_fk, sel_mq, tri = _routing_constants()
    rc, rsp, rsm = _rope_tables(seq)

    rope_spec = pl.BlockSpec((TILE, LANES), lambda g: (g % tiles_per_seq, 0))
    head_spec = pl.BlockSpec((1, N_AUG_HEADS, TILE, LANES),
                             lambda g: (g // tiles_per_seq, 0, g % tiles_per_seq, 0))
    qkv_shape = jax.ShapeDtypeStruct((batch, N_AUG_HEADS, seq, LANES), BF16)

    q_all, k_all, v_all, gz, gates = pl.pallas_call(
        functools.partial(_proj_kernel, tiles_per_seq=tiles_per_seq),
        grid=(n_tiles,),
        in_specs=[
            pl.BlockSpec((TILE, D_MODEL), lambda g: (g, 0)),
            _const_spec((D_MODEL, W_COLS)),
            _const_spec((1, D_MODEL)), _const_spec((1, LANES)), _const_spec((1, 2 * D_MODEL)),
            _const_spec((1, WIDTH)), _const_spec((1, WIDTH)),
            _const_spec((1, WIDTH)), _const_spec((1, WIDTH)),
            rope_spec, rope_spec, rope_spec,
            _const_spec((TILE, TILE)),
            _const_spec((LANES, N_HEADS * LANES)), _const_spec((LANES, N_HEADS * LANES)),
            _const_spec((LANES, N_HEADS * LANES)),
            _const_spec((1, LANES)),
        ],
        out_specs=[
            head_spec, head_spec, head_spec,
            pl.BlockSpec((TILE, 2 * WIDTH), lambda g: (g, 0)),
            pl.BlockSpec((TILE, 2 * D_MODEL), lambda g: (g, 0)),
        ],
        out_shape=[
            qkv_shape, qkv_shape, qkv_shape,
            jax.ShapeDtypeStruct((rows, 2 * WIDTH), BF16),
            jax.ShapeDtypeStruct((rows, 2 * D_MODEL), BF16),
        ],
        scratch_shapes=[pltpu.VMEM((1, LANES), F32),
                        pltpu.VMEM((LANES, WIDTH), F32)],
        compiler_params=pltpu.CompilerParams(
            dimension_semantics=("arbitrary",), vmem_limit_bytes=VMEM_LIMIT),
        name="proj_epilogue",
    )(x2, w, ng, bf, bg, gfq, gfk, gmq, gmk,
      jnp.asarray(rc), jnp.asarray(rsp), jnp.asarray(rsm),
      jnp.asarray(tri, BF16), jnp.asarray(sel_fq, BF16), jnp.asarray(sel_fk, BF16),
      jnp.asarray(sel_mq, BF16), big)

    n_pairs = N_AUG_HEADS // 2
    kv_spec = pl.BlockSpec((1, 2, seq, LANES), lambda b, p, i: (b, p, 0, 0))
    y = pl.pallas_call(
        _attn_kernel,
        grid=(batch, n_pairs, tiles_per_seq),
        in_specs=[
            pl.BlockSpec((1, 2, TILE, LANES), lambda b, p, i: (b, p, i, 0)),
            kv_spec, kv_spec,
            pl.BlockSpec((TILE, LANES), lambda b, p, i: (b * tiles_per_seq + i, p)),
        ],
        out_specs=pl.BlockSpec((TILE, LANES), lambda b, p, i: (b * tiles_per_seq + i, p)),
        out_shape=jax.ShapeDtypeStruct((rows, 2 * WIDTH), BF16),
        compiler_params=pltpu.CompilerParams(
            dimension_semantics=("arbitrary", "arbitrary", "arbitrary"),
            vmem_limit_bytes=VMEM_LIMIT),
        name="flash_attn",
    )(q_all, k_all, v_all, gz)

    out = pl.pallas_call(
        _out_kernel,
        grid=(n_tiles,),
        in_specs=[
            pl.BlockSpec((TILE, 2 * WIDTH), lambda g: (g, 0)),
            pl.BlockSpec((TILE, 2 * D_MODEL), lambda g: (g, 0)),
            pl.BlockSpec((TILE, D_MODEL), lambda g: (g, 0)),
            _const_spec((WIDTH, D_MODEL)), _const_spec((WIDTH, D_MODEL)),
            _const_spec((D_MODEL, D_MODEL)),
        ],
        out_specs=pl.BlockSpec((TILE, D_MODEL), lambda g: (g, 0)),
        out_shape=jax.ShapeDtypeStruct((rows, D_MODEL), F32),
        compiler_params=pltpu.CompilerParams(
            dimension_semantics=("arbitrary",), vmem_limit_bytes=VMEM_LIMIT),
        name="merge_out",
    )(y, gates, x2, w_fox.astype(BF16), w_moba.astype(BF16), w_out.astype(BF16))
    return out


def kernel(x, norm_g, w_in, b_f, b_gate, fox_q_g, fox_k_g, moba_q_g, moba_k_g, w_fox, w_moba, w_out):
    batch, seq, d_model = x.shape
    assert d_model == D_MODEL and seq % TILE == 0 and seq // TILE <= GROUP
    x2 = x.reshape(batch * seq, D_MODEL)
    for layer in range(norm_g.shape[0]):
        x2 = _layer(x2, batch, seq, norm_g[layer], w_in[layer], b_f[layer], b_gate[layer],
                    fox_q_g[layer], fox_k_g[layer], moba_q_g[layer], moba_k_g[layer],
                    w_fox[layer], w_moba[layer], w_out[layer])
    return x2.reshape(batch, seq, D_MODEL)
```

```python
import functools

import numpy as np
import jax
import jax.numpy as jnp
from jax import lax
from jax.experimental import pallas as pl
from jax.experimental.pallas import tpu as pltpu

D_MODEL = 1024
HEAD_DIM = 64
N_HEADS = 8
WIDTH = N_HEADS * HEAD_DIM
ROPE_DIM = HEAD_DIM // 4
ROPE_HALF = ROPE_DIM // 2
ROPE_THETA = 500000.0
MOBA_BLOCK = 256
MOBA_TOPK = 3
RMS_EPS = 1e-6

LANES = 128
TILE = 256
ATT_TILE = 512
FIXED_STABILIZER_MAX_BOUND = 55.0
GROUP = 16
N_AUG_HEADS = 2 * N_HEADS
U_COL_FOX = 6
U_COL_MOBA = GROUP
LOG2E = 1.4426950408889634

C_FQ, C_FK, C_FV, C_FZ = 0, 512, 1024, 1536
C_MQ, C_MK, C_MV, C_MZ = 2048, 2560, 3072, 3584
C_GA, C_GB, C_FL = 4096, 5120, 6144
W_COLS = 6144 + LANES

VMEM_LIMIT = 52 * 1024 * 1024

F32 = jnp.float32
BF16 = jnp.bfloat16


def _dot(a, b):
    return jnp.dot(a, b, preferred_element_type=F32)


def _dot_nt(a, b):
    return lax.dot_general(a, b, (((1,), (1,)), ((), ())), preferred_element_type=F32)


def _split3(v):
    hi = v.astype(BF16).astype(F32)
    r = v - hi
    mid = r.astype(BF16).astype(F32)
    lo = r - mid
    return hi, mid, lo


def _extra_base(h):
    return h * LANES + (HEAD_DIM if h % 2 == 0 else 0)


def _routing_constants():
    sel_fq = np.zeros((LANES, N_HEADS * LANES), np.float32)
    sel_fk = np.zeros((LANES, N_HEADS * LANES), np.float32)
    sel_mq = np.zeros((LANES, N_HEADS * LANES), np.float32)
    for h in range(N_HEADS):
        base = _extra_base(h)
        for part in range(3):
            sel_fq[part * 8 + h, base + part] = 1.0
            sel_fq[24, base + 3 + part] = 1.0
            sel_fk[24, base + part] = 1.0
            sel_fk[part * 8 + h, base + 3 + part] = -1.0
        sel_fq[25, base + U_COL_FOX] = 1.0
        sel_fk[24, base + U_COL_FOX] = 1.0
        for n in range(GROUP):
            sel_mq[h * GROUP + n, base + n] = 1.0
    tri = np.tril(np.ones((TILE, TILE), np.float32))
    return sel_fq, sel_fk, sel_mq, tri


def _rope_tables(seq):
    inv_freq = ROPE_THETA ** (-np.arange(0, ROPE_HALF, dtype=np.float32) * 2.0 / ROPE_DIM)
    ang = np.arange(seq, dtype=np.float32)[:, None] * inv_freq[None, :].astype(np.float32)
    cos, sin = np.cos(ang).astype(np.float32), np.sin(ang).astype(np.float32)
    rc = np.ones((seq, LANES), np.float32)
    rsp = np.zeros((seq, LANES), np.float32)
    rsm = np.zeros((seq, LANES), np.float32)
    for off in (0, HEAD_DIM):
        rc[:, off:off + ROPE_HALF] = cos
        rc[:, off + ROPE_HALF:off + ROPE_DIM] = cos
        rsm[:, off:off + ROPE_HALF] = -sin
        rsp[:, off + ROPE_HALF:off + ROPE_DIM] = sin
    return rc, rsp, rsm


def _proj_kernel(x_ref, w_ref, ng_ref, bf_ref, bg_ref, gfq_ref, gfk_ref, gmq_ref, gmk_ref,
                 rc_ref, rsp_ref, rsm_ref, tri_ref, selfq_ref, selfk_ref, selmq_ref, misc_ref,
                 q_out, k_out, v_out, gz_out, gate_out,
                 carry_ref, kmt_ref, *, tiles_per_seq):
    t = pl.program_id(0) % tiles_per_seq

    @pl.when(t == 0)
    def _():
        carry_ref[...] = jnp.zeros_like(carry_ref)
        kmt_ref[...] = jnp.zeros_like(kmt_ref)

    lane = lax.broadcasted_iota(jnp.int32, (TILE, LANES), 1)
    low_half = lane < HEAD_DIM

    x = x_ref[...]
    ms = jnp.mean(x * x, axis=-1, keepdims=True)
    h = (x * lax.rsqrt(ms + RMS_EPS) * ng_ref[...]).astype(BF16)

    def proj(c0, width):
        return _dot(h, w_ref[:, c0:c0 + width])

    def head_norm(a, g_ref):
        tiles = []
        for p in range(WIDTH // LANES):
            ap = a[:, p * LANES:(p + 1) * LANES]
            sq = ap * ap
            s_lo = jnp.sum(jnp.where(low_half, sq, 0.0), axis=-1, keepdims=True)
            s_hi = jnp.sum(jnp.where(low_half, 0.0, sq), axis=-1, keepdims=True)
            inv_lo = lax.rsqrt(s_lo * (1.0 / HEAD_DIM) + RMS_EPS)
            inv_hi = lax.rsqrt(s_hi * (1.0 / HEAD_DIM) + RMS_EPS)
            scale = jnp.where(low_half, inv_lo, inv_hi)
            tiles.append(ap * scale * g_ref[:, p * LANES:(p + 1) * LANES])
        return tiles

    def rope(y):
        return (y * rc_ref[...] + pltpu.roll(y, ROPE_HALF, 1) * rsp_ref[...]
                + pltpu.roll(y, LANES - ROPE_HALF, 1) * rsm_ref[...])

    def store_heads(out_ref, head0, tiles, extras):
        for p, y in enumerate(tiles):
            he, ho = 2 * p, 2 * p + 1
            out_ref[0, head0 + he] = jnp.where(low_half, y, extras(he)).astype(BF16)
            out_ref[0, head0 + ho] = jnp.where(low_half, extras(ho), y).astype(BF16)

    fl = proj(C_FL, LANES) + bf_ref[...]
    logf = jnp.minimum(fl, 0.0) - jnp.log(1.0 + jnp.exp(-jnp.abs(fl)))
    logf = jnp.where(lane < N_HEADS, logf, 0.0)
    l_hi, l_mid, l_lo = _split3(logf)
    packed = (l_hi + pltpu.roll(l_mid, 8, 1) + pltpu.roll(l_lo, 16, 1)).astype(BF16)
    cum = _dot(tri_ref[...], packed)
    c = cum + pltpu.roll(cum, LANES - 8, 1) + pltpu.roll(cum, LANES - 16, 1)
    c = jnp.where(lane < N_HEADS, c, 0.0) + carry_ref[...]
    carry_ref[...] = c[TILE - 1:TILE, :]
    c_hi, c_mid, c_lo = _split3(c * LOG2E)
    cparts = (c_hi + pltpu.roll(c_mid, 8, 1) + pltpu.roll(c_lo, 16, 1)
              + jnp.where(lane == 24, 1.0, 0.0)
              + jnp.where(lane == 25, misc_ref[1:2, :], 0.0)).astype(BF16)
    ex_fq = _dot(cparts, selfq_ref[...])
    ex_fk = _dot(cparts, selfk_ref[...])

    def ex_slice(ex):
        return lambda hh: ex[:, hh * LANES:(hh + 1) * LANES]

    store_heads(q_out, 0, head_norm(proj(C_FQ, WIDTH), gfq_ref), ex_slice(ex_fq))
    store_heads(k_out, 0, head_norm(proj(C_FK, WIDTH), gfk_ref), ex_slice(ex_fk))

    ones_lo = jnp.where(lane == 0, 1.0, 0.0)
    ones_hi = jnp.where(lane == HEAD_DIM, 1.0, 0.0)

    def v_extras(hh):
        return ones_hi if hh % 2 == 0 else ones_lo

    fv = proj(C_FV, WIDTH)
    store_heads(v_out, 0, [fv[:, p * LANES:(p + 1) * LANES] for p in range(4)], v_extras)

    fz = proj(C_FZ, WIDTH)
    gz_out[:, 0:WIDTH] = (fz * (1.0 / (1.0 + jnp.exp(-fz)))).astype(BF16)

    mk_tiles = [rope(y) for y in head_norm(proj(C_MK, WIDTH), gmk_ref)]
    row_id = lax.broadcasted_iota(jnp.int32, (LANES, LANES), 0)
    lane_sq = lax.broadcasted_iota(jnp.int32, (LANES, LANES), 1)
    for p, kr in enumerate(mk_tiles):
        km = jnp.mean(kr, axis=0, keepdims=True)
        hit = (((row_id == (2 * p) * GROUP + t) & (lane_sq < HEAD_DIM))
               | ((row_id == (2 * p + 1) * GROUP + t) & (lane_sq >= HEAD_DIM)))
        blk = kmt_ref[:, p * LANES:(p + 1) * LANES]
        kmt_ref[:, p * LANES:(p + 1) * LANES] = jnp.where(hit, km, blk)

    def mk_extras(hh):
        off = HEAD_DIM if hh % 2 == 0 else 0
        return jnp.where((lane == off + t) | (lane == off + U_COL_MOBA), 1.0, 0.0)

    store_heads(k_out, N_HEADS, mk_tiles, mk_extras)

    mq_tiles = [rope(y) for y in head_norm(proj(C_MQ, WIDTH), gmq_ref)]
    gate = jnp.zeros((TILE, LANES), F32)
    for p, qr in enumerate(mq_tiles):
        kmt = kmt_ref[:, p * LANES:(p + 1) * LANES]
        q_hi = qr.astype(BF16)
        q_lo = (qr - q_hi.astype(F32)).astype(BF16)
        k_hi = kmt.astype(BF16)
        k_lo = (kmt - k_hi.astype(F32)).astype(BF16)
        gate = gate + _dot_nt(q_hi, k_hi) + _dot_nt(q_hi, k_lo) + _dot_nt(q_lo, k_hi)

    blk_id = lane % GROUP
    past = blk_id < t
    g = jnp.where(past, gate, -jnp.inf)
    beaten = jnp.zeros((TILE, LANES), jnp.int32)
    for d in range(1, GROUP):
        lower = pltpu.roll(g, d, 1)
        beaten = beaten + jnp.where((blk_id >= d) & (lower >= g), 1, 0)
        upper = pltpu.roll(g, LANES - d, 1)
        beaten = beaten + jnp.where((blk_id < GROUP - d) & (upper > g), 1, 0)
    keep = (past & (beaten < MOBA_TOPK)) | (blk_id == t)
    maskvals = jnp.where(keep, 0.0, -misc_ref[0:1, :]).astype(BF16)
    ex_mq = _dot(maskvals, selmq_ref[...])

    def mq_extras(hh):
        off = HEAD_DIM if hh % 2 == 0 else 0
        return jnp.where(lane == off + U_COL_MOBA, misc_ref[2:3, :],
                         ex_mq[:, hh * LANES:(hh + 1) * LANES])

    store_heads(q_out, N_HEADS, mq_tiles, mq_extras)

    mv = proj(C_MV, WIDTH)
    store_heads(v_out, N_HEADS, [mv[:, p * LANES:(p + 1) * LANES] for p in range(4)], v_extras)

    mz = proj(C_MZ, WIDTH)
    gz_out[:, WIDTH:2 * WIDTH] = (mz * (1.0 / (1.0 + jnp.exp(-mz)))).astype(BF16)

    for c0, o0 in ((C_GA, 0), (C_GB, D_MODEL)):
        for half in range(2):
            lo = half * WIDTH
            ga = proj(c0 + lo, WIDTH) + bg_ref[:, o0 + lo:o0 + lo + WIDTH]
            gate_out[:, o0 + lo:o0 + lo + WIDTH] = (1.0 / (1.0 + jnp.exp(-ga))).astype(BF16)


def _attn_kernel(flag_ref, q_ref, k_ref, v_ref, gz_ref, o_ref, acc_ref, m_ref):
    i = pl.program_id(2)
    half = ATT_TILE // 2
    start = pl.multiple_of(i * ATT_TILE, ATT_TILE)
    row = lax.broadcasted_iota(jnp.int32, (half, ATT_TILE), 0)
    col = lax.broadcasted_iota(jnp.int32, (half, ATT_TILE), 1)
    mask_top = (lax.broadcasted_iota(jnp.int32, (half, half), 1)
                <= lax.broadcasted_iota(jnp.int32, (half, half), 0))
    mask_bot = col <= row + half

    def diag_scores(hh):
        kd = k_ref[0, hh, pl.ds(start, ATT_TILE), :]
        s_top = jnp.where(mask_top, _dot_nt(q_ref[0, hh, 0:half, :], kd[0:half]), -jnp.inf)
        s_bot = jnp.where(mask_bot, _dot_nt(q_ref[0, hh, half:ATT_TILE, :], kd), -jnp.inf)
        return s_top, s_bot

    @pl.when(flag_ref[0] == 1)
    def _fixed_stabilizer():
        for hh in range(2):
            vd = v_ref[0, hh, pl.ds(start, ATT_TILE), :]
            s_top, s_bot = diag_scores(hh)
            acc_ref[hh, 0:half] = _dot(jnp.exp2(s_top).astype(BF16), vd[0:half])
            acc_ref[hh, half:ATT_TILE] = _dot(jnp.exp2(s_bot).astype(BF16), vd)

        def body(j, carry):
            off = pl.multiple_of(j * ATT_TILE, ATT_TILE)
            for hh in range(2):
                s = _dot_nt(q_ref[0, hh], k_ref[0, hh, pl.ds(off, ATT_TILE), :])
                acc_ref[hh] += _dot(jnp.exp2(s).astype(BF16),
                                    v_ref[0, hh, pl.ds(off, ATT_TILE), :])
            return carry

        lax.fori_loop(0, i, body, 0)

    @pl.when(flag_ref[0] != 1)
    def _online():
        for hh in range(2):
            vd = v_ref[0, hh, pl.ds(start, ATT_TILE), :]
            s_top, s_bot = diag_scores(hh)
            m_top = jnp.max(s_top, axis=-1, keepdims=True)
            m_bot = jnp.max(s_bot, axis=-1, keepdims=True)
            m_ref[hh, 0:half] = m_top
            m_ref[hh, half:ATT_TILE] = m_bot
            acc_ref[hh, 0:half] = _dot(jnp.exp2(s_top - m_top).astype(BF16), vd[0:half])
            acc_ref[hh, half:ATT_TILE] = _dot(jnp.exp2(s_bot - m_bot).astype(BF16), vd)

        def body(j, carry):
            off = pl.multiple_of(j * ATT_TILE, ATT_TILE)
            for hh in range(2):
                s = _dot_nt(q_ref[0, hh], k_ref[0, hh, pl.ds(off, ATT_TILE), :])
                m_old = m_ref[hh]
                m_new = jnp.maximum(m_old, jnp.max(s, axis=-1, keepdims=True))
                pv = _dot(jnp.exp2(s - m_new).astype(BF16), v_ref[0, hh, pl.ds(off, ATT_TILE), :])
                acc_ref[hh] = jnp.exp2(m_old - m_new) * acc_ref[hh] + pv
                m_ref[hh] = m_new
            return carry

        lax.fori_loop(0, i, body, 0)

    lane = lax.broadcasted_iota(jnp.int32, (ATT_TILE, LANES), 1)
    acc_e, acc_o = acc_ref[0], acc_ref[1]
    o_e = acc_e * (1.0 / acc_e[:, HEAD_DIM:HEAD_DIM + 1])
    o_o = acc_o * (1.0 / acc_o[:, 0:1])
    o = jnp.where(lane < HEAD_DIM, o_e, o_o)
    o_ref[...] = (o * gz_ref[...].astype(F32)).astype(BF16)


def _out_kernel(y_ref, gate_ref, x_ref, wf_ref, wm_ref, wo_ref, o_ref):
    pf = _dot(y_ref[:, 0:WIDTH], wf_ref[...])
    pm = _dot(y_ref[:, WIDTH:2 * WIDTH], wm_ref[...])
    merged = (gate_ref[:, 0:D_MODEL].astype(F32) * pf
              + gate_ref[:, D_MODEL:2 * D_MODEL].astype(F32) * pm)
    o_ref[...] = x_ref[...] + _dot(merged.astype(BF16), wo_ref[...])


def _const_spec(shape):
    return pl.BlockSpec(shape, lambda *_: (0,) * len(shape))


def _layer(x2, batch, seq, norm_g, w_in, b_f, b_gate, fox_q_g, fox_k_g, moba_q_g, moba_k_g,
           w_fox, w_moba, w_out):
    rows = batch * seq
    tiles_per_seq = seq // TILE
    n_tiles = rows // TILE
    scale = HEAD_DIM ** -0.5

    w_fl = jnp.pad(w_in[:, 6144:], ((0, 0), (0, LANES - N_HEADS)))
    w = jnp.concatenate([w_in[:, :6144], w_fl], axis=1).astype(BF16)
    ng = norm_g.reshape(1, D_MODEL)
    bf = jnp.pad(b_f, (0, LANES - N_HEADS)).reshape(1, LANES)
    bg = b_gate.reshape(1, 2 * D_MODEL)
    gfq = jnp.tile(fox_q_g * (scale * LOG2E), N_HEADS).reshape(1, WIDTH)
    gfk = jnp.tile(fox_k_g, N_HEADS).reshape(1, WIDTH)
    gmq = jnp.tile(moba_q_g * (scale * LOG2E), N_HEADS).reshape(1, WIDTH)
    gmk = jnp.tile(moba_k_g, N_HEADS).reshape(1, WIDTH)

    def logit_bound(gq, gk):
        return 8.0 * LOG2E * 1.02 * jnp.max(jnp.abs(gq)) * jnp.max(jnp.abs(gk))

    b_fox, b_moba = logit_bound(fox_q_g, fox_k_g), logit_bound(moba_q_g, moba_k_g)
    fixed_ok = jnp.maximum(b_fox, b_moba) <= FIXED_STABILIZER_MAX_BOUND

    def stabilizer(b):
        u = -(b * (1.0 + 2.0 ** -7)).astype(BF16).astype(F32)
        return jnp.where(fixed_ok, u, 0.0)

    big = jnp.exp2(jnp.ceil(jnp.log2(2.1 * b_moba + 160.0)))
    misc = jnp.zeros((8, LANES), F32)
    misc = misc.at[0].set(big).at[1].set(stabilizer(b_fox)).at[2].set(stabilizer(b_moba))
    flag = fixed_ok.astype(jnp.int32).reshape(1)

    sel_fq, sel_fk, sel_mq, tri = _routing_constants()
    rc, rsp, rsm = _rope_tables(seq)

    rope_spec = pl.BlockSpec((TILE, LANES), lambda g: (g % tiles_per_seq, 0))
    head_spec = pl.BlockSpec((1, N_AUG_HEADS, TILE, LANES),
                             lambda g: (g // tiles_per_seq, 0, g % tiles_per_seq, 0))
    qkv_shape = jax.ShapeDtypeStruct((batch, N_AUG_HEADS, seq, LANES), BF16)

    q_all, k_all, v_all, gz, gates = pl.pallas_call(
        functools.partial(_proj_kernel, tiles_per_seq=tiles_per_seq),
        grid=(n_tiles,),
        in_specs=[
            pl.BlockSpec((TILE, D_MODEL), lambda g: (g, 0)),
            _const_spec((D_MODEL, W_COLS)),
            _const_spec((1, D_MODEL)), _const_spec((1, LANES)), _const_spec((1, 2 * D_MODEL)),
            _const_spec((1, WIDTH)), _const_spec((1, WIDTH)),
            _const_spec((1, WIDTH)), _const_spec((1, WIDTH)),
            rope_spec, rope_spec, rope_spec,
            _const_spec((TILE, TILE)),
            _const_spec((LANES, N_HEADS * LANES)), _const_spec((LANES, N_HEADS * LANES)),
            _const_spec((LANES, N_HEADS * LANES)),
            _const_spec((8, LANES)),
        ],
        out_specs=[
            head_spec, head_spec, head_spec,
            pl.BlockSpec((TILE, 2 * WIDTH), lambda g: (g, 0)),
            pl.BlockSpec((TILE, 2 * D_MODEL), lambda g: (g, 0)),
        ],
        out_shape=[
            qkv_shape, qkv_shape, qkv_shape,
            jax.ShapeDtypeStruct((rows, 2 * WIDTH), BF16),
            jax.ShapeDtypeStruct((rows, 2 * D_MODEL), BF16),
        ],
        scratch_shapes=[pltpu.VMEM((1, LANES), F32),
                        pltpu.VMEM((LANES, WIDTH), F32)],
        compiler_params=pltpu.CompilerParams(
            dimension_semantics=("arbitrary",), vmem_limit_bytes=VMEM_LIMIT),
        name="proj_epilogue",
    )(x2, w, ng, bf, bg, gfq, gfk, gmq, gmk,
      jnp.asarray(rc), jnp.asarray(rsp), jnp.asarray(rsm),
      jnp.asarray(tri, BF16), jnp.asarray(sel_fq, BF16), jnp.asarray(sel_fk, BF16),
      jnp.asarray(sel_mq, BF16), misc)

    n_pairs = N_AUG_HEADS // 2
    att_tiles = seq // ATT_TILE
    kv_spec = pl.BlockSpec((1, 2, seq, LANES), lambda b, p, i, flag: (b, p, 0, 0))
    row_spec = pl.BlockSpec((ATT_TILE, LANES), lambda b, p, i, flag: (b * att_tiles + i, p))
    y = pl.pallas_call(
        _attn_kernel,
        grid_spec=pltpu.PrefetchScalarGridSpec(
            num_scalar_prefetch=1,
            grid=(batch, n_pairs, att_tiles),
            in_specs=[
                pl.BlockSpec((1, 2, ATT_TILE, LANES), lambda b, p, i, flag: (b, p, i, 0)),
                kv_spec, kv_spec, row_spec,
            ],
            out_specs=row_spec,
            scratch_shapes=[pltpu.VMEM((2, ATT_TILE, LANES), F32),
                            pltpu.VMEM((2, ATT_TILE, 1), F32)],
        ),
        out_shape=jax.ShapeDtypeStruct((rows, 2 * WIDTH), BF16),
        compiler_params=pltpu.CompilerParams(
            dimension_semantics=("arbitrary", "arbitrary", "arbitrary"),
            vmem_limit_bytes=VMEM_LIMIT),
        name="flash_attn",
    )(flag, q_all, k_all, v_all, gz)

    out = pl.pallas_call(
        _out_kernel,
        grid=(n_tiles,),
        in_specs=[
            pl.BlockSpec((TILE, 2 * WIDTH), lambda g: (g, 0)),
            pl.BlockSpec((TILE, 2 * D_MODEL), lambda g: (g, 0)),
            pl.BlockSpec((TILE, D_MODEL), lambda g: (g, 0)),
            _const_spec((WIDTH, D_MODEL)), _const_spec((WIDTH, D_MODEL)),
            _const_spec((D_MODEL, D_MODEL)),
        ],
        out_specs=pl.BlockSpec((TILE, D_MODEL), lambda g: (g, 0)),
        out_shape=jax.ShapeDtypeStruct((rows, D_MODEL), F32),
        compiler_params=pltpu.CompilerParams(
            dimension_semantics=("arbitrary",), vmem_limit_bytes=VMEM_LIMIT),
        name="merge_out",
    )(y, gates, x2, w_fox.astype(BF16), w_moba.astype(BF16), w_out.astype(BF16))
    return out


def kernel(x, norm_g, w_in, b_f, b_gate, fox_q_g, fox_k_g, moba_q_g, moba_k_g, w_fox, w_moba, w_out):
    batch, seq, d_model = x.shape
    assert d_model == D_MODEL and seq % ATT_TILE == 0 and seq // TILE <= GROUP
    x2 = x.reshape(batch * seq, D_MODEL)
    for layer in range(norm_g.shape[0]):
        x2 = _layer(x2, batch, seq, norm_g[layer], w_in[layer], b_f[layer], b_gate[layer],
                    fox_q_g[layer], fox_k_g[layer], moba_q_g[layer], moba_k_g[layer],
                    w_fox[layer], w_moba[layer], w_out[layer])
    return x2.reshape(batch, seq, D_MODEL)
```

```python
import functools

import numpy as np
import jax
import jax.numpy as jnp
from jax import lax
from jax.experimental import pallas as pl
from jax.experimental.pallas import tpu as pltpu

D_MODEL = 1024
HEAD_DIM = 64
N_HEADS = 8
WIDTH = N_HEADS * HEAD_DIM
ROPE_DIM = HEAD_DIM // 4
ROPE_HALF = ROPE_DIM // 2
ROPE_THETA = 500000.0
MOBA_BLOCK = 256
MOBA_TOPK = 3
RMS_EPS = 1e-6

LANES = 128
TILE = 256
ATT_TILE = 512
FIXED_STABILIZER_MAX_BOUND = 55.0
EXP2_ZERO_BELOW = -152.0
GROUP = 16
N_AUG_HEADS = 2 * N_HEADS
U_COL_FOX = 6
U_COL_MOBA = GROUP
LOG2E = 1.4426950408889634

C_FQ, C_FK, C_FV, C_FZ = 0, 512, 1024, 1536
C_MQ, C_MK, C_MV, C_MZ = 2048, 2560, 3072, 3584
C_GA, C_GB, C_FL = 4096, 5120, 6144
W_COLS = 6144 + LANES

VMEM_LIMIT = 52 * 1024 * 1024

F32 = jnp.float32
BF16 = jnp.bfloat16


def _dot(a, b):
    return jnp.dot(a, b, preferred_element_type=F32)


def _dot_nt(a, b):
    return lax.dot_general(a, b, (((1,), (1,)), ((), ())), preferred_element_type=F32)


def _split3(v):
    hi = v.astype(BF16).astype(F32)
    r = v - hi
    mid = r.astype(BF16).astype(F32)
    lo = r - mid
    return hi, mid, lo


def _extra_base(h):
    return h * LANES + (HEAD_DIM if h % 2 == 0 else 0)


def _routing_constants():
    sel_fq = np.zeros((LANES, N_HEADS * LANES), np.float32)
    sel_fk = np.zeros((LANES, N_HEADS * LANES), np.float32)
    sel_mq = np.zeros((LANES, N_HEADS * LANES), np.float32)
    for h in range(N_HEADS):
        base = _extra_base(h)
        for part in range(3):
            sel_fq[part * 8 + h, base + part] = 1.0
            sel_fq[24, base + 3 + part] = 1.0
            sel_fk[24, base + part] = 1.0
            sel_fk[part * 8 + h, base + 3 + part] = -1.0
        sel_fq[25, base + U_COL_FOX] = 1.0
        sel_fk[24, base + U_COL_FOX] = 1.0
        for n in range(GROUP):
            sel_mq[h * GROUP + n, base + n] = 1.0
    tri = np.tril(np.ones((TILE, TILE), np.float32))
    return sel_fq, sel_fk, sel_mq, tri


def _rope_tables(seq):
    inv_freq = ROPE_THETA ** (-np.arange(0, ROPE_HALF, dtype=np.float32) * 2.0 / ROPE_DIM)
    ang = np.arange(seq, dtype=np.float32)[:, None] * inv_freq[None, :].astype(np.float32)
    cos, sin = np.cos(ang).astype(np.float32), np.sin(ang).astype(np.float32)
    rc = np.ones((seq, LANES), np.float32)
    rsp = np.zeros((seq, LANES), np.float32)
    rsm = np.zeros((seq, LANES), np.float32)
    for off in (0, HEAD_DIM):
        rc[:, off:off + ROPE_HALF] = cos
        rc[:, off + ROPE_HALF:off + ROPE_DIM] = cos
        rsm[:, off:off + ROPE_HALF] = -sin
        rsp[:, off + ROPE_HALF:off + ROPE_DIM] = sin
    return rc, rsp, rsm


def _proj_kernel(x_ref, w_ref, ng_ref, bf_ref, bg_ref, gfq_ref, gfk_ref, gmq_ref, gmk_ref,
                 rc_ref, rsp_ref, rsm_ref, tri_ref, selfq_ref, selfk_ref, selmq_ref, misc_ref,
                 q_out, k_out, v_out, gz_out, gate_out, cend_out,
                 carry_ref, kmt_ref, *, tiles_per_seq):
    t = pl.program_id(0) % tiles_per_seq

    @pl.when(t == 0)
    def _():
        carry_ref[...] = jnp.zeros_like(carry_ref)
        kmt_ref[...] = jnp.zeros_like(kmt_ref)

    lane = lax.broadcasted_iota(jnp.int32, (TILE, LANES), 1)
    low_half = lane < HEAD_DIM

    x = x_ref[...]
    ms = jnp.mean(x * x, axis=-1, keepdims=True)
    h = (x * lax.rsqrt(ms + RMS_EPS) * ng_ref[...]).astype(BF16)

    def proj(c0, width):
        return _dot(h, w_ref[:, c0:c0 + width])

    def head_norm(a, g_ref):
        tiles = []
        for p in range(WIDTH // LANES):
            ap = a[:, p * LANES:(p + 1) * LANES]
            sq = ap * ap
            s_lo = jnp.sum(jnp.where(low_half, sq, 0.0), axis=-1, keepdims=True)
            s_hi = jnp.sum(jnp.where(low_half, 0.0, sq), axis=-1, keepdims=True)
            inv_lo = lax.rsqrt(s_lo * (1.0 / HEAD_DIM) + RMS_EPS)
            inv_hi = lax.rsqrt(s_hi * (1.0 / HEAD_DIM) + RMS_EPS)
            scale = jnp.where(low_half, inv_lo, inv_hi)
            tiles.append(ap * scale * g_ref[:, p * LANES:(p + 1) * LANES])
        return tiles

    def rope(y):
        return (y * rc_ref[...] + pltpu.roll(y, ROPE_HALF, 1) * rsp_ref[...]
                + pltpu.roll(y, LANES - ROPE_HALF, 1) * rsm_ref[...])

    def store_heads(out_ref, head0, tiles, extras):
        for p, y in enumerate(tiles):
            he, ho = 2 * p, 2 * p + 1
            out_ref[0, head0 + he] = jnp.where(low_half, y, extras(he)).astype(BF16)
            out_ref[0, head0 + ho] = jnp.where(low_half, extras(ho), y).astype(BF16)

    fl = proj(C_FL, LANES) + bf_ref[...]
    logf = jnp.minimum(fl, 0.0) - jnp.log(1.0 + jnp.exp(-jnp.abs(fl)))
    logf = jnp.where(lane < N_HEADS, logf, 0.0)
    l_hi, l_mid, l_lo = _split3(logf)
    packed = (l_hi + pltpu.roll(l_mid, 8, 1) + pltpu.roll(l_lo, 16, 1)).astype(BF16)
    cum = _dot(tri_ref[...], packed)
    c = cum + pltpu.roll(cum, LANES - 8, 1) + pltpu.roll(cum, LANES - 16, 1)
    c = jnp.where(lane < N_HEADS, c, 0.0) + carry_ref[...]
    carry_ref[...] = c[TILE - 1:TILE, :]
    cend_out[0] = jnp.broadcast_to(c[TILE - 1:TILE, :], (8, LANES))
    c_hi, c_mid, c_lo = _split3(c * LOG2E)
    cparts = (c_hi + pltpu.roll(c_mid, 8, 1) + pltpu.roll(c_lo, 16, 1)
              + jnp.where(lane == 24, 1.0, 0.0)
              + jnp.where(lane == 25, misc_ref[1:2, :], 0.0)).astype(BF16)
    ex_fq = _dot(cparts, selfq_ref[...])
    ex_fk = _dot(cparts, selfk_ref[...])

    def ex_slice(ex):
        return lambda hh: ex[:, hh * LANES:(hh + 1) * LANES]

    store_heads(q_out, 0, head_norm(proj(C_FQ, WIDTH), gfq_ref), ex_slice(ex_fq))
    store_heads(k_out, 0, head_norm(proj(C_FK, WIDTH), gfk_ref), ex_slice(ex_fk))

    ones_lo = jnp.where(lane == 0, 1.0, 0.0)
    ones_hi = jnp.where(lane == HEAD_DIM, 1.0, 0.0)

    def v_extras(hh):
        return ones_hi if hh % 2 == 0 else ones_lo

    fv = proj(C_FV, WIDTH)
    store_heads(v_out, 0, [fv[:, p * LANES:(p + 1) * LANES] for p in range(4)], v_extras)

    fz = proj(C_FZ, WIDTH)
    gz_out[:, 0:WIDTH] = (fz * (1.0 / (1.0 + jnp.exp(-fz)))).astype(BF16)

    mk_tiles = [rope(y) for y in head_norm(proj(C_MK, WIDTH), gmk_ref)]
    row_id = lax.broadcasted_iota(jnp.int32, (LANES, LANES), 0)
    lane_sq = lax.broadcasted_iota(jnp.int32, (LANES, LANES), 1)
    for p, kr in enumerate(mk_tiles):
        km = jnp.mean(kr, axis=0, keepdims=True)
        hit = (((row_id == (2 * p) * GROUP + t) & (lane_sq < HEAD_DIM))
               | ((row_id == (2 * p + 1) * GROUP + t) & (lane_sq >= HEAD_DIM)))
        blk = kmt_ref[:, p * LANES:(p + 1) * LANES]
        kmt_ref[:, p * LANES:(p + 1) * LANES] = jnp.where(hit, km, blk)

    def mk_extras(hh):
        off = HEAD_DIM if hh % 2 == 0 else 0
        return jnp.where((lane == off + t) | (lane == off + U_COL_MOBA), 1.0, 0.0)

    store_heads(k_out, N_HEADS, mk_tiles, mk_extras)

    mq_tiles = [rope(y) for y in head_norm(proj(C_MQ, WIDTH), gmq_ref)]
    gate = jnp.zeros((TILE, LANES), F32)
    for p, qr in enumerate(mq_tiles):
        kmt = kmt_ref[:, p * LANES:(p + 1) * LANES]
        q_hi = qr.astype(BF16)
        q_lo = (qr - q_hi.astype(F32)).astype(BF16)
        k_hi = kmt.astype(BF16)
        k_lo = (kmt - k_hi.astype(F32)).astype(BF16)
        gate = gate + _dot_nt(q_hi, k_hi) + _dot_nt(q_hi, k_lo) + _dot_nt(q_lo, k_hi)

    blk_id = lane % GROUP
    past = blk_id < t
    g = jnp.where(past, gate, -jnp.inf)
    beaten = jnp.zeros((TILE, LANES), jnp.int32)
    for d in range(1, GROUP):
        lower = pltpu.roll(g, d, 1)
        beaten = beaten + jnp.where((blk_id >= d) & (lower >= g), 1, 0)
        upper = pltpu.roll(g, LANES - d, 1)
        beaten = beaten + jnp.where((blk_id < GROUP - d) & (upper > g), 1, 0)
    keep = (past & (beaten < MOBA_TOPK)) | (blk_id == t)
    maskvals = jnp.where(keep, 0.0, -misc_ref[0:1, :]).astype(BF16)
    ex_mq = _dot(maskvals, selmq_ref[...])

    def mq_extras(hh):
        off = HEAD_DIM if hh % 2 == 0 else 0
        return jnp.where(lane == off + U_COL_MOBA, misc_ref[2:3, :],
                         ex_mq[:, hh * LANES:(hh + 1) * LANES])

    store_heads(q_out, N_HEADS, mq_tiles, mq_extras)

    mv = proj(C_MV, WIDTH)
    store_heads(v_out, N_HEADS, [mv[:, p * LANES:(p + 1) * LANES] for p in range(4)], v_extras)

    mz = proj(C_MZ, WIDTH)
    gz_out[:, WIDTH:2 * WIDTH] = (mz * (1.0 / (1.0 + jnp.exp(-mz)))).astype(BF16)

    for c0, o0 in ((C_GA, 0), (C_GB, D_MODEL)):
        for half in range(2):
            lo = half * WIDTH
            ga = proj(c0 + lo, WIDTH) + bg_ref[:, o0 + lo:o0 + lo + WIDTH]
            gate_out[:, o0 + lo:o0 + lo + WIDTH] = (1.0 / (1.0 + jnp.exp(-ga))).astype(BF16)


def _attn_kernel(flag_ref, cend_ref, q_ref, k_ref, v_ref, gz_ref, o_ref, acc_ref, m_ref, *,
                 tiles_per_seq):
    b, pair, i = pl.program_id(0), pl.program_id(1), pl.program_id(2)
    half = ATT_TILE // 2
    start = pl.multiple_of(i * ATT_TILE, ATT_TILE)

    def first_live_block(hh):
        base = (b * N_AUG_HEADS + 2 * pair + hh) * tiles_per_seq
        sub = ATT_TILE // TILE
        c_q = cend_ref[base + jnp.maximum(sub * i - 1, 0)]
        dead = jnp.int32(0)
        for j in range(tiles_per_seq // sub - 1):
            c_k = cend_ref[base + sub * j + sub - 1]
            dead += ((j < i) & (c_q - c_k < EXP2_ZERO_BELOW)).astype(jnp.int32)
        return dead
    row = lax.broadcasted_iota(jnp.int32, (half, ATT_TILE), 0)
    col = lax.broadcasted_iota(jnp.int32, (half, ATT_TILE), 1)
    mask_top = (lax.broadcasted_iota(jnp.int32, (half, half), 1)
                <= lax.broadcasted_iota(jnp.int32, (half, half), 0))
    mask_bot = col <= row + half

    def diag_scores(hh):
        kd = k_ref[0, hh, pl.ds(start, ATT_TILE), :]
        s_top = jnp.where(mask_top, _dot_nt(q_ref[0, hh, 0:half, :], kd[0:half]), -jnp.inf)
        s_bot = jnp.where(mask_bot, _dot_nt(q_ref[0, hh, half:ATT_TILE, :], kd), -jnp.inf)
        return s_top, s_bot

    @pl.when(flag_ref[0] == 1)
    def _fixed_stabilizer():
        for hh in range(2):
            vd = v_ref[0, hh, pl.ds(start, ATT_TILE), :]
            s_top, s_bot = diag_scores(hh)
            acc_ref[hh, 0:half] = _dot(jnp.exp2(s_top).astype(BF16), vd[0:half])
            acc_ref[hh, half:ATT_TILE] = _dot(jnp.exp2(s_bot).astype(BF16), vd)

        def body(j, carry):
            off = pl.multiple_of(j * ATT_TILE, ATT_TILE)
            for hh in range(2):
                s = _dot_nt(q_ref[0, hh], k_ref[0, hh, pl.ds(off, ATT_TILE), :])
                acc_ref[hh] += _dot(jnp.exp2(s).astype(BF16),
                                    v_ref[0, hh, pl.ds(off, ATT_TILE), :])
            return carry

        lax.fori_loop(jnp.minimum(first_live_block(0), first_live_block(1)), i, body, 0)

    @pl.when(flag_ref[0] != 1)
    def _online():
        for hh in range(2):
            vd = v_ref[0, hh, pl.ds(start, ATT_TILE), :]
            s_top, s_bot = diag_scores(hh)
            m_top = jnp.max(s_top, axis=-1, keepdims=True)
            m_bot = jnp.max(s_bot, axis=-1, keepdims=True)
            m_ref[hh, 0:half] = m_top
            m_ref[hh, half:ATT_TILE] = m_bot
            acc_ref[hh, 0:half] = _dot(jnp.exp2(s_top - m_top).astype(BF16), vd[0:half])
            acc_ref[hh, half:ATT_TILE] = _dot(jnp.exp2(s_bot - m_bot).astype(BF16), vd)

        def body(j, carry):
            off = pl.multiple_of(j * ATT_TILE, ATT_TILE)
            for hh in range(2):
                s = _dot_nt(q_ref[0, hh], k_ref[0, hh, pl.ds(off, ATT_TILE), :])
                m_old = m_ref[hh]
                m_new = jnp.maximum(m_old, jnp.max(s, axis=-1, keepdims=True))
                pv = _dot(jnp.exp2(s - m_new).astype(BF16), v_ref[0, hh, pl.ds(off, ATT_TILE), :])
                acc_ref[hh] = jnp.exp2(m_old - m_new) * acc_ref[hh] + pv
                m_ref[hh] = m_new
            return carry

        lax.fori_loop(0, i, body, 0)

    lane = lax.broadcasted_iota(jnp.int32, (ATT_TILE, LANES), 1)
    acc_e, acc_o = acc_ref[0], acc_ref[1]
    o_e = acc_e * (1.0 / acc_e[:, HEAD_DIM:HEAD_DIM + 1])
    o_o = acc_o * (1.0 / acc_o[:, 0:1])
    o = jnp.where(lane < HEAD_DIM, o_e, o_o)
    o_ref[...] = (o * gz_ref[...].astype(F32)).astype(BF16)


def _out_kernel(y_ref, gate_ref, x_ref, wf_ref, wm_ref, wo_ref, o_ref):
    pf = _dot(y_ref[:, 0:WIDTH], wf_ref[...])
    pm = _dot(y_ref[:, WIDTH:2 * WIDTH], wm_ref[...])
    merged = (gate_ref[:, 0:D_MODEL].astype(F32) * pf
              + gate_ref[:, D_MODEL:2 * D_MODEL].astype(F32) * pm)
    o_ref[...] = x_ref[...] + _dot(merged.astype(BF16), wo_ref[...])


def _const_spec(shape):
    return pl.BlockSpec(shape, lambda *_: (0,) * len(shape))


def _layer(x2, batch, seq, norm_g, w_in, b_f, b_gate, fox_q_g, fox_k_g, moba_q_g, moba_k_g,
           w_fox, w_moba, w_out):
    rows = batch * seq
    tiles_per_seq = seq // TILE
    n_tiles = rows // TILE
    scale = HEAD_DIM ** -0.5

    w_fl = jnp.pad(w_in[:, 6144:], ((0, 0), (0, LANES - N_HEADS)))
    w = jnp.concatenate([w_in[:, :6144], w_fl], axis=1).astype(BF16)
    ng = norm_g.reshape(1, D_MODEL)
    bf = jnp.pad(b_f, (0, LANES - N_HEADS)).reshape(1, LANES)
    bg = b_gate.reshape(1, 2 * D_MODEL)
    gfq = jnp.tile(fox_q_g * (scale * LOG2E), N_HEADS).reshape(1, WIDTH)
    gfk = jnp.tile(fox_k_g, N_HEADS).reshape(1, WIDTH)
    gmq = jnp.tile(moba_q_g * (scale * LOG2E), N_HEADS).reshape(1, WIDTH)
    gmk = jnp.tile(moba_k_g, N_HEADS).reshape(1, WIDTH)

    def logit_bound(gq, gk):
        return 8.0 * LOG2E * 1.02 * jnp.max(jnp.abs(gq)) * jnp.max(jnp.abs(gk))

    b_fox, b_moba = logit_bound(fox_q_g, fox_k_g), logit_bound(moba_q_g, moba_k_g)
    fixed_ok = jnp.maximum(b_fox, b_moba) <= FIXED_STABILIZER_MAX_BOUND

    def stabilizer(b):
        u = -(b * (1.0 + 2.0 ** -7)).astype(BF16).astype(F32)
        return jnp.where(fixed_ok, u, 0.0)

    big = jnp.exp2(jnp.ceil(jnp.log2(2.1 * b_moba + 160.0)))
    misc = jnp.zeros((8, LANES), F32)
    misc = misc.at[0].set(big).at[1].set(stabilizer(b_fox)).at[2].set(stabilizer(b_moba))
    flag = fixed_ok.astype(jnp.int32).reshape(1)

    sel_fq, sel_fk, sel_mq, tri = _routing_constants()
    rc, rsp, rsm = _rope_tables(seq)

    rope_spec = pl.BlockSpec((TILE, LANES), lambda g: (g % tiles_per_seq, 0))
    head_spec = pl.BlockSpec((1, N_AUG_HEADS, TILE, LANES),
                             lambda g: (g // tiles_per_seq, 0, g % tiles_per_seq, 0))
    qkv_shape = jax.ShapeDtypeStruct((batch, N_AUG_HEADS, seq, LANES), BF16)

    q_all, k_all, v_all, gz, gates, cend = pl.pallas_call(
        functools.partial(_proj_kernel, tiles_per_seq=tiles_per_seq),
        grid=(n_tiles,),
        in_specs=[
            pl.BlockSpec((TILE, D_MODEL), lambda g: (g, 0)),
            _const_spec((D_MODEL, W_COLS)),
            _const_spec((1, D_MODEL)), _const_spec((1, LANES)), _const_spec((1, 2 * D_MODEL)),
            _const_spec((1, WIDTH)), _const_spec((1, WIDTH)),
            _const_spec((1, WIDTH)), _const_spec((1, WIDTH)),
            rope_spec, rope_spec, rope_spec,
            _const_spec((TILE, TILE)),
            _const_spec((LANES, N_HEADS * LANES)), _const_spec((LANES, N_HEADS * LANES)),
            _const_spec((LANES, N_HEADS * LANES)),
            _const_spec((8, LANES)),
        ],
        out_specs=[
            head_spec, head_spec, head_spec,
            pl.BlockSpec((TILE, 2 * WIDTH), lambda g: (g, 0)),
            pl.BlockSpec((TILE, 2 * D_MODEL), lambda g: (g, 0)),
            pl.BlockSpec((1, 8, LANES), lambda g: (g, 0, 0)),
        ],
        out_shape=[
            qkv_shape, qkv_shape, qkv_shape,
            jax.ShapeDtypeStruct((rows, 2 * WIDTH), BF16),
            jax.ShapeDtypeStruct((rows, 2 * D_MODEL), BF16),
            jax.ShapeDtypeStruct((n_tiles, 8, LANES), F32),
        ],
        scratch_shapes=[pltpu.VMEM((1, LANES), F32),
                        pltpu.VMEM((LANES, WIDTH), F32)],
        compiler_params=pltpu.CompilerParams(
            dimension_semantics=("arbitrary",), vmem_limit_bytes=VMEM_LIMIT),
        name="proj_epilogue",
    )(x2, w, ng, bf, bg, gfq, gfk, gmq, gmk,
      jnp.asarray(rc), jnp.asarray(rsp), jnp.asarray(rsm),
      jnp.asarray(tri, BF16), jnp.asarray(sel_fq, BF16), jnp.asarray(sel_fk, BF16),
      jnp.asarray(sel_mq, BF16), misc)

    n_pairs = N_AUG_HEADS // 2
    att_tiles = seq // ATT_TILE
    cend = cend[:, 0, :N_HEADS].reshape(batch, tiles_per_seq, N_HEADS).transpose(0, 2, 1) * LOG2E
    cend = jnp.concatenate([cend, jnp.zeros_like(cend)], axis=1).reshape(-1)
    kv_spec = pl.BlockSpec((1, 2, seq, LANES), lambda b, p, i, *_: (b, p, 0, 0))
    row_spec = pl.BlockSpec((ATT_TILE, LANES), lambda b, p, i, *_: (b * att_tiles + i, p))
    y = pl.pallas_call(
        functools.partial(_attn_kernel, tiles_per_seq=tiles_per_seq),
        grid_spec=pltpu.PrefetchScalarGridSpec(
            num_scalar_prefetch=2,
            grid=(batch, n_pairs, att_tiles),
            in_specs=[
                pl.BlockSpec((1, 2, ATT_TILE, LANES), lambda b, p, i, *_: (b, p, i, 0)),
                kv_spec, kv_spec, row_spec,
            ],
            out_specs=row_spec,
            scratch_shapes=[pltpu.VMEM((2, ATT_TILE, LANES), F32),
                            pltpu.VMEM((2, ATT_TILE, 1), F32)],
        ),
        out_shape=jax.ShapeDtypeStruct((rows, 2 * WIDTH), BF16),
        compiler_params=pltpu.CompilerParams(
            dimension_semantics=("arbitrary", "arbitrary", "arbitrary"),
            vmem_limit_bytes=VMEM_LIMIT),
        name="flash_attn",
    )(flag, cend, q_all, k_all, v_all, gz)

    out = pl.pallas_call(
        _out_kernel,
        grid=(n_tiles,),
        in_specs=[
            pl.BlockSpec((TILE, 2 * WIDTH), lambda g: (g, 0)),
            pl.BlockSpec((TILE, 2 * D_MODEL), lambda g: (g, 0)),
            pl.BlockSpec((TILE, D_MODEL), lambda g: (g, 0)),
            _const_spec((WIDTH, D_MODEL)), _const_spec((WIDTH, D_MODEL)),
            _const_spec((D_MODEL, D_MODEL)),
        ],
        out_specs=pl.BlockSpec((TILE, D_MODEL), lambda g: (g, 0)),
        out_shape=jax.ShapeDtypeStruct((rows, D_MODEL), F32),
        compiler_params=pltpu.CompilerParams(
            dimension_semantics=("arbitrary",), vmem_limit_bytes=VMEM_LIMIT),
        name="merge_out",
    )(y, gates, x2, w_fox.astype(BF16), w_moba.astype(BF16), w_out.astype(BF16))
    return out


def kernel(x, norm_g, w_in, b_f, b_gate, fox_q_g, fox_k_g, moba_q_g, moba_k_g, w_fox, w_moba, w_out):
    batch, seq, d_model = x.shape
    assert d_model == D_MODEL and seq % ATT_TILE == 0 and seq // TILE <= GROUP
    x2 = x.reshape(batch * seq, D_MODEL)
    for layer in range(norm_g.shape[0]):
        x2 = _layer(x2, batch, seq, norm_g[layer], w_in[layer], b_f[layer], b_gate[layer],
                    fox_q_g[layer], fox_k_g[layer], moba_q_g[layer], moba_k_g[layer],
                    w_fox[layer], w_moba[layer], w_out[layer])
    return x2.reshape(batch, seq, D_MODEL)
```

```python
import functools

import numpy as np
import jax
import jax.numpy as jnp
from jax import lax
from jax.experimental import pallas as pl
from jax.experimental.pallas import tpu as pltpu

D_MODEL = 1024
HEAD_DIM = 64
N_HEADS = 8
WIDTH = N_HEADS * HEAD_DIM
ROPE_DIM = HEAD_DIM // 4
ROPE_HALF = ROPE_DIM // 2
ROPE_THETA = 500000.0
MOBA_BLOCK = 256
MOBA_TOPK = 3
RMS_EPS = 1e-6

LANES = 128
MXU_COLS = 256
TILE = 256
ATT_TILE = 512
FIXED_STABILIZER_MAX_BOUND = 55.0
EXP2_ZERO_BELOW = -152.0
GROUP = 16
N_AUG_HEADS = 2 * N_HEADS
U_COL_FOX = 6
U_COL_MOBA = GROUP
LOG2E = 1.4426950408889634

C_FQ, C_FK, C_FV, C_FZ = 0, 512, 1024, 1536
C_MQ, C_MK, C_MV, C_MZ = 2048, 2560, 3072, 3584
C_GA, C_GB, C_FL = 4096, 5120, 6144
W_COLS = 6144 + LANES

VMEM_LIMIT = 52 * 1024 * 1024

F32 = jnp.float32
BF16 = jnp.bfloat16


def _dot(a, b):
    return jnp.dot(a, b, preferred_element_type=F32)


def _dot_nt(a, b):
    return lax.dot_general(a, b, (((1,), (1,)), ((), ())), preferred_element_type=F32)


def _split3(v):
    hi = v.astype(BF16).astype(F32)
    r = v - hi
    mid = r.astype(BF16).astype(F32)
    lo = r - mid
    return hi, mid, lo


def _extra_base(h):
    return (h // 2) * LANES + (HEAD_DIM if h % 2 == 0 else 0)


def _routing_constants():
    sel_fq = np.zeros((LANES, WIDTH), np.float32)
    sel_fk = np.zeros((LANES, WIDTH), np.float32)
    sel_mq = np.zeros((LANES, WIDTH), np.float32)
    for h in range(N_HEADS):
        base = _extra_base(h)
        for part in range(3):
            sel_fq[part * 8 + h, base + part] = 1.0
            sel_fq[24, base + 3 + part] = 1.0
            sel_fk[24, base + part] = 1.0
            sel_fk[part * 8 + h, base + 3 + part] = -1.0
        sel_fq[25, base + U_COL_FOX] = 1.0
        sel_fk[24, base + U_COL_FOX] = 1.0
        for n in range(GROUP):
            sel_mq[h * GROUP + n, base + n] = 1.0
    tri = np.tril(np.ones((TILE, TILE), np.float32))
    return sel_fq, sel_fk, sel_mq, tri


def _rope_tables(seq):
    inv_freq = ROPE_THETA ** (-np.arange(0, ROPE_HALF, dtype=np.float32) * 2.0 / ROPE_DIM)
    ang = np.arange(seq, dtype=np.float32)[:, None] * inv_freq[None, :].astype(np.float32)
    cos, sin = np.cos(ang).astype(np.float32), np.sin(ang).astype(np.float32)
    rc = np.ones((seq, LANES), np.float32)
    rsp = np.zeros((seq, LANES), np.float32)
    rsm = np.zeros((seq, LANES), np.float32)
    for off in (0, HEAD_DIM):
        rc[:, off:off + ROPE_HALF] = cos
        rc[:, off + ROPE_HALF:off + ROPE_DIM] = cos
        rsm[:, off:off + ROPE_HALF] = -sin
        rsp[:, off + ROPE_HALF:off + ROPE_DIM] = sin
    return rc, rsp, rsm


def _proj_kernel(x_ref, w_ref, ng_ref, bf_ref, bg_ref, gfq_ref, gfk_ref, gmq_ref, gmk_ref,
                 rc_ref, rsp_ref, rsm_ref, tri_ref, selfq_ref, selfk_ref, selmq_ref, misc_ref,
                 q_out, k_out, v_out, gz_out, gate_out, cend_out,
                 carry_ref, kmt_ref, h_ref, *, tiles_per_seq):
    t = pl.program_id(0) % tiles_per_seq

    @pl.when(t == 0)
    def _():
        carry_ref[...] = jnp.zeros_like(carry_ref)
        kmt_ref[...] = jnp.zeros_like(kmt_ref)

    lane = lax.broadcasted_iota(jnp.int32, (TILE, LANES), 1)
    low_half = lane < HEAD_DIM

    x = x_ref[...]
    ms = jnp.mean(x * x, axis=-1, keepdims=True)
    h_ref[...] = (x * lax.rsqrt(ms + RMS_EPS) * ng_ref[...]).astype(BF16)
    zero = pl.multiple_of(jnp.minimum(pl.program_id(0), 0) * TILE, TILE)

    def proj(c0, width):
        slabs = [_dot(h_ref[pl.ds(zero, TILE), :], w_ref[:, c:c + min(MXU_COLS, c0 + width - c)])
                 for c in range(c0, c0 + width, MXU_COLS)]
        return slabs[0] if len(slabs) == 1 else jnp.concatenate(slabs, axis=1)

    def head_norm(a, g_ref):
        tiles = []
        for p in range(WIDTH // LANES):
            ap = a[:, p * LANES:(p + 1) * LANES]
            sq = ap * ap
            s_lo = jnp.sum(jnp.where(low_half, sq, 0.0), axis=-1, keepdims=True)
            s_hi = jnp.sum(jnp.where(low_half, 0.0, sq), axis=-1, keepdims=True)
            inv_lo = lax.rsqrt(s_lo * (1.0 / HEAD_DIM) + RMS_EPS)
            inv_hi = lax.rsqrt(s_hi * (1.0 / HEAD_DIM) + RMS_EPS)
            scale = jnp.where(low_half, inv_lo, inv_hi)
            tiles.append(ap * scale * g_ref[:, p * LANES:(p + 1) * LANES])
        return tiles

    def rope(y):
        return (y * rc_ref[...] + pltpu.roll(y, ROPE_HALF, 1) * rsp_ref[...]
                + pltpu.roll(y, LANES - ROPE_HALF, 1) * rsm_ref[...])

    def split_tiles(a):
        return [a[:, p * LANES:(p + 1) * LANES] for p in range(WIDTH // LANES)]

    def store_heads(out_ref, head0, tiles, extras):
        for p, y in enumerate(tiles):
            e = extras(p)
            out_ref[0, head0 + 2 * p] = jnp.where(low_half, y, e).astype(BF16)
            out_ref[0, head0 + 2 * p + 1] = jnp.where(low_half, e, y).astype(BF16)

    def silu(z):
        hz = 0.5 * z
        return hz + hz * jnp.tanh(hz)

    def sigmoid(z):
        return 0.5 * jnp.tanh(0.5 * z) + 0.5

    fl = proj(C_FL, LANES) + bf_ref[...]
    mk_raw = proj(C_MK, WIDTH)
    mq_raw = proj(C_MQ, WIDTH)
    fv = proj(C_FV, WIDTH)
    fz = proj(C_FZ, WIDTH)

    logf = jnp.minimum(fl, 0.0) - jnp.log(1.0 + jnp.exp(-jnp.abs(fl)))
    logf = jnp.where(lane < N_HEADS, logf, 0.0)
    l_hi, l_mid, l_lo = _split3(logf)
    packed = (l_hi + pltpu.roll(l_mid, 8, 1) + pltpu.roll(l_lo, 16, 1)).astype(BF16)
    cum = _dot(tri_ref[...], packed)
    fq = proj(C_FQ, WIDTH)
    fk = proj(C_FK, WIDTH)

    mk_tiles = [rope(y) for y in head_norm(mk_raw, gmk_ref)]
    row_id = lax.broadcasted_iota(jnp.int32, (LANES, LANES), 0)
    lane_sq = lax.broadcasted_iota(jnp.int32, (LANES, LANES), 1)
    for p, kr in enumerate(mk_tiles):
        km = jnp.mean(kr, axis=0, keepdims=True)
        hit = (((row_id == (2 * p) * GROUP + t) & (lane_sq < HEAD_DIM))
               | ((row_id == (2 * p + 1) * GROUP + t) & (lane_sq >= HEAD_DIM)))
        blk = kmt_ref[:, p * LANES:(p + 1) * LANES]
        kmt_ref[:, p * LANES:(p + 1) * LANES] = jnp.where(hit, km, blk)
    mk_ones = jnp.where((lane % HEAD_DIM == t) | (lane % HEAD_DIM == U_COL_MOBA), 1.0, 0.0)
    store_heads(k_out, N_HEADS, mk_tiles, lambda p: mk_ones)

    c = cum + pltpu.roll(cum, LANES - 8, 1) + pltpu.roll(cum, LANES - 16, 1)
    c = jnp.where(lane < N_HEADS, c, 0.0) + carry_ref[...]
    carry_ref[...] = c[TILE - 1:TILE, :]
    cend_out[0] = jnp.broadcast_to(c[TILE - 1:TILE, :], (8, LANES))
    c_hi, c_mid, c_lo = _split3(c * LOG2E)
    cparts = (c_hi + pltpu.roll(c_mid, 8, 1) + pltpu.roll(c_lo, 16, 1)
              + jnp.where(lane == 24, 1.0, 0.0)
              + jnp.where(lane == 25, misc_ref[1:2, :], 0.0)).astype(BF16)
    ex_fq = _dot(cparts, selfq_ref[...])
    ex_fk = _dot(cparts, selfk_ref[...])
    mv = proj(C_MV, WIDTH)
    mz = proj(C_MZ, WIDTH)

    v_ones = jnp.where(lane % HEAD_DIM == 0, 1.0, 0.0)
    store_heads(v_out, 0, split_tiles(fv), lambda p: v_ones)
    gz_out[:, 0:WIDTH] = silu(fz).astype(BF16)

    mq_tiles = [rope(y) for y in head_norm(mq_raw, gmq_ref)]
    q_full = jnp.concatenate(mq_tiles, axis=1)
    q_hi = q_full.astype(BF16)
    q_lo = (q_full - q_hi.astype(F32)).astype(BF16)
    kmt = kmt_ref[...]
    k_hi = kmt.astype(BF16)
    k_lo = (kmt - k_hi.astype(F32)).astype(BF16)
    gate = _dot_nt(q_hi, k_hi) + _dot_nt(q_hi, k_lo) + _dot_nt(q_lo, k_hi)
    gate_chunks = [(C_GA + half * WIDTH, half * WIDTH) for half in range(2)]
    gate_chunks += [(C_GB + half * WIDTH, D_MODEL + half * WIDTH) for half in range(2)]
    gate_raw = [proj(c0, WIDTH) for c0, _ in gate_chunks]

    store_heads(q_out, 0, head_norm(fq, gfq_ref), lambda p: ex_fq[:, p * LANES:(p + 1) * LANES])
    store_heads(k_out, 0, head_norm(fk, gfk_ref), lambda p: ex_fk[:, p * LANES:(p + 1) * LANES])

    blk_id = lane % GROUP
    past = blk_id < t
    g = jnp.where(past, gate, -jnp.inf)
    beaten = jnp.zeros((TILE, LANES), jnp.int32)
    for d in range(1, GROUP):
        lower = jnp.where(blk_id >= d, pltpu.roll(g, d, 1), -jnp.inf)
        upper = jnp.where(blk_id < GROUP - d, pltpu.roll(g, LANES - d, 1), -jnp.inf)
        beaten = beaten + jnp.where(lower >= g, 1, 0) + jnp.where(upper > g, 1, 0)
    keep = (past & (beaten < MOBA_TOPK)) | (blk_id == t)
    maskvals = jnp.where(keep, 0.0, -misc_ref[0:1, :]).astype(BF16)
    ex_mq = _dot(maskvals, selmq_ref[...])

    store_heads(v_out, N_HEADS, split_tiles(mv), lambda p: v_ones)
    gz_out[:, WIDTH:2 * WIDTH] = silu(mz).astype(BF16)

    is_u_lane = lane % HEAD_DIM == U_COL_MOBA
    store_heads(q_out, N_HEADS, mq_tiles,
                lambda p: jnp.where(is_u_lane, misc_ref[2:3, :], ex_mq[:, p * LANES:(p + 1) * LANES]))
    for raw, (_, o0) in zip(gate_raw, gate_chunks):
        gate_out[:, o0:o0 + WIDTH] = sigmoid(raw + bg_ref[:, o0:o0 + WIDTH]).astype(BF16)


def _attn_kernel(flag_ref, cend_ref, q_ref, k_ref, v_ref, gz_ref, o_ref, acc_ref, m_ref, *,
                 tiles_per_seq):
    b, pair, i = pl.program_id(0), pl.program_id(1), pl.program_id(2)
    half = ATT_TILE // 2
    start = pl.multiple_of(i * ATT_TILE, ATT_TILE)

    def first_live_block(hh):
        base = (b * N_AUG_HEADS + 2 * pair + hh) * tiles_per_seq
        sub = ATT_TILE // TILE
        c_q = cend_ref[base + jnp.maximum(sub * i - 1, 0)]
        dead = jnp.int32(0)
        for j in range(tiles_per_seq // sub - 1):
            c_k = cend_ref[base + sub * j + sub - 1]
            dead += ((j < i) & (c_q - c_k < EXP2_ZERO_BELOW)).astype(jnp.int32)
        return dead
    row = lax.broadcasted_iota(jnp.int32, (half, ATT_TILE), 0)
    col = lax.broadcasted_iota(jnp.int32, (half, ATT_TILE), 1)
    mask_top = (lax.broadcasted_iota(jnp.int32, (half, half), 1)
                <= lax.broadcasted_iota(jnp.int32, (half, half), 0))
    mask_bot = col <= row + half

    def diag_scores(hh):
        kd = k_ref[0, hh, pl.ds(start, ATT_TILE), :]
        s_top = jnp.where(mask_top, _dot_nt(q_ref[0, hh, 0:half, :], kd[0:half]), -jnp.inf)
        s_bot = jnp.where(mask_bot, _dot_nt(q_ref[0, hh, half:ATT_TILE, :], kd), -jnp.inf)
        return s_top, s_bot

    @pl.when(flag_ref[0] == 1)
    def _fixed_stabilizer():
        for hh in range(2):
            vd = v_ref[0, hh, pl.ds(start, ATT_TILE), :]
            s_top, s_bot = diag_scores(hh)
            acc_ref[hh, 0:half] = _dot(jnp.exp2(s_top).astype(BF16), vd[0:half])
            acc_ref[hh, half:ATT_TILE] = _dot(jnp.exp2(s_bot).astype(BF16), vd)

        def body(j, carry):
            off = pl.multiple_of(j * ATT_TILE, ATT_TILE)
            for hh in range(2):
                s = _dot_nt(q_ref[0, hh], k_ref[0, hh, pl.ds(off, ATT_TILE), :])
                acc_ref[hh] += _dot(jnp.exp2(s).astype(BF16),
                                    v_ref[0, hh, pl.ds(off, ATT_TILE), :])
            return carry

        lax.fori_loop(jnp.minimum(first_live_block(0), first_live_block(1)), i, body, 0)

    @pl.when(flag_ref[0] != 1)
    def _online():
        for hh in range(2):
            vd = v_ref[0, hh, pl.ds(start, ATT_TILE), :]
            s_top, s_bot = diag_scores(hh)
            m_top = jnp.max(s_top, axis=-1, keepdims=True)
            m_bot = jnp.max(s_bot, axis=-1, keepdims=True)
            m_ref[hh, 0:half] = m_top
            m_ref[hh, half:ATT_TILE] = m_bot
            acc_ref[hh, 0:half] = _dot(jnp.exp2(s_top - m_top).astype(BF16), vd[0:half])
            acc_ref[hh, half:ATT_TILE] = _dot(jnp.exp2(s_bot - m_bot).astype(BF16), vd)

        def body(j, carry):
            off = pl.multiple_of(j * ATT_TILE, ATT_TILE)
            for hh in range(2):
                s = _dot_nt(q_ref[0, hh], k_ref[0, hh, pl.ds(off, ATT_TILE), :])
                m_old = m_ref[hh]
                m_new = jnp.maximum(m_old, jnp.max(s, axis=-1, keepdims=True))
                pv = _dot(jnp.exp2(s - m_new).astype(BF16), v_ref[0, hh, pl.ds(off, ATT_TILE), :])
                acc_ref[hh] = jnp.exp2(m_old - m_new) * acc_ref[hh] + pv
                m_ref[hh] = m_new
            return carry

        lax.fori_loop(0, i, body, 0)

    lane = lax.broadcasted_iota(jnp.int32, (ATT_TILE, LANES), 1)
    acc_e, acc_o = acc_ref[0], acc_ref[1]
    o_e = acc_e * (1.0 / acc_e[:, HEAD_DIM:HEAD_DIM + 1])
    o_o = acc_o * (1.0 / acc_o[:, 0:1])
    o = jnp.where(lane < HEAD_DIM, o_e, o_o)
    o_ref[...] = (o * gz_ref[...].astype(F32)).astype(BF16)


def _out_kernel(y_ref, gate_ref, x_ref, wf_ref, wm_ref, wo_ref, o_ref):
    pf = _dot(y_ref[:, 0:WIDTH], wf_ref[...])
    pm = _dot(y_ref[:, WIDTH:2 * WIDTH], wm_ref[...])
    merged = (gate_ref[:, 0:D_MODEL].astype(F32) * pf
              + gate_ref[:, D_MODEL:2 * D_MODEL].astype(F32) * pm)
    o_ref[...] = x_ref[...] + _dot(merged.astype(BF16), wo_ref[...])


def _const_spec(shape):
    return pl.BlockSpec(shape, lambda *_: (0,) * len(shape))


def _layer(x2, batch, seq, norm_g, w_in, b_f, b_gate, fox_q_g, fox_k_g, moba_q_g, moba_k_g,
           w_fox, w_moba, w_out):
    rows = batch * seq
    tiles_per_seq = seq // TILE
    n_tiles = rows // TILE
    scale = HEAD_DIM ** -0.5

    w_fl = jnp.pad(w_in[:, 6144:], ((0, 0), (0, LANES - N_HEADS)))
    w = jnp.concatenate([w_in[:, :6144], w_fl], axis=1).astype(BF16)
    ng = norm_g.reshape(1, D_MODEL)
    bf = jnp.pad(b_f, (0, LANES - N_HEADS)).reshape(1, LANES)
    bg = b_gate.reshape(1, 2 * D_MODEL)
    gfq = jnp.tile(fox_q_g * (scale * LOG2E), N_HEADS).reshape(1, WIDTH)
    gfk = jnp.tile(fox_k_g, N_HEADS).reshape(1, WIDTH)
    gmq = jnp.tile(moba_q_g * (scale * LOG2E), N_HEADS).reshape(1, WIDTH)
    gmk = jnp.tile(moba_k_g, N_HEADS).reshape(1, WIDTH)

    def logit_bound(gq, gk):
        return 8.0 * LOG2E * 1.02 * jnp.max(jnp.abs(gq)) * jnp.max(jnp.abs(gk))

    b_fox, b_moba = logit_bound(fox_q_g, fox_k_g), logit_bound(moba_q_g, moba_k_g)
    fixed_ok = jnp.maximum(b_fox, b_moba) <= FIXED_STABILIZER_MAX_BOUND

    def stabilizer(b):
        u = -(b * (1.0 + 2.0 ** -7)).astype(BF16).astype(F32)
        return jnp.where(fixed_ok, u, 0.0)

    big = jnp.exp2(jnp.ceil(jnp.log2(2.1 * b_moba + 160.0)))
    misc = jnp.zeros((8, LANES), F32)
    misc = misc.at[0].set(big).at[1].set(stabilizer(b_fox)).at[2].set(stabilizer(b_moba))
    flag = fixed_ok.astype(jnp.int32).reshape(1)

    sel_fq, sel_fk, sel_mq, tri = _routing_constants()
    rc, rsp, rsm = _rope_tables(seq)

    rope_spec = pl.BlockSpec((TILE, LANES), lambda g: (g % tiles_per_seq, 0))
    head_spec = pl.BlockSpec((1, N_AUG_HEADS, TILE, LANES),
                             lambda g: (g // tiles_per_seq, 0, g % tiles_per_seq, 0))
    qkv_shape = jax.ShapeDtypeStruct((batch, N_AUG_HEADS, seq, LANES), BF16)

    q_all, k_all, v_all, gz, gates, cend = pl.pallas_call(
        functools.partial(_proj_kernel, tiles_per_seq=tiles_per_seq),
        grid=(n_tiles,),
        in_specs=[
            pl.BlockSpec((TILE, D_MODEL), lambda g: (g, 0)),
            _const_spec((D_MODEL, W_COLS)),
            _const_spec((1, D_MODEL)), _const_spec((1, LANES)), _const_spec((1, 2 * D_MODEL)),
            _const_spec((1, WIDTH)), _const_spec((1, WIDTH)),
            _const_spec((1, WIDTH)), _const_spec((1, WIDTH)),
            rope_spec, rope_spec, rope_spec,
            _const_spec((TILE, TILE)),
            _const_spec((LANES, WIDTH)), _const_spec((LANES, WIDTH)), _const_spec((LANES, WIDTH)),
            _const_spec((8, LANES)),
        ],
        out_specs=[
            head_spec, head_spec, head_spec,
            pl.BlockSpec((TILE, 2 * WIDTH), lambda g: (g, 0)),
            pl.BlockSpec((TILE, 2 * D_MODEL), lambda g: (g, 0)),
            pl.BlockSpec((1, 8, LANES), lambda g: (g, 0, 0)),
        ],
        out_shape=[
            qkv_shape, qkv_shape, qkv_shape,
            jax.ShapeDtypeStruct((rows, 2 * WIDTH), BF16),
            jax.ShapeDtypeStruct((rows, 2 * D_MODEL), BF16),
            jax.ShapeDtypeStruct((n_tiles, 8, LANES), F32),
        ],
        scratch_shapes=[pltpu.VMEM((1, LANES), F32),
                        pltpu.VMEM((LANES, WIDTH), F32),
                        pltpu.VMEM((TILE, D_MODEL), BF16)],
        compiler_params=pltpu.CompilerParams(
            dimension_semantics=("arbitrary",), vmem_limit_bytes=VMEM_LIMIT),
        name="proj_epilogue",
    )(x2, w, ng, bf, bg, gfq, gfk, gmq, gmk,
      jnp.asarray(rc), jnp.asarray(rsp), jnp.asarray(rsm),
      jnp.asarray(tri, BF16), jnp.asarray(sel_fq, BF16), jnp.asarray(sel_fk, BF16),
      jnp.asarray(sel_mq, BF16), misc)

    n_pairs = N_AUG_HEADS // 2
    att_tiles = seq // ATT_TILE
    cend = cend[:, 0, :N_HEADS].reshape(batch, tiles_per_seq, N_HEADS).transpose(0, 2, 1) * LOG2E
    cend = jnp.concatenate([cend, jnp.zeros_like(cend)], axis=1).reshape(-1)
    kv_spec = pl.BlockSpec((1, 2, seq, LANES), lambda b, p, i, *_: (b, p, 0, 0))
    row_spec = pl.BlockSpec((ATT_TILE, LANES), lambda b, p, i, *_: (b * att_tiles + i, p))
    y = pl.pallas_call(
        functools.partial(_attn_kernel, tiles_per_seq=tiles_per_seq),
        grid_spec=pltpu.PrefetchScalarGridSpec(
            num_scalar_prefetch=2,
            grid=(batch, n_pairs, att_tiles),
            in_specs=[
                pl.BlockSpec((1, 2, ATT_TILE, LANES), lambda b, p, i, *_: (b, p, i, 0)),
                kv_spec, kv_spec, row_spec,
            ],
            out_specs=row_spec,
            scratch_shapes=[pltpu.VMEM((2, ATT_TILE, LANES), F32),
                            pltpu.VMEM((2, ATT_TILE, 1), F32)],
        ),
        out_shape=jax.ShapeDtypeStruct((rows, 2 * WIDTH), BF16),
        compiler_params=pltpu.CompilerParams(
            dimension_semantics=("arbitrary", "arbitrary", "arbitrary"),
            vmem_limit_bytes=VMEM_LIMIT),
        name="flash_attn",
    )(flag, cend, q_all, k_all, v_all, gz)

    out = pl.pallas_call(
        _out_kernel,
        grid=(n_tiles,),
        in_specs=[
            pl.BlockSpec((TILE, 2 * WIDTH), lambda g: (g, 0)),
            pl.BlockSpec((TILE, 2 * D_MODEL), lambda g: (g, 0)),
            pl.BlockSpec((TILE, D_MODEL), lambda g: (g, 0)),
            _const_spec((WIDTH, D_MODEL)), _const_spec((WIDTH, D_MODEL)),
            _const_spec((D_MODEL, D_MODEL)),
        ],
        out_specs=pl.BlockSpec((TILE, D_MODEL), lambda g: (g, 0)),
        out_shape=jax.ShapeDtypeStruct((rows, D_MODEL), F32),
        compiler_params=pltpu.CompilerParams(
            dimension_semantics=("arbitrary",), vmem_limit_bytes=VMEM_LIMIT),
        name="merge_out",
    )(y, gates, x2, w_fox.astype(BF16), w_moba.astype(BF16), w_out.astype(BF16))
    return out


def kernel(x, norm_g, w_in, b_f, b_gate, fox_q_g, fox_k_g, moba_q_g, moba_k_g, w_fox, w_moba, w_out):
    batch, seq, d_model = x.shape
    assert d_model == D_MODEL and seq % ATT_TILE == 0 and seq // TILE <= GROUP
    x2 = x.reshape(batch * seq, D_MODEL)
    for layer in range(norm_g.shape[0]):
        x2 = _layer(x2, batch, seq, norm_g[layer], w_in[layer], b_f[layer], b_gate[layer],
                    fox_q_g[layer], fox_k_g[layer], moba_q_g[layer], moba_k_g[layer],
                    w_fox[layer], w_moba[layer], w_out[layer])
    return x2.reshape(batch, seq, D_MODEL)
```

```python
import functools

import numpy as np
import jax
import jax.numpy as jnp
from jax import lax
from jax.experimental import pallas as pl
from jax.experimental.pallas import tpu as pltpu

D_MODEL = 1024
HEAD_DIM = 64
N_HEADS = 8
WIDTH = N_HEADS * HEAD_DIM
ROPE_DIM = HEAD_DIM // 4
ROPE_HALF = ROPE_DIM // 2
ROPE_THETA = 500000.0
MOBA_BLOCK = 256
MOBA_TOPK = 3
RMS_EPS = 1e-6

LANES = 128
MXU_COLS = 256
TILE = 256
ATT_TILE = 512
ATT_HEADS = 4
FIXED_STABILIZER_MAX_BOUND = 55.0
EXP2_ZERO_BELOW = -152.0
GROUP = 16
N_AUG_HEADS = 2 * N_HEADS
U_COL_FOX = 6
U_COL_MOBA = GROUP
LOG2E = 1.4426950408889634

C_FQ, C_FK, C_FV, C_FZ = 0, 512, 1024, 1536
C_MQ, C_MK, C_MV, C_MZ = 2048, 2560, 3072, 3584
C_GA, C_GB, C_FL = 4096, 5120, 6144
W_COLS = 6144 + LANES

VMEM_LIMIT = 52 * 1024 * 1024

F32 = jnp.float32
BF16 = jnp.bfloat16


def _dot(a, b):
    return jnp.dot(a, b, preferred_element_type=F32)


def _dot_nt(a, b):
    return lax.dot_general(a, b, (((1,), (1,)), ((), ())), preferred_element_type=F32)


def _split3(v):
    hi = v.astype(BF16).astype(F32)
    r = v - hi
    mid = r.astype(BF16).astype(F32)
    lo = r - mid
    return hi, mid, lo


def _extra_base(h):
    return (h // 2) * LANES + (HEAD_DIM if h % 2 == 0 else 0)


def _routing_constants():
    sel_fq = np.zeros((LANES, WIDTH), np.float32)
    sel_fk = np.zeros((LANES, WIDTH), np.float32)
    sel_mq = np.zeros((LANES, WIDTH), np.float32)
    for h in range(N_HEADS):
        base = _extra_base(h)
        for part in range(3):
            sel_fq[part * 8 + h, base + part] = 1.0
            sel_fq[24, base + 3 + part] = 1.0
            sel_fk[24, base + part] = 1.0
            sel_fk[part * 8 + h, base + 3 + part] = -1.0
        sel_fq[25, base + U_COL_FOX] = 1.0
        sel_fk[24, base + U_COL_FOX] = 1.0
        for n in range(GROUP):
            sel_mq[h * GROUP + n, base + n] = 1.0
    tri = np.tril(np.ones((TILE, TILE), np.float32))
    return sel_fq, sel_fk, sel_mq, tri


def _rope_tables(seq):
    inv_freq = ROPE_THETA ** (-np.arange(0, ROPE_HALF, dtype=np.float32) * 2.0 / ROPE_DIM)
    ang = np.arange(seq, dtype=np.float32)[:, None] * inv_freq[None, :].astype(np.float32)
    cos, sin = np.cos(ang).astype(np.float32), np.sin(ang).astype(np.float32)
    rc = np.ones((seq, LANES), np.float32)
    rsp = np.zeros((seq, LANES), np.float32)
    rsm = np.zeros((seq, LANES), np.float32)
    for off in (0, HEAD_DIM):
        rc[:, off:off + ROPE_HALF] = cos
        rc[:, off + ROPE_HALF:off + ROPE_DIM] = cos
        rsm[:, off:off + ROPE_HALF] = -sin
        rsp[:, off + ROPE_HALF:off + ROPE_DIM] = sin
    return rc, rsp, rsm


def _proj_kernel(x_ref, w_ref, ng_ref, bf_ref, bg_ref, gfq_ref, gfk_ref, gmq_ref, gmk_ref,
                 rc_ref, rsp_ref, rsm_ref, tri_ref, selfq_ref, selfk_ref, selmq_ref, misc_ref,
                 q_out, k_out, v_out, gz_out, gate_out, cend_out,
                 carry_ref, kmt_ref, h_ref, *, tiles_per_seq):
    t = pl.program_id(0) % tiles_per_seq

    @pl.when(t == 0)
    def _():
        carry_ref[...] = jnp.zeros_like(carry_ref)
        kmt_ref[...] = jnp.zeros_like(kmt_ref)

    lane = lax.broadcasted_iota(jnp.int32, (TILE, LANES), 1)
    low_half = lane < HEAD_DIM

    x = x_ref[...]
    ms = jnp.mean(x * x, axis=-1, keepdims=True)
    h_ref[...] = (x * lax.rsqrt(ms + RMS_EPS) * ng_ref[...]).astype(BF16)
    zero = pl.multiple_of(jnp.minimum(pl.program_id(0), 0) * TILE, TILE)

    def proj(c0, width):
        slabs = [_dot(h_ref[pl.ds(zero, TILE), :], w_ref[:, c:c + min(MXU_COLS, c0 + width - c)])
                 for c in range(c0, c0 + width, MXU_COLS)]
        return slabs[0] if len(slabs) == 1 else jnp.concatenate(slabs, axis=1)

    def head_norm(a, g_ref):
        tiles = []
        for p in range(WIDTH // LANES):
            ap = a[:, p * LANES:(p + 1) * LANES]
            sq = ap * ap
            s_lo = jnp.sum(jnp.where(low_half, sq, 0.0), axis=-1, keepdims=True)
            s_hi = jnp.sum(jnp.where(low_half, 0.0, sq), axis=-1, keepdims=True)
            inv_lo = lax.rsqrt(s_lo * (1.0 / HEAD_DIM) + RMS_EPS)
            inv_hi = lax.rsqrt(s_hi * (1.0 / HEAD_DIM) + RMS_EPS)
            scale = jnp.where(low_half, inv_lo, inv_hi)
            tiles.append(ap * scale * g_ref[:, p * LANES:(p + 1) * LANES])
        return tiles

    def rope(y):
        return (y * rc_ref[...] + pltpu.roll(y, ROPE_HALF, 1) * rsp_ref[...]
                + pltpu.roll(y, LANES - ROPE_HALF, 1) * rsm_ref[...])

    def split_tiles(a):
        return [a[:, p * LANES:(p + 1) * LANES] for p in range(WIDTH // LANES)]

    def store_heads(out_ref, head0, tiles, extras):
        for p, y in enumerate(tiles):
            e = extras(p)
            out_ref[0, head0 + 2 * p] = jnp.where(low_half, y, e).astype(BF16)
            out_ref[0, head0 + 2 * p + 1] = jnp.where(low_half, e, y).astype(BF16)

    def silu(z):
        hz = 0.5 * z
        return hz + hz * jnp.tanh(hz)

    def sigmoid(z):
        return 0.5 * jnp.tanh(0.5 * z) + 0.5

    fl = proj(C_FL, LANES) + bf_ref[...]
    mk_raw = proj(C_MK, WIDTH)
    mq_raw = proj(C_MQ, WIDTH)
    fv = proj(C_FV, WIDTH)
    fz = proj(C_FZ, WIDTH)

    logf = jnp.minimum(fl, 0.0) - jnp.log(1.0 + jnp.exp(-jnp.abs(fl)))
    logf = jnp.where(lane < N_HEADS, logf, 0.0)
    l_hi, l_mid, l_lo = _split3(logf)
    packed = (l_hi + pltpu.roll(l_mid, 8, 1) + pltpu.roll(l_lo, 16, 1)).astype(BF16)
    cum = _dot(tri_ref[...], packed)
    fq = proj(C_FQ, WIDTH)
    fk = proj(C_FK, WIDTH)

    mk_tiles = [rope(y) for y in head_norm(mk_raw, gmk_ref)]
    row_id = lax.broadcasted_iota(jnp.int32, (LANES, LANES), 0)
    lane_sq = lax.broadcasted_iota(jnp.int32, (LANES, LANES), 1)
    for p, kr in enumerate(mk_tiles):
        km = jnp.mean(kr, axis=0, keepdims=True)
        hit = (((row_id == (2 * p) * GROUP + t) & (lane_sq < HEAD_DIM))
               | ((row_id == (2 * p + 1) * GROUP + t) & (lane_sq >= HEAD_DIM)))
        blk = kmt_ref[:, p * LANES:(p + 1) * LANES]
        kmt_ref[:, p * LANES:(p + 1) * LANES] = jnp.where(hit, km, blk)
    mk_ones = jnp.where((lane % HEAD_DIM == t) | (lane % HEAD_DIM == U_COL_MOBA), 1.0, 0.0)
    store_heads(k_out, N_HEADS, mk_tiles, lambda p: mk_ones)

    c = cum + pltpu.roll(cum, LANES - 8, 1) + pltpu.roll(cum, LANES - 16, 1)
    c = jnp.where(lane < N_HEADS, c, 0.0) + carry_ref[...]
    carry_ref[...] = c[TILE - 1:TILE, :]
    cend_out[0] = jnp.broadcast_to(c[TILE - 1:TILE, :], (8, LANES))
    c_hi, c_mid, c_lo = _split3(c * LOG2E)
    cparts = (c_hi + pltpu.roll(c_mid, 8, 1) + pltpu.roll(c_lo, 16, 1)
              + jnp.where(lane == 24, 1.0, 0.0)
              + jnp.where(lane == 25, misc_ref[1:2, :], 0.0)).astype(BF16)
    ex_fq = _dot(cparts, selfq_ref[...])
    ex_fk = _dot(cparts, selfk_ref[...])
    mv = proj(C_MV, WIDTH)
    mz = proj(C_MZ, WIDTH)

    v_ones = jnp.where(lane % HEAD_DIM == 0, 1.0, 0.0)
    store_heads(v_out, 0, split_tiles(fv), lambda p: v_ones)
    gz_out[:, 0:WIDTH] = silu(fz).astype(BF16)

    mq_tiles = [rope(y) for y in head_norm(mq_raw, gmq_ref)]
    q_full = jnp.concatenate(mq_tiles, axis=1)
    q_hi = q_full.astype(BF16)
    q_lo = (q_full - q_hi.astype(F32)).astype(BF16)
    kmt = kmt_ref[...]
    k_hi = kmt.astype(BF16)
    k_lo = (kmt - k_hi.astype(F32)).astype(BF16)
    gate = _dot_nt(q_hi, k_hi) + _dot_nt(q_hi, k_lo) + _dot_nt(q_lo, k_hi)
    gate_chunks = [(C_GA + half * WIDTH, half * WIDTH) for half in range(2)]
    gate_chunks += [(C_GB + half * WIDTH, D_MODEL + half * WIDTH) for half in range(2)]
    gate_raw = [proj(c0, WIDTH) for c0, _ in gate_chunks]

    store_heads(q_out, 0, head_norm(fq, gfq_ref), lambda p: ex_fq[:, p * LANES:(p + 1) * LANES])
    store_heads(k_out, 0, head_norm(fk, gfk_ref), lambda p: ex_fk[:, p * LANES:(p + 1) * LANES])

    blk_id = lane % GROUP
    past = blk_id < t
    g = jnp.where(past, gate, -jnp.inf)
    beaten = jnp.zeros((TILE, LANES), jnp.int32)
    for d in range(1, GROUP):
        lower = jnp.where(blk_id >= d, pltpu.roll(g, d, 1), -jnp.inf)
        upper = jnp.where(blk_id < GROUP - d, pltpu.roll(g, LANES - d, 1), -jnp.inf)
        beaten = beaten + jnp.where(lower >= g, 1, 0) + jnp.where(upper > g, 1, 0)
    keep = (past & (beaten < MOBA_TOPK)) | (blk_id == t)
    maskvals = jnp.where(keep, 0.0, -misc_ref[0:1, :]).astype(BF16)
    ex_mq = _dot(maskvals, selmq_ref[...])

    store_heads(v_out, N_HEADS, split_tiles(mv), lambda p: v_ones)
    gz_out[:, WIDTH:2 * WIDTH] = silu(mz).astype(BF16)

    is_u_lane = lane % HEAD_DIM == U_COL_MOBA
    store_heads(q_out, N_HEADS, mq_tiles,
                lambda p: jnp.where(is_u_lane, misc_ref[2:3, :], ex_mq[:, p * LANES:(p + 1) * LANES]))
    for raw, (_, o0) in zip(gate_raw, gate_chunks):
        gate_out[:, o0:o0 + WIDTH] = sigmoid(raw + bg_ref[:, o0:o0 + WIDTH]).astype(BF16)


def _attn_kernel(flag_ref, cend_ref, q_ref, k_ref, v_ref, gz_ref, o_ref, acc_ref, m_ref, *,
                 tiles_per_seq):
    b, group, i = pl.program_id(0), pl.program_id(1), pl.program_id(2)
    half = ATT_TILE // 2
    start = pl.multiple_of(i * ATT_TILE, ATT_TILE)
    heads = range(ATT_HEADS)

    def first_live_block(hh):
        base = (b * N_AUG_HEADS + ATT_HEADS * group + hh) * tiles_per_seq
        sub = ATT_TILE // TILE
        c_q = cend_ref[base + jnp.maximum(sub * i - 1, 0)]
        dead = jnp.int32(0)
        for j in range(tiles_per_seq // sub - 1):
            c_k = cend_ref[base + sub * j + sub - 1]
            dead += ((j < i) & (c_q - c_k < EXP2_ZERO_BELOW)).astype(jnp.int32)
        return dead
    row = lax.broadcasted_iota(jnp.int32, (half, ATT_TILE), 0)
    col = lax.broadcasted_iota(jnp.int32, (half, ATT_TILE), 1)
    mask_top = (lax.broadcasted_iota(jnp.int32, (half, half), 1)
                <= lax.broadcasted_iota(jnp.int32, (half, half), 0))
    mask_bot = col <= row + half

    def diag_scores(hh):
        kd = k_ref[0, hh, pl.ds(start, ATT_TILE), :]
        s_top = jnp.where(mask_top, _dot_nt(q_ref[0, hh, 0:half, :], kd[0:half]), -jnp.inf)
        s_bot = jnp.where(mask_bot, _dot_nt(q_ref[0, hh, half:ATT_TILE, :], kd), -jnp.inf)
        return s_top, s_bot

    def finish(accs):
        lane = lax.broadcasted_iota(jnp.int32, (ATT_TILE, LANES), 1)
        for pp in range(ATT_HEADS // 2):
            acc_e, acc_o = accs[2 * pp], accs[2 * pp + 1]
            o_e = acc_e * (1.0 / acc_e[:, HEAD_DIM:HEAD_DIM + 1])
            o_o = acc_o * (1.0 / acc_o[:, 0:1])
            o = jnp.where(lane < HEAD_DIM, o_e, o_o)
            cols = slice(pp * LANES, (pp + 1) * LANES)
            o_ref[:, cols] = (o * gz_ref[:, cols].astype(F32)).astype(BF16)

    @pl.when(flag_ref[0] == 1)
    def _fixed_stabilizer():
        acc_ref[...] = jnp.zeros_like(acc_ref)

        def body(j, carry):
            off = pl.multiple_of(j * ATT_TILE, ATT_TILE)
            for hh in heads:
                s = _dot_nt(q_ref[0, hh], k_ref[0, hh, pl.ds(off, ATT_TILE), :])
                acc_ref[hh] += _dot(jnp.exp2(s).astype(BF16),
                                    v_ref[0, hh, pl.ds(off, ATT_TILE), :])
            return carry

        first = first_live_block(0)
        for hh in heads[1:]:
            first = jnp.minimum(first, first_live_block(hh))
        lax.fori_loop(first, i, body, 0)

        accs = []
        for hh in heads:
            vd = v_ref[0, hh, pl.ds(start, ATT_TILE), :]
            s_top, s_bot = diag_scores(hh)
            top = acc_ref[hh, 0:half] + _dot(jnp.exp2(s_top).astype(BF16), vd[0:half])
            bot = acc_ref[hh, half:ATT_TILE] + _dot(jnp.exp2(s_bot).astype(BF16), vd)
            accs.append(jnp.concatenate([top, bot], axis=0))
        finish(accs)

    @pl.when(flag_ref[0] != 1)
    def _online():
        acc_ref[...] = jnp.zeros_like(acc_ref)
        m_ref[...] = jnp.full_like(m_ref, -jnp.inf)

        def update(s, m_old, acc_old, v):
            m_new = jnp.maximum(m_old, jnp.max(s, axis=-1, keepdims=True))
            pv = _dot(jnp.exp2(s - m_new).astype(BF16), v)
            return m_new, jnp.exp2(m_old - m_new) * acc_old + pv

        def body(j, carry):
            off = pl.multiple_of(j * ATT_TILE, ATT_TILE)
            for hh in heads:
                s = _dot_nt(q_ref[0, hh], k_ref[0, hh, pl.ds(off, ATT_TILE), :])
                m_ref[hh], acc_ref[hh] = update(s, m_ref[hh], acc_ref[hh],
                                                v_ref[0, hh, pl.ds(off, ATT_TILE), :])
            return carry

        lax.fori_loop(0, i, body, 0)

        accs = []
        for hh in heads:
            vd = v_ref[0, hh, pl.ds(start, ATT_TILE), :]
            s_top, s_bot = diag_scores(hh)
            _, top = update(s_top, m_ref[hh, 0:half], acc_ref[hh, 0:half], vd[0:half])
            _, bot = update(s_bot, m_ref[hh, half:ATT_TILE], acc_ref[hh, half:ATT_TILE], vd)
            accs.append(jnp.concatenate([top, bot], axis=0))
        finish(accs)


def _out_kernel(y_ref, gate_ref, x_ref, wf_ref, wm_ref, wo_ref, o_ref):
    pf = _dot(y_ref[:, 0:WIDTH], wf_ref[...])
    pm = _dot(y_ref[:, WIDTH:2 * WIDTH], wm_ref[...])
    merged = (gate_ref[:, 0:D_MODEL].astype(F32) * pf
              + gate_ref[:, D_MODEL:2 * D_MODEL].astype(F32) * pm)
    o_ref[...] = x_ref[...] + _dot(merged.astype(BF16), wo_ref[...])


def _const_spec(shape):
    return pl.BlockSpec(shape, lambda *_: (0,) * len(shape))


def _layer(x2, batch, seq, norm_g, w_in, b_f, b_gate, fox_q_g, fox_k_g, moba_q_g, moba_k_g,
           w_fox, w_moba, w_out):
    rows = batch * seq
    tiles_per_seq = seq // TILE
    n_tiles = rows // TILE
    scale = HEAD_DIM ** -0.5

    w_fl = jnp.pad(w_in[:, 6144:], ((0, 0), (0, LANES - N_HEADS)))
    w = jnp.concatenate([w_in[:, :6144], w_fl], axis=1).astype(BF16)
    ng = norm_g.reshape(1, D_MODEL)
    bf = jnp.pad(b_f, (0, LANES - N_HEADS)).reshape(1, LANES)
    bg = b_gate.reshape(1, 2 * D_MODEL)
    gfq = jnp.tile(fox_q_g * (scale * LOG2E), N_HEADS).reshape(1, WIDTH)
    gfk = jnp.tile(fox_k_g, N_HEADS).reshape(1, WIDTH)
    gmq = jnp.tile(moba_q_g * (scale * LOG2E), N_HEADS).reshape(1, WIDTH)
    gmk = jnp.tile(moba_k_g, N_HEADS).reshape(1, WIDTH)

    def logit_bound(gq, gk):
        return 8.0 * LOG2E * 1.02 * jnp.max(jnp.abs(gq)) * jnp.max(jnp.abs(gk))

    b_fox, b_moba = logit_bound(fox_q_g, fox_k_g), logit_bound(moba_q_g, moba_k_g)
    fixed_ok = jnp.maximum(b_fox, b_moba) <= FIXED_STABILIZER_MAX_BOUND

    def stabilizer(b):
        u = -(b * (1.0 + 2.0 ** -7)).astype(BF16).astype(F32)
        return jnp.where(fixed_ok, u, 0.0)

    big = jnp.exp2(jnp.ceil(jnp.log2(2.1 * b_moba + 160.0)))
    misc = jnp.zeros((8, LANES), F32)
    misc = misc.at[0].set(big).at[1].set(stabilizer(b_fox)).at[2].set(stabilizer(b_moba))
    flag = fixed_ok.astype(jnp.int32).reshape(1)

    sel_fq, sel_fk, sel_mq, tri = _routing_constants()
    rc, rsp, rsm = _rope_tables(seq)

    rope_spec = pl.BlockSpec((TILE, LANES), lambda g: (g % tiles_per_seq, 0))
    head_spec = pl.BlockSpec((1, N_AUG_HEADS, TILE, LANES),
                             lambda g: (g // tiles_per_seq, 0, g % tiles_per_seq, 0))
    qkv_shape = jax.ShapeDtypeStruct((batch, N_AUG_HEADS, seq, LANES), BF16)

    q_all, k_all, v_all, gz, gates, cend = pl.pallas_call(
        functools.partial(_proj_kernel, tiles_per_seq=tiles_per_seq),
        grid=(n_tiles,),
        in_specs=[
            pl.BlockSpec((TILE, D_MODEL), lambda g: (g, 0)),
            _const_spec((D_MODEL, W_COLS)),
            _const_spec((1, D_MODEL)), _const_spec((1, LANES)), _const_spec((1, 2 * D_MODEL)),
            _const_spec((1, WIDTH)), _const_spec((1, WIDTH)),
            _const_spec((1, WIDTH)), _const_spec((1, WIDTH)),
            rope_spec, rope_spec, rope_spec,
            _const_spec((TILE, TILE)),
            _const_spec((LANES, WIDTH)), _const_spec((LANES, WIDTH)), _const_spec((LANES, WIDTH)),
            _const_spec((8, LANES)),
        ],
        out_specs=[
            head_spec, head_spec, head_spec,
            pl.BlockSpec((TILE, 2 * WIDTH), lambda g: (g, 0)),
            pl.BlockSpec((TILE, 2 * D_MODEL), lambda g: (g, 0)),
            pl.BlockSpec((1, 8, LANES), lambda g: (g, 0, 0)),
        ],
        out_shape=[
            qkv_shape, qkv_shape, qkv_shape,
            jax.ShapeDtypeStruct((rows, 2 * WIDTH), BF16),
            jax.ShapeDtypeStruct((rows, 2 * D_MODEL), BF16),
            jax.ShapeDtypeStruct((n_tiles, 8, LANES), F32),
        ],
        scratch_shapes=[pltpu.VMEM((1, LANES), F32),
                        pltpu.VMEM((LANES, WIDTH), F32),
                        pltpu.VMEM((TILE, D_MODEL), BF16)],
        compiler_params=pltpu.CompilerParams(
            dimension_semantics=("arbitrary",), vmem_limit_bytes=VMEM_LIMIT),
        name="proj_epilogue",
    )(x2, w, ng, bf, bg, gfq, gfk, gmq, gmk,
      jnp.asarray(rc), jnp.asarray(rsp), jnp.asarray(rsm),
      jnp.asarray(tri, BF16), jnp.asarray(sel_fq, BF16), jnp.asarray(sel_fk, BF16),
      jnp.asarray(sel_mq, BF16), misc)

    n_groups = N_AUG_HEADS // ATT_HEADS
    group_cols = ATT_HEADS // 2 * LANES
    att_tiles = seq // ATT_TILE
    cend = cend[:, 0, :N_HEADS].reshape(batch, tiles_per_seq, N_HEADS).transpose(0, 2, 1) * LOG2E
    cend = jnp.concatenate([cend, jnp.zeros_like(cend)], axis=1).reshape(-1)
    kv_spec = pl.BlockSpec((1, ATT_HEADS, seq, LANES), lambda b, g, i, *_: (b, g, 0, 0))
    row_spec = pl.BlockSpec((ATT_TILE, group_cols), lambda b, g, i, *_: (b * att_tiles + i, g))
    y = pl.pallas_call(
        functools.partial(_attn_kernel, tiles_per_seq=tiles_per_seq),
        grid_spec=pltpu.PrefetchScalarGridSpec(
            num_scalar_prefetch=2,
            grid=(batch, n_groups, att_tiles),
            in_specs=[
                pl.BlockSpec((1, ATT_HEADS, ATT_TILE, LANES), lambda b, g, i, *_: (b, g, i, 0)),
                kv_spec, kv_spec, row_spec,
            ],
            out_specs=row_spec,
            scratch_shapes=[pltpu.VMEM((ATT_HEADS, ATT_TILE, LANES), F32),
                            pltpu.VMEM((ATT_HEADS, ATT_TILE, 1), F32)],
        ),
        out_shape=jax.ShapeDtypeStruct((rows, 2 * WIDTH), BF16),
        compiler_params=pltpu.CompilerParams(
            dimension_semantics=("arbitrary", "arbitrary", "arbitrary"),
            vmem_limit_bytes=VMEM_LIMIT),
        name="flash_attn",
    )(flag, cend, q_all, k_all, v_all, gz)

    out = pl.pallas_call(
        _out_kernel,
        grid=(n_tiles,),
        in_specs=[
            pl.BlockSpec((TILE, 2 * WIDTH), lambda g: (g, 0)),
            pl.BlockSpec((TILE, 2 * D_MODEL), lambda g: (g, 0)),
            pl.BlockSpec((TILE, D_MODEL), lambda g: (g, 0)),
            _const_spec((WIDTH, D_MODEL)), _const_spec((WIDTH, D_MODEL)),
            _const_spec((D_MODEL, D_MODEL)),
        ],
        out_specs=pl.BlockSpec((TILE, D_MODEL), lambda g: (g, 0)),
        out_shape=jax.ShapeDtypeStruct((rows, D_MODEL), F32),
        compiler_params=pltpu.CompilerParams(
            dimension_semantics=("arbitrary",), vmem_limit_bytes=VMEM_LIMIT),
        name="merge_out",
    )(y, gates, x2, w_fox.astype(BF16), w_moba.astype(BF16), w_out.astype(BF16))
    return out


def kernel(x, norm_g, w_in, b_f, b_gate, fox_q_g, fox_k_g, moba_q_g, moba_k_g, w_fox, w_moba, w_out):
    batch, seq, d_model = x.shape
    assert d_model == D_MODEL and seq % ATT_TILE == 0 and seq // TILE <= GROUP
    x2 = x.reshape(batch * seq, D_MODEL)
    for layer in range(norm_g.shape[0]):
        x2 = _layer(x2, batch, seq, norm_g[layer], w_in[layer], b_f[layer], b_gate[layer],
                    fox_q_g[layer], fox_k_g[layer], moba_q_g[layer], moba_k_g[layer],
                    w_fox[layer], w_moba[layer], w_out[layer])
    return x2.reshape(batch, seq, D_MODEL)
```

```python
import functools

import numpy as np
import jax
import jax.numpy as jnp
from jax import lax
from jax.experimental import pallas as pl
from jax.experimental.pallas import tpu as pltpu

D_MODEL = 1024
HEAD_DIM = 64
N_HEADS = 8
WIDTH = N_HEADS * HEAD_DIM
ROPE_DIM = HEAD_DIM // 4
ROPE_HALF = ROPE_DIM // 2
ROPE_THETA = 500000.0
MOBA_BLOCK = 256
MOBA_TOPK = 3
RMS_EPS = 1e-6

LANES = 128
MXU_COLS = 256
TILE = 256
OUT_TILE = 512
ATT_TILE = 512
ATT_HEADS = 4
FIXED_STABILIZER_MAX_BOUND = 55.0
EXP2_ZERO_BELOW = -152.0
GROUP = 16
N_AUG_HEADS = 2 * N_HEADS
U_COL_FOX = 6
U_COL_MOBA = GROUP
LOG2E = 1.4426950408889634

C_FQ, C_FK, C_FV, C_FZ = 0, 512, 1024, 1536
C_MQ, C_MK, C_MV, C_MZ = 2048, 2560, 3072, 3584
C_GA, C_GB = 4096, 5120
W_COLS = 6144

VMEM_LIMIT = 52 * 1024 * 1024

F32 = jnp.float32
BF16 = jnp.bfloat16


def _dot(a, b):
    return jnp.dot(a, b, preferred_element_type=F32)


def _dot_nt(a, b):
    return lax.dot_general(a, b, (((1,), (1,)), ((), ())), preferred_element_type=F32)


def _split3(v):
    hi = v.astype(BF16).astype(F32)
    r = v - hi
    mid = r.astype(BF16).astype(F32)
    lo = r - mid
    return hi, mid, lo


def _extra_base(h):
    return (h // 2) * LANES + (HEAD_DIM if h % 2 == 0 else 0)


def _routing_constants():
    sel_fq = np.zeros((LANES, WIDTH), np.float32)
    sel_fk = np.zeros((LANES, WIDTH), np.float32)
    sel_mq = np.zeros((LANES, WIDTH), np.float32)
    for h in range(N_HEADS):
        base = _extra_base(h)
        for part in range(3):
            sel_fq[part * 8 + h, base + part] = 1.0
            sel_fq[24, base + 3 + part] = 1.0
            sel_fk[24, base + part] = 1.0
            sel_fk[part * 8 + h, base + 3 + part] = -1.0
        sel_fq[25, base + U_COL_FOX] = 1.0
        sel_fk[24, base + U_COL_FOX] = 1.0
        for n in range(GROUP):
            sel_mq[h * GROUP + n, base + n] = 1.0
    tri = np.tril(np.ones((TILE, TILE), np.float32))
    return sel_fq, sel_fk, sel_mq, tri


def _rope_tables(seq):
    inv_freq = ROPE_THETA ** (-np.arange(0, ROPE_HALF, dtype=np.float32) * 2.0 / ROPE_DIM)
    ang = np.arange(seq, dtype=np.float32)[:, None] * inv_freq[None, :].astype(np.float32)
    cos, sin = np.cos(ang).astype(np.float32), np.sin(ang).astype(np.float32)
    rc = np.ones((seq, LANES), np.float32)
    rsp = np.zeros((seq, LANES), np.float32)
    rsm = np.zeros((seq, LANES), np.float32)
    for off in (0, HEAD_DIM):
        rc[:, off:off + ROPE_HALF] = cos
        rc[:, off + ROPE_HALF:off + ROPE_DIM] = cos
        rsm[:, off:off + ROPE_HALF] = -sin
        rsp[:, off + ROPE_HALF:off + ROPE_DIM] = sin
    return rc, rsp, rsm


def _proj_kernel(x_ref, w_ref, wfl_ref, ng_ref, bf_ref, bg_ref, gfq_ref, gfk_ref, gmq_ref, gmk_ref,
                 rc_ref, rsp_ref, rsm_ref, tri_ref, selfq_ref, selfk_ref, selmq_ref, misc_ref,
                 q_out, k_out, v_out, gz_out, gate_out, cend_out,
                 carry_ref, kmt_ref, h_ref, *, tiles_per_seq):
    t = pl.program_id(0) % tiles_per_seq

    @pl.when(t == 0)
    def _():
        carry_ref[...] = jnp.zeros_like(carry_ref)
        kmt_ref[...] = jnp.zeros_like(kmt_ref)

    lane = lax.broadcasted_iota(jnp.int32, (TILE, LANES), 1)
    low_half = lane < HEAD_DIM

    x = x_ref[...]
    ms = jnp.mean(x * x, axis=-1, keepdims=True)
    h_ref[...] = (x * lax.rsqrt(ms + RMS_EPS) * ng_ref[...]).astype(BF16)
    zero = pl.multiple_of(jnp.minimum(pl.program_id(0), 0) * TILE, TILE)

    def proj(c0, width):
        slabs = [_dot(h_ref[pl.ds(zero, TILE), :], w_ref[:, c:c + min(MXU_COLS, c0 + width - c)])
                 for c in range(c0, c0 + width, MXU_COLS)]
        return slabs[0] if len(slabs) == 1 else jnp.concatenate(slabs, axis=1)

    def head_norm(a, g_ref):
        tiles = []
        for p in range(WIDTH // LANES):
            ap = a[:, p * LANES:(p + 1) * LANES]
            sq = ap * ap
            s_lo = jnp.sum(jnp.where(low_half, sq, 0.0), axis=-1, keepdims=True)
            s_hi = jnp.sum(jnp.where(low_half, 0.0, sq), axis=-1, keepdims=True)
            inv_lo = lax.rsqrt(s_lo * (1.0 / HEAD_DIM) + RMS_EPS)
            inv_hi = lax.rsqrt(s_hi * (1.0 / HEAD_DIM) + RMS_EPS)
            scale = jnp.where(low_half, inv_lo, inv_hi)
            tiles.append(ap * scale * g_ref[:, p * LANES:(p + 1) * LANES])
        return tiles

    def rope(y):
        return (y * rc_ref[...] + pltpu.roll(y, ROPE_HALF, 1) * rsp_ref[...]
                + pltpu.roll(y, LANES - ROPE_HALF, 1) * rsm_ref[...])

    def split_tiles(a):
        return [a[:, p * LANES:(p + 1) * LANES] for p in range(WIDTH // LANES)]

    def store_heads(out_ref, head0, tiles, extras):
        for p, y in enumerate(tiles):
            e = extras(p)
            out_ref[0, head0 + 2 * p] = jnp.where(low_half, y, e).astype(BF16)
            out_ref[0, head0 + 2 * p + 1] = jnp.where(low_half, e, y).astype(BF16)

    def silu(z):
        hz = 0.5 * z
        return hz + hz * jnp.tanh(hz)

    def sigmoid(z):
        return 0.5 * jnp.tanh(0.5 * z) + 0.5

    fl = _dot(h_ref[pl.ds(zero, TILE), :], wfl_ref[...]) + bf_ref[...]
    mk_raw = proj(C_MK, WIDTH)
    mq_raw = proj(C_MQ, WIDTH)
    fv = proj(C_FV, WIDTH)
    fz = proj(C_FZ, WIDTH)

    logf = jnp.minimum(fl, 0.0) - jnp.log(1.0 + jnp.exp(-jnp.abs(fl)))
    logf = jnp.where(lane < N_HEADS, logf, 0.0)
    l_hi, l_mid, l_lo = _split3(logf)
    packed = (l_hi + pltpu.roll(l_mid, 8, 1) + pltpu.roll(l_lo, 16, 1)).astype(BF16)
    cum = _dot(tri_ref[...], packed)
    fq = proj(C_FQ, WIDTH)
    fk = proj(C_FK, WIDTH)

    mk_tiles = [rope(y) for y in head_norm(mk_raw, gmk_ref)]
    row_id = lax.broadcasted_iota(jnp.int32, (LANES, LANES), 0)
    lane_sq = lax.broadcasted_iota(jnp.int32, (LANES, LANES), 1)
    for p, kr in enumerate(mk_tiles):
        km = jnp.mean(kr, axis=0, keepdims=True)
        hit = (((row_id == (2 * p) * GROUP + t) & (lane_sq < HEAD_DIM))
               | ((row_id == (2 * p + 1) * GROUP + t) & (lane_sq >= HEAD_DIM)))
        blk = kmt_ref[:, p * LANES:(p + 1) * LANES]
        kmt_ref[:, p * LANES:(p + 1) * LANES] = jnp.where(hit, km, blk)
    mk_ones = jnp.where((lane % HEAD_DIM == t) | (lane % HEAD_DIM == U_COL_MOBA), 1.0, 0.0)
    store_heads(k_out, N_HEADS, mk_tiles, lambda p: mk_ones)

    c = cum + pltpu.roll(cum, LANES - 8, 1) + pltpu.roll(cum, LANES - 16, 1)
    c = jnp.where(lane < N_HEADS, c, 0.0) + carry_ref[...]
    carry_ref[...] = c[TILE - 1:TILE, :]
    cend_out[0] = jnp.broadcast_to(c[TILE - 1:TILE, :], (8, LANES))
    c_hi, c_mid, c_lo = _split3(c * LOG2E)
    cparts = (c_hi + pltpu.roll(c_mid, 8, 1) + pltpu.roll(c_lo, 16, 1)
              + jnp.where(lane == 24, 1.0, 0.0)
              + jnp.where(lane == 25, misc_ref[1:2, :], 0.0)).astype(BF16)
    ex_fq = _dot(cparts, selfq_ref[...])
    ex_fk = _dot(cparts, selfk_ref[...])
    mv = proj(C_MV, WIDTH)
    mz = proj(C_MZ, WIDTH)

    v_ones = jnp.where(lane % HEAD_DIM == 0, 1.0, 0.0)
    store_heads(v_out, 0, split_tiles(fv), lambda p: v_ones)
    gz_out[:, 0:WIDTH] = silu(fz).astype(BF16)

    mq_tiles = [rope(y) for y in head_norm(mq_raw, gmq_ref)]
    q_full = jnp.concatenate(mq_tiles, axis=1)
    q_hi = q_full.astype(BF16)
    q_lo = (q_full - q_hi.astype(F32)).astype(BF16)
    kmt = kmt_ref[...]
    k_hi = kmt.astype(BF16)
    k_lo = (kmt - k_hi.astype(F32)).astype(BF16)
    gate = _dot_nt(q_hi, k_hi) + _dot_nt(q_hi, k_lo) + _dot_nt(q_lo, k_hi)
    gate_chunks = [(C_GA + half * WIDTH, half * WIDTH) for half in range(2)]
    gate_chunks += [(C_GB + half * WIDTH, D_MODEL + half * WIDTH) for half in range(2)]
    gate_raw = [proj(c0, WIDTH) for c0, _ in gate_chunks]

    store_heads(q_out, 0, head_norm(fq, gfq_ref), lambda p: ex_fq[:, p * LANES:(p + 1) * LANES])
    store_heads(k_out, 0, head_norm(fk, gfk_ref), lambda p: ex_fk[:, p * LANES:(p + 1) * LANES])

    blk_id = lane % GROUP
    past = blk_id < t
    g = jnp.where(past, gate, -jnp.inf)
    beaten = jnp.zeros((TILE, LANES), jnp.int32)
    for d in range(1, GROUP):
        lower = jnp.where(blk_id >= d, pltpu.roll(g, d, 1), -jnp.inf)
        upper = jnp.where(blk_id < GROUP - d, pltpu.roll(g, LANES - d, 1), -jnp.inf)
        beaten = beaten + jnp.where(lower >= g, 1, 0) + jnp.where(upper > g, 1, 0)
    keep = (past & (beaten < MOBA_TOPK)) | (blk_id == t)
    maskvals = jnp.where(keep, 0.0, -misc_ref[0:1, :]).astype(BF16)
    ex_mq = _dot(maskvals, selmq_ref[...])

    store_heads(v_out, N_HEADS, split_tiles(mv), lambda p: v_ones)
    gz_out[:, WIDTH:2 * WIDTH] = silu(mz).astype(BF16)

    is_u_lane = lane % HEAD_DIM == U_COL_MOBA
    store_heads(q_out, N_HEADS, mq_tiles,
                lambda p: jnp.where(is_u_lane, misc_ref[2:3, :], ex_mq[:, p * LANES:(p + 1) * LANES]))
    for raw, (_, o0) in zip(gate_raw, gate_chunks):
        gate_out[:, o0:o0 + WIDTH] = sigmoid(raw + bg_ref[:, o0:o0 + WIDTH]).astype(BF16)


def _attn_kernel(flag_ref, cend_ref, q_ref, k_ref, v_ref, gz_ref, o_ref, acc_ref, m_ref, *,
                 tiles_per_seq):
    b, group, i = pl.program_id(0), pl.program_id(1), pl.program_id(2)
    half = ATT_TILE // 2
    start = pl.multiple_of(i * ATT_TILE, ATT_TILE)
    heads = range(ATT_HEADS)

    def first_live_block(hh):
        base = (b * N_AUG_HEADS + ATT_HEADS * group + hh) * tiles_per_seq
        sub = ATT_TILE // TILE
        c_q = cend_ref[base + jnp.maximum(sub * i - 1, 0)]
        dead = jnp.int32(0)
        for j in range(tiles_per_seq // sub - 1):
            c_k = cend_ref[base + sub * j + sub - 1]
            dead += ((j < i) & (c_q - c_k < EXP2_ZERO_BELOW)).astype(jnp.int32)
        return dead
    row = lax.broadcasted_iota(jnp.int32, (half, ATT_TILE), 0)
    col = lax.broadcasted_iota(jnp.int32, (half, ATT_TILE), 1)
    mask_top = (lax.broadcasted_iota(jnp.int32, (half, half), 1)
                <= lax.broadcasted_iota(jnp.int32, (half, half), 0))
    mask_bot = col <= row + half

    def diag_scores(hh):
        kd = k_ref[0, hh, pl.ds(start, ATT_TILE), :]
        s_top = jnp.where(mask_top, _dot_nt(q_ref[0, hh, 0:half, :], kd[0:half]), -jnp.inf)
        s_bot = jnp.where(mask_bot, _dot_nt(q_ref[0, hh, half:ATT_TILE, :], kd), -jnp.inf)
        return s_top, s_bot

    def finish(accs):
        lane = lax.broadcasted_iota(jnp.int32, (ATT_TILE, LANES), 1)
        for pp in range(ATT_HEADS // 2):
            acc_e, acc_o = accs[2 * pp], accs[2 * pp + 1]
            o_e = acc_e * (1.0 / acc_e[:, HEAD_DIM:HEAD_DIM + 1])
            o_o = acc_o * (1.0 / acc_o[:, 0:1])
            o = jnp.where(lane < HEAD_DIM, o_e, o_o)
            cols = slice(pp * LANES, (pp + 1) * LANES)
            o_ref[:, cols] = (o * gz_ref[:, cols].astype(F32)).astype(BF16)

    @pl.when(flag_ref[0] == 1)
    def _fixed_stabilizer():
        acc_ref[...] = jnp.zeros_like(acc_ref)

        def body(j, carry):
            off = pl.multiple_of(j * ATT_TILE, ATT_TILE)
            for hh in heads:
                s = _dot_nt(q_ref[0, hh], k_ref[0, hh, pl.ds(off, ATT_TILE), :])
                acc_ref[hh] += _dot(jnp.exp2(s).astype(BF16),
                                    v_ref[0, hh, pl.ds(off, ATT_TILE), :])
            return carry

        first = first_live_block(0)
        for hh in heads[1:]:
            first = jnp.minimum(first, first_live_block(hh))
        lax.fori_loop(first, i, body, 0)

        accs = []
        for hh in heads:
            vd = v_ref[0, hh, pl.ds(start, ATT_TILE), :]
            s_top, s_bot = diag_scores(hh)
            top = acc_ref[hh, 0:half] + _dot(jnp.exp2(s_top).astype(BF16), vd[0:half])
            bot = acc_ref[hh, half:ATT_TILE] + _dot(jnp.exp2(s_bot).astype(BF16), vd)
            accs.append(jnp.concatenate([top, bot], axis=0))
        finish(accs)

    @pl.when(flag_ref[0] != 1)
    def _online():
        acc_ref[...] = jnp.zeros_like(acc_ref)
        m_ref[...] = jnp.full_like(m_ref, -jnp.inf)

        def update(s, m_old, acc_old, v):
            m_new = jnp.maximum(m_old, jnp.max(s, axis=-1, keepdims=True))
            pv = _dot(jnp.exp2(s - m_new).astype(BF16), v)
            return m_new, jnp.exp2(m_old - m_new) * acc_old + pv

        def body(j, carry):
            off = pl.multiple_of(j * ATT_TILE, ATT_TILE)
            for hh in heads:
                s = _dot_nt(q_ref[0, hh], k_ref[0, hh, pl.ds(off, ATT_TILE), :])
                m_ref[hh], acc_ref[hh] = update(s, m_ref[hh], acc_ref[hh],
                                                v_ref[0, hh, pl.ds(off, ATT_TILE), :])
            return carry

        lax.fori_loop(0, i, body, 0)

        accs = []
        for hh in heads:
            vd = v_ref[0, hh, pl.ds(start, ATT_TILE), :]
            s_top, s_bot = diag_scores(hh)
            _, top = update(s_top, m_ref[hh, 0:half], acc_ref[hh, 0:half], vd[0:half])
            _, bot = update(s_bot, m_ref[hh, half:ATT_TILE], acc_ref[hh, half:ATT_TILE], vd)
            accs.append(jnp.concatenate([top, bot], axis=0))
        finish(accs)


def _out_kernel(y_ref, gate_ref, x_ref, wf_ref, wm_ref, wo_ref, o_ref):
    slabs = [slice(c, c + MXU_COLS) for c in range(0, D_MODEL, MXU_COLS)]
    yf, ym = y_ref[:, 0:WIDTH], y_ref[:, WIDTH:2 * WIDTH]
    merged = []
    for cols in slabs:
        pf = _dot(yf, wf_ref[:, cols])
        pm = _dot(ym, wm_ref[:, cols])
        ga = gate_ref[:, cols].astype(F32)
        gb = gate_ref[:, D_MODEL + cols.start:D_MODEL + cols.stop].astype(F32)
        merged.append((ga * pf + gb * pm).astype(BF16))
    merged = jnp.concatenate(merged, axis=1)
    for cols in slabs:
        o_ref[:, cols] = x_ref[:, cols] + _dot(merged, wo_ref[:, cols])


def _const_spec(shape):
    return pl.BlockSpec(shape, lambda *_: (0,) * len(shape))


def _layer(x2, batch, seq, norm_g, w_in, b_f, b_gate, fox_q_g, fox_k_g, moba_q_g, moba_k_g,
           w_fox, w_moba, w_out):
    rows = batch * seq
    tiles_per_seq = seq // TILE
    n_tiles = rows // TILE
    scale = HEAD_DIM ** -0.5

    w = w_in[:, :W_COLS].astype(BF16)
    w_fl = jnp.pad(w_in[:, W_COLS:], ((0, 0), (0, LANES - N_HEADS))).astype(BF16)
    ng = norm_g.reshape(1, D_MODEL)
    bf = jnp.pad(b_f, (0, LANES - N_HEADS)).reshape(1, LANES)
    bg = b_gate.reshape(1, 2 * D_MODEL)
    gfq = jnp.tile(fox_q_g * (scale * LOG2E), N_HEADS).reshape(1, WIDTH)
    gfk = jnp.tile(fox_k_g, N_HEADS).reshape(1, WIDTH)
    gmq = jnp.tile(moba_q_g * (scale * LOG2E), N_HEADS).reshape(1, WIDTH)
    gmk = jnp.tile(moba_k_g, N_HEADS).reshape(1, WIDTH)

    def logit_bound(gq, gk):
        return 8.0 * LOG2E * 1.02 * jnp.max(jnp.abs(gq)) * jnp.max(jnp.abs(gk))

    b_fox, b_moba = logit_bound(fox_q_g, fox_k_g), logit_bound(moba_q_g, moba_k_g)
    fixed_ok = jnp.maximum(b_fox, b_moba) <= FIXED_STABILIZER_MAX_BOUND

    def stabilizer(b):
        u = -(b * (1.0 + 2.0 ** -7)).astype(BF16).astype(F32)
        return jnp.where(fixed_ok, u, 0.0)

    big = jnp.exp2(jnp.ceil(jnp.log2(2.1 * b_moba + 160.0)))
    misc = jnp.zeros((8, LANES), F32)
    misc = misc.at[0].set(big).at[1].set(stabilizer(b_fox)).at[2].set(stabilizer(b_moba))
    flag = fixed_ok.astype(jnp.int32).reshape(1)

    sel_fq, sel_fk, sel_mq, tri = _routing_constants()
    rc, rsp, rsm = _rope_tables(seq)

    rope_spec = pl.BlockSpec((TILE, LANES), lambda g: (g % tiles_per_seq, 0))
    head_spec = pl.BlockSpec((1, N_AUG_HEADS, TILE, LANES),
                             lambda g: (g // tiles_per_seq, 0, g % tiles_per_seq, 0))
    qkv_shape = jax.ShapeDtypeStruct((batch, N_AUG_HEADS, seq, LANES), BF16)

    q_all, k_all, v_all, gz, gates, cend = pl.pallas_call(
        functools.partial(_proj_kernel, tiles_per_seq=tiles_per_seq),
        grid=(n_tiles,),
        in_specs=[
            pl.BlockSpec((TILE, D_MODEL), lambda g: (g, 0)),
            _const_spec((D_MODEL, W_COLS)), _const_spec((D_MODEL, LANES)),
            _const_spec((1, D_MODEL)), _const_spec((1, LANES)), _const_spec((1, 2 * D_MODEL)),
            _const_spec((1, WIDTH)), _const_spec((1, WIDTH)),
            _const_spec((1, WIDTH)), _const_spec((1, WIDTH)),
            rope_spec, rope_spec, rope_spec,
            _const_spec((TILE, TILE)),
            _const_spec((LANES, WIDTH)), _const_spec((LANES, WIDTH)), _const_spec((LANES, WIDTH)),
            _const_spec((8, LANES)),
        ],
        out_specs=[
            head_spec, head_spec, head_spec,
            pl.BlockSpec((TILE, 2 * WIDTH), lambda g: (g, 0)),
            pl.BlockSpec((TILE, 2 * D_MODEL), lambda g: (g, 0)),
            pl.BlockSpec((1, 8, LANES), lambda g: (g, 0, 0)),
        ],
        out_shape=[
            qkv_shape, qkv_shape, qkv_shape,
            jax.ShapeDtypeStruct((rows, 2 * WIDTH), BF16),
            jax.ShapeDtypeStruct((rows, 2 * D_MODEL), BF16),
            jax.ShapeDtypeStruct((n_tiles, 8, LANES), F32),
        ],
        scratch_shapes=[pltpu.VMEM((1, LANES), F32),
                        pltpu.VMEM((LANES, WIDTH), F32),
                        pltpu.VMEM((TILE, D_MODEL), BF16)],
        compiler_params=pltpu.CompilerParams(
            dimension_semantics=("arbitrary",), vmem_limit_bytes=VMEM_LIMIT),
        name="proj_epilogue",
    )(x2, w, w_fl, ng, bf, bg, gfq, gfk, gmq, gmk,
      jnp.asarray(rc), jnp.asarray(rsp), jnp.asarray(rsm),
      jnp.asarray(tri, BF16), jnp.asarray(sel_fq, BF16), jnp.asarray(sel_fk, BF16),
      jnp.asarray(sel_mq, BF16), misc)

    n_groups = N_AUG_HEADS // ATT_HEADS
    group_cols = ATT_HEADS // 2 * LANES
    att_tiles = seq // ATT_TILE
    cend = cend[:, 0, :N_HEADS].reshape(batch, tiles_per_seq, N_HEADS).transpose(0, 2, 1) * LOG2E
    cend = jnp.concatenate([cend, jnp.zeros_like(cend)], axis=1).reshape(-1)
    kv_spec = pl.BlockSpec((1, ATT_HEADS, seq, LANES), lambda b, g, i, *_: (b, g, 0, 0))
    row_spec = pl.BlockSpec((ATT_TILE, group_cols), lambda b, g, i, *_: (b * att_tiles + i, g))
    y = pl.pallas_call(
        functools.partial(_attn_kernel, tiles_per_seq=tiles_per_seq),
        grid_spec=pltpu.PrefetchScalarGridSpec(
            num_scalar_prefetch=2,
            grid=(batch, n_groups, att_tiles),
            in_specs=[
                pl.BlockSpec((1, ATT_HEADS, ATT_TILE, LANES), lambda b, g, i, *_: (b, g, i, 0)),
                kv_spec, kv_spec, row_spec,
            ],
            out_specs=row_spec,
            scratch_shapes=[pltpu.VMEM((ATT_HEADS, ATT_TILE, LANES), F32),
                            pltpu.VMEM((ATT_HEADS, ATT_TILE, 1), F32)],
        ),
        out_shape=jax.ShapeDtypeStruct((rows, 2 * WIDTH), BF16),
        compiler_params=pltpu.CompilerParams(
            dimension_semantics=("arbitrary", "arbitrary", "arbitrary"),
            vmem_limit_bytes=VMEM_LIMIT),
        name="flash_attn",
    )(flag, cend, q_all, k_all, v_all, gz)

    out = pl.pallas_call(
        _out_kernel,
        grid=(rows // OUT_TILE,),
        in_specs=[
            pl.BlockSpec((OUT_TILE, 2 * WIDTH), lambda g: (g, 0)),
            pl.BlockSpec((OUT_TILE, 2 * D_MODEL), lambda g: (g, 0)),
            pl.BlockSpec((OUT_TILE, D_MODEL), lambda g: (g, 0)),
            _const_spec((WIDTH, D_MODEL)), _const_spec((WIDTH, D_MODEL)),
            _const_spec((D_MODEL, D_MODEL)),
        ],
        out_specs=pl.BlockSpec((OUT_TILE, D_MODEL), lambda g: (g, 0)),
        out_shape=jax.ShapeDtypeStruct((rows, D_MODEL), F32),
        compiler_params=pltpu.CompilerParams(
            dimension_semantics=("arbitrary",), vmem_limit_bytes=VMEM_LIMIT),
        name="merge_out",
    )(y, gates, x2, w_fox.astype(BF16), w_moba.astype(BF16), w_out.astype(BF16))
    return out


def kernel(x, norm_g, w_in, b_f, b_gate, fox_q_g, fox_k_g, moba_q_g, moba_k_g, w_fox, w_moba, w_out):
    batch, seq, d_model = x.shape
    assert d_model == D_MODEL and seq % ATT_TILE == 0 and seq // TILE <= GROUP
    x2 = x.reshape(batch * seq, D_MODEL)
    for layer in range(norm_g.shape[0]):
        x2 = _layer(x2, batch, seq, norm_g[layer], w_in[layer], b_f[layer], b_gate[layer],
                    fox_q_g[layer], fox_k_g[layer], moba_q_g[layer], moba_k_g[layer],
                    w_fox[layer], w_moba[layer], w_out[layer])
    return x2.reshape(batch, seq, D_MODEL)
```

```python
import functools

import numpy as np
import jax
import jax.numpy as jnp
from jax import lax
from jax.experimental import pallas as pl
from jax.experimental.pallas import tpu as pltpu

D_MODEL = 1024
HEAD_DIM = 64
N_HEADS = 8
WIDTH = N_HEADS * HEAD_DIM
ROPE_DIM = HEAD_DIM // 4
ROPE_HALF = ROPE_DIM // 2
ROPE_THETA = 500000.0
MOBA_BLOCK = 256
MOBA_TOPK = 3
RMS_EPS = 1e-6

LANES = 128
MXU_COLS = 256
TILE = 256
OUT_TILE = 512
ATT_TILE = 512
ATT_HEADS = 8
FIXED_STABILIZER_MAX_BOUND = 55.0
EXP2_ZERO_BELOW = -152.0
GROUP = 16
N_AUG_HEADS = 2 * N_HEADS
U_COL_FOX = 6
U_COL_MOBA = GROUP
LOG2E = 1.4426950408889634

C_FQ, C_FK, C_FV, C_FZ = 0, 512, 1024, 1536
C_MQ, C_MK, C_MV, C_MZ = 2048, 2560, 3072, 3584
C_GA, C_GB = 4096, 5120
W_COLS = 6144

VMEM_LIMIT = 52 * 1024 * 1024

F32 = jnp.float32
BF16 = jnp.bfloat16


def _dot(a, b):
    return jnp.dot(a, b, preferred_element_type=F32)


def _dot_nt(a, b):
    return lax.dot_general(a, b, (((1,), (1,)), ((), ())), preferred_element_type=F32)


def _split3(v):
    hi = v.astype(BF16).astype(F32)
    r = v - hi
    mid = r.astype(BF16).astype(F32)
    lo = r - mid
    return hi, mid, lo


def _extra_base(h):
    return (h // 2) * LANES + (HEAD_DIM if h % 2 == 0 else 0)


def _routing_constants():
    sel_fq = np.zeros((LANES, WIDTH), np.float32)
    sel_fk = np.zeros((LANES, WIDTH), np.float32)
    sel_mq = np.zeros((LANES, WIDTH), np.float32)
    for h in range(N_HEADS):
        base = _extra_base(h)
        for part in range(3):
            sel_fq[part * 8 + h, base + part] = 1.0
            sel_fq[24, base + 3 + part] = 1.0
            sel_fk[24, base + part] = 1.0
            sel_fk[part * 8 + h, base + 3 + part] = -1.0
        sel_fq[25, base + U_COL_FOX] = 1.0
        sel_fk[24, base + U_COL_FOX] = 1.0
        for n in range(GROUP):
            sel_mq[h * GROUP + n, base + n] = 1.0
    tri = np.tril(np.ones((TILE, TILE), np.float32))
    return sel_fq, sel_fk, sel_mq, tri


def _rope_tables(seq):
    inv_freq = ROPE_THETA ** (-np.arange(0, ROPE_HALF, dtype=np.float32) * 2.0 / ROPE_DIM)
    ang = np.arange(seq, dtype=np.float32)[:, None] * inv_freq[None, :].astype(np.float32)
    cos, sin = np.cos(ang).astype(np.float32), np.sin(ang).astype(np.float32)
    rc = np.ones((seq, LANES), np.float32)
    rsp = np.zeros((seq, LANES), np.float32)
    rsm = np.zeros((seq, LANES), np.float32)
    for off in (0, HEAD_DIM):
        rc[:, off:off + ROPE_HALF] = cos
        rc[:, off + ROPE_HALF:off + ROPE_DIM] = cos
        rsm[:, off:off + ROPE_HALF] = -sin
        rsp[:, off + ROPE_HALF:off + ROPE_DIM] = sin
    return rc, rsp, rsm


def _proj_kernel(x_ref, w_ref, wfl_ref, ng_ref, bf_ref, bg_ref, gfq_ref, gfk_ref, gmq_ref, gmk_ref,
                 rc_ref, rsp_ref, rsm_ref, tri_ref, selfq_ref, selfk_ref, selmq_ref, misc_ref,
                 q_out, k_out, v_out, gz_out, gate_out, cend_out,
                 carry_ref, kmt_ref, h_ref, *, tiles_per_seq):
    t = pl.program_id(0) % tiles_per_seq

    @pl.when(t == 0)
    def _():
        carry_ref[...] = jnp.zeros_like(carry_ref)
        kmt_ref[...] = jnp.zeros_like(kmt_ref)

    lane = lax.broadcasted_iota(jnp.int32, (TILE, LANES), 1)
    low_half = lane < HEAD_DIM

    x = x_ref[...]
    ms = jnp.mean(x * x, axis=-1, keepdims=True)
    h_ref[...] = (x * lax.rsqrt(ms + RMS_EPS) * ng_ref[...]).astype(BF16)
    zero = pl.multiple_of(jnp.minimum(pl.program_id(0), 0) * TILE, TILE)

    def proj(c0, width):
        slabs = [_dot(h_ref[pl.ds(zero, TILE), :], w_ref[:, c:c + min(MXU_COLS, c0 + width - c)])
                 for c in range(c0, c0 + width, MXU_COLS)]
        return slabs[0] if len(slabs) == 1 else jnp.concatenate(slabs, axis=1)

    def head_norm(a, g_ref):
        tiles = []
        for p in range(WIDTH // LANES):
            ap = a[:, p * LANES:(p + 1) * LANES]
            sq = ap * ap
            s_lo = jnp.sum(jnp.where(low_half, sq, 0.0), axis=-1, keepdims=True)
            s_hi = jnp.sum(jnp.where(low_half, 0.0, sq), axis=-1, keepdims=True)
            inv_lo = lax.rsqrt(s_lo * (1.0 / HEAD_DIM) + RMS_EPS)
            inv_hi = lax.rsqrt(s_hi * (1.0 / HEAD_DIM) + RMS_EPS)
            scale = jnp.where(low_half, inv_lo, inv_hi)
            tiles.append(ap * scale * g_ref[:, p * LANES:(p + 1) * LANES])
        return tiles

    def rope(y):
        return (y * rc_ref[...] + pltpu.roll(y, ROPE_HALF, 1) * rsp_ref[...]
                + pltpu.roll(y, LANES - ROPE_HALF, 1) * rsm_ref[...])

    def split_tiles(a):
        return [a[:, p * LANES:(p + 1) * LANES] for p in range(WIDTH // LANES)]

    def store_heads(out_ref, head0, tiles, extras):
        for p, y in enumerate(tiles):
            e = extras(p)
            out_ref[0, head0 + 2 * p] = jnp.where(low_half, y, e).astype(BF16)
            out_ref[0, head0 + 2 * p + 1] = jnp.where(low_half, e, y).astype(BF16)

    def silu(z):
        hz = 0.5 * z
        return hz + hz * jnp.tanh(hz)

    def sigmoid(z):
        return 0.5 * jnp.tanh(0.5 * z) + 0.5

    fl = _dot(h_ref[pl.ds(zero, TILE), :], wfl_ref[...]) + bf_ref[...]
    mk_raw = proj(C_MK, WIDTH)
    mq_raw = proj(C_MQ, WIDTH)
    fv = proj(C_FV, WIDTH)
    fz = proj(C_FZ, WIDTH)

    logf = jnp.minimum(fl, 0.0) - jnp.log(1.0 + jnp.exp(-jnp.abs(fl)))
    logf = jnp.where(lane < N_HEADS, logf, 0.0)
    l_hi, l_mid, l_lo = _split3(logf)
    packed = (l_hi + pltpu.roll(l_mid, 8, 1) + pltpu.roll(l_lo, 16, 1)).astype(BF16)
    cum = _dot(tri_ref[...], packed)
    fq = proj(C_FQ, WIDTH)
    fk = proj(C_FK, WIDTH)

    mk_tiles = [rope(y) for y in head_norm(mk_raw, gmk_ref)]
    row_id = lax.broadcasted_iota(jnp.int32, (LANES, LANES), 0)
    lane_sq = lax.broadcasted_iota(jnp.int32, (LANES, LANES), 1)
    for p, kr in enumerate(mk_tiles):
        km = jnp.mean(kr, axis=0, keepdims=True)
        hit = (((row_id == (2 * p) * GROUP + t) & (lane_sq < HEAD_DIM))
               | ((row_id == (2 * p + 1) * GROUP + t) & (lane_sq >= HEAD_DIM)))
        blk = kmt_ref[:, p * LANES:(p + 1) * LANES]
        kmt_ref[:, p * LANES:(p + 1) * LANES] = jnp.where(hit, km, blk)
    mk_ones = jnp.where((lane % HEAD_DIM == t) | (lane % HEAD_DIM == U_COL_MOBA), 1.0, 0.0)
    store_heads(k_out, N_HEADS, mk_tiles, lambda p: mk_ones)

    c = cum + pltpu.roll(cum, LANES - 8, 1) + pltpu.roll(cum, LANES - 16, 1)
    c = jnp.where(lane < N_HEADS, c, 0.0) + carry_ref[...]
    carry_ref[...] = c[TILE - 1:TILE, :]
    cend_out[0] = jnp.broadcast_to(c[TILE - 1:TILE, :], (8, LANES))
    c_hi, c_mid, c_lo = _split3(c * LOG2E)
    cparts = (c_hi + pltpu.roll(c_mid, 8, 1) + pltpu.roll(c_lo, 16, 1)
              + jnp.where(lane == 24, 1.0, 0.0)
              + jnp.where(lane == 25, misc_ref[1:2, :], 0.0)).astype(BF16)
    ex_fq = _dot(cparts, selfq_ref[...])
    ex_fk = _dot(cparts, selfk_ref[...])
    mv = proj(C_MV, WIDTH)
    mz = proj(C_MZ, WIDTH)

    v_ones = jnp.where(lane % HEAD_DIM == 0, 1.0, 0.0)
    store_heads(v_out, 0, split_tiles(fv), lambda p: v_ones)
    gz_out[:, 0:WIDTH] = silu(fz).astype(BF16)

    mq_tiles = [rope(y) for y in head_norm(mq_raw, gmq_ref)]
    q_full = jnp.concatenate(mq_tiles, axis=1)
    q_hi = q_full.astype(BF16)
    q_lo = (q_full - q_hi.astype(F32)).astype(BF16)
    kmt = kmt_ref[...]
    k_hi = kmt.astype(BF16)
    k_lo = (kmt - k_hi.astype(F32)).astype(BF16)
    gate = _dot_nt(q_hi, k_hi) + _dot_nt(q_hi, k_lo) + _dot_nt(q_lo, k_hi)
    gate_chunks = [(C_GA + half * WIDTH, half * WIDTH) for half in range(2)]
    gate_chunks += [(C_GB + half * WIDTH, D_MODEL + half * WIDTH) for half in range(2)]
    gate_raw = [proj(c0, WIDTH) for c0, _ in gate_chunks]

    store_heads(q_out, 0, head_norm(fq, gfq_ref), lambda p: ex_fq[:, p * LANES:(p + 1) * LANES])
    store_heads(k_out, 0, head_norm(fk, gfk_ref), lambda p: ex_fk[:, p * LANES:(p + 1) * LANES])

    blk_id = lane % GROUP
    past = blk_id < t
    g = jnp.where(past, gate, -jnp.inf)
    beaten = jnp.zeros((TILE, LANES), jnp.int32)
    for d in range(1, GROUP):
        lower = jnp.where(blk_id >= d, pltpu.roll(g, d, 1), -jnp.inf)
        upper = jnp.where(blk_id < GROUP - d, pltpu.roll(g, LANES - d, 1), -jnp.inf)
        beaten = beaten + jnp.where(lower >= g, 1, 0) + jnp.where(upper > g, 1, 0)
    keep = (past & (beaten < MOBA_TOPK)) | (blk_id == t)
    maskvals = jnp.where(keep, 0.0, -misc_ref[0:1, :]).astype(BF16)
    ex_mq = _dot(maskvals, selmq_ref[...])

    store_heads(v_out, N_HEADS, split_tiles(mv), lambda p: v_ones)
    gz_out[:, WIDTH:2 * WIDTH] = silu(mz).astype(BF16)

    is_u_lane = lane % HEAD_DIM == U_COL_MOBA
    store_heads(q_out, N_HEADS, mq_tiles,
                lambda p: jnp.where(is_u_lane, misc_ref[2:3, :], ex_mq[:, p * LANES:(p + 1) * LANES]))
    for raw, (_, o0) in zip(gate_raw, gate_chunks):
        gate_out[:, o0:o0 + WIDTH] = sigmoid(raw + bg_ref[:, o0:o0 + WIDTH]).astype(BF16)


def _attn_kernel(flag_ref, cend_ref, q_ref, k_ref, v_ref, gz_ref, o_ref, acc_ref, m_ref, *,
                 tiles_per_seq):
    b, group, i = pl.program_id(0), pl.program_id(1), pl.program_id(2)
    half = ATT_TILE // 2
    start = pl.multiple_of(i * ATT_TILE, ATT_TILE)
    heads = range(ATT_HEADS)

    def first_live_block(hh):
        base = (b * N_AUG_HEADS + ATT_HEADS * group + hh) * tiles_per_seq
        sub = ATT_TILE // TILE
        c_q = cend_ref[base + jnp.maximum(sub * i - 1, 0)]
        dead = jnp.int32(0)
        for j in range(tiles_per_seq // sub - 1):
            c_k = cend_ref[base + sub * j + sub - 1]
            dead += ((j < i) & (c_q - c_k < EXP2_ZERO_BELOW)).astype(jnp.int32)
        return dead
    row = lax.broadcasted_iota(jnp.int32, (half, ATT_TILE), 0)
    col = lax.broadcasted_iota(jnp.int32, (half, ATT_TILE), 1)
    mask_top = (lax.broadcasted_iota(jnp.int32, (half, half), 1)
                <= lax.broadcasted_iota(jnp.int32, (half, half), 0))
    mask_bot = col <= row + half

    def diag_scores(hh):
        kd = k_ref[0, hh, pl.ds(start, ATT_TILE), :]
        s_top = jnp.where(mask_top, _dot_nt(q_ref[0, hh, 0:half, :], kd[0:half]), -jnp.inf)
        s_bot = jnp.where(mask_bot, _dot_nt(q_ref[0, hh, half:ATT_TILE, :], kd), -jnp.inf)
        return s_top, s_bot

    def finish(accs):
        lane = lax.broadcasted_iota(jnp.int32, (ATT_TILE, LANES), 1)
        for pp in range(ATT_HEADS // 2):
            acc_e, acc_o = accs[2 * pp], accs[2 * pp + 1]
            o_e = acc_e * (1.0 / acc_e[:, HEAD_DIM:HEAD_DIM + 1])
            o_o = acc_o * (1.0 / acc_o[:, 0:1])
            o = jnp.where(lane < HEAD_DIM, o_e, o_o)
            cols = slice(pp * LANES, (pp + 1) * LANES)
            o_ref[:, cols] = (o * gz_ref[:, cols].astype(F32)).astype(BF16)

    @pl.when(flag_ref[0] == 1)
    def _fixed_stabilizer():
        acc_ref[...] = jnp.zeros_like(acc_ref)

        def body(j, carry):
            off = pl.multiple_of(j * ATT_TILE, ATT_TILE)
            for hh in heads:
                s = _dot_nt(q_ref[0, hh], k_ref[0, hh, pl.ds(off, ATT_TILE), :])
                acc_ref[hh] += _dot(jnp.exp2(s).astype(BF16),
                                    v_ref[0, hh, pl.ds(off, ATT_TILE), :])
            return carry

        first = first_live_block(0)
        for hh in heads[1:]:
            first = jnp.minimum(first, first_live_block(hh))
        lax.fori_loop(first, i, body, 0)

        accs = []
        for hh in heads:
            vd = v_ref[0, hh, pl.ds(start, ATT_TILE), :]
            s_top, s_bot = diag_scores(hh)
            top = acc_ref[hh, 0:half] + _dot(jnp.exp2(s_top).astype(BF16), vd[0:half])
            bot = acc_ref[hh, half:ATT_TILE] + _dot(jnp.exp2(s_bot).astype(BF16), vd)
            accs.append(jnp.concatenate([top, bot], axis=0))
        finish(accs)

    @pl.when(flag_ref[0] != 1)
    def _online():
        acc_ref[...] = jnp.zeros_like(acc_ref)
        m_ref[...] = jnp.full_like(m_ref, -jnp.inf)

        def update(s, m_old, acc_old, v):
            m_new = jnp.maximum(m_old, jnp.max(s, axis=-1, keepdims=True))
            pv = _dot(jnp.exp2(s - m_new).astype(BF16), v)
            return m_new, jnp.exp2(m_old - m_new) * acc_old + pv

        def body(j, carry):
            off = pl.multiple_of(j * ATT_TILE, ATT_TILE)
            for hh in heads:
                s = _dot_nt(q_ref[0, hh], k_ref[0, hh, pl.ds(off, ATT_TILE), :])
                m_ref[hh], acc_ref[hh] = update(s, m_ref[hh], acc_ref[hh],
                                                v_ref[0, hh, pl.ds(off, ATT_TILE), :])
            return carry

        lax.fori_loop(0, i, body, 0)

        accs = []
        for hh in heads:
            vd = v_ref[0, hh, pl.ds(start, ATT_TILE), :]
            s_top, s_bot = diag_scores(hh)
            _, top = update(s_top, m_ref[hh, 0:half], acc_ref[hh, 0:half], vd[0:half])
            _, bot = update(s_bot, m_ref[hh, half:ATT_TILE], acc_ref[hh, half:ATT_TILE], vd)
            accs.append(jnp.concatenate([top, bot], axis=0))
        finish(accs)


def _out_kernel(y_ref, gate_ref, x_ref, wf_ref, wm_ref, wo_ref, o_ref):
    slabs = [slice(c, c + MXU_COLS) for c in range(0, D_MODEL, MXU_COLS)]
    yf, ym = y_ref[:, 0:WIDTH], y_ref[:, WIDTH:2 * WIDTH]
    merged = []
    for cols in slabs:
        pf = _dot(yf, wf_ref[:, cols])
        pm = _dot(ym, wm_ref[:, cols])
        ga = gate_ref[:, cols].astype(F32)
        gb = gate_ref[:, D_MODEL + cols.start:D_MODEL + cols.stop].astype(F32)
        merged.append((ga * pf + gb * pm).astype(BF16))
    merged = jnp.concatenate(merged, axis=1)
    for cols in slabs:
        o_ref[:, cols] = x_ref[:, cols] + _dot(merged, wo_ref[:, cols])


def _const_spec(shape):
    return pl.BlockSpec(shape, lambda *_: (0,) * len(shape))


def _layer(x2, batch, seq, norm_g, w_in, b_f, b_gate, fox_q_g, fox_k_g, moba_q_g, moba_k_g,
           w_fox, w_moba, w_out):
    rows = batch * seq
    tiles_per_seq = seq // TILE
    n_tiles = rows // TILE
    scale = HEAD_DIM ** -0.5

    w = w_in.astype(BF16)
    w_fl = jnp.pad(w_in[:, W_COLS:], ((0, 0), (0, LANES - N_HEADS))).astype(BF16)
    ng = norm_g.reshape(1, D_MODEL)
    bf = jnp.pad(b_f, (0, LANES - N_HEADS)).reshape(1, LANES)
    bg = b_gate.reshape(1, 2 * D_MODEL)
    gfq = jnp.tile(fox_q_g * (scale * LOG2E), N_HEADS).reshape(1, WIDTH)
    gfk = jnp.tile(fox_k_g, N_HEADS).reshape(1, WIDTH)
    gmq = jnp.tile(moba_q_g * (scale * LOG2E), N_HEADS).reshape(1, WIDTH)
    gmk = jnp.tile(moba_k_g, N_HEADS).reshape(1, WIDTH)

    def logit_bound(gq, gk):
        return 8.0 * LOG2E * 1.02 * jnp.max(jnp.abs(gq)) * jnp.max(jnp.abs(gk))

    b_fox, b_moba = logit_bound(fox_q_g, fox_k_g), logit_bound(moba_q_g, moba_k_g)
    fixed_ok = jnp.maximum(b_fox, b_moba) <= FIXED_STABILIZER_MAX_BOUND

    def stabilizer(b):
        u = -(b * (1.0 + 2.0 ** -7)).astype(BF16).astype(F32)
        return jnp.where(fixed_ok, u, 0.0)

    big = jnp.exp2(jnp.ceil(jnp.log2(2.1 * b_moba + 160.0)))
    misc = jnp.zeros((8, LANES), F32)
    misc = misc.at[0].set(big).at[1].set(stabilizer(b_fox)).at[2].set(stabilizer(b_moba))
    flag = fixed_ok.astype(jnp.int32).reshape(1)

    sel_fq, sel_fk, sel_mq, tri = _routing_constants()
    rc, rsp, rsm = _rope_tables(seq)

    rope_spec = pl.BlockSpec((TILE, LANES), lambda g: (g % tiles_per_seq, 0))
    head_spec = pl.BlockSpec((1, N_AUG_HEADS, TILE, LANES),
                             lambda g: (g // tiles_per_seq, 0, g % tiles_per_seq, 0))
    qkv_shape = jax.ShapeDtypeStruct((batch, N_AUG_HEADS, seq, LANES), BF16)

    q_all, k_all, v_all, gz, gates, cend = pl.pallas_call(
        functools.partial(_proj_kernel, tiles_per_seq=tiles_per_seq),
        grid=(n_tiles,),
        in_specs=[
            pl.BlockSpec((TILE, D_MODEL), lambda g: (g, 0)),
            _const_spec(w_in.shape), _const_spec((D_MODEL, LANES)),
            _const_spec((1, D_MODEL)), _const_spec((1, LANES)), _const_spec((1, 2 * D_MODEL)),
            _const_spec((1, WIDTH)), _const_spec((1, WIDTH)),
            _const_spec((1, WIDTH)), _const_spec((1, WIDTH)),
            rope_spec, rope_spec, rope_spec,
            _const_spec((TILE, TILE)),
            _const_spec((LANES, WIDTH)), _const_spec((LANES, WIDTH)), _const_spec((LANES, WIDTH)),
            _const_spec((8, LANES)),
        ],
        out_specs=[
            head_spec, head_spec, head_spec,
            pl.BlockSpec((TILE, 2 * WIDTH), lambda g: (g, 0)),
            pl.BlockSpec((TILE, 2 * D_MODEL), lambda g: (g, 0)),
            pl.BlockSpec((1, 8, LANES), lambda g: (g, 0, 0)),
        ],
        out_shape=[
            qkv_shape, qkv_shape, qkv_shape,
            jax.ShapeDtypeStruct((rows, 2 * WIDTH), BF16),
            jax.ShapeDtypeStruct((rows, 2 * D_MODEL), BF16),
            jax.ShapeDtypeStruct((n_tiles, 8, LANES), F32),
        ],
        scratch_shapes=[pltpu.VMEM((1, LANES), F32),
                        pltpu.VMEM((LANES, WIDTH), F32),
                        pltpu.VMEM((TILE, D_MODEL), BF16)],
        compiler_params=pltpu.CompilerParams(
            dimension_semantics=("arbitrary",), vmem_limit_bytes=VMEM_LIMIT),
        name="proj_epilogue",
    )(x2, w, w_fl, ng, bf, bg, gfq, gfk, gmq, gmk,
      jnp.asarray(rc), jnp.asarray(rsp), jnp.asarray(rsm),
      jnp.asarray(tri, BF16), jnp.asarray(sel_fq, BF16), jnp.asarray(sel_fk, BF16),
      jnp.asarray(sel_mq, BF16), misc)

    n_groups = N_AUG_HEADS // ATT_HEADS
    group_cols = ATT_HEADS // 2 * LANES
    att_tiles = seq // ATT_TILE
    cend = cend[:, 0, :N_HEADS].reshape(batch, tiles_per_seq, N_HEADS).transpose(0, 2, 1) * LOG2E
    cend = jnp.concatenate([cend, jnp.zeros_like(cend)], axis=1).reshape(-1)
    kv_spec = pl.BlockSpec((1, ATT_HEADS, seq, LANES), lambda b, g, i, *_: (b, g, 0, 0))
    row_spec = pl.BlockSpec((ATT_TILE, group_cols), lambda b, g, i, *_: (b * att_tiles + i, g))
    y = pl.pallas_call(
        functools.partial(_attn_kernel, tiles_per_seq=tiles_per_seq),
        grid_spec=pltpu.PrefetchScalarGridSpec(
            num_scalar_prefetch=2,
            grid=(batch, n_groups, att_tiles),
            in_specs=[
                pl.BlockSpec((1, ATT_HEADS, ATT_TILE, LANES), lambda b, g, i, *_: (b, g, i, 0)),
                kv_spec, kv_spec, row_spec,
            ],
            out_specs=row_spec,
            scratch_shapes=[pltpu.VMEM((ATT_HEADS, ATT_TILE, LANES), F32),
                            pltpu.VMEM((ATT_HEADS, ATT_TILE, 1), F32)],
        ),
        out_shape=jax.ShapeDtypeStruct((rows, 2 * WIDTH), BF16),
        compiler_params=pltpu.CompilerParams(
            dimension_semantics=("arbitrary", "arbitrary", "arbitrary"),
            vmem_limit_bytes=VMEM_LIMIT),
        name="flash_attn",
    )(flag, cend, q_all, k_all, v_all, gz)

    out = pl.pallas_call(
        _out_kernel,
        grid=(rows // OUT_TILE,),
        in_specs=[
            pl.BlockSpec((OUT_TILE, 2 * WIDTH), lambda g: (g, 0)),
            pl.BlockSpec((OUT_TILE, 2 * D_MODEL), lambda g: (g, 0)),
            pl.BlockSpec((OUT_TILE, D_MODEL), lambda g: (g, 0)),
            _const_spec((WIDTH, D_MODEL)), _const_spec((WIDTH, D_MODEL)),
            _const_spec((D_MODEL, D_MODEL)),
        ],
        out_specs=pl.BlockSpec((OUT_TILE, D_MODEL), lambda g: (g, 0)),
        out_shape=jax.ShapeDtypeStruct((rows, D_MODEL), F32),
        compiler_params=pltpu.CompilerParams(
            dimension_semantics=("arbitrary",), vmem_limit_bytes=VMEM_LIMIT),
        name="merge_out",
    )(y, gates, x2, w_fox.astype(BF16), w_moba.astype(BF16), w_out.astype(BF16))
    return out


def kernel(x, norm_g, w_in, b_f, b_gate, fox_q_g, fox_k_g, moba_q_g, moba_k_g, w_fox, w_moba, w_out):
    batch, seq, d_model = x.shape
    assert d_model == D_MODEL and seq % ATT_TILE == 0 and seq // TILE <= GROUP
    x2 = x.reshape(batch * seq, D_MODEL)
    for layer in range(norm_g.shape[0]):
        x2 = _layer(x2, batch, seq, norm_g[layer], w_in[layer], b_f[layer], b_gate[layer],
                    fox_q_g[layer], fox_k_g[layer], moba_q_g[layer], moba_k_g[layer],
                    w_fox[layer], w_moba[layer], w_out[layer])
    return x2.reshape(batch, seq, D_MODEL)
```

```python
import functools

import numpy as np
import jax
import jax.numpy as jnp
from jax import lax
from jax.experimental import pallas as pl
from jax.experimental.pallas import tpu as pltpu

D_MODEL = 1024
HEAD_DIM = 64
N_HEADS = 8
WIDTH = N_HEADS * HEAD_DIM
ROPE_DIM = HEAD_DIM // 4
ROPE_HALF = ROPE_DIM // 2
ROPE_THETA = 500000.0
MOBA_BLOCK = 256
MOBA_TOPK = 3
RMS_EPS = 1e-6

LANES = 128
MXU_COLS = 256
TILE = 256
SUB = TILE // MOBA_BLOCK
OUT_TILE = 1024
ATT_TILE = 512
ATT_HEADS = 8
FIXED_STABILIZER_MAX_BOUND = 55.0
EXP2_ZERO_BELOW = -152.0
GROUP = 16
N_AUG_HEADS = 2 * N_HEADS
U_COL_FOX = 6
U_COL_MOBA = GROUP
LOG2E = 1.4426950408889634

C_FQ, C_FK, C_FV, C_FZ = 0, 512, 1024, 1536
C_MQ, C_MK, C_MV, C_MZ = 2048, 2560, 3072, 3584
C_GA, C_GB = 4096, 5120
W_COLS = 6144

VMEM_LIMIT = 52 * 1024 * 1024

F32 = jnp.float32
BF16 = jnp.bfloat16


def _dot(a, b):
    return jnp.dot(a, b, preferred_element_type=F32)


def _dot_nt(a, b):
    return lax.dot_general(a, b, (((1,), (1,)), ((), ())), preferred_element_type=F32)


def _split3(v):
    hi = v.astype(BF16).astype(F32)
    r = v - hi
    mid = r.astype(BF16).astype(F32)
    lo = r - mid
    return hi, mid, lo


def _extra_base(h):
    return (h // 2) * LANES + (HEAD_DIM if h % 2 == 0 else 0)


def _routing_constants():
    sel_fq = np.zeros((LANES, WIDTH), np.float32)
    sel_fk = np.zeros((LANES, WIDTH), np.float32)
    sel_mq = np.zeros((LANES, WIDTH), np.float32)
    for h in range(N_HEADS):
        base = _extra_base(h)
        for part in range(3):
            sel_fq[part * 8 + h, base + part] = 1.0
            sel_fq[24, base + 3 + part] = 1.0
            sel_fk[24, base + part] = 1.0
            sel_fk[part * 8 + h, base + 3 + part] = -1.0
        sel_fq[25, base + U_COL_FOX] = 1.0
        sel_fk[24, base + U_COL_FOX] = 1.0
        for n in range(GROUP):
            sel_mq[h * GROUP + n, base + n] = 1.0
    tri = np.tril(np.ones((TILE, TILE), np.float32))
    return sel_fq, sel_fk, sel_mq, tri


def _rope_tables(seq):
    inv_freq = ROPE_THETA ** (-np.arange(0, ROPE_HALF, dtype=np.float32) * 2.0 / ROPE_DIM)
    ang = np.arange(seq, dtype=np.float32)[:, None] * inv_freq[None, :].astype(np.float32)
    cos, sin = np.cos(ang).astype(np.float32), np.sin(ang).astype(np.float32)
    rc = np.ones((seq, LANES), np.float32)
    rsp = np.zeros((seq, LANES), np.float32)
    rsm = np.zeros((seq, LANES), np.float32)
    for off in (0, HEAD_DIM):
        rc[:, off:off + ROPE_HALF] = cos
        rc[:, off + ROPE_HALF:off + ROPE_DIM] = cos
        rsm[:, off:off + ROPE_HALF] = -sin
        rsp[:, off + ROPE_HALF:off + ROPE_DIM] = sin
    return rc, rsp, rsm


def _proj_kernel(x_ref, w_ref, wfl_ref, prm_ref,
                 rc_ref, rsp_ref, rsm_ref, tri_ref, selfq_ref, selfk_ref, selmq_ref,
                 q_out, k_out, v_out, gz_out, gate_out, cend_out,
                 carry_ref, kmt_ref, h_ref, *, tiles_per_seq):
    tile_in_seq = pl.program_id(0) % tiles_per_seq
    t0 = tile_in_seq * SUB

    ng_ref = prm_ref.at[0:1, 0:D_MODEL]
    bg_ref = prm_ref.at[1:2, :]
    gfq_ref, gfk_ref, gmq_ref, gmk_ref = (
        prm_ref.at[2:3, n * WIDTH:(n + 1) * WIDTH] for n in range(4))
    bf_ref, big_ref, ufox_ref, umoba_ref = (
        prm_ref.at[3:4, n * LANES:(n + 1) * LANES] for n in range(4))

    @pl.when(tile_in_seq == 0)
    def _():
        carry_ref[...] = jnp.zeros_like(carry_ref)
        kmt_ref[...] = jnp.zeros_like(kmt_ref)

    lane = lax.broadcasted_iota(jnp.int32, (TILE, LANES), 1)
    low_half = lane < HEAD_DIM
    own = t0 + lax.broadcasted_iota(jnp.int32, (TILE, LANES), 0) // MOBA_BLOCK

    x = x_ref[...]
    ms = jnp.mean(x * x, axis=-1, keepdims=True)
    h_ref[...] = (x * lax.rsqrt(ms + RMS_EPS) * ng_ref[...]).astype(BF16)
    zero = pl.multiple_of(jnp.minimum(pl.program_id(0), 0) * TILE, TILE)

    def proj(c0, width):
        slabs = [_dot(h_ref[pl.ds(zero, TILE), :], w_ref[:, c:c + min(MXU_COLS, c0 + width - c)])
                 for c in range(c0, c0 + width, MXU_COLS)]
        return slabs[0] if len(slabs) == 1 else jnp.concatenate(slabs, axis=1)

    def head_norm(a, g_ref):
        tiles = []
        for p in range(WIDTH // LANES):
            ap = a[:, p * LANES:(p + 1) * LANES]
            sq = ap * ap
            s_lo = jnp.sum(jnp.where(low_half, sq, 0.0), axis=-1, keepdims=True)
            s_hi = jnp.sum(jnp.where(low_half, 0.0, sq), axis=-1, keepdims=True)
            inv_lo = lax.rsqrt(s_lo * (1.0 / HEAD_DIM) + RMS_EPS)
            inv_hi = lax.rsqrt(s_hi * (1.0 / HEAD_DIM) + RMS_EPS)
            scale = jnp.where(low_half, inv_lo, inv_hi)
            tiles.append(ap * scale * g_ref[:, p * LANES:(p + 1) * LANES])
        return tiles

    def rope(y):
        return (y * rc_ref[...] + pltpu.roll(y, ROPE_HALF, 1) * rsp_ref[...]
                + pltpu.roll(y, LANES - ROPE_HALF, 1) * rsm_ref[...])

    def split_tiles(a):
        return [a[:, p * LANES:(p + 1) * LANES] for p in range(WIDTH // LANES)]

    def store_heads(out_ref, head0, tiles, extras):
        for p, y in enumerate(tiles):
            e = extras(p)
            out_ref[0, head0 + 2 * p] = jnp.where(low_half, y, e).astype(BF16)
            out_ref[0, head0 + 2 * p + 1] = jnp.where(low_half, e, y).astype(BF16)

    def silu_of_half(hz):
        return hz + hz * jnp.tanh(hz)

    fl = _dot(h_ref[pl.ds(zero, TILE), :], wfl_ref[...]) + bf_ref[...]
    mk_raw = proj(C_MK, WIDTH)
    mq_raw = proj(C_MQ, WIDTH)
    fv = proj(C_FV, WIDTH)
    fz = proj(C_FZ, WIDTH)

    logf = jnp.minimum(fl, 0.0) - jnp.log(1.0 + jnp.exp(-jnp.abs(fl)))
    logf = jnp.where(lane < N_HEADS, logf, 0.0)
    l_hi, l_mid, l_lo = _split3(logf)
    packed = (l_hi + pltpu.roll(l_mid, 8, 1) + pltpu.roll(l_lo, 16, 1)).astype(BF16)
    cum = _dot(tri_ref[...], packed)
    fq = proj(C_FQ, WIDTH)
    fk = proj(C_FK, WIDTH)

    mk_tiles = [rope(y) for y in head_norm(mk_raw, gmk_ref)]
    row_id = lax.broadcasted_iota(jnp.int32, (LANES, LANES), 0)
    lane_sq = lax.broadcasted_iota(jnp.int32, (LANES, LANES), 1)
    for p, kr in enumerate(mk_tiles):
        blk = kmt_ref[:, p * LANES:(p + 1) * LANES]
        for sub in range(SUB):
            km = jnp.mean(kr[sub * MOBA_BLOCK:(sub + 1) * MOBA_BLOCK], axis=0, keepdims=True)
            t = t0 + sub
            hit = (((row_id == (2 * p) * GROUP + t) & (lane_sq < HEAD_DIM))
                   | ((row_id == (2 * p + 1) * GROUP + t) & (lane_sq >= HEAD_DIM)))
            blk = jnp.where(hit, km, blk)
        kmt_ref[:, p * LANES:(p + 1) * LANES] = blk
    mk_ones = jnp.where((lane % HEAD_DIM == own) | (lane % HEAD_DIM == U_COL_MOBA), 1.0, 0.0)
    store_heads(k_out, N_HEADS, mk_tiles, lambda p: mk_ones)

    c = cum + pltpu.roll(cum, LANES - 8, 1) + pltpu.roll(cum, LANES - 16, 1)
    c = jnp.where(lane < N_HEADS, c, 0.0) + carry_ref[...]
    carry_ref[...] = c[TILE - 1:TILE, :]
    for sub in range(SUB):
        last = (sub + 1) * MOBA_BLOCK - 1
        cend_out[0, sub * (8 // SUB):(sub + 1) * (8 // SUB)] = jnp.broadcast_to(
            c[last:last + 1, :], (8 // SUB, LANES))
    c_hi, c_mid, c_lo = _split3(c * LOG2E)
    cparts = (c_hi + pltpu.roll(c_mid, 8, 1) + pltpu.roll(c_lo, 16, 1)
              + jnp.where(lane == 24, 1.0, 0.0)
              + jnp.where(lane == 25, ufox_ref[...], 0.0)).astype(BF16)
    ex_fq = _dot(cparts, selfq_ref[...])
    ex_fk = _dot(cparts, selfk_ref[...])
    mv = proj(C_MV, WIDTH)
    mz = proj(C_MZ, WIDTH)

    v_ones = jnp.where(lane % HEAD_DIM == 0, 1.0, 0.0)
    store_heads(v_out, 0, split_tiles(fv), lambda p: v_ones)
    gz_out[:, 0:WIDTH] = silu_of_half(fz).astype(BF16)

    mq_tiles = [rope(y) for y in head_norm(mq_raw, gmq_ref)]
    q_full = jnp.concatenate(mq_tiles, axis=1)
    q_hi = q_full.astype(BF16)
    q_lo = (q_full - q_hi.astype(F32)).astype(BF16)
    kmt = kmt_ref[...]
    k_hi = kmt.astype(BF16)
    k_lo = (kmt - k_hi.astype(F32)).astype(BF16)
    gate = _dot_nt(q_hi, k_hi) + _dot_nt(q_hi, k_lo) + _dot_nt(q_lo, k_hi)
    gate_chunks = [(C_GA + half * WIDTH, half * WIDTH) for half in range(2)]
    gate_chunks += [(C_GB + half * WIDTH, D_MODEL + half * WIDTH) for half in range(2)]
    gate_raw = [proj(c0, WIDTH) for c0, _ in gate_chunks]

    store_heads(q_out, 0, head_norm(fq, gfq_ref), lambda p: ex_fq[:, p * LANES:(p + 1) * LANES])
    store_heads(k_out, 0, head_norm(fk, gfk_ref), lambda p: ex_fk[:, p * LANES:(p + 1) * LANES])

    blk_id = lane % GROUP
    past = blk_id < own
    g = jnp.where(past, gate, -jnp.inf)
    beaten = jnp.zeros((TILE, LANES), jnp.int32)
    for d in range(1, GROUP):
        lower = jnp.where(blk_id >= d, pltpu.roll(g, d, 1), -jnp.inf)
        upper = jnp.where(blk_id < GROUP - d, pltpu.roll(g, LANES - d, 1), -jnp.inf)
        beaten = beaten + jnp.where(lower >= g, 1, 0) + jnp.where(upper > g, 1, 0)
    keep = (past & (beaten < MOBA_TOPK)) | (blk_id == own)
    maskvals = jnp.where(keep, 0.0, -big_ref[...]).astype(BF16)
    ex_mq = _dot(maskvals, selmq_ref[...])

    store_heads(v_out, N_HEADS, split_tiles(mv), lambda p: v_ones)
    gz_out[:, WIDTH:2 * WIDTH] = silu_of_half(mz).astype(BF16)

    is_u_lane = lane % HEAD_DIM == U_COL_MOBA
    store_heads(q_out, N_HEADS, mq_tiles,
                lambda p: jnp.where(is_u_lane, umoba_ref[...], ex_mq[:, p * LANES:(p + 1) * LANES]))
    for raw, (_, o0) in zip(gate_raw, gate_chunks):
        gate_out[:, o0:o0 + WIDTH] = jnp.tanh(raw + bg_ref[:, o0:o0 + WIDTH]).astype(BF16)


def _attn_kernel(flag_ref, cend_ref, q_ref, k_ref, v_ref, gz_ref, o_ref, acc_ref, m_ref, *,
                 blocks_per_seq):
    b, group, i = pl.program_id(0), pl.program_id(1), pl.program_id(2)
    half = ATT_TILE // 2
    start = pl.multiple_of(i * ATT_TILE, ATT_TILE)
    heads = range(ATT_HEADS)

    def first_live_block(hh):
        base = (b * N_AUG_HEADS + ATT_HEADS * group + hh) * blocks_per_seq
        sub = ATT_TILE // MOBA_BLOCK
        c_q = cend_ref[base + jnp.maximum(sub * i - 1, 0)]
        dead = jnp.int32(0)
        for j in range(blocks_per_seq // sub - 1):
            c_k = cend_ref[base + sub * j + sub - 1]
            dead += ((j < i) & (c_q - c_k < EXP2_ZERO_BELOW)).astype(jnp.int32)
        return dead
    row = lax.broadcasted_iota(jnp.int32, (half, ATT_TILE), 0)
    col = lax.broadcasted_iota(jnp.int32, (half, ATT_TILE), 1)
    mask_top = (lax.broadcasted_iota(jnp.int32, (half, half), 1)
                <= lax.broadcasted_iota(jnp.int32, (half, half), 0))
    mask_bot = col <= row + half

    def diag_scores(hh):
        kd = k_ref[0, hh, pl.ds(start, ATT_TILE), :]
        s_top = jnp.where(mask_top, _dot_nt(q_ref[0, hh, 0:half, :], kd[0:half]), -jnp.inf)
        s_bot = jnp.where(mask_bot, _dot_nt(q_ref[0, hh, half:ATT_TILE, :], kd), -jnp.inf)
        return s_top, s_bot

    def finish(accs):
        lane = lax.broadcasted_iota(jnp.int32, (ATT_TILE, LANES), 1)
        for pp in range(ATT_HEADS // 2):
            acc_e, acc_o = accs[2 * pp], accs[2 * pp + 1]
            o_e = acc_e * (1.0 / acc_e[:, HEAD_DIM:HEAD_DIM + 1])
            o_o = acc_o * (1.0 / acc_o[:, 0:1])
            o = jnp.where(lane < HEAD_DIM, o_e, o_o)
            cols = slice(pp * LANES, (pp + 1) * LANES)
            o_ref[:, cols] = (o * gz_ref[:, cols].astype(F32)).astype(BF16)

    @pl.when(flag_ref[0] == 1)
    def _fixed_stabilizer():
        acc_ref[...] = jnp.zeros_like(acc_ref)

        def body(j, carry):
            off = pl.multiple_of(j * ATT_TILE, ATT_TILE)
            for hh in heads:
                s = _dot_nt(q_ref[0, hh], k_ref[0, hh, pl.ds(off, ATT_TILE), :])
                acc_ref[hh] += _dot(jnp.exp2(s).astype(BF16),
                                    v_ref[0, hh, pl.ds(off, ATT_TILE), :])
            return carry

        first = first_live_block(0)
        for hh in heads[1:]:
            first = jnp.minimum(first, first_live_block(hh))
        lax.fori_loop(first, i, body, 0)

        accs = []
        for hh in heads:
            vd = v_ref[0, hh, pl.ds(start, ATT_TILE), :]
            s_top, s_bot = diag_scores(hh)
            top = acc_ref[hh, 0:half] + _dot(jnp.exp2(s_top).astype(BF16), vd[0:half])
            bot = acc_ref[hh, half:ATT_TILE] + _dot(jnp.exp2(s_bot).astype(BF16), vd)
            accs.append(jnp.concatenate([top, bot], axis=0))
        finish(accs)

    @pl.when(flag_ref[0] != 1)
    def _online():
        acc_ref[...] = jnp.zeros_like(acc_ref)
        m_ref[...] = jnp.full_like(m_ref, -jnp.inf)

        def update(s, m_old, acc_old, v):
            m_new = jnp.maximum(m_old, jnp.max(s, axis=-1, keepdims=True))
            pv = _dot(jnp.exp2(s - m_new).astype(BF16), v)
            return m_new, jnp.exp2(m_old - m_new) * acc_old + pv

        def body(j, carry):
            off = pl.multiple_of(j * ATT_TILE, ATT_TILE)
            for hh in heads:
                s = _dot_nt(q_ref[0, hh], k_ref[0, hh, pl.ds(off, ATT_TILE), :])
                m_ref[hh], acc_ref[hh] = update(s, m_ref[hh], acc_ref[hh],
                                                v_ref[0, hh, pl.ds(off, ATT_TILE), :])
            return carry

        lax.fori_loop(0, i, body, 0)

        accs = []
        for hh in heads:
            vd = v_ref[0, hh, pl.ds(start, ATT_TILE), :]
            s_top, s_bot = diag_scores(hh)
            _, top = update(s_top, m_ref[hh, 0:half], acc_ref[hh, 0:half], vd[0:half])
            _, bot = update(s_bot, m_ref[hh, half:ATT_TILE], acc_ref[hh, half:ATT_TILE], vd)
            accs.append(jnp.concatenate([top, bot], axis=0))
        finish(accs)


def _out_kernel(y_ref, gate_ref, x_ref, wf_ref, wm_ref, wo_ref, o_ref):
    slabs = [slice(c, c + MXU_COLS) for c in range(0, D_MODEL, MXU_COLS)]
    yf, ym = y_ref[:, 0:WIDTH], y_ref[:, WIDTH:2 * WIDTH]
    merged = []
    for cols in slabs:
        pf = _dot(yf, wf_ref[:, cols])
        pm = _dot(ym, wm_ref[:, cols])
        ta = gate_ref[:, cols].astype(F32)
        tb = gate_ref[:, D_MODEL + cols.start:D_MODEL + cols.stop].astype(F32)
        merged.append((0.5 * ((pf + ta * pf) + (pm + tb * pm))).astype(BF16))
    merged = jnp.concatenate(merged, axis=1)
    for cols in slabs:
        o_ref[:, cols] = x_ref[:, cols] + _dot(merged, wo_ref[:, cols])


def _const_spec(shape):
    return pl.BlockSpec(shape, lambda *_: (0,) * len(shape))


def _layer(x2, batch, seq, norm_g, w_in, b_f, b_gate, fox_q_g, fox_k_g, moba_q_g, moba_k_g,
           w_fox, w_moba, w_out):
    rows = batch * seq
    tiles_per_seq = seq // TILE
    n_tiles = rows // TILE
    scale = HEAD_DIM ** -0.5

    col_scale = np.ones((w_in.shape[1],), np.float32)
    for c0, width in ((C_FZ, WIDTH), (C_MZ, WIDTH), (C_GA, D_MODEL), (C_GB, D_MODEL)):
        col_scale[c0:c0 + width] = 0.5
    w = (w_in * col_scale).astype(BF16)
    w_fl = jnp.pad(w_in[:, W_COLS:], ((0, 0), (0, LANES - N_HEADS))).astype(BF16)

    def logit_bound(gq, gk):
        return 8.0 * LOG2E * 1.02 * jnp.max(jnp.abs(gq)) * jnp.max(jnp.abs(gk))

    b_fox, b_moba = logit_bound(fox_q_g, fox_k_g), logit_bound(moba_q_g, moba_k_g)
    fixed_ok = jnp.maximum(b_fox, b_moba) <= FIXED_STABILIZER_MAX_BOUND

    def stabilizer(b):
        u = -(b * (1.0 + 2.0 ** -7)).astype(BF16).astype(F32)
        return jnp.where(fixed_ok, u, 0.0)

    big = jnp.exp2(jnp.ceil(jnp.log2(2.1 * b_moba + 160.0)))
    flag = fixed_ok.astype(jnp.int32).reshape(1)

    def row(*pieces):
        r = jnp.concatenate([jnp.ravel(p).astype(F32) for p in pieces])
        return jnp.pad(r, (0, 2 * D_MODEL - r.shape[0]))

    def lanes(v):
        return jnp.full((LANES,), v, F32)

    q_scale = scale * LOG2E
    params = jnp.stack([
        row(norm_g),
        row(0.5 * b_gate),
        row(jnp.tile(fox_q_g * q_scale, N_HEADS), jnp.tile(fox_k_g, N_HEADS),
            jnp.tile(moba_q_g * q_scale, N_HEADS), jnp.tile(moba_k_g, N_HEADS)),
        row(b_f, jnp.zeros(LANES - N_HEADS), lanes(big), lanes(stabilizer(b_fox)),
            lanes(stabilizer(b_moba))),
    ] + [jnp.zeros(2 * D_MODEL, F32)] * 4)

    sel_fq, sel_fk, sel_mq, tri = _routing_constants()
    rc, rsp, rsm = _rope_tables(seq)

    rope_spec = pl.BlockSpec((TILE, LANES), lambda g: (g % tiles_per_seq, 0))
    head_spec = pl.BlockSpec((1, N_AUG_HEADS, TILE, LANES),
                             lambda g: (g // tiles_per_seq, 0, g % tiles_per_seq, 0))
    qkv_shape = jax.ShapeDtypeStruct((batch, N_AUG_HEADS, seq, LANES), BF16)

    q_all, k_all, v_all, gz, gates, cend = pl.pallas_call(
        functools.partial(_proj_kernel, tiles_per_seq=tiles_per_seq),
        grid=(n_tiles,),
        in_specs=[
            pl.BlockSpec((TILE, D_MODEL), lambda g: (g, 0)),
            pl.BlockSpec(w_in.shape, lambda g: (0, 0), pipeline_mode=pl.Buffered(1)),
            _const_spec((D_MODEL, LANES)),
            _const_spec((8, 2 * D_MODEL)),
            rope_spec, rope_spec, rope_spec,
            _const_spec((TILE, TILE)),
            _const_spec((LANES, WIDTH)), _const_spec((LANES, WIDTH)), _const_spec((LANES, WIDTH)),
        ],
        out_specs=[
            head_spec, head_spec, head_spec,
            pl.BlockSpec((TILE, 2 * WIDTH), lambda g: (g, 0)),
            pl.BlockSpec((TILE, 2 * D_MODEL), lambda g: (g, 0)),
            pl.BlockSpec((1, 8, LANES), lambda g: (g, 0, 0)),
        ],
        out_shape=[
            qkv_shape, qkv_shape, qkv_shape,
            jax.ShapeDtypeStruct((rows, 2 * WIDTH), BF16),
            jax.ShapeDtypeStruct((rows, 2 * D_MODEL), BF16),
            jax.ShapeDtypeStruct((n_tiles, 8, LANES), F32),
        ],
        scratch_shapes=[pltpu.VMEM((1, LANES), F32),
                        pltpu.VMEM((LANES, WIDTH), F32),
                        pltpu.VMEM((TILE, D_MODEL), BF16)],
        compiler_params=pltpu.CompilerParams(
            dimension_semantics=("arbitrary",), vmem_limit_bytes=VMEM_LIMIT),
        name="proj_epilogue",
    )(x2, w, w_fl, params,
      jnp.asarray(rc), jnp.asarray(rsp), jnp.asarray(rsm),
      jnp.asarray(tri, BF16), jnp.asarray(sel_fq, BF16), jnp.asarray(sel_fk, BF16),
      jnp.asarray(sel_mq, BF16))

    n_groups = N_AUG_HEADS // ATT_HEADS
    group_cols = ATT_HEADS // 2 * LANES
    att_tiles = seq // ATT_TILE
    blocks_per_seq = seq // MOBA_BLOCK
    cend = cend[:, ::8 // SUB, :N_HEADS].reshape(batch, blocks_per_seq, N_HEADS).transpose(0, 2, 1) * LOG2E
    cend = jnp.concatenate([cend, jnp.zeros_like(cend)], axis=1).reshape(-1)
    kv_spec = pl.BlockSpec((1, ATT_HEADS, seq, LANES), lambda b, g, i, *_: (b, g, 0, 0))
    row_spec = pl.BlockSpec((ATT_TILE, group_cols), lambda b, g, i, *_: (b * att_tiles + i, g))
    y = pl.pallas_call(
        functools.partial(_attn_kernel, blocks_per_seq=blocks_per_seq),
        grid_spec=pltpu.PrefetchScalarGridSpec(
            num_scalar_prefetch=2,
            grid=(batch, n_groups, att_tiles),
            in_specs=[
                pl.BlockSpec((1, ATT_HEADS, ATT_TILE, LANES), lambda b, g, i, *_: (b, g, i, 0)),
                kv_spec, kv_spec, row_spec,
            ],
            out_specs=row_spec,
            scratch_shapes=[pltpu.VMEM((ATT_HEADS, ATT_TILE, LANES), F32),
                            pltpu.VMEM((ATT_HEADS, ATT_TILE, 1), F32)],
        ),
        out_shape=jax.ShapeDtypeStruct((rows, 2 * WIDTH), BF16),
        compiler_params=pltpu.CompilerParams(
            dimension_semantics=("arbitrary", "arbitrary", "arbitrary"),
            vmem_limit_bytes=VMEM_LIMIT),
        name="flash_attn",
    )(flag, cend, q_all, k_all, v_all, gz)

    out = pl.pallas_call(
        _out_kernel,
        grid=(rows // OUT_TILE,),
        in_specs=[
            pl.BlockSpec((OUT_TILE, 2 * WIDTH), lambda g: (g, 0)),
            pl.BlockSpec((OUT_TILE, 2 * D_MODEL), lambda g: (g, 0)),
            pl.BlockSpec((OUT_TILE, D_MODEL), lambda g: (g, 0)),
            _const_spec((WIDTH, D_MODEL)), _const_spec((WIDTH, D_MODEL)),
            _const_spec((D_MODEL, D_MODEL)),
        ],
        out_specs=pl.BlockSpec((OUT_TILE, D_MODEL), lambda g: (g, 0)),
        out_shape=jax.ShapeDtypeStruct((rows, D_MODEL), F32),
        compiler_params=pltpu.CompilerParams(
            dimension_semantics=("arbitrary",), vmem_limit_bytes=VMEM_LIMIT),
        name="merge_out",
    )(y, gates, x2, w_fox.astype(BF16), w_moba.astype(BF16), w_out.astype(BF16))
    return out


def kernel(x, norm_g, w_in, b_f, b_gate, fox_q_g, fox_k_g, moba_q_g, moba_k_g, w_fox, w_moba, w_out):
    batch, seq, d_model = x.shape
    assert d_model == D_MODEL and seq % ATT_TILE == 0 and seq % TILE == 0
    assert seq // MOBA_BLOCK <= GROUP
    x2 = x.reshape(batch * seq, D_MODEL)
    for layer in range(norm_g.shape[0]):
        x2 = _layer(x2, batch, seq, norm_g[layer], w_in[layer], b_f[layer], b_gate[layer],
                    fox_q_g[layer], fox_k_g[layer], moba_q_g[layer], moba_k_g[layer],
                    w_fox[layer], w_moba[layer], w_out[layer])
    return x2.reshape(batch, seq, D_MODEL)
```

```python
import functools

import numpy as np
import jax
import jax.numpy as jnp
from jax import lax
from jax.experimental import pallas as pl
from jax.experimental.pallas import tpu as pltpu

D_MODEL = 1024
HEAD_DIM = 64
N_HEADS = 8
WIDTH = N_HEADS * HEAD_DIM
ROPE_DIM = HEAD_DIM // 4
ROPE_HALF = ROPE_DIM // 2
ROPE_THETA = 500000.0
MOBA_BLOCK = 256
MOBA_TOPK = 3
RMS_EPS = 1e-6

LANES = 128
MXU_COLS = 256
TILE = 256
SUB = TILE // MOBA_BLOCK
OUT_TILE = 1024
ATT_TILE = 512
ATT_HEADS = 8
ATT_STEP_TILES = 2
FIXED_STABILIZER_MAX_BOUND = 55.0
EXP2_ZERO_BELOW = -152.0
GROUP = 16
N_AUG_HEADS = 2 * N_HEADS
U_COL_FOX = 6
U_COL_MOBA = GROUP
LOG2E = 1.4426950408889634

C_FQ, C_FK, C_FV, C_FZ = 0, 512, 1024, 1536
C_MQ, C_MK, C_MV, C_MZ = 2048, 2560, 3072, 3584
C_GA, C_GB = 4096, 5120
W_COLS = 6144

VMEM_LIMIT = 52 * 1024 * 1024

F32 = jnp.float32
BF16 = jnp.bfloat16


def _dot(a, b):
    return jnp.dot(a, b, preferred_element_type=F32)


def _dot_nt(a, b):
    return lax.dot_general(a, b, (((1,), (1,)), ((), ())), preferred_element_type=F32)


def _split3(v):
    hi = v.astype(BF16).astype(F32)
    r = v - hi
    mid = r.astype(BF16).astype(F32)
    lo = r - mid
    return hi, mid, lo


def _extra_base(h):
    return (h // 2) * LANES + (HEAD_DIM if h % 2 == 0 else 0)


def _routing_constants():
    sel_fq = np.zeros((LANES, WIDTH), np.float32)
    sel_fk = np.zeros((LANES, WIDTH), np.float32)
    sel_mq = np.zeros((LANES, WIDTH), np.float32)
    for h in range(N_HEADS):
        base = _extra_base(h)
        for part in range(3):
            sel_fq[part * 8 + h, base + part] = 1.0
            sel_fq[24, base + 3 + part] = 1.0
            sel_fk[24, base + part] = 1.0
            sel_fk[part * 8 + h, base + 3 + part] = -1.0
        sel_fq[25, base + U_COL_FOX] = 1.0
        sel_fk[24, base + U_COL_FOX] = 1.0
        for n in range(GROUP):
            sel_mq[h * GROUP + n, base + n] = 1.0
    tri = np.tril(np.ones((TILE, TILE), np.float32))
    return sel_fq, sel_fk, sel_mq, tri


def _rope_tables(seq):
    inv_freq = ROPE_THETA ** (-np.arange(0, ROPE_HALF, dtype=np.float32) * 2.0 / ROPE_DIM)
    ang = np.arange(seq, dtype=np.float32)[:, None] * inv_freq[None, :].astype(np.float32)
    cos, sin = np.cos(ang).astype(np.float32), np.sin(ang).astype(np.float32)
    rc = np.ones((seq, LANES), np.float32)
    rsp = np.zeros((seq, LANES), np.float32)
    rsm = np.zeros((seq, LANES), np.float32)
    for off in (0, HEAD_DIM):
        rc[:, off:off + ROPE_HALF] = cos
        rc[:, off + ROPE_HALF:off + ROPE_DIM] = cos
        rsm[:, off:off + ROPE_HALF] = -sin
        rsp[:, off + ROPE_HALF:off + ROPE_DIM] = sin
    return rc, rsp, rsm


def _proj_kernel(x_ref, w_ref, wfl_ref, prm_ref,
                 rc_ref, rsp_ref, rsm_ref, tri_ref, selfq_ref, selfk_ref, selmq_ref,
                 q_out, k_out, v_out, gz_out, gate_out, cend_out,
                 carry_ref, kmt_ref, h_ref, *, tiles_per_seq):
    tile_in_seq = pl.program_id(0) % tiles_per_seq
    t0 = tile_in_seq * SUB

    ng_ref = prm_ref.at[0:1, 0:D_MODEL]
    bg_ref = prm_ref.at[1:2, :]
    gfq_ref, gfk_ref, gmq_ref, gmk_ref = (
        prm_ref.at[2:3, n * WIDTH:(n + 1) * WIDTH] for n in range(4))
    bf_ref, big_ref, ufox_ref, umoba_ref = (
        prm_ref.at[3:4, n * LANES:(n + 1) * LANES] for n in range(4))

    @pl.when(tile_in_seq == 0)
    def _():
        carry_ref[...] = jnp.zeros_like(carry_ref)
        kmt_ref[...] = jnp.zeros_like(kmt_ref)

    lane = lax.broadcasted_iota(jnp.int32, (TILE, LANES), 1)
    low_half = lane < HEAD_DIM
    own = t0 + lax.broadcasted_iota(jnp.int32, (TILE, LANES), 0) // MOBA_BLOCK

    x = x_ref[...]
    ms = jnp.mean(x * x, axis=-1, keepdims=True)
    h_ref[...] = (x * lax.rsqrt(ms + RMS_EPS) * ng_ref[...]).astype(BF16)
    zero = pl.multiple_of(jnp.minimum(pl.program_id(0), 0) * TILE, TILE)

    def proj(c0, width):
        slabs = [_dot(h_ref[pl.ds(zero, TILE), :], w_ref[:, c:c + min(MXU_COLS, c0 + width - c)])
                 for c in range(c0, c0 + width, MXU_COLS)]
        return slabs[0] if len(slabs) == 1 else jnp.concatenate(slabs, axis=1)

    def head_norm(a, g_ref):
        tiles = []
        for p in range(WIDTH // LANES):
            ap = a[:, p * LANES:(p + 1) * LANES]
            sq = ap * ap
            s_lo = jnp.sum(jnp.where(low_half, sq, 0.0), axis=-1, keepdims=True)
            s_hi = jnp.sum(jnp.where(low_half, 0.0, sq), axis=-1, keepdims=True)
            inv_lo = lax.rsqrt(s_lo * (1.0 / HEAD_DIM) + RMS_EPS)
            inv_hi = lax.rsqrt(s_hi * (1.0 / HEAD_DIM) + RMS_EPS)
            scale = jnp.where(low_half, inv_lo, inv_hi)
            tiles.append(ap * scale * g_ref[:, p * LANES:(p + 1) * LANES])
        return tiles

    def rope(y):
        return (y * rc_ref[...] + pltpu.roll(y, ROPE_HALF, 1) * rsp_ref[...]
                + pltpu.roll(y, LANES - ROPE_HALF, 1) * rsm_ref[...])

    def split_tiles(a):
        return [a[:, p * LANES:(p + 1) * LANES] for p in range(WIDTH // LANES)]

    def store_heads(out_ref, head0, tiles, extras):
        for p, y in enumerate(tiles):
            e = extras(p)
            out_ref[0, head0 + 2 * p] = jnp.where(low_half, y, e).astype(BF16)
            out_ref[0, head0 + 2 * p + 1] = jnp.where(low_half, e, y).astype(BF16)

    def silu(z):
        hz = 0.5 * z
        return hz + hz * jnp.tanh(hz)

    def sigmoid(z):
        return 0.5 * jnp.tanh(0.5 * z) + 0.5

    fl = _dot(h_ref[pl.ds(zero, TILE), :], wfl_ref[...]) + bf_ref[...]
    mk_raw = proj(C_MK, WIDTH)
    mq_raw = proj(C_MQ, WIDTH)
    fv = proj(C_FV, WIDTH)
    fz = proj(C_FZ, WIDTH)

    logf = jnp.minimum(fl, 0.0) - jnp.log(1.0 + jnp.exp(-jnp.abs(fl)))
    logf = jnp.where(lane < N_HEADS, logf, 0.0)
    l_hi, l_mid, l_lo = _split3(logf)
    packed = (l_hi + pltpu.roll(l_mid, 8, 1) + pltpu.roll(l_lo, 16, 1)).astype(BF16)
    cum = _dot(tri_ref[...], packed)
    fq = proj(C_FQ, WIDTH)
    fk = proj(C_FK, WIDTH)

    mk_tiles = [rope(y) for y in head_norm(mk_raw, gmk_ref)]
    row_id = lax.broadcasted_iota(jnp.int32, (LANES, LANES), 0)
    lane_sq = lax.broadcasted_iota(jnp.int32, (LANES, LANES), 1)
    for p, kr in enumerate(mk_tiles):
        blk = kmt_ref[:, p * LANES:(p + 1) * LANES]
        for sub in range(SUB):
            km = jnp.mean(kr[sub * MOBA_BLOCK:(sub + 1) * MOBA_BLOCK], axis=0, keepdims=True)
            t = t0 + sub
            hit = (((row_id == (2 * p) * GROUP + t) & (lane_sq < HEAD_DIM))
                   | ((row_id == (2 * p + 1) * GROUP + t) & (lane_sq >= HEAD_DIM)))
            blk = jnp.where(hit, km, blk)
        kmt_ref[:, p * LANES:(p + 1) * LANES] = blk
    mk_ones = jnp.where((lane % HEAD_DIM == own) | (lane % HEAD_DIM == U_COL_MOBA), 1.0, 0.0)
    store_heads(k_out, N_HEADS, mk_tiles, lambda p: mk_ones)

    c = cum + pltpu.roll(cum, LANES - 8, 1) + pltpu.roll(cum, LANES - 16, 1)
    c = jnp.where(lane < N_HEADS, c, 0.0) + carry_ref[...]
    carry_ref[...] = c[TILE - 1:TILE, :]
    for sub in range(SUB):
        last = (sub + 1) * MOBA_BLOCK - 1
        cend_out[0, sub * (8 // SUB):(sub + 1) * (8 // SUB)] = jnp.broadcast_to(
            c[last:last + 1, :], (8 // SUB, LANES))
    c_hi, c_mid, c_lo = _split3(c * LOG2E)
    cparts = (c_hi + pltpu.roll(c_mid, 8, 1) + pltpu.roll(c_lo, 16, 1)
              + jnp.where(lane == 24, 1.0, 0.0)
              + jnp.where(lane == 25, ufox_ref[...], 0.0)).astype(BF16)
    ex_fq = _dot(cparts, selfq_ref[...])
    ex_fk = _dot(cparts, selfk_ref[...])
    mv = proj(C_MV, WIDTH)
    mz = proj(C_MZ, WIDTH)

    v_ones = jnp.where(lane % HEAD_DIM == 0, 1.0, 0.0)
    store_heads(v_out, 0, split_tiles(fv), lambda p: v_ones)
    gz_out[:, 0:WIDTH] = silu(fz).astype(BF16)

    mq_tiles = [rope(y) for y in head_norm(mq_raw, gmq_ref)]
    q_full = jnp.concatenate(mq_tiles, axis=1)
    q_hi = q_full.astype(BF16)
    q_lo = (q_full - q_hi.astype(F32)).astype(BF16)
    kmt = kmt_ref[...]
    k_hi = kmt.astype(BF16)
    k_lo = (kmt - k_hi.astype(F32)).astype(BF16)
    gate = _dot_nt(q_hi, k_hi) + _dot_nt(q_hi, k_lo) + _dot_nt(q_lo, k_hi)
    gate_chunks = [(C_GA + half * WIDTH, half * WIDTH) for half in range(2)]
    gate_chunks += [(C_GB + half * WIDTH, D_MODEL + half * WIDTH) for half in range(2)]
    gate_raw = [proj(c0, WIDTH) for c0, _ in gate_chunks]

    store_heads(q_out, 0, head_norm(fq, gfq_ref), lambda p: ex_fq[:, p * LANES:(p + 1) * LANES])
    store_heads(k_out, 0, head_norm(fk, gfk_ref), lambda p: ex_fk[:, p * LANES:(p + 1) * LANES])

    blk_id = lane % GROUP
    past = blk_id < own
    g = jnp.where(past, gate, -jnp.inf)
    beaten = jnp.zeros((TILE, LANES), jnp.int32)
    for d in range(1, GROUP):
        lower = jnp.where(blk_id >= d, pltpu.roll(g, d, 1), -jnp.inf)
        upper = jnp.where(blk_id < GROUP - d, pltpu.roll(g, LANES - d, 1), -jnp.inf)
        beaten = beaten + jnp.where(lower >= g, 1, 0) + jnp.where(upper > g, 1, 0)
    keep = (past & (beaten < MOBA_TOPK)) | (blk_id == own)
    maskvals = jnp.where(keep, 0.0, -big_ref[...]).astype(BF16)
    ex_mq = _dot(maskvals, selmq_ref[...])

    store_heads(v_out, N_HEADS, split_tiles(mv), lambda p: v_ones)
    gz_out[:, WIDTH:2 * WIDTH] = silu(mz).astype(BF16)

    is_u_lane = lane % HEAD_DIM == U_COL_MOBA
    store_heads(q_out, N_HEADS, mq_tiles,
                lambda p: jnp.where(is_u_lane, umoba_ref[...], ex_mq[:, p * LANES:(p + 1) * LANES]))
    for raw, (_, o0) in zip(gate_raw, gate_chunks):
        gate_out[:, o0:o0 + WIDTH] = sigmoid(raw + bg_ref[:, o0:o0 + WIDTH]).astype(BF16)


def _attn_kernel(flag_ref, cend_ref, q_ref, k_ref, v_ref, gz_ref, o_ref, acc_ref, m_ref, *,
                 blocks_per_seq):
    b, group = pl.program_id(0), pl.program_id(1)
    half = ATT_TILE // 2
    heads = range(ATT_HEADS)
    row = lax.broadcasted_iota(jnp.int32, (half, ATT_TILE), 0)
    col = lax.broadcasted_iota(jnp.int32, (half, ATT_TILE), 1)
    mask_top = (lax.broadcasted_iota(jnp.int32, (half, half), 1)
                <= lax.broadcasted_iota(jnp.int32, (half, half), 0))
    mask_bot = col <= row + half

    for st in range(ATT_STEP_TILES):
        _attn_tile(flag_ref, cend_ref, q_ref, k_ref, v_ref, gz_ref, o_ref, acc_ref, m_ref,
                   b, group, pl.program_id(2) * ATT_STEP_TILES + st, st * ATT_TILE,
                   mask_top, mask_bot, blocks_per_seq)


def _attn_tile(flag_ref, cend_ref, q_ref, k_ref, v_ref, gz_ref, o_ref, acc_ref, m_ref,
               b, group, i, r0, mask_top, mask_bot, blocks_per_seq):
    half = ATT_TILE // 2
    start = pl.multiple_of(i * ATT_TILE, ATT_TILE)
    heads = range(ATT_HEADS)
    rows = slice(r0, r0 + ATT_TILE)

    def first_live_block(hh):
        base = (b * N_AUG_HEADS + ATT_HEADS * group + hh) * blocks_per_seq
        sub = ATT_TILE // MOBA_BLOCK
        c_q = cend_ref[base + jnp.maximum(sub * i - 1, 0)]
        dead = jnp.int32(0)
        for j in range(blocks_per_seq // sub - 1):
            c_k = cend_ref[base + sub * j + sub - 1]
            dead += ((j < i) & (c_q - c_k < EXP2_ZERO_BELOW)).astype(jnp.int32)
        return dead

    def diag_scores(hh):
        kd = k_ref[0, hh, pl.ds(start, ATT_TILE), :]
        s_top = jnp.where(mask_top, _dot_nt(q_ref[0, hh, r0:r0 + half, :], kd[0:half]), -jnp.inf)
        s_bot = jnp.where(mask_bot, _dot_nt(q_ref[0, hh, r0 + half:r0 + ATT_TILE, :], kd), -jnp.inf)
        return s_top, s_bot

    def finish(accs):
        lane = lax.broadcasted_iota(jnp.int32, (ATT_TILE, LANES), 1)
        for pp in range(ATT_HEADS // 2):
            acc_e, acc_o = accs[2 * pp], accs[2 * pp + 1]
            o_e = acc_e * (1.0 / acc_e[:, HEAD_DIM:HEAD_DIM + 1])
            o_o = acc_o * (1.0 / acc_o[:, 0:1])
            o = jnp.where(lane < HEAD_DIM, o_e, o_o)
            cols = slice(pp * LANES, (pp + 1) * LANES)
            o_ref[rows, cols] = (o * gz_ref[rows, cols].astype(F32)).astype(BF16)

    @pl.when(flag_ref[0] == 1)
    def _fixed_stabilizer():
        acc_ref[...] = jnp.zeros_like(acc_ref)

        def add_blocks(j, n_blocks):
            off = pl.multiple_of(j * ATT_TILE, ATT_TILE)
            keys = n_blocks * ATT_TILE
            for hh in heads:
                s = _dot_nt(q_ref[0, hh, rows, :], k_ref[0, hh, pl.ds(off, keys), :])
                acc_ref[hh] += _dot(jnp.exp2(s).astype(BF16), v_ref[0, hh, pl.ds(off, keys), :])

        first = first_live_block(0)
        for hh in heads[1:]:
            first = jnp.minimum(first, first_live_block(hh))
        n_live = i - first
        odd = n_live & 1

        @pl.when(odd == 1)
        def _():
            add_blocks(first, 1)

        def pair(p, carry):
            add_blocks(first + odd + 2 * p, 2)
            return carry

        lax.fori_loop(0, n_live >> 1, pair, 0)

        accs = []
        for hh in heads:
            vd = v_ref[0, hh, pl.ds(start, ATT_TILE), :]
            s_top, s_bot = diag_scores(hh)
            top = acc_ref[hh, 0:half] + _dot(jnp.exp2(s_top).astype(BF16), vd[0:half])
            bot = acc_ref[hh, half:ATT_TILE] + _dot(jnp.exp2(s_bot).astype(BF16), vd)
            accs.append(jnp.concatenate([top, bot], axis=0))
        finish(accs)

    @pl.when(flag_ref[0] != 1)
    def _online():
        acc_ref[...] = jnp.zeros_like(acc_ref)
        m_ref[...] = jnp.full_like(m_ref, -jnp.inf)

        def update(s, m_old, acc_old, v):
            m_new = jnp.maximum(m_old, jnp.max(s, axis=-1, keepdims=True))
            pv = _dot(jnp.exp2(s - m_new).astype(BF16), v)
            return m_new, jnp.exp2(m_old - m_new) * acc_old + pv

        def body(j, carry):
            off = pl.multiple_of(j * ATT_TILE, ATT_TILE)
            for hh in heads:
                s = _dot_nt(q_ref[0, hh, rows, :], k_ref[0, hh, pl.ds(off, ATT_TILE), :])
                m_ref[hh], acc_ref[hh] = update(s, m_ref[hh], acc_ref[hh],
                                                v_ref[0, hh, pl.ds(off, ATT_TILE), :])
            return carry

        lax.fori_loop(0, i, body, 0)

        accs = []
        for hh in heads:
            vd = v_ref[0, hh, pl.ds(start, ATT_TILE), :]
            s_top, s_bot = diag_scores(hh)
            _, top = update(s_top, m_ref[hh, 0:half], acc_ref[hh, 0:half], vd[0:half])
            _, bot = update(s_bot, m_ref[hh, half:ATT_TILE], acc_ref[hh, half:ATT_TILE], vd)
            accs.append(jnp.concatenate([top, bot], axis=0))
        finish(accs)


def _out_kernel(y_ref, gate_ref, x_ref, wf_ref, wm_ref, wo_ref, o_ref):
    slabs = [slice(c, c + MXU_COLS) for c in range(0, D_MODEL, MXU_COLS)]
    yf, ym = y_ref[:, 0:WIDTH], y_ref[:, WIDTH:2 * WIDTH]
    merged = []
    for cols in slabs:
        pf = _dot(yf, wf_ref[:, cols])
        pm = _dot(ym, wm_ref[:, cols])
        ga = gate_ref[:, cols].astype(F32)
        gb = gate_ref[:, D_MODEL + cols.start:D_MODEL + cols.stop].astype(F32)
        merged.append((ga * pf + gb * pm).astype(BF16))
    merged = jnp.concatenate(merged, axis=1)
    for cols in slabs:
        o_ref[:, cols] = x_ref[:, cols] + _dot(merged, wo_ref[:, cols])


def _const_spec(shape):
    return pl.BlockSpec(shape, lambda *_: (0,) * len(shape))


def _layer(x2, batch, seq, norm_g, w_in, b_f, b_gate, fox_q_g, fox_k_g, moba_q_g, moba_k_g,
           w_fox, w_moba, w_out):
    rows = batch * seq
    tiles_per_seq = seq // TILE
    n_tiles = rows // TILE
    scale = HEAD_DIM ** -0.5

    w = w_in.astype(BF16)
    w_fl = jnp.pad(w_in[:, W_COLS:], ((0, 0), (0, LANES - N_HEADS))).astype(BF16)

    def logit_bound(gq, gk):
        return 8.0 * LOG2E * 1.02 * jnp.max(jnp.abs(gq)) * jnp.max(jnp.abs(gk))

    b_fox, b_moba = logit_bound(fox_q_g, fox_k_g), logit_bound(moba_q_g, moba_k_g)
    fixed_ok = jnp.maximum(b_fox, b_moba) <= FIXED_STABILIZER_MAX_BOUND

    def stabilizer(b):
        u = -(b * (1.0 + 2.0 ** -7)).astype(BF16).astype(F32)
        return jnp.where(fixed_ok, u, 0.0)

    big = jnp.exp2(jnp.ceil(jnp.log2(2.1 * b_moba + 160.0)))
    flag = fixed_ok.astype(jnp.int32).reshape(1)

    def row(*pieces):
        r = jnp.concatenate([jnp.ravel(p).astype(F32) for p in pieces])
        return jnp.pad(r, (0, 2 * D_MODEL - r.shape[0]))

    def lanes(v):
        return jnp.full((LANES,), v, F32)

    q_scale = scale * LOG2E
    params = jnp.stack([
        row(norm_g),
        row(b_gate),
        row(jnp.tile(fox_q_g * q_scale, N_HEADS), jnp.tile(fox_k_g, N_HEADS),
            jnp.tile(moba_q_g * q_scale, N_HEADS), jnp.tile(moba_k_g, N_HEADS)),
        row(b_f, jnp.zeros(LANES - N_HEADS), lanes(big), lanes(stabilizer(b_fox)),
            lanes(stabilizer(b_moba))),
    ] + [jnp.zeros(2 * D_MODEL, F32)] * 4)

    sel_fq, sel_fk, sel_mq, tri = _routing_constants()
    rc, rsp, rsm = _rope_tables(seq)

    rope_spec = pl.BlockSpec((TILE, LANES), lambda g: (g % tiles_per_seq, 0))
    head_spec = pl.BlockSpec((1, N_AUG_HEADS, TILE, LANES),
                             lambda g: (g // tiles_per_seq, 0, g % tiles_per_seq, 0))
    qkv_shape = jax.ShapeDtypeStruct((batch, N_AUG_HEADS, seq, LANES), BF16)

    q_all, k_all, v_all, gz, gates, cend = pl.pallas_call(
        functools.partial(_proj_kernel, tiles_per_seq=tiles_per_seq),
        grid=(n_tiles,),
        in_specs=[
            pl.BlockSpec((TILE, D_MODEL), lambda g: (g, 0)),
            pl.BlockSpec(w_in.shape, lambda g: (0, 0), pipeline_mode=pl.Buffered(1)),
            _const_spec((D_MODEL, LANES)),
            _const_spec((8, 2 * D_MODEL)),
            rope_spec, rope_spec, rope_spec,
            _const_spec((TILE, TILE)),
            _const_spec((LANES, WIDTH)), _const_spec((LANES, WIDTH)), _const_spec((LANES, WIDTH)),
        ],
        out_specs=[
            head_spec, head_spec, head_spec,
            pl.BlockSpec((TILE, 2 * WIDTH), lambda g: (g, 0)),
            pl.BlockSpec((TILE, 2 * D_MODEL), lambda g: (g, 0)),
            pl.BlockSpec((1, 8, LANES), lambda g: (g, 0, 0)),
        ],
        out_shape=[
            qkv_shape, qkv_shape, qkv_shape,
            jax.ShapeDtypeStruct((rows, 2 * WIDTH), BF16),
            jax.ShapeDtypeStruct((rows, 2 * D_MODEL), BF16),
            jax.ShapeDtypeStruct((n_tiles, 8, LANES), F32),
        ],
        scratch_shapes=[pltpu.VMEM((1, LANES), F32),
                        pltpu.VMEM((LANES, WIDTH), F32),
                        pltpu.VMEM((TILE, D_MODEL), BF16)],
        compiler_params=pltpu.CompilerParams(
            dimension_semantics=("arbitrary",), vmem_limit_bytes=VMEM_LIMIT),
        name="proj_epilogue",
    )(x2, w, w_fl, params,
      jnp.asarray(rc), jnp.asarray(rsp), jnp.asarray(rsm),
      jnp.asarray(tri, BF16), jnp.asarray(sel_fq, BF16), jnp.asarray(sel_fk, BF16),
      jnp.asarray(sel_mq, BF16))

    n_groups = N_AUG_HEADS // ATT_HEADS
    group_cols = ATT_HEADS // 2 * LANES
    step_rows = ATT_STEP_TILES * ATT_TILE
    att_steps = seq // step_rows
    blocks_per_seq = seq // MOBA_BLOCK
    cend = cend[:, ::8 // SUB, :N_HEADS].reshape(batch, blocks_per_seq, N_HEADS).transpose(0, 2, 1) * LOG2E
    cend = jnp.concatenate([cend, jnp.zeros_like(cend)], axis=1).reshape(-1)
    kv_spec = pl.BlockSpec((1, ATT_HEADS, seq, LANES), lambda b, g, i, *_: (b, g, 0, 0))
    row_spec = pl.BlockSpec((step_rows, group_cols), lambda b, g, i, *_: (b * att_steps + i, g))
    y = pl.pallas_call(
        functools.partial(_attn_kernel, blocks_per_seq=blocks_per_seq),
        grid_spec=pltpu.PrefetchScalarGridSpec(
            num_scalar_prefetch=2,
            grid=(batch, n_groups, att_steps),
            in_specs=[
                pl.BlockSpec((1, ATT_HEADS, step_rows, LANES), lambda b, g, i, *_: (b, g, i, 0)),
                kv_spec, kv_spec, row_spec,
            ],
            out_specs=row_spec,
            scratch_shapes=[pltpu.VMEM((ATT_HEADS, ATT_TILE, LANES), F32),
                            pltpu.VMEM((ATT_HEADS, ATT_TILE, 1), F32)],
        ),
        out_shape=jax.ShapeDtypeStruct((rows, 2 * WIDTH), BF16),
        compiler_params=pltpu.CompilerParams(
            dimension_semantics=("arbitrary", "arbitrary", "arbitrary"),
            vmem_limit_bytes=VMEM_LIMIT),
        name="flash_attn",
    )(flag, cend, q_all, k_all, v_all, gz)

    out = pl.pallas_call(
        _out_kernel,
        grid=(rows // OUT_TILE,),
        in_specs=[
            pl.BlockSpec((OUT_TILE, 2 * WIDTH), lambda g: (g, 0)),
            pl.BlockSpec((OUT_TILE, 2 * D_MODEL), lambda g: (g, 0)),
            pl.BlockSpec((OUT_TILE, D_MODEL), lambda g: (g, 0)),
            _const_spec((WIDTH, D_MODEL)), _const_spec((WIDTH, D_MODEL)),
            _const_spec((D_MODEL, D_MODEL)),
        ],
        out_specs=pl.BlockSpec((OUT_TILE, D_MODEL), lambda g: (g, 0)),
        out_shape=jax.ShapeDtypeStruct((rows, D_MODEL), F32),
        compiler_params=pltpu.CompilerParams(
            dimension_semantics=("arbitrary",), vmem_limit_bytes=VMEM_LIMIT),
        name="merge_out",
    )(y, gates, x2, w_fox.astype(BF16), w_moba.astype(BF16), w_out.astype(BF16))
    return out


def kernel(x, norm_g, w_in, b_f, b_gate, fox_q_g, fox_k_g, moba_q_g, moba_k_g, w_fox, w_moba, w_out):
    batch, seq, d_model = x.shape
    assert d_model == D_MODEL and seq % (ATT_STEP_TILES * ATT_TILE) == 0 and seq % TILE == 0
    assert seq // MOBA_BLOCK <= GROUP
    x2 = x.reshape(batch * seq, D_MODEL)
    for layer in range(norm_g.shape[0]):
        x2 = _layer(x2, batch, seq, norm_g[layer], w_in[layer], b_f[layer], b_gate[layer],
                    fox_q_g[layer], fox_k_g[layer], moba_q_g[layer], moba_k_g[layer],
                    w_fox[layer], w_moba[layer], w_out[layer])
    return x2.reshape(batch, seq, D_MODEL)
```

```python
import functools

import numpy as np
import jax
import jax.numpy as jnp
from jax import lax
from jax.experimental import pallas as pl
from jax.experimental.pallas import tpu as pltpu

D_MODEL = 1024
HEAD_DIM = 64
N_HEADS = 8
WIDTH = N_HEADS * HEAD_DIM
ROPE_DIM = HEAD_DIM // 4
ROPE_HALF = ROPE_DIM // 2
ROPE_THETA = 500000.0
MOBA_BLOCK = 256
MOBA_TOPK = 3
RMS_EPS = 1e-6

LANES = 128
SUBLANES = 8
MXU_COLS = 256
TILE = MOBA_BLOCK
OUT_TILE = 1024
ATT_TILE = 512
ATT_HEADS = N_HEADS
FIXED_STABILIZER_MAX_BOUND = 55.0
EXP2_ZERO_BELOW = -152.0
BOUND_MARGIN = 1.02
GROUP = 16
N_AUG_HEADS = 2 * N_HEADS
U_COL_FOX = 6
U_COL_MOBA = GROUP
LOG2E = 1.4426950408889634

C_FQ, C_FK, C_FV, C_FZ = 0, 512, 1024, 1536
C_MQ, C_MK, C_MV, C_MZ = 2048, 2560, 3072, 3584
C_GA, C_GB = 4096, 5120
W_COLS = 6144

VMEM_LIMIT = 52 * 1024 * 1024

F32 = jnp.float32
BF16 = jnp.bfloat16


def _dot(a, b):
    return jnp.dot(a, b, preferred_element_type=F32)


def _dot_nt(a, b):
    return lax.dot_general(a, b, (((1,), (1,)), ((), ())), preferred_element_type=F32)


def _split3(v):
    hi = v.astype(BF16).astype(F32)
    r = v - hi
    mid = r.astype(BF16).astype(F32)
    lo = r - mid
    return hi, mid, lo


def _extra_base(h):
    return (h // 2) * LANES + (HEAD_DIM if h % 2 == 0 else 0)


def _routing_constants():
    sel_fq = np.zeros((LANES, WIDTH), np.float32)
    sel_fk = np.zeros((LANES, WIDTH), np.float32)
    sel_mq = np.zeros((LANES, WIDTH), np.float32)
    for h in range(N_HEADS):
        base = _extra_base(h)
        for part in range(3):
            sel_fq[part * 8 + h, base + part] = 1.0
            sel_fq[24, base + 3 + part] = 1.0
            sel_fk[24, base + part] = 1.0
            sel_fk[part * 8 + h, base + 3 + part] = -1.0
        sel_fq[25, base + U_COL_FOX] = 1.0
        sel_fk[24, base + U_COL_FOX] = 1.0
        for n in range(GROUP):
            sel_mq[h * GROUP + n, base + n] = 1.0
    tri = np.tril(np.ones((TILE, TILE), np.float32))
    return sel_fq, sel_fk, sel_mq, tri


def _rope_tables(seq):
    inv_freq = ROPE_THETA ** (-np.arange(0, ROPE_HALF, dtype=np.float32) * 2.0 / ROPE_DIM)
    ang = np.arange(seq, dtype=np.float32)[:, None] * inv_freq[None, :].astype(np.float32)
    cos, sin = np.cos(ang).astype(np.float32), np.sin(ang).astype(np.float32)
    rc = np.ones((seq, LANES), np.float32)
    rsp = np.zeros((seq, LANES), np.float32)
    rsm = np.zeros((seq, LANES), np.float32)
    for off in (0, HEAD_DIM):
        rc[:, off:off + ROPE_HALF] = cos
        rc[:, off + ROPE_HALF:off + ROPE_DIM] = cos
        rsm[:, off:off + ROPE_HALF] = -sin
        rsp[:, off + ROPE_HALF:off + ROPE_DIM] = sin
    return rc, rsp, rsm


def _proj_kernel(x_ref, w_ref, wfl_ref, prm_ref,
                 rc_ref, rsp_ref, rsm_ref, tri_ref, selfq_ref, selfk_ref, selmq_ref,
                 q_out, k_out, v_out, gz_out, gate_out, cend_out,
                 carry_ref, kmt_ref, h_ref, *, tiles_per_seq):
    t = pl.program_id(0) % tiles_per_seq

    ng_ref = prm_ref.at[0:1, 0:D_MODEL]
    bg_ref = prm_ref.at[1:2, :]
    gfq_ref, gfk_ref, gmq_ref, gmk_ref = (
        prm_ref.at[2:3, n * WIDTH:(n + 1) * WIDTH] for n in range(4))
    bf_ref, big_ref, ufox_ref, umoba_ref = (
        prm_ref.at[3:4, n * LANES:(n + 1) * LANES] for n in range(4))

    @pl.when(t == 0)
    def _():
        carry_ref[...] = jnp.zeros_like(carry_ref)
        kmt_ref[...] = jnp.zeros_like(kmt_ref)

    lane = lax.broadcasted_iota(jnp.int32, (TILE, LANES), 1)
    low_half = lane < HEAD_DIM

    x = x_ref[...]
    ms = jnp.mean(x * x, axis=-1, keepdims=True)
    h_ref[...] = (x * lax.rsqrt(ms + RMS_EPS) * ng_ref[...]).astype(BF16)
    zero = pl.multiple_of(jnp.minimum(pl.program_id(0), 0) * TILE, TILE)

    def proj(c0, width):
        slabs = [_dot(h_ref[pl.ds(zero, TILE), :], w_ref[:, c:c + min(MXU_COLS, c0 + width - c)])
                 for c in range(c0, c0 + width, MXU_COLS)]
        return slabs[0] if len(slabs) == 1 else jnp.concatenate(slabs, axis=1)

    def head_norm(a, g_ref):
        tiles = []
        for p in range(WIDTH // LANES):
            ap = a[:, p * LANES:(p + 1) * LANES]
            sq = ap * ap
            s_lo = jnp.sum(jnp.where(low_half, sq, 0.0), axis=-1, keepdims=True)
            s_hi = jnp.sum(jnp.where(low_half, 0.0, sq), axis=-1, keepdims=True)
            inv_lo = lax.rsqrt(s_lo * (1.0 / HEAD_DIM) + RMS_EPS)
            inv_hi = lax.rsqrt(s_hi * (1.0 / HEAD_DIM) + RMS_EPS)
            scale = jnp.where(low_half, inv_lo, inv_hi)
            tiles.append(ap * scale * g_ref[:, p * LANES:(p + 1) * LANES])
        return tiles

    def rope(y):
        return (y * rc_ref[...] + pltpu.roll(y, ROPE_HALF, 1) * rsp_ref[...]
                + pltpu.roll(y, LANES - ROPE_HALF, 1) * rsm_ref[...])

    def split_tiles(a):
        return [a[:, p * LANES:(p + 1) * LANES] for p in range(WIDTH // LANES)]

    def store_heads(out_ref, head0, tiles, extras):
        for p, y in enumerate(tiles):
            e = extras(p)
            out_ref[0, head0 + 2 * p] = jnp.where(low_half, y, e).astype(BF16)
            out_ref[0, head0 + 2 * p + 1] = jnp.where(low_half, e, y).astype(BF16)

    def silu(z):
        hz = 0.5 * z
        return hz + hz * jnp.tanh(hz)

    def sigmoid(z):
        return 0.5 * jnp.tanh(0.5 * z) + 0.5

    fl = _dot(h_ref[pl.ds(zero, TILE), :], wfl_ref[...]) + bf_ref[...]
    mk_raw = proj(C_MK, WIDTH)
    mq_raw = proj(C_MQ, WIDTH)
    fv = proj(C_FV, WIDTH)
    fz = proj(C_FZ, WIDTH)

    logf = jnp.minimum(fl, 0.0) - jnp.log(1.0 + jnp.exp(-jnp.abs(fl)))
    logf = jnp.where(lane < N_HEADS, logf, 0.0)
    l_hi, l_mid, l_lo = _split3(logf)
    packed = (l_hi + pltpu.roll(l_mid, 8, 1) + pltpu.roll(l_lo, 16, 1)).astype(BF16)
    cum = _dot(tri_ref[...], packed)
    fq = proj(C_FQ, WIDTH)
    fk = proj(C_FK, WIDTH)

    mk_tiles = [rope(y) for y in head_norm(mk_raw, gmk_ref)]
    row_id = lax.broadcasted_iota(jnp.int32, (LANES, LANES), 0)
    lane_sq = lax.broadcasted_iota(jnp.int32, (LANES, LANES), 1)
    for p, kr in enumerate(mk_tiles):
        km = jnp.mean(kr, axis=0, keepdims=True)
        hit = (((row_id == (2 * p) * GROUP + t) & (lane_sq < HEAD_DIM))
               | ((row_id == (2 * p + 1) * GROUP + t) & (lane_sq >= HEAD_DIM)))
        blk = kmt_ref[:, p * LANES:(p + 1) * LANES]
        kmt_ref[:, p * LANES:(p + 1) * LANES] = jnp.where(hit, km, blk)
    mk_ones = jnp.where((lane % HEAD_DIM == t) | (lane % HEAD_DIM == U_COL_MOBA), 1.0, 0.0)
    store_heads(k_out, N_HEADS, mk_tiles, lambda p: mk_ones)

    c = cum + pltpu.roll(cum, LANES - 8, 1) + pltpu.roll(cum, LANES - 16, 1)
    c = jnp.where(lane < N_HEADS, c, 0.0) + carry_ref[...]
    carry_ref[...] = c[TILE - 1:TILE, :]
    cend_out[0] = jnp.broadcast_to(c[TILE - 1:TILE, :], (SUBLANES, LANES))
    c_hi, c_mid, c_lo = _split3(c * LOG2E)
    cparts = (c_hi + pltpu.roll(c_mid, 8, 1) + pltpu.roll(c_lo, 16, 1)
              + jnp.where(lane == 24, 1.0, 0.0)
              + jnp.where(lane == 25, ufox_ref[...], 0.0)).astype(BF16)
    ex_fq = _dot(cparts, selfq_ref[...])
    ex_fk = _dot(cparts, selfk_ref[...])
    mv = proj(C_MV, WIDTH)
    mz = proj(C_MZ, WIDTH)

    v_ones = jnp.where(lane % HEAD_DIM == 0, 1.0, 0.0)
    store_heads(v_out, 0, split_tiles(fv), lambda p: v_ones)
    gz_out[:, 0:WIDTH] = silu(fz).astype(BF16)

    mq_tiles = [rope(y) for y in head_norm(mq_raw, gmq_ref)]
    q_full = jnp.concatenate(mq_tiles, axis=1)
    q_hi = q_full.astype(BF16)
    q_lo = (q_full - q_hi.astype(F32)).astype(BF16)
    kmt = kmt_ref[...]
    k_hi = kmt.astype(BF16)
    k_lo = (kmt - k_hi.astype(F32)).astype(BF16)
    gate = _dot_nt(q_hi, k_hi) + _dot_nt(q_hi, k_lo) + _dot_nt(q_lo, k_hi)
    gate_chunks = [(C_GA + half * WIDTH, half * WIDTH) for half in range(2)]
    gate_chunks += [(C_GB + half * WIDTH, D_MODEL + half * WIDTH) for half in range(2)]
    gate_raw = [proj(c0, WIDTH) for c0, _ in gate_chunks]

    store_heads(q_out, 0, head_norm(fq, gfq_ref), lambda p: ex_fq[:, p * LANES:(p + 1) * LANES])
    store_heads(k_out, 0, head_norm(fk, gfk_ref), lambda p: ex_fk[:, p * LANES:(p + 1) * LANES])

    blk_id = lane % GROUP
    past = blk_id < t
    g = jnp.where(past, gate, -jnp.inf)
    beaten = jnp.zeros((TILE, LANES), jnp.int32)
    for d in range(1, GROUP):
        lower = jnp.where(blk_id >= d, pltpu.roll(g, d, 1), -jnp.inf)
        upper = jnp.where(blk_id < GROUP - d, pltpu.roll(g, LANES - d, 1), -jnp.inf)
        beaten = beaten + jnp.where(lower >= g, 1, 0) + jnp.where(upper > g, 1, 0)
    keep = (past & (beaten < MOBA_TOPK)) | (blk_id == t)
    maskvals = jnp.where(keep, 0.0, -big_ref[...]).astype(BF16)
    ex_mq = _dot(maskvals, selmq_ref[...])

    store_heads(v_out, N_HEADS, split_tiles(mv), lambda p: v_ones)
    gz_out[:, WIDTH:2 * WIDTH] = silu(mz).astype(BF16)

    is_u_lane = lane % HEAD_DIM == U_COL_MOBA
    store_heads(q_out, N_HEADS, mq_tiles,
                lambda p: jnp.where(is_u_lane, umoba_ref[...], ex_mq[:, p * LANES:(p + 1) * LANES]))
    for raw, (_, o0) in zip(gate_raw, gate_chunks):
        gate_out[:, o0:o0 + WIDTH] = sigmoid(raw + bg_ref[:, o0:o0 + WIDTH]).astype(BF16)


def _attn_kernel(flag_ref, cend_ref, q_ref, k_ref, v_ref, gz_ref, o_ref, acc_ref, m_ref, *,
                 tiles_per_seq):
    b, group, i = pl.program_id(0), pl.program_id(1), pl.program_id(2)
    half = ATT_TILE // 2
    start = pl.multiple_of(i * ATT_TILE, ATT_TILE)
    heads = range(ATT_HEADS)

    def first_live_block(hh):
        base = (b * N_AUG_HEADS + ATT_HEADS * group + hh) * tiles_per_seq
        sub = ATT_TILE // TILE
        c_q = cend_ref[base + jnp.maximum(sub * i - 1, 0)]
        dead = jnp.int32(0)
        for j in range(tiles_per_seq // sub - 1):
            c_k = cend_ref[base + sub * j + sub - 1]
            dead += ((j < i) & (c_q - c_k < EXP2_ZERO_BELOW)).astype(jnp.int32)
        return dead

    row = lax.broadcasted_iota(jnp.int32, (half, ATT_TILE), 0)
    col = lax.broadcasted_iota(jnp.int32, (half, ATT_TILE), 1)
    mask_top = (lax.broadcasted_iota(jnp.int32, (half, half), 1)
                <= lax.broadcasted_iota(jnp.int32, (half, half), 0))
    mask_bot = col <= row + half

    def diag_scores(hh):
        kd = k_ref[0, hh, pl.ds(start, ATT_TILE), :]
        s_top = jnp.where(mask_top, _dot_nt(q_ref[0, hh, 0:half, :], kd[0:half]), -jnp.inf)
        s_bot = jnp.where(mask_bot, _dot_nt(q_ref[0, hh, half:ATT_TILE, :], kd), -jnp.inf)
        return s_top, s_bot

    def finish(accs):
        lane = lax.broadcasted_iota(jnp.int32, (ATT_TILE, LANES), 1)
        for pp in range(ATT_HEADS // 2):
            acc_e, acc_o = accs[2 * pp], accs[2 * pp + 1]
            o_e = acc_e * (1.0 / acc_e[:, HEAD_DIM:HEAD_DIM + 1])
            o_o = acc_o * (1.0 / acc_o[:, 0:1])
            o = jnp.where(lane < HEAD_DIM, o_e, o_o)
            cols = slice(pp * LANES, (pp + 1) * LANES)
            o_ref[:, cols] = (o * gz_ref[:, cols].astype(F32)).astype(BF16)

    @pl.when(flag_ref[0] == 1)
    def _fixed_stabilizer():
        acc_ref[...] = jnp.zeros_like(acc_ref)

        def body(j, carry):
            off = pl.multiple_of(j * ATT_TILE, ATT_TILE)
            for hh in heads:
                s = _dot_nt(q_ref[0, hh], k_ref[0, hh, pl.ds(off, ATT_TILE), :])
                acc_ref[hh] += _dot(jnp.exp2(s).astype(BF16),
                                    v_ref[0, hh, pl.ds(off, ATT_TILE), :])
            return carry

        first = first_live_block(0)
        for hh in heads[1:]:
            first = jnp.minimum(first, first_live_block(hh))
        lax.fori_loop(first, i, body, 0)

        accs = []
        for hh in heads:
            vd = v_ref[0, hh, pl.ds(start, ATT_TILE), :]
            s_top, s_bot = diag_scores(hh)
            top = acc_ref[hh, 0:half] + _dot(jnp.exp2(s_top).astype(BF16), vd[0:half])
            bot = acc_ref[hh, half:ATT_TILE] + _dot(jnp.exp2(s_bot).astype(BF16), vd)
            accs.append(jnp.concatenate([top, bot], axis=0))
        finish(accs)

    @pl.when(flag_ref[0] != 1)
    def _online():
        acc_ref[...] = jnp.zeros_like(acc_ref)
        m_ref[...] = jnp.full_like(m_ref, -jnp.inf)

        def update(s, m_old, acc_old, v):
            m_new = jnp.maximum(m_old, jnp.max(s, axis=-1, keepdims=True))
            pv = _dot(jnp.exp2(s - m_new).astype(BF16), v)
            return m_new, jnp.exp2(m_old - m_new) * acc_old + pv

        def body(j, carry):
            off = pl.multiple_of(j * ATT_TILE, ATT_TILE)
            for hh in heads:
                s = _dot_nt(q_ref[0, hh], k_ref[0, hh, pl.ds(off, ATT_TILE), :])
                m_ref[hh], acc_ref[hh] = update(s, m_ref[hh], acc_ref[hh],
                                                v_ref[0, hh, pl.ds(off, ATT_TILE), :])
            return carry

        lax.fori_loop(0, i, body, 0)

        accs = []
        for hh in heads:
            vd = v_ref[0, hh, pl.ds(start, ATT_TILE), :]
            s_top, s_bot = diag_scores(hh)
            _, top = update(s_top, m_ref[hh, 0:half], acc_ref[hh, 0:half], vd[0:half])
            _, bot = update(s_bot, m_ref[hh, half:ATT_TILE], acc_ref[hh, half:ATT_TILE], vd)
            accs.append(jnp.concatenate([top, bot], axis=0))
        finish(accs)


def _out_kernel(y_ref, gate_ref, x_ref, wf_ref, wm_ref, wo_ref, o_ref):
    slabs = [slice(c, c + MXU_COLS) for c in range(0, D_MODEL, MXU_COLS)]
    yf, ym = y_ref[:, 0:WIDTH], y_ref[:, WIDTH:2 * WIDTH]
    merged = []
    for cols in slabs:
        pf = _dot(yf, wf_ref[:, cols])
        pm = _dot(ym, wm_ref[:, cols])
        ga = gate_ref[:, cols].astype(F32)
        gb = gate_ref[:, D_MODEL + cols.start:D_MODEL + cols.stop].astype(F32)
        merged.append((ga * pf + gb * pm).astype(BF16))
    merged = jnp.concatenate(merged, axis=1)
    for cols in slabs:
        o_ref[:, cols] = x_ref[:, cols] + _dot(merged, wo_ref[:, cols])


def _const_spec(shape):
    return pl.BlockSpec(shape, lambda *_: (0,) * len(shape))


def _layer(x2, batch, seq, norm_g, w_in, b_f, b_gate, fox_q_g, fox_k_g, moba_q_g, moba_k_g,
           w_fox, w_moba, w_out):
    rows = batch * seq
    tiles_per_seq = seq // TILE
    n_tiles = rows // TILE
    scale = HEAD_DIM ** -0.5

    w = w_in.astype(BF16)
    w_fl = jnp.pad(w_in[:, W_COLS:], ((0, 0), (0, LANES - N_HEADS))).astype(BF16)

    def logit_bound(gq, gk):
        return 8.0 * LOG2E * BOUND_MARGIN * jnp.max(jnp.abs(gq)) * jnp.max(jnp.abs(gk))

    b_fox, b_moba = logit_bound(fox_q_g, fox_k_g), logit_bound(moba_q_g, moba_k_g)
    fixed_ok = jnp.maximum(b_fox, b_moba) <= FIXED_STABILIZER_MAX_BOUND

    def stabilizer(b):
        u = -(b * (1.0 + 2.0 ** -7)).astype(BF16).astype(F32)
        return jnp.where(fixed_ok, u, 0.0)

    big = jnp.exp2(jnp.ceil(jnp.log2(2.1 * b_moba + 160.0)))
    flag = fixed_ok.astype(jnp.int32).reshape(1)

    def row(*pieces):
        r = jnp.concatenate([jnp.ravel(p).astype(F32) for p in pieces])
        return jnp.pad(r, (0, 2 * D_MODEL - r.shape[0]))

    def lanes(v):
        return jnp.full((LANES,), v, F32)

    q_scale = scale * LOG2E
    params = jnp.stack([
        row(norm_g),
        row(b_gate),
        row(jnp.tile(fox_q_g * q_scale, N_HEADS), jnp.tile(fox_k_g, N_HEADS),
            jnp.tile(moba_q_g * q_scale, N_HEADS), jnp.tile(moba_k_g, N_HEADS)),
        row(b_f, jnp.zeros(LANES - N_HEADS), lanes(big), lanes(stabilizer(b_fox)),
            lanes(stabilizer(b_moba))),
    ] + [jnp.zeros(2 * D_MODEL, F32)] * (SUBLANES - 4))

    sel_fq, sel_fk, sel_mq, tri = _routing_constants()
    rc, rsp, rsm = _rope_tables(seq)

    rope_spec = pl.BlockSpec((TILE, LANES), lambda g: (g % tiles_per_seq, 0))
    head_spec = pl.BlockSpec((1, N_AUG_HEADS, TILE, LANES),
                             lambda g: (g // tiles_per_seq, 0, g % tiles_per_seq, 0))
    qkv_shape = jax.ShapeDtypeStruct((batch, N_AUG_HEADS, seq, LANES), BF16)

    q_all, k_all, v_all, gz, gates, cend = pl.pallas_call(
        functools.partial(_proj_kernel, tiles_per_seq=tiles_per_seq),
        grid=(n_tiles,),
        in_specs=[
            pl.BlockSpec((TILE, D_MODEL), lambda g: (g, 0)),
            _const_spec(w_in.shape), _const_spec((D_MODEL, LANES)),
            _const_spec((SUBLANES, 2 * D_MODEL)),
            rope_spec, rope_spec, rope_spec,
            _const_spec((TILE, TILE)),
            _const_spec((LANES, WIDTH)), _const_spec((LANES, WIDTH)), _const_spec((LANES, WIDTH)),
        ],
        out_specs=[
            head_spec, head_spec, head_spec,
            pl.BlockSpec((TILE, 2 * WIDTH), lambda g: (g, 0)),
            pl.BlockSpec((TILE, 2 * D_MODEL), lambda g: (g, 0)),
            pl.BlockSpec((1, SUBLANES, LANES), lambda g: (g, 0, 0)),
        ],
        out_shape=[
            qkv_shape, qkv_shape, qkv_shape,
            jax.ShapeDtypeStruct((rows, 2 * WIDTH), BF16),
            jax.ShapeDtypeStruct((rows, 2 * D_MODEL), BF16),
            jax.ShapeDtypeStruct((n_tiles, SUBLANES, LANES), F32),
        ],
        scratch_shapes=[pltpu.VMEM((1, LANES), F32),
                        pltpu.VMEM((LANES, WIDTH), F32),
                        pltpu.VMEM((TILE, D_MODEL), BF16)],
        compiler_params=pltpu.CompilerParams(
            dimension_semantics=("arbitrary",), vmem_limit_bytes=VMEM_LIMIT),
        name="proj_epilogue",
    )(x2, w, w_fl, params,
      jnp.asarray(rc), jnp.asarray(rsp), jnp.asarray(rsm),
      jnp.asarray(tri, BF16), jnp.asarray(sel_fq, BF16), jnp.asarray(sel_fk, BF16),
      jnp.asarray(sel_mq, BF16))

    n_groups = N_AUG_HEADS // ATT_HEADS
    group_cols = ATT_HEADS // 2 * LANES
    att_tiles = seq // ATT_TILE
    cend = cend[:, 0, :N_HEADS].reshape(batch, tiles_per_seq, N_HEADS).transpose(0, 2, 1) * LOG2E
    cend = jnp.concatenate([cend, jnp.zeros_like(cend)], axis=1).reshape(-1)
    kv_spec = pl.BlockSpec((1, ATT_HEADS, seq, LANES), lambda b, g, i, *_: (b, g, 0, 0))
    row_spec = pl.BlockSpec((ATT_TILE, group_cols), lambda b, g, i, *_: (b * att_tiles + i, g))
    y = pl.pallas_call(
        functools.partial(_attn_kernel, tiles_per_seq=tiles_per_seq),
        grid_spec=pltpu.PrefetchScalarGridSpec(
            num_scalar_prefetch=2,
            grid=(batch, n_groups, att_tiles),
            in_specs=[
                pl.BlockSpec((1, ATT_HEADS, ATT_TILE, LANES), lambda b, g, i, *_: (b, g, i, 0)),
                kv_spec, kv_spec, row_spec,
            ],
            out_specs=row_spec,
            scratch_shapes=[pltpu.VMEM((ATT_HEADS, ATT_TILE, LANES), F32),
                            pltpu.VMEM((ATT_HEADS, ATT_TILE, 1), F32)],
        ),
        out_shape=jax.ShapeDtypeStruct((rows, 2 * WIDTH), BF16),
        compiler_params=pltpu.CompilerParams(
            dimension_semantics=("arbitrary", "arbitrary", "arbitrary"),
            vmem_limit_bytes=VMEM_LIMIT),
        name="flash_attn",
    )(flag, cend, q_all, k_all, v_all, gz)

    out = pl.pallas_call(
        _out_kernel,
        grid=(rows // OUT_TILE,),
        in_specs=[
            pl.BlockSpec((OUT_TILE, 2 * WIDTH), lambda g: (g, 0)),
            pl.BlockSpec((OUT_TILE, 2 * D_MODEL), lambda g: (g, 0)),
            pl.BlockSpec((OUT_TILE, D_MODEL), lambda g: (g, 0)),
            _const_spec((WIDTH, D_MODEL)), _const_spec((WIDTH, D_MODEL)),
            _const_spec((D_MODEL, D_MODEL)),
        ],
        out_specs=pl.BlockSpec((OUT_TILE, D_MODEL), lambda g: (g, 0)),
        out_shape=jax.ShapeDtypeStruct((rows, D_MODEL), F32),
        compiler_params=pltpu.CompilerParams(
            dimension_semantics=("arbitrary",), vmem_limit_bytes=VMEM_LIMIT),
        name="merge_out",
    )(y, gates, x2, w_fox.astype(BF16), w_moba.astype(BF16), w_out.astype(BF16))
    return out


def kernel(x, norm_g, w_in, b_f, b_gate, fox_q_g, fox_k_g, moba_q_g, moba_k_g, w_fox, w_moba, w_out):
    batch, seq, d_model = x.shape
    assert d_model == D_MODEL and seq % ATT_TILE == 0 and seq // MOBA_BLOCK <= GROUP
    assert (batch * seq) % OUT_TILE == 0
    x2 = x.reshape(batch * seq, D_MODEL)
    for layer in range(norm_g.shape[0]):
        x2 = _layer(x2, batch, seq, norm_g[layer], w_in[layer], b_f[layer], b_gate[layer],
                    fox_q_g[layer], fox_k_g[layer], moba_q_g[layer], moba_k_g[layer],
                    w_fox[layer], w_moba[layer], w_out[layer])
    return x2.reshape(batch, seq, D_MODEL)
```

```python
import functools

import numpy as np
import jax
import jax.numpy as jnp
from jax import lax
from jax.experimental import pallas as pl
from jax.experimental.pallas import tpu as pltpu

D_MODEL = 1024
HEAD_DIM = 64
N_HEADS = 8
WIDTH = N_HEADS * HEAD_DIM
ROPE_DIM = HEAD_DIM // 4
ROPE_HALF = ROPE_DIM // 2
ROPE_THETA = 500000.0
MOBA_BLOCK = 256
MOBA_TOPK = 3
RMS_EPS = 1e-6

LANES = 128
SUBLANES = 8
MXU_COLS = 256
TILE = MOBA_BLOCK
OUT_TILE = 1024
ATT_TILE = 512
ATT_HEADS = N_HEADS
FIXED_STABILIZER_MAX_BOUND = 55.0
EXP2_ZERO_BELOW = -152.0
BOUND_MARGIN = 1.02
GROUP = 16
N_AUG_HEADS = 2 * N_HEADS
U_COL_FOX = 6
U_COL_MOBA = GROUP
LOG2E = 1.4426950408889634

C_FQ, C_FK, C_FV, C_FZ = 0, 512, 1024, 1536
C_MQ, C_MK, C_MV, C_MZ = 2048, 2560, 3072, 3584
C_GA, C_GB = 4096, 5120
W_COLS = 6144

VMEM_LIMIT = 52 * 1024 * 1024

F32 = jnp.float32
BF16 = jnp.bfloat16


def _dot(a, b):
    return jnp.dot(a, b, preferred_element_type=F32)


def _dot_nt(a, b):
    return lax.dot_general(a, b, (((1,), (1,)), ((), ())), preferred_element_type=F32)


def _split3(v):
    hi = v.astype(BF16).astype(F32)
    r = v - hi
    mid = r.astype(BF16).astype(F32)
    lo = r - mid
    return hi, mid, lo


def _extra_base(h):
    return (h // 2) * LANES + (HEAD_DIM if h % 2 == 0 else 0)


def _routing_constants():
    sel_fq = np.zeros((LANES, WIDTH), np.float32)
    sel_fk = np.zeros((LANES, WIDTH), np.float32)
    sel_mq = np.zeros((LANES, WIDTH), np.float32)
    for h in range(N_HEADS):
        base = _extra_base(h)
        for part in range(3):
            sel_fq[part * 8 + h, base + part] = 1.0
            sel_fq[24, base + 3 + part] = 1.0
            sel_fk[24, base + part] = 1.0
            sel_fk[part * 8 + h, base + 3 + part] = -1.0
        sel_fq[25, base + U_COL_FOX] = 1.0
        sel_fk[24, base + U_COL_FOX] = 1.0
        for n in range(GROUP):
            sel_mq[h * GROUP + n, base + n] = 1.0
    tri = np.tril(np.ones((TILE, TILE), np.float32))
    return sel_fq, sel_fk, sel_mq, tri


def _rope_tables(seq):
    inv_freq = ROPE_THETA ** (-np.arange(0, ROPE_HALF, dtype=np.float32) * 2.0 / ROPE_DIM)
    ang = np.arange(seq, dtype=np.float32)[:, None] * inv_freq[None, :].astype(np.float32)
    cos, sin = np.cos(ang).astype(np.float32), np.sin(ang).astype(np.float32)
    rc = np.ones((seq, LANES), np.float32)
    rsp = np.zeros((seq, LANES), np.float32)
    rsm = np.zeros((seq, LANES), np.float32)
    for off in (0, HEAD_DIM):
        rc[:, off:off + ROPE_HALF] = cos
        rc[:, off + ROPE_HALF:off + ROPE_DIM] = cos
        rsm[:, off:off + ROPE_HALF] = -sin
        rsp[:, off + ROPE_HALF:off + ROPE_DIM] = sin
    return rc, rsp, rsm


def _proj_kernel(x_ref, w_ref, wfl_ref, prm_ref,
                 rc_ref, rsp_ref, rsm_ref, tri_ref, selfq_ref, selfk_ref, selmq_ref,
                 q_out, k_out, v_out, gz_out, gate_out, cend_out,
                 carry_ref, kmt_ref, h_ref, *, tiles_per_seq):
    t = pl.program_id(0) % tiles_per_seq

    ng_ref = prm_ref.at[0:1, 0:D_MODEL]
    bg_ref = prm_ref.at[1:2, :]
    gfq_ref, gfk_ref, gmq_ref, gmk_ref = (
        prm_ref.at[2:3, n * WIDTH:(n + 1) * WIDTH] for n in range(4))
    bf_ref, big_ref, ufox_ref, umoba_ref = (
        prm_ref.at[3:4, n * LANES:(n + 1) * LANES] for n in range(4))

    @pl.when(t == 0)
    def _():
        carry_ref[...] = jnp.zeros_like(carry_ref)
        kmt_ref[...] = jnp.zeros_like(kmt_ref)

    lane = lax.broadcasted_iota(jnp.int32, (TILE, LANES), 1)
    low_half = lane < HEAD_DIM

    x = x_ref[...]
    ms = jnp.mean(x * x, axis=-1, keepdims=True)
    h_ref[...] = (x * lax.rsqrt(ms + RMS_EPS) * ng_ref[...]).astype(BF16)
    zero = pl.multiple_of(jnp.minimum(pl.program_id(0), 0) * TILE, TILE)

    def proj(c0, width):
        slabs = [_dot(h_ref[pl.ds(zero, TILE), :], w_ref[:, c:c + min(MXU_COLS, c0 + width - c)])
                 for c in range(c0, c0 + width, MXU_COLS)]
        return slabs[0] if len(slabs) == 1 else jnp.concatenate(slabs, axis=1)

    def head_norm(a, g_ref):
        tiles = []
        for p in range(WIDTH // LANES):
            ap = a[:, p * LANES:(p + 1) * LANES]
            sq = ap * ap
            s_lo = jnp.sum(jnp.where(low_half, sq, 0.0), axis=-1, keepdims=True)
            s_hi = jnp.sum(jnp.where(low_half, 0.0, sq), axis=-1, keepdims=True)
            inv_lo = lax.rsqrt(s_lo * (1.0 / HEAD_DIM) + RMS_EPS)
            inv_hi = lax.rsqrt(s_hi * (1.0 / HEAD_DIM) + RMS_EPS)
            scale = jnp.where(low_half, inv_lo, inv_hi)
            tiles.append(ap * scale * g_ref[:, p * LANES:(p + 1) * LANES])
        return tiles

    def rope(y):
        return (y * rc_ref[...] + pltpu.roll(y, ROPE_HALF, 1) * rsp_ref[...]
                + pltpu.roll(y, LANES - ROPE_HALF, 1) * rsm_ref[...])

    def split_tiles(a):
        return [a[:, p * LANES:(p + 1) * LANES] for p in range(WIDTH // LANES)]

    def store_heads(out_ref, head0, tiles, extras):
        for p, y in enumerate(tiles):
            e = extras(p)
            out_ref[0, head0 + 2 * p] = jnp.where(low_half, y, e).astype(BF16)
            out_ref[0, head0 + 2 * p + 1] = jnp.where(low_half, e, y).astype(BF16)

    def silu(z):
        hz = 0.5 * z
        return hz + hz * jnp.tanh(hz)

    def sigmoid(z):
        return 0.5 * jnp.tanh(0.5 * z) + 0.5

    fl = _dot(h_ref[pl.ds(zero, TILE), :], wfl_ref[...]) + bf_ref[...]
    mk_raw = proj(C_MK, WIDTH)
    mq_raw = proj(C_MQ, WIDTH)
    gate_chunks = [(C_GA + half * WIDTH, half * WIDTH) for half in range(2)]
    gate_chunks += [(C_GB + half * WIDTH, D_MODEL + half * WIDTH) for half in range(2)]
    gate_raw = [proj(c0, WIDTH) for c0, _ in gate_chunks]

    logf = jnp.minimum(fl, 0.0) - jnp.log(1.0 + jnp.exp(-jnp.abs(fl)))
    logf = jnp.where(lane < N_HEADS, logf, 0.0)
    l_hi, l_mid, l_lo = _split3(logf)
    packed = (l_hi + pltpu.roll(l_mid, 8, 1) + pltpu.roll(l_lo, 16, 1)).astype(BF16)
    cum = _dot(tri_ref[...], packed)
    fq = proj(C_FQ, WIDTH)
    fk = proj(C_FK, WIDTH)

    mk_tiles = [rope(y) for y in head_norm(mk_raw, gmk_ref)]
    row_id = lax.broadcasted_iota(jnp.int32, (LANES, LANES), 0)
    lane_sq = lax.broadcasted_iota(jnp.int32, (LANES, LANES), 1)
    for p, kr in enumerate(mk_tiles):
        km = jnp.mean(kr, axis=0, keepdims=True)
        hit = (((row_id == (2 * p) * GROUP + t) & (lane_sq < HEAD_DIM))
               | ((row_id == (2 * p + 1) * GROUP + t) & (lane_sq >= HEAD_DIM)))
        blk = kmt_ref[:, p * LANES:(p + 1) * LANES]
        kmt_ref[:, p * LANES:(p + 1) * LANES] = jnp.where(hit, km, blk)
    mk_ones = jnp.where((lane % HEAD_DIM == t) | (lane % HEAD_DIM == U_COL_MOBA), 1.0, 0.0)
    store_heads(k_out, N_HEADS, mk_tiles, lambda p: mk_ones)

    mq_tiles = [rope(y) for y in head_norm(mq_raw, gmq_ref)]
    q_full = jnp.concatenate(mq_tiles, axis=1)
    q_hi = q_full.astype(BF16)
    q_lo = (q_full - q_hi.astype(F32)).astype(BF16)
    kmt = kmt_ref[...]
    k_hi = kmt.astype(BF16)
    k_lo = (kmt - k_hi.astype(F32)).astype(BF16)
    gate = _dot_nt(q_hi, k_hi) + _dot_nt(q_hi, k_lo) + _dot_nt(q_lo, k_hi)

    c = cum + pltpu.roll(cum, LANES - 8, 1) + pltpu.roll(cum, LANES - 16, 1)
    c = jnp.where(lane < N_HEADS, c, 0.0) + carry_ref[...]
    carry_ref[...] = c[TILE - 1:TILE, :]
    cend_out[0] = jnp.broadcast_to(c[TILE - 1:TILE, :], (SUBLANES, LANES))
    c_hi, c_mid, c_lo = _split3(c * LOG2E)
    cparts = (c_hi + pltpu.roll(c_mid, 8, 1) + pltpu.roll(c_lo, 16, 1)
              + jnp.where(lane == 24, 1.0, 0.0)
              + jnp.where(lane == 25, ufox_ref[...], 0.0)).astype(BF16)
    ex_fq = _dot(cparts, selfq_ref[...])
    ex_fk = _dot(cparts, selfk_ref[...])
    fz = proj(C_FZ, WIDTH)
    mz = proj(C_MZ, WIDTH)
    fv = proj(C_FV, WIDTH)
    mv = proj(C_MV, WIDTH)

    v_ones = jnp.where(lane % HEAD_DIM == 0, 1.0, 0.0)
    store_heads(v_out, 0, split_tiles(fv), lambda p: v_ones)
    gz_out[:, 0:WIDTH] = silu(fz).astype(BF16)

    store_heads(q_out, 0, head_norm(fq, gfq_ref), lambda p: ex_fq[:, p * LANES:(p + 1) * LANES])
    store_heads(k_out, 0, head_norm(fk, gfk_ref), lambda p: ex_fk[:, p * LANES:(p + 1) * LANES])

    blk_id = lane % GROUP
    past = blk_id < t
    g = jnp.where(past, gate, -jnp.inf)
    beaten = jnp.zeros((TILE, LANES), jnp.int32)
    for d in range(1, GROUP):
        lower = jnp.where(blk_id >= d, pltpu.roll(g, d, 1), -jnp.inf)
        upper = jnp.where(blk_id < GROUP - d, pltpu.roll(g, LANES - d, 1), -jnp.inf)
        beaten = beaten + jnp.where(lower >= g, 1, 0) + jnp.where(upper > g, 1, 0)
    keep = (past & (beaten < MOBA_TOPK)) | (blk_id == t)
    maskvals = jnp.where(keep, 0.0, -big_ref[...]).astype(BF16)
    ex_mq = _dot(maskvals, selmq_ref[...])

    store_heads(v_out, N_HEADS, split_tiles(mv), lambda p: v_ones)
    gz_out[:, WIDTH:2 * WIDTH] = silu(mz).astype(BF16)

    is_u_lane = lane % HEAD_DIM == U_COL_MOBA
    store_heads(q_out, N_HEADS, mq_tiles,
                lambda p: jnp.where(is_u_lane, umoba_ref[...], ex_mq[:, p * LANES:(p + 1) * LANES]))
    for raw, (_, o0) in zip(gate_raw, gate_chunks):
        gate_out[:, o0:o0 + WIDTH] = sigmoid(raw + bg_ref[:, o0:o0 + WIDTH]).astype(BF16)


def _attn_kernel(flag_ref, cend_ref, q_ref, k_ref, v_ref, gz_ref, o_ref, acc_ref, m_ref, *,
                 tiles_per_seq):
    b, group, i = pl.program_id(0), pl.program_id(1), pl.program_id(2)
    half = ATT_TILE // 2
    start = pl.multiple_of(i * ATT_TILE, ATT_TILE)
    heads = range(ATT_HEADS)

    def first_live_block(hh):
        base = (b * N_AUG_HEADS + ATT_HEADS * group + hh) * tiles_per_seq
        sub = ATT_TILE // TILE
        c_q = cend_ref[base + jnp.maximum(sub * i - 1, 0)]
        dead = jnp.int32(0)
        for j in range(tiles_per_seq // sub - 1):
            c_k = cend_ref[base + sub * j + sub - 1]
            dead += ((j < i) & (c_q - c_k < EXP2_ZERO_BELOW)).astype(jnp.int32)
        return dead

    row = lax.broadcasted_iota(jnp.int32, (half, ATT_TILE), 0)
    col = lax.broadcasted_iota(jnp.int32, (half, ATT_TILE), 1)
    mask_top = (lax.broadcasted_iota(jnp.int32, (half, half), 1)
                <= lax.broadcasted_iota(jnp.int32, (half, half), 0))
    mask_bot = col <= row + half

    def diag_scores(hh):
        kd = k_ref[0, hh, pl.ds(start, ATT_TILE), :]
        s_top = jnp.where(mask_top, _dot_nt(q_ref[0, hh, 0:half, :], kd[0:half]), -jnp.inf)
        s_bot = jnp.where(mask_bot, _dot_nt(q_ref[0, hh, half:ATT_TILE, :], kd), -jnp.inf)
        return s_top, s_bot

    def finish(accs):
        lane = lax.broadcasted_iota(jnp.int32, (ATT_TILE, LANES), 1)
        for pp in range(ATT_HEADS // 2):
            acc_e, acc_o = accs[2 * pp], accs[2 * pp + 1]
            o_e = acc_e * (1.0 / acc_e[:, HEAD_DIM:HEAD_DIM + 1])
            o_o = acc_o * (1.0 / acc_o[:, 0:1])
            o = jnp.where(lane < HEAD_DIM, o_e, o_o)
            cols = slice(pp * LANES, (pp + 1) * LANES)
            o_ref[:, cols] = (o * gz_ref[:, cols].astype(F32)).astype(BF16)

    @pl.when(flag_ref[0] == 1)
    def _fixed_stabilizer():
        acc_ref[...] = jnp.zeros_like(acc_ref)

        def body(j, carry):
            off = pl.multiple_of(j * ATT_TILE, ATT_TILE)
            for hh in heads:
                s = _dot_nt(q_ref[0, hh], k_ref[0, hh, pl.ds(off, ATT_TILE), :])
                acc_ref[hh] += _dot(jnp.exp2(s).astype(BF16),
                                    v_ref[0, hh, pl.ds(off, ATT_TILE), :])
            return carry

        first = first_live_block(0)
        for hh in heads[1:]:
            first = jnp.minimum(first, first_live_block(hh))
        lax.fori_loop(first, i, body, 0)

        accs = []
        for hh in heads:
            vd = v_ref[0, hh, pl.ds(start, ATT_TILE), :]
            s_top, s_bot = diag_scores(hh)
            top = acc_ref[hh, 0:half] + _dot(jnp.exp2(s_top).astype(BF16), vd[0:half])
            bot = acc_ref[hh, half:ATT_TILE] + _dot(jnp.exp2(s_bot).astype(BF16), vd)
            accs.append(jnp.concatenate([top, bot], axis=0))
        finish(accs)

    @pl.when(flag_ref[0] != 1)
    def _online():
        acc_ref[...] = jnp.zeros_like(acc_ref)
        m_ref[...] = jnp.full_like(m_ref, -jnp.inf)

        def update(s, m_old, acc_old, v):
            m_new = jnp.maximum(m_old, jnp.max(s, axis=-1, keepdims=True))
            pv = _dot(jnp.exp2(s - m_new).astype(BF16), v)
            return m_new, jnp.exp2(m_old - m_new) * acc_old + pv

        def body(j, carry):
            off = pl.multiple_of(j * ATT_TILE, ATT_TILE)
            for hh in heads:
                s = _dot_nt(q_ref[0, hh], k_ref[0, hh, pl.ds(off, ATT_TILE), :])
                m_ref[hh], acc_ref[hh] = update(s, m_ref[hh], acc_ref[hh],
                                                v_ref[0, hh, pl.ds(off, ATT_TILE), :])
            return carry

        lax.fori_loop(0, i, body, 0)

        accs = []
        for hh in heads:
            vd = v_ref[0, hh, pl.ds(start, ATT_TILE), :]
            s_top, s_bot = diag_scores(hh)
            _, top = update(s_top, m_ref[hh, 0:half], acc_ref[hh, 0:half], vd[0:half])
            _, bot = update(s_bot, m_ref[hh, half:ATT_TILE], acc_ref[hh, half:ATT_TILE], vd)
            accs.append(jnp.concatenate([top, bot], axis=0))
        finish(accs)


def _out_kernel(y_ref, gate_ref, x_ref, wf_ref, wm_ref, wo_ref, o_ref):
    slabs = [slice(c, c + MXU_COLS) for c in range(0, D_MODEL, MXU_COLS)]
    yf, ym = y_ref[:, 0:WIDTH], y_ref[:, WIDTH:2 * WIDTH]
    merged = []
    for cols in slabs:
        pf = _dot(yf, wf_ref[:, cols])
        pm = _dot(ym, wm_ref[:, cols])
        ga = gate_ref[:, cols].astype(F32)
        gb = gate_ref[:, D_MODEL + cols.start:D_MODEL + cols.stop].astype(F32)
        merged.append((ga * pf + gb * pm).astype(BF16))
    merged = jnp.concatenate(merged, axis=1)
    for cols in slabs:
        o_ref[:, cols] = x_ref[:, cols] + _dot(merged, wo_ref[:, cols])


def _const_spec(shape):
    return pl.BlockSpec(shape, lambda *_: (0,) * len(shape))


def _layer(x2, batch, seq, norm_g, w_in, b_f, b_gate, fox_q_g, fox_k_g, moba_q_g, moba_k_g,
           w_fox, w_moba, w_out):
    rows = batch * seq
    tiles_per_seq = seq // TILE
    n_tiles = rows // TILE
    scale = HEAD_DIM ** -0.5

    w = w_in.astype(BF16)
    w_fl = jnp.pad(w_in[:, W_COLS:], ((0, 0), (0, LANES - N_HEADS))).astype(BF16)

    def logit_bound(gq, gk):
        return 8.0 * LOG2E * BOUND_MARGIN * jnp.max(jnp.abs(gq)) * jnp.max(jnp.abs(gk))

    b_fox, b_moba = logit_bound(fox_q_g, fox_k_g), logit_bound(moba_q_g, moba_k_g)
    fixed_ok = jnp.maximum(b_fox, b_moba) <= FIXED_STABILIZER_MAX_BOUND

    def stabilizer(b):
        u = -(b * (1.0 + 2.0 ** -7)).astype(BF16).astype(F32)
        return jnp.where(fixed_ok, u, 0.0)

    big = jnp.exp2(jnp.ceil(jnp.log2(2.1 * b_moba + 160.0)))
    flag = fixed_ok.astype(jnp.int32).reshape(1)

    def row(*pieces):
        r = jnp.concatenate([jnp.ravel(p).astype(F32) for p in pieces])
        return jnp.pad(r, (0, 2 * D_MODEL - r.shape[0]))

    def lanes(v):
        return jnp.full((LANES,), v, F32)

    q_scale = scale * LOG2E
    params = jnp.stack([
        row(norm_g),
        row(b_gate),
        row(jnp.tile(fox_q_g * q_scale, N_HEADS), jnp.tile(fox_k_g, N_HEADS),
            jnp.tile(moba_q_g * q_scale, N_HEADS), jnp.tile(moba_k_g, N_HEADS)),
        row(b_f, jnp.zeros(LANES - N_HEADS), lanes(big), lanes(stabilizer(b_fox)),
            lanes(stabilizer(b_moba))),
    ] + [jnp.zeros(2 * D_MODEL, F32)] * (SUBLANES - 4))

    sel_fq, sel_fk, sel_mq, tri = _routing_constants()
    rc, rsp, rsm = _rope_tables(seq)

    rope_spec = pl.BlockSpec((TILE, LANES), lambda g: (g % tiles_per_seq, 0))
    head_spec = pl.BlockSpec((1, N_AUG_HEADS, TILE, LANES),
                             lambda g: (g // tiles_per_seq, 0, g % tiles_per_seq, 0))
    qkv_shape = jax.ShapeDtypeStruct((batch, N_AUG_HEADS, seq, LANES), BF16)

    q_all, k_all, v_all, gz, gates, cend = pl.pallas_call(
        functools.partial(_proj_kernel, tiles_per_seq=tiles_per_seq),
        grid=(n_tiles,),
        in_specs=[
            pl.BlockSpec((TILE, D_MODEL), lambda g: (g, 0)),
            _const_spec(w_in.shape), _const_spec((D_MODEL, LANES)),
            _const_spec((SUBLANES, 2 * D_MODEL)),
            rope_spec, rope_spec, rope_spec,
            _const_spec((TILE, TILE)),
            _const_spec((LANES, WIDTH)), _const_spec((LANES, WIDTH)), _const_spec((LANES, WIDTH)),
        ],
        out_specs=[
            head_spec, head_spec, head_spec,
            pl.BlockSpec((TILE, 2 * WIDTH), lambda g: (g, 0)),
            pl.BlockSpec((TILE, 2 * D_MODEL), lambda g: (g, 0)),
            pl.BlockSpec((1, SUBLANES, LANES), lambda g: (g, 0, 0)),
        ],
        out_shape=[
            qkv_shape, qkv_shape, qkv_shape,
            jax.ShapeDtypeStruct((rows, 2 * WIDTH), BF16),
            jax.ShapeDtypeStruct((rows, 2 * D_MODEL), BF16),
            jax.ShapeDtypeStruct((n_tiles, SUBLANES, LANES), F32),
        ],
        scratch_shapes=[pltpu.VMEM((1, LANES), F32),
                        pltpu.VMEM((LANES, WIDTH), F32),
                        pltpu.VMEM((TILE, D_MODEL), BF16)],
        compiler_params=pltpu.CompilerParams(
            dimension_semantics=("arbitrary",), vmem_limit_bytes=VMEM_LIMIT),
        name="proj_epilogue",
    )(x2, w, w_fl, params,
      jnp.asarray(rc), jnp.asarray(rsp), jnp.asarray(rsm),
      jnp.asarray(tri, BF16), jnp.asarray(sel_fq, BF16), jnp.asarray(sel_fk, BF16),
      jnp.asarray(sel_mq, BF16))

    n_groups = N_AUG_HEADS // ATT_HEADS
    group_cols = ATT_HEADS // 2 * LANES
    att_tiles = seq // ATT_TILE
    cend = cend[:, 0, :N_HEADS].reshape(batch, tiles_per_seq, N_HEADS).transpose(0, 2, 1) * LOG2E
    cend = jnp.concatenate([cend, jnp.zeros_like(cend)], axis=1).reshape(-1)
    kv_spec = pl.BlockSpec((1, ATT_HEADS, seq, LANES), lambda b, g, i, *_: (b, g, 0, 0))
    row_spec = pl.BlockSpec((ATT_TILE, group_cols), lambda b, g, i, *_: (b * att_tiles + i, g))
    y = pl.pallas_call(
        functools.partial(_attn_kernel, tiles_per_seq=tiles_per_seq),
        grid_spec=pltpu.PrefetchScalarGridSpec(
            num_scalar_prefetch=2,
            grid=(batch, n_groups, att_tiles),
            in_specs=[
                pl.BlockSpec((1, ATT_HEADS, ATT_TILE, LANES), lambda b, g, i, *_: (b, g, i, 0)),
                kv_spec, kv_spec, row_spec,
            ],
            out_specs=row_spec,
            scratch_shapes=[pltpu.VMEM((ATT_HEADS, ATT_TILE, LANES), F32),
                            pltpu.VMEM((ATT_HEADS, ATT_TILE, 1), F32)],
        ),
        out_shape=jax.ShapeDtypeStruct((rows, 2 * WIDTH), BF16),
        compiler_params=pltpu.CompilerParams(
            dimension_semantics=("arbitrary", "arbitrary", "arbitrary"),
            vmem_limit_bytes=VMEM_LIMIT),
        name="flash_attn",
    )(flag, cend, q_all, k_all, v_all, gz)

    out = pl.pallas_call(
        _out_kernel,
        grid=(rows // OUT_TILE,),
        in_specs=[
            pl.BlockSpec((OUT_TILE, 2 * WIDTH), lambda g: (g, 0)),
            pl.BlockSpec((OUT_TILE, 2 * D_MODEL), lambda g: (g, 0)),
            pl.BlockSpec((OUT_TILE, D_MODEL), lambda g: (g, 0)),
            _const_spec((WIDTH, D_MODEL)), _const_spec((WIDTH, D_MODEL)),
            _const_spec((D_MODEL, D_MODEL)),
        ],
        out_specs=pl.BlockSpec((OUT_TILE, D_MODEL), lambda g: (g, 0)),
        out_shape=jax.ShapeDtypeStruct((rows, D_MODEL), F32),
        compiler_params=pltpu.CompilerParams(
            dimension_semantics=("arbitrary",), vmem_limit_bytes=VMEM_LIMIT),
        name="merge_out",
    )(y, gates, x2, w_fox.astype(BF16), w_moba.astype(BF16), w_out.astype(BF16))
    return out


def kernel(x, norm_g, w_in, b_f, b_gate, fox_q_g, fox_k_g, moba_q_g, moba_k_g, w_fox, w_moba, w_out):
    batch, seq, d_model = x.shape
    assert d_model == D_MODEL and seq % ATT_TILE == 0 and seq // MOBA_BLOCK <= GROUP
    assert (batch * seq) % OUT_TILE == 0
    x2 = x.reshape(batch * seq, D_MODEL)
    for layer in range(norm_g.shape[0]):
        x2 = _layer(x2, batch, seq, norm_g[layer], w_in[layer], b_f[layer], b_gate[layer],
                    fox_q_g[layer], fox_k_g[layer], moba_q_g[layer], moba_k_g[layer],
                    w_fox[layer], w_moba[layer], w_out[layer])
    return x2.reshape(batch, seq, D_MODEL)
```

```python
import functools

import numpy as np
import jax
import jax.numpy as jnp
from jax import lax
from jax.experimental import pallas as pl
from jax.experimental.pallas import tpu as pltpu

D_MODEL = 1024
HEAD_DIM = 64
N_HEADS = 8
WIDTH = N_HEADS * HEAD_DIM
ROPE_DIM = HEAD_DIM // 4
ROPE_HALF = ROPE_DIM // 2
ROPE_THETA = 500000.0
MOBA_BLOCK = 256
MOBA_TOPK = 3
RMS_EPS = 1e-6

LANES = 128
SUBLANES = 8
MXU_COLS = 256
TILE = MOBA_BLOCK
OUT_TILE = 1024
ATT_TILE = 512
ATT_HEADS = N_HEADS
FIXED_STABILIZER_MAX_BOUND = 55.0
EXP2_ZERO_BELOW = -152.0
BOUND_MARGIN = 1.02
GROUP = 16
N_AUG_HEADS = 2 * N_HEADS
U_COL_FOX = 6
U_COL_MOBA = GROUP
LOG2E = 1.4426950408889634

C_FQ, C_FK, C_FV, C_FZ = 0, 512, 1024, 1536
C_MQ, C_MK, C_MV, C_MZ = 2048, 2560, 3072, 3584
C_GA, C_GB = 4096, 5120
W_COLS = 6144

VMEM_LIMIT = 52 * 1024 * 1024

F32 = jnp.float32
BF16 = jnp.bfloat16


def _dot(a, b):
    return jnp.dot(a, b, preferred_element_type=F32)


def _dot_nt(a, b):
    return lax.dot_general(a, b, (((1,), (1,)), ((), ())), preferred_element_type=F32)


def _split3(v):
    hi = v.astype(BF16).astype(F32)
    r = v - hi
    mid = r.astype(BF16).astype(F32)
    lo = r - mid
    return hi, mid, lo


def _extra_base(h):
    return (h // 2) * LANES + (HEAD_DIM if h % 2 == 0 else 0)


def _routing_constants():
    sel_fq = np.zeros((LANES, WIDTH), np.float32)
    sel_fk = np.zeros((LANES, WIDTH), np.float32)
    sel_mq = np.zeros((LANES, WIDTH), np.float32)
    for h in range(N_HEADS):
        base = _extra_base(h)
        for part in range(3):
            sel_fq[part * 8 + h, base + part] = 1.0
            sel_fq[24, base + 3 + part] = 1.0
            sel_fk[24, base + part] = 1.0
            sel_fk[part * 8 + h, base + 3 + part] = -1.0
        sel_fq[25, base + U_COL_FOX] = 1.0
        sel_fk[24, base + U_COL_FOX] = 1.0
        for n in range(GROUP):
            sel_mq[h * GROUP + n, base + n] = 1.0
    tri = np.tril(np.ones((TILE, TILE), np.float32))
    return sel_fq, sel_fk, sel_mq, tri


def _rope_tables(seq):
    inv_freq = ROPE_THETA ** (-np.arange(0, ROPE_HALF, dtype=np.float32) * 2.0 / ROPE_DIM)
    ang = np.arange(seq, dtype=np.float32)[:, None] * inv_freq[None, :].astype(np.float32)
    cos, sin = np.cos(ang).astype(np.float32), np.sin(ang).astype(np.float32)
    rc = np.ones((seq, LANES), np.float32)
    rsp = np.zeros((seq, LANES), np.float32)
    rsm = np.zeros((seq, LANES), np.float32)
    for off in (0, HEAD_DIM):
        rc[:, off:off + ROPE_HALF] = cos
        rc[:, off + ROPE_HALF:off + ROPE_DIM] = cos
        rsm[:, off:off + ROPE_HALF] = -sin
        rsp[:, off + ROPE_HALF:off + ROPE_DIM] = sin
    return rc, rsp, rsm


def _proj_kernel(x_ref, w_ref, wfl_ref, prm_ref,
                 rc_ref, rsp_ref, rsm_ref, tri_ref, selfq_ref, selfk_ref, selmq_ref,
                 q_out, k_out, v_out, gz_out, gate_out, cend_out,
                 carry_ref, kmt_ref, h_ref, *, tiles_per_seq):
    t = pl.program_id(0) % tiles_per_seq

    ng_ref = prm_ref.at[0:1, 0:D_MODEL]
    bg_ref = prm_ref.at[1:2, :]
    gfq_ref, gfk_ref, gmq_ref, gmk_ref = (
        prm_ref.at[2:3, n * WIDTH:(n + 1) * WIDTH] for n in range(4))
    bf_ref, big_ref, ufox_ref, umoba_ref = (
        prm_ref.at[3:4, n * LANES:(n + 1) * LANES] for n in range(4))

    @pl.when(t == 0)
    def _():
        carry_ref[...] = jnp.zeros_like(carry_ref)
        kmt_ref[...] = jnp.zeros_like(kmt_ref)

    lane = lax.broadcasted_iota(jnp.int32, (TILE, LANES), 1)
    low_half = lane < HEAD_DIM

    x = x_ref[...]
    ms = jnp.mean(x * x, axis=-1, keepdims=True)
    h_ref[...] = (x * lax.rsqrt(ms + RMS_EPS) * ng_ref[...]).astype(BF16)
    zero = pl.multiple_of(jnp.minimum(pl.program_id(0), 0) * TILE, TILE)

    def proj(c0, width):
        slabs = [_dot(h_ref[pl.ds(zero, TILE), :], w_ref[:, c:c + min(MXU_COLS, c0 + width - c)])
                 for c in range(c0, c0 + width, MXU_COLS)]
        return slabs[0] if len(slabs) == 1 else jnp.concatenate(slabs, axis=1)

    def head_norm(a, g_ref):
        tiles = []
        for p in range(WIDTH // LANES):
            ap = a[:, p * LANES:(p + 1) * LANES]
            sq = ap * ap
            s_lo = jnp.sum(jnp.where(low_half, sq, 0.0), axis=-1, keepdims=True)
            s_hi = jnp.sum(jnp.where(low_half, 0.0, sq), axis=-1, keepdims=True)
            inv_lo = lax.rsqrt(s_lo * (1.0 / HEAD_DIM) + RMS_EPS)
            inv_hi = lax.rsqrt(s_hi * (1.0 / HEAD_DIM) + RMS_EPS)
            scale = jnp.where(low_half, inv_lo, inv_hi)
            tiles.append(ap * scale * g_ref[:, p * LANES:(p + 1) * LANES])
        return tiles

    def rope(y):
        return (y * rc_ref[...] + pltpu.roll(y, ROPE_HALF, 1) * rsp_ref[...]
                + pltpu.roll(y, LANES - ROPE_HALF, 1) * rsm_ref[...])

    def split_tiles(a):
        return [a[:, p * LANES:(p + 1) * LANES] for p in range(WIDTH // LANES)]

    def store_heads(out_ref, head0, tiles, extras):
        for p, y in enumerate(tiles):
            e = extras(p)
            out_ref[0, head0 + 2 * p] = jnp.where(low_half, y, e).astype(BF16)
            out_ref[0, head0 + 2 * p + 1] = jnp.where(low_half, e, y).astype(BF16)

    def silu(z):
        hz = 0.5 * z
        return hz + hz * jnp.tanh(hz)

    def sigmoid(z):
        return 0.5 * jnp.tanh(0.5 * z) + 0.5

    fl = _dot(h_ref[pl.ds(zero, TILE), :], wfl_ref[...]) + bf_ref[...]
    mk_raw = proj(C_MK, WIDTH)
    mq_raw = proj(C_MQ, WIDTH)
    gate_chunks = [(C_GA + half * WIDTH, half * WIDTH) for half in range(2)]
    gate_chunks += [(C_GB + half * WIDTH, D_MODEL + half * WIDTH) for half in range(2)]
    gate_raw = [proj(c0, WIDTH) for c0, _ in gate_chunks]

    logf = jnp.minimum(fl, 0.0) - jnp.log(1.0 + jnp.exp(-jnp.abs(fl)))
    logf = jnp.where(lane < N_HEADS, logf, 0.0)
    l_hi, l_mid, l_lo = _split3(logf)
    packed = (l_hi + pltpu.roll(l_mid, 8, 1) + pltpu.roll(l_lo, 16, 1)).astype(BF16)
    cum = _dot(tri_ref[...], packed)
    fq = proj(C_FQ, WIDTH)
    fk = proj(C_FK, WIDTH)

    mk_tiles = [rope(y) for y in head_norm(mk_raw, gmk_ref)]
    row_id = lax.broadcasted_iota(jnp.int32, (LANES, LANES), 0)
    lane_sq = lax.broadcasted_iota(jnp.int32, (LANES, LANES), 1)
    for p, kr in enumerate(mk_tiles):
        km = jnp.mean(kr, axis=0, keepdims=True)
        hit = (((row_id == (2 * p) * GROUP + t) & (lane_sq < HEAD_DIM))
               | ((row_id == (2 * p + 1) * GROUP + t) & (lane_sq >= HEAD_DIM)))
        blk = kmt_ref[:, p * LANES:(p + 1) * LANES]
        kmt_ref[:, p * LANES:(p + 1) * LANES] = jnp.where(hit, km, blk)
    mk_ones = jnp.where((lane % HEAD_DIM == t) | (lane % HEAD_DIM == U_COL_MOBA), 1.0, 0.0)
    store_heads(k_out, N_HEADS, mk_tiles, lambda p: mk_ones)

    mq_tiles = [rope(y) for y in head_norm(mq_raw, gmq_ref)]
    q_full = jnp.concatenate(mq_tiles, axis=1)
    q_hi = q_full.astype(BF16)
    q_lo = (q_full - q_hi.astype(F32)).astype(BF16)
    kmt = kmt_ref[...]
    k_hi = kmt.astype(BF16)
    k_lo = (kmt - k_hi.astype(F32)).astype(BF16)
    gate = _dot_nt(q_hi, k_hi) + _dot_nt(q_hi, k_lo) + _dot_nt(q_lo, k_hi)

    c = cum + pltpu.roll(cum, LANES - 8, 1) + pltpu.roll(cum, LANES - 16, 1)
    c = jnp.where(lane < N_HEADS, c, 0.0) + carry_ref[...]
    carry_ref[...] = c[TILE - 1:TILE, :]
    cend_out[0] = jnp.broadcast_to(c[TILE - 1:TILE, :], (SUBLANES, LANES))
    c_hi, c_mid, c_lo = _split3(c * LOG2E)
    cparts = (c_hi + pltpu.roll(c_mid, 8, 1) + pltpu.roll(c_lo, 16, 1)
              + jnp.where(lane == 24, 1.0, 0.0)
              + jnp.where(lane == 25, ufox_ref[...], 0.0)).astype(BF16)
    ex_fq = _dot(cparts, selfq_ref[...])
    ex_fk = _dot(cparts, selfk_ref[...])
    fz = proj(C_FZ, WIDTH)
    mz = proj(C_MZ, WIDTH)
    fv = proj(C_FV, WIDTH)
    mv = proj(C_MV, WIDTH)

    v_ones = jnp.where(lane % HEAD_DIM == 0, 1.0, 0.0)
    store_heads(v_out, 0, split_tiles(fv), lambda p: v_ones)
    gz_out[:, 0:WIDTH] = silu(fz).astype(BF16)

    store_heads(q_out, 0, head_norm(fq, gfq_ref), lambda p: ex_fq[:, p * LANES:(p + 1) * LANES])
    store_heads(k_out, 0, head_norm(fk, gfk_ref), lambda p: ex_fk[:, p * LANES:(p + 1) * LANES])

    blk_id = lane % GROUP
    past = blk_id < t
    g = jnp.where(past, gate, -jnp.inf)
    beaten = jnp.zeros((TILE, LANES), jnp.int32)
    for d in range(1, GROUP):
        lower = jnp.where(blk_id >= d, pltpu.roll(g, d, 1), -jnp.inf)
        upper = jnp.where(blk_id < GROUP - d, pltpu.roll(g, LANES - d, 1), -jnp.inf)
        beaten = beaten + jnp.where(lower >= g, 1, 0) + jnp.where(upper > g, 1, 0)
    keep = (past & (beaten < MOBA_TOPK)) | (blk_id == t)
    maskvals = jnp.where(keep, 0.0, -big_ref[...]).astype(BF16)
    ex_mq = _dot(maskvals, selmq_ref[...])

    store_heads(v_out, N_HEADS, split_tiles(mv), lambda p: v_ones)
    gz_out[:, WIDTH:2 * WIDTH] = silu(mz).astype(BF16)

    is_u_lane = lane % HEAD_DIM == U_COL_MOBA
    store_heads(q_out, N_HEADS, mq_tiles,
                lambda p: jnp.where(is_u_lane, umoba_ref[...], ex_mq[:, p * LANES:(p + 1) * LANES]))
    for raw, (_, o0) in zip(gate_raw, gate_chunks):
        gate_out[:, o0:o0 + WIDTH] = sigmoid(raw + bg_ref[:, o0:o0 + WIDTH]).astype(BF16)


def _attn_kernel(flag_ref, cend_ref, q_ref, k_ref, v_ref, gz_ref, o_ref, acc_ref, m_ref, *,
                 tiles_per_seq):
    b, group, i = pl.program_id(0), pl.program_id(1), pl.program_id(2)
    half = ATT_TILE // 2
    start = pl.multiple_of(i * ATT_TILE, ATT_TILE)
    heads = range(ATT_HEADS)

    def first_live_block(hh):
        base = (b * N_AUG_HEADS + ATT_HEADS * group + hh) * tiles_per_seq
        sub = ATT_TILE // TILE
        c_q = cend_ref[base + jnp.maximum(sub * i - 1, 0)]
        dead = jnp.int32(0)
        for j in range(tiles_per_seq // sub - 1):
            c_k = cend_ref[base + sub * j + sub - 1]
            dead += ((j < i) & (c_q - c_k < EXP2_ZERO_BELOW)).astype(jnp.int32)
        return dead

    row = lax.broadcasted_iota(jnp.int32, (half, ATT_TILE), 0)
    col = lax.broadcasted_iota(jnp.int32, (half, ATT_TILE), 1)
    mask_top = (lax.broadcasted_iota(jnp.int32, (half, half), 1)
                <= lax.broadcasted_iota(jnp.int32, (half, half), 0))
    mask_bot = col <= row + half

    def diag_scores(hh):
        kd = k_ref[0, hh, pl.ds(start, ATT_TILE), :]
        s_top = jnp.where(mask_top, _dot_nt(q_ref[0, hh, 0:half, :], kd[0:half]), -jnp.inf)
        s_bot = jnp.where(mask_bot, _dot_nt(q_ref[0, hh, half:ATT_TILE, :], kd), -jnp.inf)
        return s_top, s_bot

    def finish(accs):
        lane = lax.broadcasted_iota(jnp.int32, (ATT_TILE, LANES), 1)
        for pp in range(ATT_HEADS // 2):
            acc_e, acc_o = accs[2 * pp], accs[2 * pp + 1]
            o_e = acc_e * (1.0 / acc_e[:, HEAD_DIM:HEAD_DIM + 1])
            o_o = acc_o * (1.0 / acc_o[:, 0:1])
            o = jnp.where(lane < HEAD_DIM, o_e, o_o)
            cols = slice(pp * LANES, (pp + 1) * LANES)
            o_ref[:, cols] = (o * gz_ref[:, cols].astype(F32)).astype(BF16)

    @pl.when(flag_ref[0] == 1)
    def _fixed_stabilizer():
        acc_ref[...] = jnp.zeros_like(acc_ref)

        def add_blocks(j, n_blocks):
            for hh in heads:
                pv = None
                for jj in range(n_blocks):
                    off = pl.multiple_of((j + jj) * ATT_TILE, ATT_TILE)
                    s = _dot_nt(q_ref[0, hh], k_ref[0, hh, pl.ds(off, ATT_TILE), :])
                    d = _dot(jnp.exp2(s).astype(BF16), v_ref[0, hh, pl.ds(off, ATT_TILE), :])
                    pv = d if pv is None else pv + d
                acc_ref[hh] += pv

        first = first_live_block(0)
        for hh in heads[1:]:
            first = jnp.minimum(first, first_live_block(hh))
        n_live = i - first
        odd = n_live & 1

        @pl.when(odd == 1)
        def _():
            add_blocks(first, 1)

        def pair(p, carry):
            add_blocks(first + odd + 2 * p, 2)
            return carry

        lax.fori_loop(0, n_live >> 1, pair, 0)

        accs = []
        for hh in heads:
            vd = v_ref[0, hh, pl.ds(start, ATT_TILE), :]
            s_top, s_bot = diag_scores(hh)
            top = acc_ref[hh, 0:half] + _dot(jnp.exp2(s_top).astype(BF16), vd[0:half])
            bot = acc_ref[hh, half:ATT_TILE] + _dot(jnp.exp2(s_bot).astype(BF16), vd)
            accs.append(jnp.concatenate([top, bot], axis=0))
        finish(accs)

    @pl.when(flag_ref[0] != 1)
    def _online():
        acc_ref[...] = jnp.zeros_like(acc_ref)
        m_ref[...] = jnp.full_like(m_ref, -jnp.inf)

        def update(s, m_old, acc_old, v):
            m_new = jnp.maximum(m_old, jnp.max(s, axis=-1, keepdims=True))
            pv = _dot(jnp.exp2(s - m_new).astype(BF16), v)
            return m_new, jnp.exp2(m_old - m_new) * acc_old + pv

        def body(j, carry):
            off = pl.multiple_of(j * ATT_TILE, ATT_TILE)
            for hh in heads:
                s = _dot_nt(q_ref[0, hh], k_ref[0, hh, pl.ds(off, ATT_TILE), :])
                m_ref[hh], acc_ref[hh] = update(s, m_ref[hh], acc_ref[hh],
                                                v_ref[0, hh, pl.ds(off, ATT_TILE), :])
            return carry

        lax.fori_loop(0, i, body, 0)

        accs = []
        for hh in heads:
            vd = v_ref[0, hh, pl.ds(start, ATT_TILE), :]
            s_top, s_bot = diag_scores(hh)
            _, top = update(s_top, m_ref[hh, 0:half], acc_ref[hh, 0:half], vd[0:half])
            _, bot = update(s_bot, m_ref[hh, half:ATT_TILE], acc_ref[hh, half:ATT_TILE], vd)
            accs.append(jnp.concatenate([top, bot], axis=0))
        finish(accs)


def _out_kernel(y_ref, gate_ref, x_ref, wf_ref, wm_ref, wo_ref, o_ref):
    slabs = [slice(c, c + MXU_COLS) for c in range(0, D_MODEL, MXU_COLS)]
    yf, ym = y_ref[:, 0:WIDTH], y_ref[:, WIDTH:2 * WIDTH]
    merged = []
    for cols in slabs:
        pf = _dot(yf, wf_ref[:, cols])
        pm = _dot(ym, wm_ref[:, cols])
        ga = gate_ref[:, cols].astype(F32)
        gb = gate_ref[:, D_MODEL + cols.start:D_MODEL + cols.stop].astype(F32)
        merged.append((ga * pf + gb * pm).astype(BF16))
    merged = jnp.concatenate(merged, axis=1)
    for cols in slabs:
        o_ref[:, cols] = x_ref[:, cols] + _dot(merged, wo_ref[:, cols])


def _const_spec(shape):
    return pl.BlockSpec(shape, lambda *_: (0,) * len(shape))


def _layer(x2, batch, seq, norm_g, w_in, b_f, b_gate, fox_q_g, fox_k_g, moba_q_g, moba_k_g,
           w_fox, w_moba, w_out):
    rows = batch * seq
    tiles_per_seq = seq // TILE
    n_tiles = rows // TILE
    scale = HEAD_DIM ** -0.5

    w = w_in.astype(BF16)
    w_fl = jnp.pad(w_in[:, W_COLS:], ((0, 0), (0, LANES - N_HEADS))).astype(BF16)

    def logit_bound(gq, gk):
        return 8.0 * LOG2E * BOUND_MARGIN * jnp.max(jnp.abs(gq)) * jnp.max(jnp.abs(gk))

    b_fox, b_moba = logit_bound(fox_q_g, fox_k_g), logit_bound(moba_q_g, moba_k_g)
    fixed_ok = jnp.maximum(b_fox, b_moba) <= FIXED_STABILIZER_MAX_BOUND

    def stabilizer(b):
        u = -(b * (1.0 + 2.0 ** -7)).astype(BF16).astype(F32)
        return jnp.where(fixed_ok, u, 0.0)

    big = jnp.exp2(jnp.ceil(jnp.log2(2.1 * b_moba + 160.0)))
    flag = fixed_ok.astype(jnp.int32).reshape(1)

    def row(*pieces):
        r = jnp.concatenate([jnp.ravel(p).astype(F32) for p in pieces])
        return jnp.pad(r, (0, 2 * D_MODEL - r.shape[0]))

    def lanes(v):
        return jnp.full((LANES,), v, F32)

    q_scale = scale * LOG2E
    params = jnp.stack([
        row(norm_g),
        row(b_gate),
        row(jnp.tile(fox_q_g * q_scale, N_HEADS), jnp.tile(fox_k_g, N_HEADS),
            jnp.tile(moba_q_g * q_scale, N_HEADS), jnp.tile(moba_k_g, N_HEADS)),
        row(b_f, jnp.zeros(LANES - N_HEADS), lanes(big), lanes(stabilizer(b_fox)),
            lanes(stabilizer(b_moba))),
    ] + [jnp.zeros(2 * D_MODEL, F32)] * (SUBLANES - 4))

    sel_fq, sel_fk, sel_mq, tri = _routing_constants()
    rc, rsp, rsm = _rope_tables(seq)

    rope_spec = pl.BlockSpec((TILE, LANES), lambda g: (g % tiles_per_seq, 0))
    head_spec = pl.BlockSpec((1, N_AUG_HEADS, TILE, LANES),
                             lambda g: (g // tiles_per_seq, 0, g % tiles_per_seq, 0))
    qkv_shape = jax.ShapeDtypeStruct((batch, N_AUG_HEADS, seq, LANES), BF16)

    q_all, k_all, v_all, gz, gates, cend = pl.pallas_call(
        functools.partial(_proj_kernel, tiles_per_seq=tiles_per_seq),
        grid=(n_tiles,),
        in_specs=[
            pl.BlockSpec((TILE, D_MODEL), lambda g: (g, 0)),
            _const_spec(w_in.shape), _const_spec((D_MODEL, LANES)),
            _const_spec((SUBLANES, 2 * D_MODEL)),
            rope_spec, rope_spec, rope_spec,
            _const_spec((TILE, TILE)),
            _const_spec((LANES, WIDTH)), _const_spec((LANES, WIDTH)), _const_spec((LANES, WIDTH)),
        ],
        out_specs=[
            head_spec, head_spec, head_spec,
            pl.BlockSpec((TILE, 2 * WIDTH), lambda g: (g, 0)),
            pl.BlockSpec((TILE, 2 * D_MODEL), lambda g: (g, 0)),
            pl.BlockSpec((1, SUBLANES, LANES), lambda g: (g, 0, 0)),
        ],
        out_shape=[
            qkv_shape, qkv_shape, qkv_shape,
            jax.ShapeDtypeStruct((rows, 2 * WIDTH), BF16),
            jax.ShapeDtypeStruct((rows, 2 * D_MODEL), BF16),
            jax.ShapeDtypeStruct((n_tiles, SUBLANES, LANES), F32),
        ],
        scratch_shapes=[pltpu.VMEM((1, LANES), F32),
                        pltpu.VMEM((LANES, WIDTH), F32),
                        pltpu.VMEM((TILE, D_MODEL), BF16)],
        compiler_params=pltpu.CompilerParams(
            dimension_semantics=("arbitrary",), vmem_limit_bytes=VMEM_LIMIT),
        name="proj_epilogue",
    )(x2, w, w_fl, params,
      jnp.asarray(rc), jnp.asarray(rsp), jnp.asarray(rsm),
      jnp.asarray(tri, BF16), jnp.asarray(sel_fq, BF16), jnp.asarray(sel_fk, BF16),
      jnp.asarray(sel_mq, BF16))

    n_groups = N_AUG_HEADS // ATT_HEADS
    group_cols = ATT_HEADS // 2 * LANES
    att_tiles = seq // ATT_TILE
    cend = cend[:, 0, :N_HEADS].reshape(batch, tiles_per_seq, N_HEADS).transpose(0, 2, 1) * LOG2E
    cend = jnp.concatenate([cend, jnp.zeros_like(cend)], axis=1).reshape(-1)
    kv_spec = pl.BlockSpec((1, ATT_HEADS, seq, LANES), lambda b, g, i, *_: (b, g, 0, 0))
    row_spec = pl.BlockSpec((ATT_TILE, group_cols), lambda b, g, i, *_: (b * att_tiles + i, g))
    y = pl.pallas_call(
        functools.partial(_attn_kernel, tiles_per_seq=tiles_per_seq),
        grid_spec=pltpu.PrefetchScalarGridSpec(
            num_scalar_prefetch=2,
            grid=(batch, n_groups, att_tiles),
            in_specs=[
                pl.BlockSpec((1, ATT_HEADS, ATT_TILE, LANES), lambda b, g, i, *_: (b, g, i, 0)),
                kv_spec, kv_spec, row_spec,
            ],
            out_specs=row_spec,
            scratch_shapes=[pltpu.VMEM((ATT_HEADS, ATT_TILE, LANES), F32),
                            pltpu.VMEM((ATT_HEADS, ATT_TILE, 1), F32)],
        ),
        out_shape=jax.ShapeDtypeStruct((rows, 2 * WIDTH), BF16),
        compiler_params=pltpu.CompilerParams(
            dimension_semantics=("arbitrary", "arbitrary", "arbitrary"),
            vmem_limit_bytes=VMEM_LIMIT),
        name="flash_attn",
    )(flag, cend, q_all, k_all, v_all, gz)

    out = pl.pallas_call(
        _out_kernel,
        grid=(rows // OUT_TILE,),
        in_specs=[
            pl.BlockSpec((OUT_TILE, 2 * WIDTH), lambda g: (g, 0)),
            pl.BlockSpec((OUT_TILE, 2 * D_MODEL), lambda g: (g, 0)),
            pl.BlockSpec((OUT_TILE, D_MODEL), lambda g: (g, 0)),
            _const_spec((WIDTH, D_MODEL)), _const_spec((WIDTH, D_MODEL)),
            _const_spec((D_MODEL, D_MODEL)),
        ],
        out_specs=pl.BlockSpec((OUT_TILE, D_MODEL), lambda g: (g, 0)),
        out_shape=jax.ShapeDtypeStruct((rows, D_MODEL), F32),
        compiler_params=pltpu.CompilerParams(
            dimension_semantics=("arbitrary",), vmem_limit_bytes=VMEM_LIMIT),
        name="merge_out",
    )(y, gates, x2, w_fox.astype(BF16), w_moba.astype(BF16), w_out.astype(BF16))
    return out


def kernel(x, norm_g, w_in, b_f, b_gate, fox_q_g, fox_k_g, moba_q_g, moba_k_g, w_fox, w_moba, w_out):
    batch, seq, d_model = x.shape
    assert d_model == D_MODEL and seq % ATT_TILE == 0 and seq // MOBA_BLOCK <= GROUP
    assert (batch * seq) % OUT_TILE == 0
    x2 = x.reshape(batch * seq, D_MODEL)
    for layer in range(norm_g.shape[0]):
        x2 = _layer(x2, batch, seq, norm_g[layer], w_in[layer], b_f[layer], b_gate[layer],
                    fox_q_g[layer], fox_k_g[layer], moba_q_g[layer], moba_k_g[layer],
                    w_fox[layer], w_moba[layer], w_out[layer])
    return x2.reshape(batch, seq, D_MODEL)
```

```python
import functools

import numpy as np
import jax
import jax.numpy as jnp
from jax import lax
from jax.experimental import pallas as pl
from jax.experimental.pallas import tpu as pltpu

D_MODEL = 1024
HEAD_DIM = 64
N_HEADS = 8
WIDTH = N_HEADS * HEAD_DIM
ROPE_DIM = HEAD_DIM // 4
ROPE_HALF = ROPE_DIM // 2
ROPE_THETA = 500000.0
MOBA_BLOCK = 256
MOBA_TOPK = 3
RMS_EPS = 1e-6

LANES = 128
SUBLANES = 8
MXU_COLS = 256
TILE = MOBA_BLOCK
OUT_TILE = 1024
ATT_TILE = 512
ATT_HEADS = N_HEADS
FIXED_STABILIZER_MAX_BOUND = 55.0
EXP2_ZERO_BELOW = -152.0
BOUND_MARGIN = 1.02
GROUP = 16
N_AUG_HEADS = 2 * N_HEADS
U_COL_FOX = 6
U_COL_MOBA = GROUP
LOG2E = 1.4426950408889634

C_FQ, C_FK, C_FV, C_FZ = 0, 512, 1024, 1536
C_MQ, C_MK, C_MV, C_MZ = 2048, 2560, 3072, 3584
C_GA, C_GB = 4096, 5120
W_COLS = 6144

VMEM_LIMIT = 52 * 1024 * 1024

F32 = jnp.float32
BF16 = jnp.bfloat16


def _dot(a, b):
    return jnp.dot(a, b, preferred_element_type=F32)


def _dot_nt(a, b):
    return lax.dot_general(a, b, (((1,), (1,)), ((), ())), preferred_element_type=F32)


def _split3(v):
    hi = v.astype(BF16).astype(F32)
    r = v - hi
    mid = r.astype(BF16).astype(F32)
    lo = r - mid
    return hi, mid, lo


def _extra_base(h):
    return (h // 2) * LANES + (HEAD_DIM if h % 2 == 0 else 0)


def _routing_constants():
    sel_fq = np.zeros((LANES, WIDTH), np.float32)
    sel_fk = np.zeros((LANES, WIDTH), np.float32)
    sel_mq = np.zeros((LANES, WIDTH), np.float32)
    for h in range(N_HEADS):
        base = _extra_base(h)
        for part in range(3):
            sel_fq[part * 8 + h, base + part] = 1.0
            sel_fq[24, base + 3 + part] = 1.0
            sel_fk[24, base + part] = 1.0
            sel_fk[part * 8 + h, base + 3 + part] = -1.0
        sel_fq[25, base + U_COL_FOX] = 1.0
        sel_fk[24, base + U_COL_FOX] = 1.0
        for n in range(GROUP):
            sel_mq[h * GROUP + n, base + n] = 1.0
    tri = np.tril(np.ones((TILE, TILE), np.float32))
    return sel_fq, sel_fk, sel_mq, tri


def _rope_tables(seq):
    inv_freq = ROPE_THETA ** (-np.arange(0, ROPE_HALF, dtype=np.float32) * 2.0 / ROPE_DIM)
    ang = np.arange(seq, dtype=np.float32)[:, None] * inv_freq[None, :].astype(np.float32)
    cos, sin = np.cos(ang).astype(np.float32), np.sin(ang).astype(np.float32)
    rc = np.ones((seq, LANES), np.float32)
    rsp = np.zeros((seq, LANES), np.float32)
    rsm = np.zeros((seq, LANES), np.float32)
    for off in (0, HEAD_DIM):
        rc[:, off:off + ROPE_HALF] = cos
        rc[:, off + ROPE_HALF:off + ROPE_DIM] = cos
        rsm[:, off:off + ROPE_HALF] = -sin
        rsp[:, off + ROPE_HALF:off + ROPE_DIM] = sin
    return rc, rsp, rsm


def _proj_kernel(x0_ref, xn_ref, w_ref, wfl_ref, prm_ref,
                 rc_ref, rsp_ref, rsm_ref, tri_ref, selfq_ref, selfk_ref, selmq_ref,
                 q_out, k_out, v_out, gz_out, gate_out, cend_out,
                 carry_ref, kmt_ref, h_ref, *, tiles_per_seq):
    step = pl.program_id(0)
    t = step % tiles_per_seq
    slot = step % 2

    ng_ref = prm_ref.at[0:1, 0:D_MODEL]
    bg_ref = prm_ref.at[1:2, :]
    gfq_ref, gfk_ref, gmq_ref, gmk_ref = (
        prm_ref.at[2:3, n * WIDTH:(n + 1) * WIDTH] for n in range(4))
    bf_ref, big_ref, ufox_ref, umoba_ref = (
        prm_ref.at[3:4, n * LANES:(n + 1) * LANES] for n in range(4))

    @pl.when(t == 0)
    def _():
        carry_ref[...] = jnp.zeros_like(carry_ref)
        kmt_ref[...] = jnp.zeros_like(kmt_ref)

    lane = lax.broadcasted_iota(jnp.int32, (TILE, LANES), 1)
    low_half = lane < HEAD_DIM

    def normalized(x):
        ms = jnp.mean(x * x, axis=-1, keepdims=True)
        return (x * lax.rsqrt(ms + RMS_EPS) * ng_ref[...]).astype(BF16)

    @pl.when(step == 0)
    def _():
        h_ref[0] = normalized(x0_ref[...])

    def proj(c0, width):
        slabs = [_dot(h_ref[slot], w_ref[:, c:c + min(MXU_COLS, c0 + width - c)])
                 for c in range(c0, c0 + width, MXU_COLS)]
        return slabs[0] if len(slabs) == 1 else jnp.concatenate(slabs, axis=1)

    def head_norm(a, g_ref):
        tiles = []
        for p in range(WIDTH // LANES):
            ap = a[:, p * LANES:(p + 1) * LANES]
            sq = ap * ap
            s_lo = jnp.sum(jnp.where(low_half, sq, 0.0), axis=-1, keepdims=True)
            s_hi = jnp.sum(jnp.where(low_half, 0.0, sq), axis=-1, keepdims=True)
            inv_lo = lax.rsqrt(s_lo * (1.0 / HEAD_DIM) + RMS_EPS)
            inv_hi = lax.rsqrt(s_hi * (1.0 / HEAD_DIM) + RMS_EPS)
            scale = jnp.where(low_half, inv_lo, inv_hi)
            tiles.append(ap * scale * g_ref[:, p * LANES:(p + 1) * LANES])
        return tiles

    def rope(y):
        return (y * rc_ref[...] + pltpu.roll(y, ROPE_HALF, 1) * rsp_ref[...]
                + pltpu.roll(y, LANES - ROPE_HALF, 1) * rsm_ref[...])

    def split_tiles(a):
        return [a[:, p * LANES:(p + 1) * LANES] for p in range(WIDTH // LANES)]

    def store_heads(out_ref, head0, tiles, extras):
        for p, y in enumerate(tiles):
            e = extras(p)
            out_ref[0, head0 + 2 * p] = jnp.where(low_half, y, e).astype(BF16)
            out_ref[0, head0 + 2 * p + 1] = jnp.where(low_half, e, y).astype(BF16)

    def silu(z):
        hz = 0.5 * z
        return hz + hz * jnp.tanh(hz)

    def sigmoid(z):
        return 0.5 * jnp.tanh(0.5 * z) + 0.5

    fl = _dot(h_ref[slot], wfl_ref[...]) + bf_ref[...]
    mk_raw = proj(C_MK, WIDTH)
    mq_raw = proj(C_MQ, WIDTH)
    gate_chunks = [(C_GA + half * WIDTH, half * WIDTH) for half in range(2)]
    gate_chunks += [(C_GB + half * WIDTH, D_MODEL + half * WIDTH) for half in range(2)]
    gate_raw = [proj(c0, WIDTH) for c0, _ in gate_chunks]

    logf = jnp.minimum(fl, 0.0) - jnp.log(1.0 + jnp.exp(-jnp.abs(fl)))
    logf = jnp.where(lane < N_HEADS, logf, 0.0)
    l_hi, l_mid, l_lo = _split3(logf)
    packed = (l_hi + pltpu.roll(l_mid, 8, 1) + pltpu.roll(l_lo, 16, 1)).astype(BF16)
    cum = _dot(tri_ref[...], packed)
    fq = proj(C_FQ, WIDTH)
    fk = proj(C_FK, WIDTH)

    mk_tiles = [rope(y) for y in head_norm(mk_raw, gmk_ref)]
    row_id = lax.broadcasted_iota(jnp.int32, (LANES, LANES), 0)
    lane_sq = lax.broadcasted_iota(jnp.int32, (LANES, LANES), 1)
    for p, kr in enumerate(mk_tiles):
        km = jnp.mean(kr, axis=0, keepdims=True)
        hit = (((row_id == (2 * p) * GROUP + t) & (lane_sq < HEAD_DIM))
               | ((row_id == (2 * p + 1) * GROUP + t) & (lane_sq >= HEAD_DIM)))
        blk = kmt_ref[:, p * LANES:(p + 1) * LANES]
        kmt_ref[:, p * LANES:(p + 1) * LANES] = jnp.where(hit, km, blk)
    mk_ones = jnp.where((lane % HEAD_DIM == t) | (lane % HEAD_DIM == U_COL_MOBA), 1.0, 0.0)
    store_heads(k_out, N_HEADS, mk_tiles, lambda p: mk_ones)

    mq_tiles = [rope(y) for y in head_norm(mq_raw, gmq_ref)]
    q_full = jnp.concatenate(mq_tiles, axis=1)
    q_hi = q_full.astype(BF16)
    q_lo = (q_full - q_hi.astype(F32)).astype(BF16)
    kmt = kmt_ref[...]
    k_hi = kmt.astype(BF16)
    k_lo = (kmt - k_hi.astype(F32)).astype(BF16)
    gate = _dot_nt(q_hi, k_hi) + _dot_nt(q_hi, k_lo) + _dot_nt(q_lo, k_hi)

    c = cum + pltpu.roll(cum, LANES - 8, 1) + pltpu.roll(cum, LANES - 16, 1)
    c = jnp.where(lane < N_HEADS, c, 0.0) + carry_ref[...]
    carry_ref[...] = c[TILE - 1:TILE, :]
    cend_out[0] = jnp.broadcast_to(c[TILE - 1:TILE, :], (SUBLANES, LANES))
    c_hi, c_mid, c_lo = _split3(c * LOG2E)
    cparts = (c_hi + pltpu.roll(c_mid, 8, 1) + pltpu.roll(c_lo, 16, 1)
              + jnp.where(lane == 24, 1.0, 0.0)
              + jnp.where(lane == 25, ufox_ref[...], 0.0)).astype(BF16)
    ex_fq = _dot(cparts, selfq_ref[...])
    ex_fk = _dot(cparts, selfk_ref[...])
    fz = proj(C_FZ, WIDTH)
    mz = proj(C_MZ, WIDTH)
    fv = proj(C_FV, WIDTH)
    mv = proj(C_MV, WIDTH)

    v_ones = jnp.where(lane % HEAD_DIM == 0, 1.0, 0.0)
    store_heads(v_out, 0, split_tiles(fv), lambda p: v_ones)
    gz_out[:, 0:WIDTH] = silu(fz).astype(BF16)

    store_heads(q_out, 0, head_norm(fq, gfq_ref), lambda p: ex_fq[:, p * LANES:(p + 1) * LANES])
    store_heads(k_out, 0, head_norm(fk, gfk_ref), lambda p: ex_fk[:, p * LANES:(p + 1) * LANES])

    blk_id = lane % GROUP
    past = blk_id < t
    g = jnp.where(past, gate, -jnp.inf)
    beaten = jnp.zeros((TILE, LANES), jnp.int32)
    for d in range(1, GROUP):
        lower = jnp.where(blk_id >= d, pltpu.roll(g, d, 1), -jnp.inf)
        upper = jnp.where(blk_id < GROUP - d, pltpu.roll(g, LANES - d, 1), -jnp.inf)
        beaten = beaten + jnp.where(lower >= g, 1, 0) + jnp.where(upper > g, 1, 0)
    keep = (past & (beaten < MOBA_TOPK)) | (blk_id == t)
    maskvals = jnp.where(keep, 0.0, -big_ref[...]).astype(BF16)
    ex_mq = _dot(maskvals, selmq_ref[...])

    store_heads(v_out, N_HEADS, split_tiles(mv), lambda p: v_ones)
    gz_out[:, WIDTH:2 * WIDTH] = silu(mz).astype(BF16)

    is_u_lane = lane % HEAD_DIM == U_COL_MOBA
    store_heads(q_out, N_HEADS, mq_tiles,
                lambda p: jnp.where(is_u_lane, umoba_ref[...], ex_mq[:, p * LANES:(p + 1) * LANES]))
    for raw, (_, o0) in zip(gate_raw, gate_chunks):
        gate_out[:, o0:o0 + WIDTH] = sigmoid(raw + bg_ref[:, o0:o0 + WIDTH]).astype(BF16)

    h_ref[1 - slot] = normalized(xn_ref[...])


def _attn_kernel(flag_ref, cend_ref, q_ref, k_ref, v_ref, gz_ref, o_ref, acc_ref, m_ref, *,
                 tiles_per_seq):
    b, group, i = pl.program_id(0), pl.program_id(1), pl.program_id(2)
    half = ATT_TILE // 2
    start = pl.multiple_of(i * ATT_TILE, ATT_TILE)
    heads = range(ATT_HEADS)

    def first_live_block(hh):
        base = (b * N_AUG_HEADS + ATT_HEADS * group + hh) * tiles_per_seq
        sub = ATT_TILE // TILE
        c_q = cend_ref[base + jnp.maximum(sub * i - 1, 0)]
        dead = jnp.int32(0)
        for j in range(tiles_per_seq // sub - 1):
            c_k = cend_ref[base + sub * j + sub - 1]
            dead += ((j < i) & (c_q - c_k < EXP2_ZERO_BELOW)).astype(jnp.int32)
        return dead

    row = lax.broadcasted_iota(jnp.int32, (half, ATT_TILE), 0)
    col = lax.broadcasted_iota(jnp.int32, (half, ATT_TILE), 1)
    mask_top = (lax.broadcasted_iota(jnp.int32, (half, half), 1)
                <= lax.broadcasted_iota(jnp.int32, (half, half), 0))
    mask_bot = col <= row + half

    def diag_scores(hh):
        kd = k_ref[0, hh, pl.ds(start, ATT_TILE), :]
        s_top = jnp.where(mask_top, _dot_nt(q_ref[0, hh, 0:half, :], kd[0:half]), -jnp.inf)
        s_bot = jnp.where(mask_bot, _dot_nt(q_ref[0, hh, half:ATT_TILE, :], kd), -jnp.inf)
        return s_top, s_bot

    def finish(accs):
        lane = lax.broadcasted_iota(jnp.int32, (ATT_TILE, LANES), 1)
        for pp in range(ATT_HEADS // 2):
            acc_e, acc_o = accs[2 * pp], accs[2 * pp + 1]
            o_e = acc_e * (1.0 / acc_e[:, HEAD_DIM:HEAD_DIM + 1])
            o_o = acc_o * (1.0 / acc_o[:, 0:1])
            o = jnp.where(lane < HEAD_DIM, o_e, o_o)
            cols = slice(pp * LANES, (pp + 1) * LANES)
            o_ref[:, cols] = (o * gz_ref[:, cols].astype(F32)).astype(BF16)

    @pl.when(flag_ref[0] == 1)
    def _fixed_stabilizer():
        acc_ref[...] = jnp.zeros_like(acc_ref)

        def add_blocks(j, n_blocks):
            for hh in heads:
                pv = None
                for jj in range(n_blocks):
                    off = pl.multiple_of((j + jj) * ATT_TILE, ATT_TILE)
                    s = _dot_nt(q_ref[0, hh], k_ref[0, hh, pl.ds(off, ATT_TILE), :])
                    d = _dot(jnp.exp2(s).astype(BF16), v_ref[0, hh, pl.ds(off, ATT_TILE), :])
                    pv = d if pv is None else pv + d
                acc_ref[hh] += pv

        first = first_live_block(0)
        for hh in heads[1:]:
            first = jnp.minimum(first, first_live_block(hh))
        n_live = i - first
        odd = n_live & 1

        @pl.when(odd == 1)
        def _():
            add_blocks(first, 1)

        def pair(p, carry):
            add_blocks(first + odd + 2 * p, 2)
            return carry

        lax.fori_loop(0, n_live >> 1, pair, 0)

        accs = []
        for hh in heads:
            vd = v_ref[0, hh, pl.ds(start, ATT_TILE), :]
            s_top, s_bot = diag_scores(hh)
            top = acc_ref[hh, 0:half] + _dot(jnp.exp2(s_top).astype(BF16), vd[0:half])
            bot = acc_ref[hh, half:ATT_TILE] + _dot(jnp.exp2(s_bot).astype(BF16), vd)
            accs.append(jnp.concatenate([top, bot], axis=0))
        finish(accs)

    @pl.when(flag_ref[0] != 1)
    def _online():
        acc_ref[...] = jnp.zeros_like(acc_ref)
        m_ref[...] = jnp.full_like(m_ref, -jnp.inf)

        def update(s, m_old, acc_old, v):
            m_new = jnp.maximum(m_old, jnp.max(s, axis=-1, keepdims=True))
            pv = _dot(jnp.exp2(s - m_new).astype(BF16), v)
            return m_new, jnp.exp2(m_old - m_new) * acc_old + pv

        def body(j, carry):
            off = pl.multiple_of(j * ATT_TILE, ATT_TILE)
            for hh in heads:
                s = _dot_nt(q_ref[0, hh], k_ref[0, hh, pl.ds(off, ATT_TILE), :])
                m_ref[hh], acc_ref[hh] = update(s, m_ref[hh], acc_ref[hh],
                                                v_ref[0, hh, pl.ds(off, ATT_TILE), :])
            return carry

        lax.fori_loop(0, i, body, 0)

        accs = []
        for hh in heads:
            vd = v_ref[0, hh, pl.ds(start, ATT_TILE), :]
            s_top, s_bot = diag_scores(hh)
            _, top = update(s_top, m_ref[hh, 0:half], acc_ref[hh, 0:half], vd[0:half])
            _, bot = update(s_bot, m_ref[hh, half:ATT_TILE], acc_ref[hh, half:ATT_TILE], vd)
            accs.append(jnp.concatenate([top, bot], axis=0))
        finish(accs)


def _out_kernel(y_ref, gate_ref, x_ref, wf_ref, wm_ref, wo_ref, o_ref):
    slabs = [slice(c, c + MXU_COLS) for c in range(0, D_MODEL, MXU_COLS)]
    yf, ym = y_ref[:, 0:WIDTH], y_ref[:, WIDTH:2 * WIDTH]
    merged = []
    for cols in slabs:
        pf = _dot(yf, wf_ref[:, cols])
        pm = _dot(ym, wm_ref[:, cols])
        ga = gate_ref[:, cols].astype(F32)
        gb = gate_ref[:, D_MODEL + cols.start:D_MODEL + cols.stop].astype(F32)
        merged.append((ga * pf + gb * pm).astype(BF16))
    merged = jnp.concatenate(merged, axis=1)
    for cols in slabs:
        o_ref[:, cols] = x_ref[:, cols] + _dot(merged, wo_ref[:, cols])


def _const_spec(shape):
    return pl.BlockSpec(shape, lambda *_: (0,) * len(shape))


def _layer(x2, batch, seq, norm_g, w_in, b_f, b_gate, fox_q_g, fox_k_g, moba_q_g, moba_k_g,
           w_fox, w_moba, w_out):
    rows = batch * seq
    tiles_per_seq = seq // TILE
    n_tiles = rows // TILE
    scale = HEAD_DIM ** -0.5

    w = w_in.astype(BF16)
    w_fl = jnp.pad(w_in[:, W_COLS:], ((0, 0), (0, LANES - N_HEADS))).astype(BF16)

    def logit_bound(gq, gk):
        return 8.0 * LOG2E * BOUND_MARGIN * jnp.max(jnp.abs(gq)) * jnp.max(jnp.abs(gk))

    b_fox, b_moba = logit_bound(fox_q_g, fox_k_g), logit_bound(moba_q_g, moba_k_g)
    fixed_ok = jnp.maximum(b_fox, b_moba) <= FIXED_STABILIZER_MAX_BOUND

    def stabilizer(b):
        u = -(b * (1.0 + 2.0 ** -7)).astype(BF16).astype(F32)
        return jnp.where(fixed_ok, u, 0.0)

    big = jnp.exp2(jnp.ceil(jnp.log2(2.1 * b_moba + 160.0)))
    flag = fixed_ok.astype(jnp.int32).reshape(1)

    def row(*pieces):
        r = jnp.concatenate([jnp.ravel(p).astype(F32) for p in pieces])
        return jnp.pad(r, (0, 2 * D_MODEL - r.shape[0]))

    def lanes(v):
        return jnp.full((LANES,), v, F32)

    q_scale = scale * LOG2E
    params = jnp.stack([
        row(norm_g),
        row(b_gate),
        row(jnp.tile(fox_q_g * q_scale, N_HEADS), jnp.tile(fox_k_g, N_HEADS),
            jnp.tile(moba_q_g * q_scale, N_HEADS), jnp.tile(moba_k_g, N_HEADS)),
        row(b_f, jnp.zeros(LANES - N_HEADS), lanes(big), lanes(stabilizer(b_fox)),
            lanes(stabilizer(b_moba))),
    ] + [jnp.zeros(2 * D_MODEL, F32)] * (SUBLANES - 4))

    sel_fq, sel_fk, sel_mq, tri = _routing_constants()
    rc, rsp, rsm = _rope_tables(seq)

    rope_spec = pl.BlockSpec((TILE, LANES), lambda g: (g % tiles_per_seq, 0))
    head_spec = pl.BlockSpec((1, N_AUG_HEADS, TILE, LANES),
                             lambda g: (g // tiles_per_seq, 0, g % tiles_per_seq, 0))
    qkv_shape = jax.ShapeDtypeStruct((batch, N_AUG_HEADS, seq, LANES), BF16)

    q_all, k_all, v_all, gz, gates, cend = pl.pallas_call(
        functools.partial(_proj_kernel, tiles_per_seq=tiles_per_seq),
        grid=(n_tiles,),
        in_specs=[
            _const_spec((TILE, D_MODEL)),
            pl.BlockSpec((TILE, D_MODEL), lambda g: (jnp.minimum(g + 1, n_tiles - 1), 0)),
            _const_spec(w_in.shape), _const_spec((D_MODEL, LANES)),
            _const_spec((SUBLANES, 2 * D_MODEL)),
            rope_spec, rope_spec, rope_spec,
            _const_spec((TILE, TILE)),
            _const_spec((LANES, WIDTH)), _const_spec((LANES, WIDTH)), _const_spec((LANES, WIDTH)),
        ],
        out_specs=[
            head_spec, head_spec, head_spec,
            pl.BlockSpec((TILE, 2 * WIDTH), lambda g: (g, 0)),
            pl.BlockSpec((TILE, 2 * D_MODEL), lambda g: (g, 0)),
            pl.BlockSpec((1, SUBLANES, LANES), lambda g: (g, 0, 0)),
        ],
        out_shape=[
            qkv_shape, qkv_shape, qkv_shape,
            jax.ShapeDtypeStruct((rows, 2 * WIDTH), BF16),
            jax.ShapeDtypeStruct((rows, 2 * D_MODEL), BF16),
            jax.ShapeDtypeStruct((n_tiles, SUBLANES, LANES), F32),
        ],
        scratch_shapes=[pltpu.VMEM((1, LANES), F32),
                        pltpu.VMEM((LANES, WIDTH), F32),
                        pltpu.VMEM((2, TILE, D_MODEL), BF16)],
        compiler_params=pltpu.CompilerParams(
            dimension_semantics=("arbitrary",), vmem_limit_bytes=VMEM_LIMIT),
        name="proj_epilogue",
    )(x2, x2, w, w_fl, params,
      jnp.asarray(rc), jnp.asarray(rsp), jnp.asarray(rsm),
      jnp.asarray(tri, BF16), jnp.asarray(sel_fq, BF16), jnp.asarray(sel_fk, BF16),
      jnp.asarray(sel_mq, BF16))

    n_groups = N_AUG_HEADS // ATT_HEADS
    group_cols = ATT_HEADS // 2 * LANES
    att_tiles = seq // ATT_TILE
    cend = cend[:, 0, :N_HEADS].reshape(batch, tiles_per_seq, N_HEADS).transpose(0, 2, 1) * LOG2E
    cend = jnp.concatenate([cend, jnp.zeros_like(cend)], axis=1).reshape(-1)
    kv_spec = pl.BlockSpec((1, ATT_HEADS, seq, LANES), lambda b, g, i, *_: (b, g, 0, 0))
    row_spec = pl.BlockSpec((ATT_TILE, group_cols), lambda b, g, i, *_: (b * att_tiles + i, g))
    y = pl.pallas_call(
        functools.partial(_attn_kernel, tiles_per_seq=tiles_per_seq),
        grid_spec=pltpu.PrefetchScalarGridSpec(
            num_scalar_prefetch=2,
            grid=(batch, n_groups, att_tiles),
            in_specs=[
                pl.BlockSpec((1, ATT_HEADS, ATT_TILE, LANES), lambda b, g, i, *_: (b, g, i, 0)),
                kv_spec, kv_spec, row_spec,
            ],
            out_specs=row_spec,
            scratch_shapes=[pltpu.VMEM((ATT_HEADS, ATT_TILE, LANES), F32),
                            pltpu.VMEM((ATT_HEADS, ATT_TILE, 1), F32)],
        ),
        out_shape=jax.ShapeDtypeStruct((rows, 2 * WIDTH), BF16),
        compiler_params=pltpu.CompilerParams(
            dimension_semantics=("arbitrary", "arbitrary", "arbitrary"),
            vmem_limit_bytes=VMEM_LIMIT),
        name="flash_attn",
    )(flag, cend, q_all, k_all, v_all, gz)

    out = pl.pallas_call(
        _out_kernel,
        grid=(rows // OUT_TILE,),
        in_specs=[
            pl.BlockSpec((OUT_TILE, 2 * WIDTH), lambda g: (g, 0)),
            pl.BlockSpec((OUT_TILE, 2 * D_MODEL), lambda g: (g, 0)),
            pl.BlockSpec((OUT_TILE, D_MODEL), lambda g: (g, 0)),
            _const_spec((WIDTH, D_MODEL)), _const_spec((WIDTH, D_MODEL)),
            _const_spec((D_MODEL, D_MODEL)),
        ],
        out_specs=pl.BlockSpec((OUT_TILE, D_MODEL), lambda g: (g, 0)),
        out_shape=jax.ShapeDtypeStruct((rows, D_MODEL), F32),
        compiler_params=pltpu.CompilerParams(
            dimension_semantics=("arbitrary",), vmem_limit_bytes=VMEM_LIMIT),
        name="merge_out",
    )(y, gates, x2, w_fox.astype(BF16), w_moba.astype(BF16), w_out.astype(BF16))
    return out


def kernel(x, norm_g, w_in, b_f, b_gate, fox_q_g, fox_k_g, moba_q_g, moba_k_g, w_fox, w_moba, w_out):
    batch, seq, d_model = x.shape
    assert d_model == D_MODEL and seq % ATT_TILE == 0 and seq // MOBA_BLOCK <= GROUP
    assert (batch * seq) % OUT_TILE == 0
    x2 = x.reshape(batch * seq, D_MODEL)
    for layer in range(norm_g.shape[0]):
        x2 = _layer(x2, batch, seq, norm_g[layer], w_in[layer], b_f[layer], b_gate[layer],
                    fox_q_g[layer], fox_k_g[layer], moba_q_g[layer], moba_k_g[layer],
                    w_fox[layer], w_moba[layer], w_out[layer])
    return x2.reshape(batch, seq, D_MODEL)
```

```python
import functools

import numpy as np
import jax
import jax.numpy as jnp
from jax import lax
from jax.experimental import pallas as pl
from jax.experimental.pallas import tpu as pltpu

D_MODEL = 1024
HEAD_DIM = 64
N_HEADS = 8
WIDTH = N_HEADS * HEAD_DIM
ROPE_DIM = HEAD_DIM // 4
ROPE_HALF = ROPE_DIM // 2
ROPE_THETA = 500000.0
MOBA_BLOCK = 256
MOBA_TOPK = 3
RMS_EPS = 1e-6

LANES = 128
SUBLANES = 8
MXU_COLS = 256
TILE = MOBA_BLOCK
OUT_TILE = 1024
CAST_ROWS = 128
ATT_TILE = 512
ATT_HEADS = N_HEADS
FIXED_STABILIZER_MAX_BOUND = 55.0
EXP2_ZERO_BELOW = -152.0
BOUND_MARGIN = 1.02
GROUP = 16
N_AUG_HEADS = 2 * N_HEADS
U_COL_FOX = 6
U_COL_MOBA = GROUP
LOG2E = 1.4426950408889634

C_FQ, C_FK, C_FV, C_FZ = 0, 512, 1024, 1536
C_MQ, C_MK, C_MV, C_MZ = 2048, 2560, 3072, 3584
C_GA, C_GB = 4096, 5120
W_COLS = 6144

VMEM_LIMIT = 52 * 1024 * 1024

F32 = jnp.float32
BF16 = jnp.bfloat16


def _dot(a, b):
    return jnp.dot(a, b, preferred_element_type=F32)


def _dot_nt(a, b):
    return lax.dot_general(a, b, (((1,), (1,)), ((), ())), preferred_element_type=F32)


def _split3(v):
    hi = v.astype(BF16).astype(F32)
    r = v - hi
    mid = r.astype(BF16).astype(F32)
    lo = r - mid
    return hi, mid, lo


def _extra_base(h):
    return (h // 2) * LANES + (HEAD_DIM if h % 2 == 0 else 0)


def _routing_constants():
    sel_fq = np.zeros((LANES, WIDTH), np.float32)
    sel_fk = np.zeros((LANES, WIDTH), np.float32)
    sel_mq = np.zeros((LANES, WIDTH), np.float32)
    for h in range(N_HEADS):
        base = _extra_base(h)
        for part in range(3):
            sel_fq[part * 8 + h, base + part] = 1.0
            sel_fq[24, base + 3 + part] = 1.0
            sel_fk[24, base + part] = 1.0
            sel_fk[part * 8 + h, base + 3 + part] = -1.0
        sel_fq[25, base + U_COL_FOX] = 1.0
        sel_fk[24, base + U_COL_FOX] = 1.0
        for n in range(GROUP):
            sel_mq[h * GROUP + n, base + n] = 1.0
    tri = np.tril(np.ones((TILE, TILE), np.float32))
    return sel_fq, sel_fk, sel_mq, tri


def _rope_tables(seq):
    inv_freq = ROPE_THETA ** (-np.arange(0, ROPE_HALF, dtype=np.float32) * 2.0 / ROPE_DIM)
    ang = np.arange(seq, dtype=np.float32)[:, None] * inv_freq[None, :].astype(np.float32)
    cos, sin = np.cos(ang).astype(np.float32), np.sin(ang).astype(np.float32)
    rc = np.ones((seq, LANES), np.float32)
    rsp = np.zeros((seq, LANES), np.float32)
    rsm = np.zeros((seq, LANES), np.float32)
    for off in (0, HEAD_DIM):
        rc[:, off:off + ROPE_HALF] = cos
        rc[:, off + ROPE_HALF:off + ROPE_DIM] = cos
        rsm[:, off:off + ROPE_HALF] = -sin
        rsp[:, off + ROPE_HALF:off + ROPE_DIM] = sin
    return rc, rsp, rsm


def _proj_kernel(x0_ref, xn_ref, w_ref, wfl_ref, prm_ref,
                 rc_ref, rsp_ref, rsm_ref, tri_ref, selfq_ref, selfk_ref, selmq_ref,
                 q_out, k_out, v_out, gz_out, gate_out, cend_out,
                 carry_ref, kmt_ref, h_ref, *, tiles_per_seq):
    step = pl.program_id(0)
    t = step % tiles_per_seq
    slot = step % 2

    ng_ref = prm_ref.at[0:1, 0:D_MODEL]
    bg_ref = prm_ref.at[1:2, :]
    gfq_ref, gfk_ref, gmq_ref, gmk_ref = (
        prm_ref.at[2:3, n * WIDTH:(n + 1) * WIDTH] for n in range(4))
    bf_ref, big_ref, ufox_ref, umoba_ref = (
        prm_ref.at[3:4, n * LANES:(n + 1) * LANES] for n in range(4))

    @pl.when(t == 0)
    def _():
        carry_ref[...] = jnp.zeros_like(carry_ref)
        kmt_ref[...] = jnp.zeros_like(kmt_ref)

    lane = lax.broadcasted_iota(jnp.int32, (TILE, LANES), 1)
    low_half = lane < HEAD_DIM

    def normalized(x):
        ms = jnp.mean(x * x, axis=-1, keepdims=True)
        return (x * lax.rsqrt(ms + RMS_EPS) * ng_ref[...]).astype(BF16)

    @pl.when(step == 0)
    def _():
        h_ref[0] = normalized(x0_ref[...])

    def proj(c0, width):
        slabs = [_dot(h_ref[slot], w_ref[:, c:c + min(MXU_COLS, c0 + width - c)])
                 for c in range(c0, c0 + width, MXU_COLS)]
        return slabs[0] if len(slabs) == 1 else jnp.concatenate(slabs, axis=1)

    def head_norm(a, g_ref):
        tiles = []
        for p in range(WIDTH // LANES):
            ap = a[:, p * LANES:(p + 1) * LANES]
            sq = ap * ap
            s_lo = jnp.sum(jnp.where(low_half, sq, 0.0), axis=-1, keepdims=True)
            s_hi = jnp.sum(jnp.where(low_half, 0.0, sq), axis=-1, keepdims=True)
            inv_lo = lax.rsqrt(s_lo * (1.0 / HEAD_DIM) + RMS_EPS)
            inv_hi = lax.rsqrt(s_hi * (1.0 / HEAD_DIM) + RMS_EPS)
            scale = jnp.where(low_half, inv_lo, inv_hi)
            tiles.append(ap * scale * g_ref[:, p * LANES:(p + 1) * LANES])
        return tiles

    def rope(y):
        return (y * rc_ref[...] + pltpu.roll(y, ROPE_HALF, 1) * rsp_ref[...]
                + pltpu.roll(y, LANES - ROPE_HALF, 1) * rsm_ref[...])

    def split_tiles(a):
        return [a[:, p * LANES:(p + 1) * LANES] for p in range(WIDTH // LANES)]

    def store_heads(out_ref, head0, tiles, extras):
        for p, y in enumerate(tiles):
            e = extras(p)
            out_ref[0, head0 + 2 * p] = jnp.where(low_half, y, e).astype(BF16)
            out_ref[0, head0 + 2 * p + 1] = jnp.where(low_half, e, y).astype(BF16)

    def silu(z):
        hz = 0.5 * z
        return hz + hz * jnp.tanh(hz)

    def sigmoid(z):
        return 0.5 * jnp.tanh(0.5 * z) + 0.5

    fl = _dot(h_ref[slot], wfl_ref[...]) + bf_ref[...]
    mk_raw = proj(C_MK, WIDTH)
    mq_raw = proj(C_MQ, WIDTH)
    gate_chunks = [(C_GA + half * WIDTH, half * WIDTH) for half in range(2)]
    gate_chunks += [(C_GB + half * WIDTH, D_MODEL + half * WIDTH) for half in range(2)]
    gate_raw = [proj(c0, WIDTH) for c0, _ in gate_chunks]

    logf = jnp.minimum(fl, 0.0) - jnp.log(1.0 + jnp.exp(-jnp.abs(fl)))
    logf = jnp.where(lane < N_HEADS, logf, 0.0)
    l_hi, l_mid, l_lo = _split3(logf)
    packed = (l_hi + pltpu.roll(l_mid, 8, 1) + pltpu.roll(l_lo, 16, 1)).astype(BF16)
    cum = _dot(tri_ref[...], packed)
    fq = proj(C_FQ, WIDTH)
    fk = proj(C_FK, WIDTH)

    mk_tiles = [rope(y) for y in head_norm(mk_raw, gmk_ref)]
    row_id = lax.broadcasted_iota(jnp.int32, (LANES, LANES), 0)
    lane_sq = lax.broadcasted_iota(jnp.int32, (LANES, LANES), 1)
    for p, kr in enumerate(mk_tiles):
        km = jnp.mean(kr, axis=0, keepdims=True)
        hit = (((row_id == (2 * p) * GROUP + t) & (lane_sq < HEAD_DIM))
               | ((row_id == (2 * p + 1) * GROUP + t) & (lane_sq >= HEAD_DIM)))
        blk = kmt_ref[:, p * LANES:(p + 1) * LANES]
        kmt_ref[:, p * LANES:(p + 1) * LANES] = jnp.where(hit, km, blk)
    mk_ones = jnp.where((lane % HEAD_DIM == t) | (lane % HEAD_DIM == U_COL_MOBA), 1.0, 0.0)
    store_heads(k_out, N_HEADS, mk_tiles, lambda p: mk_ones)

    mq_tiles = [rope(y) for y in head_norm(mq_raw, gmq_ref)]
    q_full = jnp.concatenate(mq_tiles, axis=1)
    q_hi = q_full.astype(BF16)
    q_lo = (q_full - q_hi.astype(F32)).astype(BF16)
    kmt = kmt_ref[...]
    k_hi = kmt.astype(BF16)
    k_lo = (kmt - k_hi.astype(F32)).astype(BF16)
    gate = _dot_nt(q_hi, k_hi) + _dot_nt(q_hi, k_lo) + _dot_nt(q_lo, k_hi)

    c = cum + pltpu.roll(cum, LANES - 8, 1) + pltpu.roll(cum, LANES - 16, 1)
    c = jnp.where(lane < N_HEADS, c, 0.0) + carry_ref[...]
    carry_ref[...] = c[TILE - 1:TILE, :]
    cend_out[0] = jnp.broadcast_to(c[TILE - 1:TILE, :], (SUBLANES, LANES))
    c_hi, c_mid, c_lo = _split3(c * LOG2E)
    cparts = (c_hi + pltpu.roll(c_mid, 8, 1) + pltpu.roll(c_lo, 16, 1)
              + jnp.where(lane == 24, 1.0, 0.0)
              + jnp.where(lane == 25, ufox_ref[...], 0.0)).astype(BF16)
    ex_fq = _dot(cparts, selfq_ref[...])
    ex_fk = _dot(cparts, selfk_ref[...])
    fz = proj(C_FZ, WIDTH)
    mz = proj(C_MZ, WIDTH)
    fv = proj(C_FV, WIDTH)
    mv = proj(C_MV, WIDTH)

    v_ones = jnp.where(lane % HEAD_DIM == 0, 1.0, 0.0)
    store_heads(v_out, 0, split_tiles(fv), lambda p: v_ones)
    gz_out[:, 0:WIDTH] = silu(fz).astype(BF16)

    store_heads(q_out, 0, head_norm(fq, gfq_ref), lambda p: ex_fq[:, p * LANES:(p + 1) * LANES])
    store_heads(k_out, 0, head_norm(fk, gfk_ref), lambda p: ex_fk[:, p * LANES:(p + 1) * LANES])

    blk_id = lane % GROUP
    past = blk_id < t
    g = jnp.where(past, gate, -jnp.inf)
    beaten = jnp.zeros((TILE, LANES), jnp.int32)
    for d in range(1, GROUP):
        lower = jnp.where(blk_id >= d, pltpu.roll(g, d, 1), -jnp.inf)
        upper = jnp.where(blk_id < GROUP - d, pltpu.roll(g, LANES - d, 1), -jnp.inf)
        beaten = beaten + jnp.where(lower >= g, 1, 0) + jnp.where(upper > g, 1, 0)
    keep = (past & (beaten < MOBA_TOPK)) | (blk_id == t)
    maskvals = jnp.where(keep, 0.0, -big_ref[...]).astype(BF16)
    ex_mq = _dot(maskvals, selmq_ref[...])

    store_heads(v_out, N_HEADS, split_tiles(mv), lambda p: v_ones)
    gz_out[:, WIDTH:2 * WIDTH] = silu(mz).astype(BF16)

    is_u_lane = lane % HEAD_DIM == U_COL_MOBA
    store_heads(q_out, N_HEADS, mq_tiles,
                lambda p: jnp.where(is_u_lane, umoba_ref[...], ex_mq[:, p * LANES:(p + 1) * LANES]))
    for raw, (_, o0) in zip(gate_raw, gate_chunks):
        gate_out[:, o0:o0 + WIDTH] = sigmoid(raw + bg_ref[:, o0:o0 + WIDTH]).astype(BF16)

    h_ref[1 - slot] = normalized(xn_ref[...])


def _attn_kernel(flag_ref, cend_ref, q_ref, k_ref, v_ref, gz_ref, o_ref, acc_ref, m_ref, *,
                 tiles_per_seq):
    b, group, i = pl.program_id(0), pl.program_id(1), pl.program_id(2)
    half = ATT_TILE // 2
    start = pl.multiple_of(i * ATT_TILE, ATT_TILE)
    heads = range(ATT_HEADS)

    def first_live_block(hh):
        base = (b * N_AUG_HEADS + ATT_HEADS * group + hh) * tiles_per_seq
        sub = ATT_TILE // TILE
        c_q = cend_ref[base + jnp.maximum(sub * i - 1, 0)]
        dead = jnp.int32(0)
        for j in range(tiles_per_seq // sub - 1):
            c_k = cend_ref[base + sub * j + sub - 1]
            dead += ((j < i) & (c_q - c_k < EXP2_ZERO_BELOW)).astype(jnp.int32)
        return dead

    row = lax.broadcasted_iota(jnp.int32, (half, ATT_TILE), 0)
    col = lax.broadcasted_iota(jnp.int32, (half, ATT_TILE), 1)
    mask_top = (lax.broadcasted_iota(jnp.int32, (half, half), 1)
                <= lax.broadcasted_iota(jnp.int32, (half, half), 0))
    mask_bot = col <= row + half

    def diag_scores(hh):
        kd = k_ref[0, hh, pl.ds(start, ATT_TILE), :]
        s_top = jnp.where(mask_top, _dot_nt(q_ref[0, hh, 0:half, :], kd[0:half]), -jnp.inf)
        s_bot = jnp.where(mask_bot, _dot_nt(q_ref[0, hh, half:ATT_TILE, :], kd), -jnp.inf)
        return s_top, s_bot

    def finish(accs):
        lane = lax.broadcasted_iota(jnp.int32, (ATT_TILE, LANES), 1)
        for pp in range(ATT_HEADS // 2):
            acc_e, acc_o = accs[2 * pp], accs[2 * pp + 1]
            o_e = acc_e * (1.0 / acc_e[:, HEAD_DIM:HEAD_DIM + 1])
            o_o = acc_o * (1.0 / acc_o[:, 0:1])
            o = jnp.where(lane < HEAD_DIM, o_e, o_o)
            cols = slice(pp * LANES, (pp + 1) * LANES)
            o_ref[:, cols] = (o * gz_ref[:, cols].astype(F32)).astype(BF16)

    @pl.when(flag_ref[0] == 1)
    def _fixed_stabilizer():
        acc_ref[...] = jnp.zeros_like(acc_ref)

        def add_blocks(j, n_blocks):
            for hh in heads:
                pv = None
                for jj in range(n_blocks):
                    off = pl.multiple_of((j + jj) * ATT_TILE, ATT_TILE)
                    s = _dot_nt(q_ref[0, hh], k_ref[0, hh, pl.ds(off, ATT_TILE), :])
                    d = _dot(jnp.exp2(s).astype(BF16), v_ref[0, hh, pl.ds(off, ATT_TILE), :])
                    pv = d if pv is None else pv + d
                acc_ref[hh] += pv

        first = first_live_block(0)
        for hh in heads[1:]:
            first = jnp.minimum(first, first_live_block(hh))
        n_live = i - first
        odd = n_live & 1

        @pl.when(odd == 1)
        def _():
            add_blocks(first, 1)

        def pair(p, carry):
            add_blocks(first + odd + 2 * p, 2)
            return carry

        lax.fori_loop(0, n_live >> 1, pair, 0)

        accs = []
        for hh in heads:
            vd = v_ref[0, hh, pl.ds(start, ATT_TILE), :]
            s_top, s_bot = diag_scores(hh)
            top = acc_ref[hh, 0:half] + _dot(jnp.exp2(s_top).astype(BF16), vd[0:half])
            bot = acc_ref[hh, half:ATT_TILE] + _dot(jnp.exp2(s_bot).astype(BF16), vd)
            accs.append(jnp.concatenate([top, bot], axis=0))
        finish(accs)

    @pl.when(flag_ref[0] != 1)
    def _online():
        acc_ref[...] = jnp.zeros_like(acc_ref)
        m_ref[...] = jnp.full_like(m_ref, -jnp.inf)

        def update(s, m_old, acc_old, v):
            m_new = jnp.maximum(m_old, jnp.max(s, axis=-1, keepdims=True))
            pv = _dot(jnp.exp2(s - m_new).astype(BF16), v)
            return m_new, jnp.exp2(m_old - m_new) * acc_old + pv

        def body(j, carry):
            off = pl.multiple_of(j * ATT_TILE, ATT_TILE)
            for hh in heads:
                s = _dot_nt(q_ref[0, hh], k_ref[0, hh, pl.ds(off, ATT_TILE), :])
                m_ref[hh], acc_ref[hh] = update(s, m_ref[hh], acc_ref[hh],
                                                v_ref[0, hh, pl.ds(off, ATT_TILE), :])
            return carry

        lax.fori_loop(0, i, body, 0)

        accs = []
        for hh in heads:
            vd = v_ref[0, hh, pl.ds(start, ATT_TILE), :]
            s_top, s_bot = diag_scores(hh)
            _, top = update(s_top, m_ref[hh, 0:half], acc_ref[hh, 0:half], vd[0:half])
            _, bot = update(s_bot, m_ref[hh, half:ATT_TILE], acc_ref[hh, half:ATT_TILE], vd)
            accs.append(jnp.concatenate([top, bot], axis=0))
        finish(accs)


def _out_kernel(y_ref, gate_ref, x_ref, wf_ref, wm_ref, wo_ref, o_ref):
    slabs = [slice(c, c + MXU_COLS) for c in range(0, D_MODEL, MXU_COLS)]
    yf, ym = y_ref[:, 0:WIDTH], y_ref[:, WIDTH:2 * WIDTH]
    merged = []
    for cols in slabs:
        pf = _dot(yf, wf_ref[:, cols])
        pm = _dot(ym, wm_ref[:, cols])
        ga = gate_ref[:, cols].astype(F32)
        gb = gate_ref[:, D_MODEL + cols.start:D_MODEL + cols.stop].astype(F32)
        merged.append((ga * pf + gb * pm).astype(BF16))
    merged = jnp.concatenate(merged, axis=1)
    for cols in slabs:
        o_ref[:, cols] = x_ref[:, cols] + _dot(merged, wo_ref[:, cols])


def _const_spec(shape):
    return pl.BlockSpec(shape, lambda *_: (0,) * len(shape))


def _cast_kernel(x_ref, o_ref):
    o_ref[...] = x_ref[...].astype(o_ref.dtype)


def _to_bf16(a):
    rows, cols = a.shape
    assert rows % CAST_ROWS == 0
    spec = pl.BlockSpec((CAST_ROWS, cols), lambda g: (g, 0))
    return pl.pallas_call(
        _cast_kernel, grid=(rows // CAST_ROWS,), in_specs=[spec], out_specs=spec,
        out_shape=jax.ShapeDtypeStruct(a.shape, BF16),
        compiler_params=pltpu.CompilerParams(dimension_semantics=("arbitrary",)),
        name="cast_bf16",
    )(a)


def _layer(x2, batch, seq, norm_g, w_in, b_f, b_gate, fox_q_g, fox_k_g, moba_q_g, moba_k_g,
           w_fox, w_moba, w_out):
    rows = batch * seq
    tiles_per_seq = seq // TILE
    n_tiles = rows // TILE
    scale = HEAD_DIM ** -0.5

    w = _to_bf16(w_in)
    w_fl =jnp.pad(w_in[:, W_COLS:], ((0, 0), (0, LANES - N_HEADS))).astype(BF16)

    def logit_bound(gq, gk):
        return 8.0 * LOG2E * BOUND_MARGIN * jnp.max(jnp.abs(gq)) * jnp.max(jnp.abs(gk))

    b_fox, b_moba = logit_bound(fox_q_g, fox_k_g), logit_bound(moba_q_g, moba_k_g)
    fixed_ok = jnp.maximum(b_fox, b_moba) <= FIXED_STABILIZER_MAX_BOUND

    def stabilizer(b):
        u = -(b * (1.0 + 2.0 ** -7)).astype(BF16).astype(F32)
        return jnp.where(fixed_ok, u, 0.0)

    big = jnp.exp2(jnp.ceil(jnp.log2(2.1 * b_moba + 160.0)))
    flag = fixed_ok.astype(jnp.int32).reshape(1)

    def row(*pieces):
        r = jnp.concatenate([jnp.ravel(p).astype(F32) for p in pieces])
        return jnp.pad(r, (0, 2 * D_MODEL - r.shape[0]))

    def lanes(v):
        return jnp.full((LANES,), v, F32)

    q_scale = scale * LOG2E
    params = jnp.stack([
        row(norm_g),
        row(b_gate),
        row(jnp.tile(fox_q_g * q_scale, N_HEADS), jnp.tile(fox_k_g, N_HEADS),
            jnp.tile(moba_q_g * q_scale, N_HEADS), jnp.tile(moba_k_g, N_HEADS)),
        row(b_f, jnp.zeros(LANES - N_HEADS), lanes(big), lanes(stabilizer(b_fox)),
            lanes(stabilizer(b_moba))),
    ] + [jnp.zeros(2 * D_MODEL, F32)] * (SUBLANES - 4))

    sel_fq, sel_fk, sel_mq, tri = _routing_constants()
    rc, rsp, rsm = _rope_tables(seq)

    rope_spec = pl.BlockSpec((TILE, LANES), lambda g: (g % tiles_per_seq, 0))
    head_spec = pl.BlockSpec((1, N_AUG_HEADS, TILE, LANES),
                             lambda g: (g // tiles_per_seq, 0, g % tiles_per_seq, 0))
    qkv_shape = jax.ShapeDtypeStruct((batch, N_AUG_HEADS, seq, LANES), BF16)

    q_all, k_all, v_all, gz, gates, cend = pl.pallas_call(
        functools.partial(_proj_kernel, tiles_per_seq=tiles_per_seq),
        grid=(n_tiles,),
        in_specs=[
            _const_spec((TILE, D_MODEL)),
            pl.BlockSpec((TILE, D_MODEL), lambda g: (jnp.minimum(g + 1, n_tiles - 1), 0)),
            _const_spec(w_in.shape), _const_spec((D_MODEL, LANES)),
            _const_spec((SUBLANES, 2 * D_MODEL)),
            rope_spec, rope_spec, rope_spec,
            _const_spec((TILE, TILE)),
            _const_spec((LANES, WIDTH)), _const_spec((LANES, WIDTH)), _const_spec((LANES, WIDTH)),
        ],
        out_specs=[
            head_spec, head_spec, head_spec,
            pl.BlockSpec((TILE, 2 * WIDTH), lambda g: (g, 0)),
            pl.BlockSpec((TILE, 2 * D_MODEL), lambda g: (g, 0)),
            pl.BlockSpec((1, SUBLANES, LANES), lambda g: (g, 0, 0)),
        ],
        out_shape=[
            qkv_shape, qkv_shape, qkv_shape,
            jax.ShapeDtypeStruct((rows, 2 * WIDTH), BF16),
            jax.ShapeDtypeStruct((rows, 2 * D_MODEL), BF16),
            jax.ShapeDtypeStruct((n_tiles, SUBLANES, LANES), F32),
        ],
        scratch_shapes=[pltpu.VMEM((1, LANES), F32),
                        pltpu.VMEM((LANES, WIDTH), F32),
                        pltpu.VMEM((2, TILE, D_MODEL), BF16)],
        compiler_params=pltpu.CompilerParams(
            dimension_semantics=("arbitrary",), vmem_limit_bytes=VMEM_LIMIT),
        name="proj_epilogue",
    )(x2, x2, w, w_fl, params,
      jnp.asarray(rc), jnp.asarray(rsp), jnp.asarray(rsm),
      jnp.asarray(tri, BF16), jnp.asarray(sel_fq, BF16), jnp.asarray(sel_fk, BF16),
      jnp.asarray(sel_mq, BF16))

    n_groups = N_AUG_HEADS // ATT_HEADS
    group_cols = ATT_HEADS // 2 * LANES
    att_tiles = seq // ATT_TILE
    cend = cend[:, 0, :N_HEADS].reshape(batch, tiles_per_seq, N_HEADS).transpose(0, 2, 1) * LOG2E
    cend = jnp.concatenate([cend, jnp.zeros_like(cend)], axis=1).reshape(-1)
    kv_spec = pl.BlockSpec((1, ATT_HEADS, seq, LANES), lambda b, g, i, *_: (b, g, 0, 0))
    row_spec = pl.BlockSpec((ATT_TILE, group_cols), lambda b, g, i, *_: (b * att_tiles + i, g))
    y = pl.pallas_call(
        functools.partial(_attn_kernel, tiles_per_seq=tiles_per_seq),
        grid_spec=pltpu.PrefetchScalarGridSpec(
            num_scalar_prefetch=2,
            grid=(batch, n_groups, att_tiles),
            in_specs=[
                pl.BlockSpec((1, ATT_HEADS, ATT_TILE, LANES), lambda b, g, i, *_: (b, g, i, 0)),
                kv_spec, kv_spec, row_spec,
            ],
            out_specs=row_spec,
            scratch_shapes=[pltpu.VMEM((ATT_HEADS, ATT_TILE, LANES), F32),
                            pltpu.VMEM((ATT_HEADS, ATT_TILE, 1), F32)],
        ),
        out_shape=jax.ShapeDtypeStruct((rows, 2 * WIDTH), BF16),
        compiler_params=pltpu.CompilerParams(
            dimension_semantics=("arbitrary", "arbitrary", "arbitrary"),
            vmem_limit_bytes=VMEM_LIMIT),
        name="flash_attn",
    )(flag, cend, q_all, k_all, v_all, gz)

    out = pl.pallas_call(
        _out_kernel,
        grid=(rows // OUT_TILE,),
        in_specs=[
            pl.BlockSpec((OUT_TILE, 2 * WIDTH), lambda g: (g, 0)),
            pl.BlockSpec((OUT_TILE, 2 * D_MODEL), lambda g: (g, 0)),
            pl.BlockSpec((OUT_TILE, D_MODEL), lambda g: (g, 0)),
            _const_spec((WIDTH, D_MODEL)), _const_spec((WIDTH, D_MODEL)),
            _const_spec((D_MODEL, D_MODEL)),
        ],
        out_specs=pl.BlockSpec((OUT_TILE, D_MODEL), lambda g: (g, 0)),
        out_shape=jax.ShapeDtypeStruct((rows, D_MODEL), F32),
        compiler_params=pltpu.CompilerParams(
            dimension_semantics=("arbitrary",), vmem_limit_bytes=VMEM_LIMIT),
        name="merge_out",
    )(y, gates, x2, w_fox.astype(BF16), w_moba.astype(BF16), w_out.astype(BF16))
    return out


def kernel(x, norm_g, w_in, b_f, b_gate, fox_q_g, fox_k_g, moba_q_g, moba_k_g, w_fox, w_moba, w_out):
    batch, seq, d_model = x.shape
    assert d_model == D_MODEL and seq % ATT_TILE == 0 and seq // MOBA_BLOCK <= GROUP
    assert (batch * seq) % OUT_TILE == 0
    x2 = x.reshape(batch * seq, D_MODEL)
    for layer in range(norm_g.shape[0]):
        x2 = _layer(x2, batch, seq, norm_g[layer], w_in[layer], b_f[layer], b_gate[layer],
                    fox_q_g[layer], fox_k_g[layer], moba_q_g[layer], moba_k_g[layer],
                    w_fox[layer], w_moba[layer], w_out[layer])
    return x2.reshape(batch, seq, D_MODEL)
```

```python
import functools

import numpy as np
import jax
import jax.numpy as jnp
from jax import lax
from jax.experimental import pallas as pl
from jax.experimental.pallas import tpu as pltpu

D_MODEL = 1024
HEAD_DIM = 64
N_HEADS = 8
WIDTH = N_HEADS * HEAD_DIM
ROPE_DIM = HEAD_DIM // 4
ROPE_HALF = ROPE_DIM // 2
ROPE_THETA = 500000.0
MOBA_BLOCK = 256
MOBA_TOPK = 3
RMS_EPS = 1e-6

LANES = 128
SUBLANES = 8
MXU_COLS = 256
TILE = MOBA_BLOCK
OUT_TILE = 1024
ATT_TILE = 512
ATT_HEADS = N_HEADS
FIXED_STABILIZER_MAX_BOUND = 55.0
EXP2_ZERO_BELOW = -152.0
BOUND_MARGIN = 1.02
GROUP = 16
N_AUG_HEADS = 2 * N_HEADS
U_COL_FOX = 6
U_COL_MOBA = GROUP
LOG2E = 1.4426950408889634

C_FQ, C_FK, C_FV, C_FZ = 0, 512, 1024, 1536
C_MQ, C_MK, C_MV, C_MZ = 2048, 2560, 3072, 3584
C_GA, C_GB = 4096, 5120
W_COLS = 6144

VMEM_LIMIT = 52 * 1024 * 1024

F32 = jnp.float32
BF16 = jnp.bfloat16


def _dot(a, b):
    return jnp.dot(a, b, preferred_element_type=F32)


def _dot_nt(a, b):
    return lax.dot_general(a, b, (((1,), (1,)), ((), ())), preferred_element_type=F32)


def _split3(v):
    hi = v.astype(BF16).astype(F32)
    r = v - hi
    mid = r.astype(BF16).astype(F32)
    lo = r - mid
    return hi, mid, lo


def _extra_base(h):
    return (h // 2) * LANES + (HEAD_DIM if h % 2 == 0 else 0)


def _routing_constants():
    sel_fq = np.zeros((LANES, WIDTH), np.float32)
    sel_fk = np.zeros((LANES, WIDTH), np.float32)
    sel_mq = np.zeros((LANES, WIDTH), np.float32)
    for h in range(N_HEADS):
        base = _extra_base(h)
        for part in range(3):
            sel_fq[part * 8 + h, base + part] = 1.0
            sel_fq[24, base + 3 + part] = 1.0
            sel_fk[24, base + part] = 1.0
            sel_fk[part * 8 + h, base + 3 + part] = -1.0
        sel_fq[25, base + U_COL_FOX] = 1.0
        sel_fk[24, base + U_COL_FOX] = 1.0
        for n in range(GROUP):
            sel_mq[h * GROUP + n, base + n] = 1.0
    tri = np.tril(np.ones((TILE, TILE), np.float32))
    return sel_fq, sel_fk, sel_mq, tri


def _rope_tables(seq):
    inv_freq = ROPE_THETA ** (-np.arange(0, ROPE_HALF, dtype=np.float32) * 2.0 / ROPE_DIM)
    ang = np.arange(seq, dtype=np.float32)[:, None] * inv_freq[None, :].astype(np.float32)
    cos, sin = np.cos(ang).astype(np.float32), np.sin(ang).astype(np.float32)
    rc = np.ones((seq, LANES), np.float32)
    rsp = np.zeros((seq, LANES), np.float32)
    rsm = np.zeros((seq, LANES), np.float32)
    for off in (0, HEAD_DIM):
        rc[:, off:off + ROPE_HALF] = cos
        rc[:, off + ROPE_HALF:off + ROPE_DIM] = cos
        rsm[:, off:off + ROPE_HALF] = -sin
        rsp[:, off + ROPE_HALF:off + ROPE_DIM] = sin
    return rc, rsp, rsm


def _proj_kernel(x0_ref, xn_ref, w_ref, wfl_ref, prm_ref,
                 rc_ref, rsp_ref, rsm_ref, tri_ref, selfq_ref, selfk_ref, selmq_ref,
                 q_out, k_out, v_out, gz_out, gate_out, cend_out,
                 carry_ref, kmt_ref, h_ref, *, tiles_per_seq):
    step = pl.program_id(0)
    t = step % tiles_per_seq
    slot = step % 2

    ng_ref = prm_ref.at[0:1, 0:D_MODEL]
    bg_ref = prm_ref.at[1:2, :]
    gfq_ref, gfk_ref, gmq_ref, gmk_ref = (
        prm_ref.at[2:3, n * WIDTH:(n + 1) * WIDTH] for n in range(4))
    bf_ref, big_ref, ufox_ref, umoba_ref = (
        prm_ref.at[3:4, n * LANES:(n + 1) * LANES] for n in range(4))

    @pl.when(t == 0)
    def _():
        carry_ref[...] = jnp.zeros_like(carry_ref)
        kmt_ref[...] = jnp.zeros_like(kmt_ref)

    lane = lax.broadcasted_iota(jnp.int32, (TILE, LANES), 1)
    low_half = lane < HEAD_DIM

    def normalized(x):
        ms = jnp.mean(x * x, axis=-1, keepdims=True)
        return (x * lax.rsqrt(ms + RMS_EPS) * ng_ref[...]).astype(BF16)

    @pl.when(step == 0)
    def _():
        h_ref[0] = normalized(x0_ref[...])

    def proj(c0, width):
        slabs = [_dot(h_ref[slot], w_ref[:, c:c + min(MXU_COLS, c0 + width - c)])
                 for c in range(c0, c0 + width, MXU_COLS)]
        return slabs[0] if len(slabs) == 1 else jnp.concatenate(slabs, axis=1)

    def head_norm(a, g_ref):
        tiles = []
        for p in range(WIDTH // LANES):
            ap = a[:, p * LANES:(p + 1) * LANES]
            sq = ap * ap
            s_lo = jnp.sum(jnp.where(low_half, sq, 0.0), axis=-1, keepdims=True)
            s_hi = jnp.sum(jnp.where(low_half, 0.0, sq), axis=-1, keepdims=True)
            inv_lo = lax.rsqrt(s_lo * (1.0 / HEAD_DIM) + RMS_EPS)
            inv_hi = lax.rsqrt(s_hi * (1.0 / HEAD_DIM) + RMS_EPS)
            scale = jnp.where(low_half, inv_lo, inv_hi)
            tiles.append(ap * scale * g_ref[:, p * LANES:(p + 1) * LANES])
        return tiles

    def rope(y):
        return (y * rc_ref[...] + pltpu.roll(y, ROPE_HALF, 1) * rsp_ref[...]
                + pltpu.roll(y, LANES - ROPE_HALF, 1) * rsm_ref[...])

    def split_tiles(a):
        return [a[:, p * LANES:(p + 1) * LANES] for p in range(WIDTH // LANES)]

    def store_heads(out_ref, head0, tiles, extras):
        for p, y in enumerate(tiles):
            e = extras(p)
            out_ref[0, head0 + 2 * p] = jnp.where(low_half, y, e).astype(BF16)
            out_ref[0, head0 + 2 * p + 1] = jnp.where(low_half, e, y).astype(BF16)

    def silu(z):
        hz = 0.5 * z
        return hz + hz * jnp.tanh(hz)

    def sigmoid(z):
        return 0.5 * jnp.tanh(0.5 * z) + 0.5

    fl = _dot(h_ref[slot], wfl_ref[...]) + bf_ref[...]
    mk_raw = proj(C_MK, WIDTH)
    mq_raw = proj(C_MQ, WIDTH)
    gate_chunks = [(C_GA + half * WIDTH, half * WIDTH) for half in range(2)]
    gate_chunks += [(C_GB + half * WIDTH, D_MODEL + half * WIDTH) for half in range(2)]
    gate_raw = [proj(c0, WIDTH) for c0, _ in gate_chunks]

    logf = jnp.minimum(fl, 0.0) - jnp.log(1.0 + jnp.exp(-jnp.abs(fl)))
    logf = jnp.where(lane < N_HEADS, logf, 0.0)
    l_hi, l_mid, l_lo = _split3(logf)
    packed = (l_hi + pltpu.roll(l_mid, 8, 1) + pltpu.roll(l_lo, 16, 1)).astype(BF16)
    cum = _dot(tri_ref[...], packed)
    fq = proj(C_FQ, WIDTH)
    fk = proj(C_FK, WIDTH)

    mk_tiles = [rope(y) for y in head_norm(mk_raw, gmk_ref)]
    row_id = lax.broadcasted_iota(jnp.int32, (LANES, LANES), 0)
    lane_sq = lax.broadcasted_iota(jnp.int32, (LANES, LANES), 1)
    for p, kr in enumerate(mk_tiles):
        km = jnp.mean(kr, axis=0, keepdims=True)
        hit = (((row_id == (2 * p) * GROUP + t) & (lane_sq < HEAD_DIM))
               | ((row_id == (2 * p + 1) * GROUP + t) & (lane_sq >= HEAD_DIM)))
        blk = kmt_ref[:, p * LANES:(p + 1) * LANES]
        kmt_ref[:, p * LANES:(p + 1) * LANES] = jnp.where(hit, km, blk)
    mk_ones = jnp.where((lane % HEAD_DIM == t) | (lane % HEAD_DIM == U_COL_MOBA), 1.0, 0.0)
    store_heads(k_out, N_HEADS, mk_tiles, lambda p: mk_ones)

    mq_tiles = [rope(y) for y in head_norm(mq_raw, gmq_ref)]
    q_full = jnp.concatenate(mq_tiles, axis=1)
    q_hi = q_full.astype(BF16)
    q_lo = (q_full - q_hi.astype(F32)).astype(BF16)
    kmt = kmt_ref[...]
    k_hi = kmt.astype(BF16)
    k_lo = (kmt - k_hi.astype(F32)).astype(BF16)
    gate = _dot_nt(q_hi, k_hi) + _dot_nt(q_hi, k_lo) + _dot_nt(q_lo, k_hi)

    c = cum + pltpu.roll(cum, LANES - 8, 1) + pltpu.roll(cum, LANES - 16, 1)
    c = jnp.where(lane < N_HEADS, c, 0.0) + carry_ref[...]
    carry_ref[...] = c[TILE - 1:TILE, :]
    cend_out[0] = jnp.broadcast_to(c[TILE - 1:TILE, :], (SUBLANES, LANES))
    c_hi, c_mid, c_lo = _split3(c * LOG2E)
    cparts = (c_hi + pltpu.roll(c_mid, 8, 1) + pltpu.roll(c_lo, 16, 1)
              + jnp.where(lane == 24, 1.0, 0.0)
              + jnp.where(lane == 25, ufox_ref[...], 0.0)).astype(BF16)
    ex_fq = _dot(cparts, selfq_ref[...])
    ex_fk = _dot(cparts, selfk_ref[...])
    fz = proj(C_FZ, WIDTH)
    mz = proj(C_MZ, WIDTH)
    fv = proj(C_FV, WIDTH)
    mv = proj(C_MV, WIDTH)

    v_ones = jnp.where(lane % HEAD_DIM == 0, 1.0, 0.0)
    store_heads(v_out, 0, split_tiles(fv), lambda p: v_ones)
    gz_out[:, 0:WIDTH] = silu(fz).astype(BF16)

    store_heads(q_out, 0, head_norm(fq, gfq_ref), lambda p: ex_fq[:, p * LANES:(p + 1) * LANES])
    store_heads(k_out, 0, head_norm(fk, gfk_ref), lambda p: ex_fk[:, p * LANES:(p + 1) * LANES])

    blk_id = lane % GROUP
    past = blk_id < t
    g = jnp.where(past, gate, -jnp.inf)
    beaten = jnp.zeros((TILE, LANES), jnp.int32)
    for d in range(1, GROUP):
        lower = jnp.where(blk_id >= d, pltpu.roll(g, d, 1), -jnp.inf)
        upper = jnp.where(blk_id < GROUP - d, pltpu.roll(g, LANES - d, 1), -jnp.inf)
        beaten = beaten + jnp.where(lower >= g, 1, 0) + jnp.where(upper > g, 1, 0)
    keep = (past & (beaten < MOBA_TOPK)) | (blk_id == t)
    maskvals = jnp.where(keep, 0.0, -big_ref[...]).astype(BF16)
    ex_mq = _dot(maskvals, selmq_ref[...])

    store_heads(v_out, N_HEADS, split_tiles(mv), lambda p: v_ones)
    gz_out[:, WIDTH:2 * WIDTH] = silu(mz).astype(BF16)

    is_u_lane = lane % HEAD_DIM == U_COL_MOBA
    store_heads(q_out, N_HEADS, mq_tiles,
                lambda p: jnp.where(is_u_lane, umoba_ref[...], ex_mq[:, p * LANES:(p + 1) * LANES]))
    for raw, (_, o0) in zip(gate_raw, gate_chunks):
        gate_out[:, o0:o0 + WIDTH] = sigmoid(raw + bg_ref[:, o0:o0 + WIDTH]).astype(BF16)

    h_ref[1 - slot] = normalized(xn_ref[...])


def _attn_kernel(flag_ref, first_ref, q_ref, k_ref, v_ref, gz_ref, o_ref, acc_ref, m_ref):
    b, group, i = pl.program_id(0), pl.program_id(1), pl.program_id(2)
    half = ATT_TILE // 2
    start = pl.multiple_of(i * ATT_TILE, ATT_TILE)
    heads = range(ATT_HEADS)

    row = lax.broadcasted_iota(jnp.int32, (half, ATT_TILE), 0)
    col = lax.broadcasted_iota(jnp.int32, (half, ATT_TILE), 1)
    mask_top = (lax.broadcasted_iota(jnp.int32, (half, half), 1)
                <= lax.broadcasted_iota(jnp.int32, (half, half), 0))
    mask_bot = col <= row + half

    def diag_scores(hh):
        kd = k_ref[0, hh, pl.ds(start, ATT_TILE), :]
        s_top = jnp.where(mask_top, _dot_nt(q_ref[0, hh, 0:half, :], kd[0:half]), -jnp.inf)
        s_bot = jnp.where(mask_bot, _dot_nt(q_ref[0, hh, half:ATT_TILE, :], kd), -jnp.inf)
        return s_top, s_bot

    def finish(accs, rows=slice(0, ATT_TILE)):
        lane = lax.broadcasted_iota(jnp.int32, accs[0].shape, 1)
        for pp in range(ATT_HEADS // 2):
            acc_e, acc_o = accs[2 * pp], accs[2 * pp + 1]
            o_e = acc_e * (1.0 / acc_e[:, HEAD_DIM:HEAD_DIM + 1])
            o_o = acc_o * (1.0 / acc_o[:, 0:1])
            o = jnp.where(lane < HEAD_DIM, o_e, o_o)
            cols = slice(pp * LANES, (pp + 1) * LANES)
            o_ref[rows, cols] = (o * gz_ref[rows, cols].astype(F32)).astype(BF16)

    @pl.when(flag_ref[0] == 1)
    def _fixed_stabilizer():
        def add_blocks(j, n_blocks, is_first):
            for hh in heads:
                pv = None
                for jj in range(n_blocks):
                    off = pl.multiple_of((j + jj) * ATT_TILE, ATT_TILE)
                    s = _dot_nt(q_ref[0, hh], k_ref[0, hh, pl.ds(off, ATT_TILE), :])
                    d = _dot(jnp.exp2(s).astype(BF16), v_ref[0, hh, pl.ds(off, ATT_TILE), :])
                    pv = d if pv is None else pv + d
                if is_first is True:
                    acc_ref[hh] = pv
                else:
                    acc_ref[hh] = pv + jnp.where(is_first, 0.0, acc_ref[hh])

        @pl.when(i == 0)
        def _():
            def zero(hh, carry):
                acc_ref[hh] = jnp.zeros((ATT_TILE, LANES), F32)
                return carry
            lax.fori_loop(0, ATT_HEADS, zero, 0)

        first = first_ref[(b * pl.num_programs(1) + group) * pl.num_programs(2) + i]
        n_live = i - first
        odd = n_live & 1

        @pl.when(odd == 1)
        def _():
            add_blocks(first, 1, True)

        def pair(p, carry):
            add_blocks(first + odd + 2 * p, 2, (odd == 0) & (p == 0))
            return carry

        lax.fori_loop(0, n_live >> 1, pair, 0)

        def past(hh, rows):
            return jnp.where(n_live > 0, acc_ref[hh, rows], 0.0)

        top_rows, bot_rows = slice(0, half), slice(half, ATT_TILE)
        units = ([(hh, bot_rows, ATT_TILE, mask_bot) for hh in heads]
                 + [(hh, top_rows, half, mask_top) for hh in heads])

        def scores(hh, rows, n_keys, mask):
            s = _dot_nt(q_ref[0, hh, rows, :], k_ref[0, hh, pl.ds(start, n_keys), :])
            return jnp.where(mask, s, -jnp.inf)

        accs, s_next = [], scores(*units[0])
        for n, (hh, rows, n_keys, _) in enumerate(units):
            s = s_next
            if n + 1 < len(units):
                s_next = scores(*units[n + 1])
            accs.append(past(hh, rows) + _dot(jnp.exp2(s).astype(BF16),
                                              v_ref[0, hh, pl.ds(start, n_keys), :]))
        finish(accs[:ATT_HEADS], bot_rows)
        finish(accs[ATT_HEADS:], top_rows)

    @pl.when(flag_ref[0] != 1)
    def _online():
        acc_ref[...] = jnp.zeros_like(acc_ref)
        m_ref[...] = jnp.full_like(m_ref, -jnp.inf)

        def update(s, m_old, acc_old, v):
            m_new = jnp.maximum(m_old, jnp.max(s, axis=-1, keepdims=True))
            pv = _dot(jnp.exp2(s - m_new).astype(BF16), v)
            return m_new, jnp.exp2(m_old - m_new) * acc_old + pv

        def body(j, carry):
            off = pl.multiple_of(j * ATT_TILE, ATT_TILE)
            for hh in heads:
                s = _dot_nt(q_ref[0, hh], k_ref[0, hh, pl.ds(off, ATT_TILE), :])
                m_ref[hh], acc_ref[hh] = update(s, m_ref[hh], acc_ref[hh],
                                                v_ref[0, hh, pl.ds(off, ATT_TILE), :])
            return carry

        lax.fori_loop(0, i, body, 0)

        accs = []
        for hh in heads:
            vd = v_ref[0, hh, pl.ds(start, ATT_TILE), :]
            s_top, s_bot = diag_scores(hh)
            _, top = update(s_top, m_ref[hh, 0:half], acc_ref[hh, 0:half], vd[0:half])
            _, bot = update(s_bot, m_ref[hh, half:ATT_TILE], acc_ref[hh, half:ATT_TILE], vd)
            accs.append(jnp.concatenate([top, bot], axis=0))
        finish(accs)


def _out_kernel(y_ref, gate_ref, x_ref, wf_ref, wm_ref, wo_ref, o_ref):
    slabs = [slice(c, c + MXU_COLS) for c in range(0, D_MODEL, MXU_COLS)]
    yf, ym = y_ref[:, 0:WIDTH], y_ref[:, WIDTH:2 * WIDTH]
    merged = []
    for cols in slabs:
        pf = _dot(yf, wf_ref[:, cols])
        pm = _dot(ym, wm_ref[:, cols])
        ga = gate_ref[:, cols].astype(F32)
        gb = gate_ref[:, D_MODEL + cols.start:D_MODEL + cols.stop].astype(F32)
        merged.append((ga * pf + gb * pm).astype(BF16))
    merged = jnp.concatenate(merged, axis=1)
    for cols in slabs:
        o_ref[:, cols] = x_ref[:, cols] + _dot(merged, wo_ref[:, cols])


def _first_live_blocks(cend):
    sub = ATT_TILE // TILE
    c_k = cend[:, sub - 1::sub]
    c_q = jnp.concatenate([cend[:, :1], cend[:, sub - 1:-1:sub]], axis=1)
    n = c_k.shape[1]
    past = jnp.arange(n)[None, :] < jnp.arange(n)[:, None]
    dead = past[None, :, :, None] & (c_q[:, :, None, :] - c_k[:, None, :, :] < EXP2_ZERO_BELOW)
    return jnp.min(jnp.sum(dead.astype(jnp.int32), axis=2), axis=-1)


def _const_spec(shape):
    return pl.BlockSpec(shape, lambda *_: (0,) * len(shape))


def _layer(x2, batch, seq, norm_g, w_in, b_f, b_gate, fox_q_g, fox_k_g, moba_q_g, moba_k_g,
           w_fox, w_moba, w_out):
    rows = batch * seq
    tiles_per_seq = seq // TILE
    n_tiles = rows // TILE
    scale = HEAD_DIM ** -0.5

    w = w_in.astype(BF16)
    w_fl = jnp.pad(w_in[:, W_COLS:], ((0, 0), (0, LANES - N_HEADS))).astype(BF16)

    def logit_bound(gq, gk):
        return 8.0 * LOG2E * BOUND_MARGIN * jnp.max(jnp.abs(gq)) * jnp.max(jnp.abs(gk))

    b_fox, b_moba = logit_bound(fox_q_g, fox_k_g), logit_bound(moba_q_g, moba_k_g)
    fixed_ok = jnp.maximum(b_fox, b_moba) <= FIXED_STABILIZER_MAX_BOUND

    def stabilizer(b):
        u = -(b * (1.0 + 2.0 ** -7)).astype(BF16).astype(F32)
        return jnp.where(fixed_ok, u, 0.0)

    big = jnp.exp2(jnp.ceil(jnp.log2(2.1 * b_moba + 160.0)))
    flag = fixed_ok.astype(jnp.int32).reshape(1)

    def row(*pieces):
        r = jnp.concatenate([jnp.ravel(p).astype(F32) for p in pieces])
        return jnp.pad(r, (0, 2 * D_MODEL - r.shape[0]))

    def lanes(v):
        return jnp.full((LANES,), v, F32)

    q_scale = scale * LOG2E
    params = jnp.stack([
        row(norm_g),
        row(b_gate),
        row(jnp.tile(fox_q_g * q_scale, N_HEADS), jnp.tile(fox_k_g, N_HEADS),
            jnp.tile(moba_q_g * q_scale, N_HEADS), jnp.tile(moba_k_g, N_HEADS)),
        row(b_f, jnp.zeros(LANES - N_HEADS), lanes(big), lanes(stabilizer(b_fox)),
            lanes(stabilizer(b_moba))),
    ] + [jnp.zeros(2 * D_MODEL, F32)] * (SUBLANES - 4))

    sel_fq, sel_fk, sel_mq, tri = _routing_constants()
    rc, rsp, rsm = _rope_tables(seq)

    rope_spec = pl.BlockSpec((TILE, LANES), lambda g: (g % tiles_per_seq, 0))
    head_spec = pl.BlockSpec((1, N_AUG_HEADS, TILE, LANES),
                             lambda g: (g // tiles_per_seq, 0, g % tiles_per_seq, 0))
    qkv_shape = jax.ShapeDtypeStruct((batch, N_AUG_HEADS, seq, LANES), BF16)

    q_all, k_all, v_all, gz, gates, cend = pl.pallas_call(
        functools.partial(_proj_kernel, tiles_per_seq=tiles_per_seq),
        grid=(n_tiles,),
        in_specs=[
            _const_spec((TILE, D_MODEL)),
            pl.BlockSpec((TILE, D_MODEL), lambda g: (jnp.minimum(g + 1, n_tiles - 1), 0)),
            _const_spec(w_in.shape), _const_spec((D_MODEL, LANES)),
            _const_spec((SUBLANES, 2 * D_MODEL)),
            rope_spec, rope_spec, rope_spec,
            _const_spec((TILE, TILE)),
            _const_spec((LANES, WIDTH)), _const_spec((LANES, WIDTH)), _const_spec((LANES, WIDTH)),
        ],
        out_specs=[
            head_spec, head_spec, head_spec,
            pl.BlockSpec((TILE, 2 * WIDTH), lambda g: (g, 0)),
            pl.BlockSpec((TILE, 2 * D_MODEL), lambda g: (g, 0)),
            pl.BlockSpec((1, SUBLANES, LANES), lambda g: (g, 0, 0)),
        ],
        out_shape=[
            qkv_shape, qkv_shape, qkv_shape,
            jax.ShapeDtypeStruct((rows, 2 * WIDTH), BF16),
            jax.ShapeDtypeStruct((rows, 2 * D_MODEL), BF16),
            jax.ShapeDtypeStruct((n_tiles, SUBLANES, LANES), F32),
        ],
        scratch_shapes=[pltpu.VMEM((1, LANES), F32),
                        pltpu.VMEM((LANES, WIDTH), F32),
                        pltpu.VMEM((2, TILE, D_MODEL), BF16)],
        compiler_params=pltpu.CompilerParams(
            dimension_semantics=("arbitrary",), vmem_limit_bytes=VMEM_LIMIT),
        name="proj_epilogue",
    )(x2, x2, w, w_fl, params,
      jnp.asarray(rc), jnp.asarray(rsp), jnp.asarray(rsm),
      jnp.asarray(tri, BF16), jnp.asarray(sel_fq, BF16), jnp.asarray(sel_fk, BF16),
      jnp.asarray(sel_mq, BF16))

    n_groups = N_AUG_HEADS // ATT_HEADS
    group_cols = ATT_HEADS // 2 * LANES
    att_tiles = seq // ATT_TILE
    cend = cend[:, 0, :N_HEADS].reshape(batch, tiles_per_seq, N_HEADS) * LOG2E
    first = jnp.stack([_first_live_blocks(cend), jnp.zeros((batch, att_tiles), jnp.int32)], axis=1)
    kv_spec = pl.BlockSpec((1, ATT_HEADS, seq, LANES), lambda b, g, i, *_: (b, g, 0, 0))
    row_spec = pl.BlockSpec((ATT_TILE, group_cols), lambda b, g, i, *_: (b * att_tiles + i, g))
    y = pl.pallas_call(
        _attn_kernel,
        grid_spec=pltpu.PrefetchScalarGridSpec(
            num_scalar_prefetch=2,
            grid=(batch, n_groups, att_tiles),
            in_specs=[
                pl.BlockSpec((1, ATT_HEADS, ATT_TILE, LANES), lambda b, g, i, *_: (b, g, i, 0)),
                kv_spec, kv_spec, row_spec,
            ],
            out_specs=row_spec,
            scratch_shapes=[pltpu.VMEM((ATT_HEADS, ATT_TILE, LANES), F32),
                            pltpu.VMEM((ATT_HEADS, ATT_TILE, 1), F32)],
        ),
        out_shape=jax.ShapeDtypeStruct((rows, 2 * WIDTH), BF16),
        compiler_params=pltpu.CompilerParams(
            dimension_semantics=("arbitrary", "arbitrary", "arbitrary"),
            vmem_limit_bytes=VMEM_LIMIT),
        name="flash_attn",
    )(flag, first.reshape(-1), q_all, k_all, v_all, gz)

    out = pl.pallas_call(
        _out_kernel,
        grid=(rows // OUT_TILE,),
        in_specs=[
            pl.BlockSpec((OUT_TILE, 2 * WIDTH), lambda g: (g, 0)),
            pl.BlockSpec((OUT_TILE, 2 * D_MODEL), lambda g: (g, 0)),
            pl.BlockSpec((OUT_TILE, D_MODEL), lambda g: (g, 0)),
            _const_spec((WIDTH, D_MODEL)), _const_spec((WIDTH, D_MODEL)),
            _const_spec((D_MODEL, D_MODEL)),
        ],
        out_specs=pl.BlockSpec((OUT_TILE, D_MODEL), lambda g: (g, 0)),
        out_shape=jax.ShapeDtypeStruct((rows, D_MODEL), F32),
        compiler_params=pltpu.CompilerParams(
            dimension_semantics=("arbitrary",), vmem_limit_bytes=VMEM_LIMIT),
        name="merge_out",
    )(y, gates, x2, w_fox.astype(BF16), w_moba.astype(BF16), w_out.astype(BF16))
    return out


def kernel(x, norm_g, w_in, b_f, b_gate, fox_q_g, fox_k_g, moba_q_g, moba_k_g, w_fox, w_moba, w_out):
    batch, seq, d_model = x.shape
    assert d_model == D_MODEL and seq % ATT_TILE == 0 and seq // MOBA_BLOCK <= GROUP
    assert (batch * seq) % OUT_TILE == 0
    x2 = x.reshape(batch * seq, D_MODEL)
    for layer in range(norm_g.shape[0]):
        x2 = _layer(x2, batch, seq, norm_g[layer], w_in[layer], b_f[layer], b_gate[layer],
                    fox_q_g[layer], fox_k_g[layer], moba_q_g[layer], moba_k_g[layer],
                    w_fox[layer], w_moba[layer], w_out[layer])
    return x2.reshape(batch, seq, D_MODEL)
```

```python
import functools

import numpy as np
import jax
import jax.numpy as jnp
from jax import lax
from jax.experimental import pallas as pl
from jax.experimental.pallas import tpu as pltpu

D_MODEL = 1024
HEAD_DIM = 64
N_HEADS = 8
WIDTH = N_HEADS * HEAD_DIM
ROPE_DIM = HEAD_DIM // 4
ROPE_HALF = ROPE_DIM // 2
ROPE_THETA = 500000.0
MOBA_BLOCK = 256
MOBA_TOPK = 3
RMS_EPS = 1e-6

LANES = 128
SUBLANES = 8
MXU_COLS = 256
TILE = MOBA_BLOCK
OUT_TILE = 1024
ATT_TILE = 512
ATT_HEADS = N_HEADS
FIXED_STABILIZER_MAX_BOUND = 55.0
EXP2_ZERO_BELOW = -152.0
BOUND_MARGIN = 1.02
GROUP = 16
N_AUG_HEADS = 2 * N_HEADS
U_COL_FOX = 6
U_COL_MOBA = GROUP
LOG2E = 1.4426950408889634

C_FQ, C_FK, C_FV, C_FZ = 0, 512, 1024, 1536
C_MQ, C_MK, C_MV, C_MZ = 2048, 2560, 3072, 3584
C_GA, C_GB = 4096, 5120
W_COLS = 6144

VMEM_LIMIT = 52 * 1024 * 1024

F32 = jnp.float32
BF16 = jnp.bfloat16


def _dot(a, b):
    return jnp.dot(a, b, preferred_element_type=F32)


def _dot_nt(a, b):
    return lax.dot_general(a, b, (((1,), (1,)), ((), ())), preferred_element_type=F32)


def _split3(v):
    hi = v.astype(BF16).astype(F32)
    r = v - hi
    mid = r.astype(BF16).astype(F32)
    lo = r - mid
    return hi, mid, lo


def _extra_base(h):
    return (h // 2) * LANES + (HEAD_DIM if h % 2 == 0 else 0)


def _routing_constants():
    sel_fq = np.zeros((LANES, WIDTH), np.float32)
    sel_fk = np.zeros((LANES, WIDTH), np.float32)
    sel_mq = np.zeros((LANES, WIDTH), np.float32)
    for h in range(N_HEADS):
        base = _extra_base(h)
        for part in range(3):
            sel_fq[part * 8 + h, base + part] = 1.0
            sel_fq[24, base + 3 + part] = 1.0
            sel_fk[24, base + part] = 1.0
            sel_fk[part * 8 + h, base + 3 + part] = -1.0
        sel_fq[25, base + U_COL_FOX] = 1.0
        sel_fk[24, base + U_COL_FOX] = 1.0
        for n in range(GROUP):
            sel_mq[h * GROUP + n, base + n] = 1.0
    tri = np.tril(np.ones((TILE, TILE), np.float32))
    return sel_fq, sel_fk, sel_mq, tri


def _rope_tables(seq):
    inv_freq = ROPE_THETA ** (-np.arange(0, ROPE_HALF, dtype=np.float32) * 2.0 / ROPE_DIM)
    ang = np.arange(seq, dtype=np.float32)[:, None] * inv_freq[None, :].astype(np.float32)
    cos, sin = np.cos(ang).astype(np.float32), np.sin(ang).astype(np.float32)
    rc = np.ones((seq, LANES), np.float32)
    rsp = np.zeros((seq, LANES), np.float32)
    rsm = np.zeros((seq, LANES), np.float32)
    for off in (0, HEAD_DIM):
        rc[:, off:off + ROPE_HALF] = cos
        rc[:, off + ROPE_HALF:off + ROPE_DIM] = cos
        rsm[:, off:off + ROPE_HALF] = -sin
        rsp[:, off + ROPE_HALF:off + ROPE_DIM] = sin
    return rc, rsp, rsm


def _proj_kernel(x0_ref, xn_ref, w_ref, wfl_ref, prm_ref,
                 rc_ref, rsp_ref, rsm_ref, tri_ref, selfq_ref, selfk_ref, selmq_ref,
                 q_out, k_out, v_out, gz_out, gate_out, cend_out,
                 carry_ref, kmt_ref, h_ref, *, tiles_per_seq):
    step = pl.program_id(0)
    t = step % tiles_per_seq
    slot = step % 2

    ng_ref = prm_ref.at[0:1, 0:D_MODEL]
    bg_ref = prm_ref.at[1:2, :]
    gfq_ref, gfk_ref, gmq_ref, gmk_ref = (
        prm_ref.at[2:3, n * WIDTH:(n + 1) * WIDTH] for n in range(4))
    bf_ref, big_ref, ufox_ref, umoba_ref = (
        prm_ref.at[3:4, n * LANES:(n + 1) * LANES] for n in range(4))

    @pl.when(t == 0)
    def _():
        carry_ref[...] = jnp.zeros_like(carry_ref)
        kmt_ref[...] = jnp.zeros_like(kmt_ref)

    lane = lax.broadcasted_iota(jnp.int32, (TILE, LANES), 1)
    low_half = lane < HEAD_DIM

    def normalized(x):
        ms = jnp.mean(x * x, axis=-1, keepdims=True)
        return (x * lax.rsqrt(ms + RMS_EPS) * ng_ref[...]).astype(BF16)

    @pl.when(step == 0)
    def _():
        h_ref[0] = normalized(x0_ref[...])

    def proj(c0, width):
        slabs = [_dot(h_ref[slot], w_ref[:, c:c + min(MXU_COLS, c0 + width - c)])
                 for c in range(c0, c0 + width, MXU_COLS)]
        return slabs[0] if len(slabs) == 1 else jnp.concatenate(slabs, axis=1)

    def head_norm(a, g_ref):
        tiles = []
        for p in range(WIDTH // LANES):
            ap = a[:, p * LANES:(p + 1) * LANES]
            sq = ap * ap
            s_lo = jnp.sum(jnp.where(low_half, sq, 0.0), axis=-1, keepdims=True)
            s_hi = jnp.sum(jnp.where(low_half, 0.0, sq), axis=-1, keepdims=True)
            inv_lo = lax.rsqrt(s_lo * (1.0 / HEAD_DIM) + RMS_EPS)
            inv_hi = lax.rsqrt(s_hi * (1.0 / HEAD_DIM) + RMS_EPS)
            scale = jnp.where(low_half, inv_lo, inv_hi)
            tiles.append(ap * scale * g_ref[:, p * LANES:(p + 1) * LANES])
        return tiles

    def rope(y):
        return (y * rc_ref[...] + pltpu.roll(y, ROPE_HALF, 1) * rsp_ref[...]
                + pltpu.roll(y, LANES - ROPE_HALF, 1) * rsm_ref[...])

    def split_tiles(a):
        return [a[:, p * LANES:(p + 1) * LANES] for p in range(WIDTH // LANES)]

    def store_heads(out_ref, head0, tiles, extras):
        for p, y in enumerate(tiles):
            e = extras(p)
            out_ref[0, head0 + 2 * p] = jnp.where(low_half, y, e).astype(BF16)
            out_ref[0, head0 + 2 * p + 1] = jnp.where(low_half, e, y).astype(BF16)

    def silu(z):
        hz = 0.5 * z
        return hz + hz * jnp.tanh(hz)

    def sigmoid(z):
        return 0.5 * jnp.tanh(0.5 * z) + 0.5

    fl = _dot(h_ref[slot], wfl_ref[...]) + bf_ref[...]
    mk_raw = proj(C_MK, WIDTH)
    mq_raw = proj(C_MQ, WIDTH)
    gate_chunks = [(C_GA + half * WIDTH, half * WIDTH) for half in range(2)]
    gate_chunks += [(C_GB + half * WIDTH, D_MODEL + half * WIDTH) for half in range(2)]
    gate_raw = [proj(c0, WIDTH) for c0, _ in gate_chunks]

    logf = jnp.minimum(fl, 0.0) - jnp.log(1.0 + jnp.exp(-jnp.abs(fl)))
    logf = jnp.where(lane < N_HEADS, logf, 0.0)
    l_hi, l_mid, l_lo = _split3(logf)
    packed = (l_hi + pltpu.roll(l_mid, 8, 1) + pltpu.roll(l_lo, 16, 1)).astype(BF16)
    cum = _dot(tri_ref[...], packed)
    fq = proj(C_FQ, WIDTH)
    fk = proj(C_FK, WIDTH)

    mk_tiles = [rope(y) for y in head_norm(mk_raw, gmk_ref)]
    row_id = lax.broadcasted_iota(jnp.int32, (LANES, LANES), 0)
    lane_sq = lax.broadcasted_iota(jnp.int32, (LANES, LANES), 1)
    for p, kr in enumerate(mk_tiles):
        km = jnp.mean(kr, axis=0, keepdims=True)
        hit = (((row_id == (2 * p) * GROUP + t) & (lane_sq < HEAD_DIM))
               | ((row_id == (2 * p + 1) * GROUP + t) & (lane_sq >= HEAD_DIM)))
        blk = kmt_ref[:, p * LANES:(p + 1) * LANES]
        kmt_ref[:, p * LANES:(p + 1) * LANES] = jnp.where(hit, km, blk)
    mk_ones = jnp.where((lane % HEAD_DIM == t) | (lane % HEAD_DIM == U_COL_MOBA), 1.0, 0.0)
    store_heads(k_out, N_HEADS, mk_tiles, lambda p: mk_ones)

    mq_tiles = [rope(y) for y in head_norm(mq_raw, gmq_ref)]
    q_full = jnp.concatenate(mq_tiles, axis=1)
    q_hi = q_full.astype(BF16)
    q_lo = (q_full - q_hi.astype(F32)).astype(BF16)
    kmt = kmt_ref[...]
    k_hi = kmt.astype(BF16)
    k_lo = (kmt - k_hi.astype(F32)).astype(BF16)
    gate = _dot_nt(q_hi, k_hi) + _dot_nt(q_hi, k_lo) + _dot_nt(q_lo, k_hi)

    c = cum + pltpu.roll(cum, LANES - 8, 1) + pltpu.roll(cum, LANES - 16, 1)
    c = jnp.where(lane < N_HEADS, c, 0.0) + carry_ref[...]
    carry_ref[...] = c[TILE - 1:TILE, :]
    cend_out[0] = jnp.broadcast_to(c[TILE - 1:TILE, :], (SUBLANES, LANES))
    c_hi, c_mid, c_lo = _split3(c * LOG2E)
    cparts = (c_hi + pltpu.roll(c_mid, 8, 1) + pltpu.roll(c_lo, 16, 1)
              + jnp.where(lane == 24, 1.0, 0.0)
              + jnp.where(lane == 25, ufox_ref[...], 0.0)).astype(BF16)
    ex_fq = _dot(cparts, selfq_ref[...])
    ex_fk = _dot(cparts, selfk_ref[...])
    fz = proj(C_FZ, WIDTH)
    mz = proj(C_MZ, WIDTH)
    fv = proj(C_FV, WIDTH)
    mv = proj(C_MV, WIDTH)

    v_ones = jnp.where(lane % HEAD_DIM == 0, 1.0, 0.0)
    store_heads(v_out, 0, split_tiles(fv), lambda p: v_ones)
    gz_out[:, 0:WIDTH] = silu(fz).astype(BF16)

    store_heads(q_out, 0, head_norm(fq, gfq_ref), lambda p: ex_fq[:, p * LANES:(p + 1) * LANES])
    store_heads(k_out, 0, head_norm(fk, gfk_ref), lambda p: ex_fk[:, p * LANES:(p + 1) * LANES])

    blk_id = lane % GROUP
    past = blk_id < t
    g = jnp.where(past, gate, -jnp.inf)
    beaten = jnp.zeros((TILE, LANES), jnp.int32)
    for d in range(1, GROUP):
        lower = jnp.where(blk_id >= d, pltpu.roll(g, d, 1), -jnp.inf)
        upper = jnp.where(blk_id < GROUP - d, pltpu.roll(g, LANES - d, 1), -jnp.inf)
        beaten = beaten + jnp.where(lower >= g, 1, 0) + jnp.where(upper > g, 1, 0)
    keep = (past & (beaten < MOBA_TOPK)) | (blk_id == t)
    maskvals = jnp.where(keep, 0.0, -big_ref[...]).astype(BF16)
    ex_mq = _dot(maskvals, selmq_ref[...])

    store_heads(v_out, N_HEADS, split_tiles(mv), lambda p: v_ones)
    gz_out[:, WIDTH:2 * WIDTH] = silu(mz).astype(BF16)

    is_u_lane = lane % HEAD_DIM == U_COL_MOBA
    store_heads(q_out, N_HEADS, mq_tiles,
                lambda p: jnp.where(is_u_lane, umoba_ref[...], ex_mq[:, p * LANES:(p + 1) * LANES]))
    for raw, (_, o0) in zip(gate_raw, gate_chunks):
        gate_out[:, o0:o0 + WIDTH] = sigmoid(raw + bg_ref[:, o0:o0 + WIDTH]).astype(BF16)

    h_ref[1 - slot] = normalized(xn_ref[...])


def _attn_kernel(flag_ref, first_ref, q_ref, k_ref, v_ref, gz_ref, o_ref, acc_ref, m_ref):
    b, group, i = pl.program_id(0), pl.program_id(1), pl.program_id(2)
    half = ATT_TILE // 2
    start = pl.multiple_of(i * ATT_TILE, ATT_TILE)
    heads = range(ATT_HEADS)

    row = lax.broadcasted_iota(jnp.int32, (half, ATT_TILE), 0)
    col = lax.broadcasted_iota(jnp.int32, (half, ATT_TILE), 1)
    mask_top = (lax.broadcasted_iota(jnp.int32, (half, half), 1)
                <= lax.broadcasted_iota(jnp.int32, (half, half), 0))
    mask_bot = col <= row + half

    def diag_scores(hh):
        kd = k_ref[0, hh, pl.ds(start, ATT_TILE), :]
        s_top = jnp.where(mask_top, _dot_nt(q_ref[0, hh, 0:half, :], kd[0:half]), -jnp.inf)
        s_bot = jnp.where(mask_bot, _dot_nt(q_ref[0, hh, half:ATT_TILE, :], kd), -jnp.inf)
        return s_top, s_bot

    def finish(accs, rows=slice(0, ATT_TILE)):
        lane = lax.broadcasted_iota(jnp.int32, accs[0].shape, 1)
        for pp in range(ATT_HEADS // 2):
            acc_e, acc_o = accs[2 * pp], accs[2 * pp + 1]
            o_e = acc_e * (1.0 / acc_e[:, HEAD_DIM:HEAD_DIM + 1])
            o_o = acc_o * (1.0 / acc_o[:, 0:1])
            o = jnp.where(lane < HEAD_DIM, o_e, o_o)
            cols = slice(pp * LANES, (pp + 1) * LANES)
            o_ref[rows, cols] = (o * gz_ref[rows, cols].astype(F32)).astype(BF16)

    @pl.when(flag_ref[0] == 1)
    def _fixed_stabilizer():
        def add_blocks(j, n_blocks, is_first):
            def block_scores(hh, jj):
                off = pl.multiple_of((j + jj) * ATT_TILE, ATT_TILE)
                return _dot_nt(q_ref[0, hh], k_ref[0, hh, pl.ds(off, ATT_TILE), :])

            ahead = n_blocks == 1
            s_next = block_scores(0, 0) if ahead else None
            for hh in heads:
                pv = None
                for jj in range(n_blocks):
                    off = pl.multiple_of((j + jj) * ATT_TILE, ATT_TILE)
                    s = s_next if ahead else block_scores(hh, jj)
                    if ahead and hh + 1 < ATT_HEADS:
                        s_next = block_scores(hh + 1, 0)
                    d = _dot(jnp.exp2(s).astype(BF16), v_ref[0, hh, pl.ds(off, ATT_TILE), :])
                    pv = d if pv is None else pv + d
                if is_first is True:
                    acc_ref[hh] = pv
                else:
                    acc_ref[hh] = pv + jnp.where(is_first, 0.0, acc_ref[hh])

        @pl.when(i == 0)
        def _():
            def zero(hh, carry):
                acc_ref[hh] = jnp.zeros((ATT_TILE, LANES), F32)
                return carry
            lax.fori_loop(0, ATT_HEADS, zero, 0)

        first = first_ref[(b * pl.num_programs(1) + group) * pl.num_programs(2) + i]
        n_live = i - first
        odd = n_live & 1

        @pl.when(odd == 1)
        def _():
            add_blocks(first, 1, True)

        def pair(p, carry):
            add_blocks(first + odd + 2 * p, 2, (odd == 0) & (p == 0))
            return carry

        lax.fori_loop(0, n_live >> 1, pair, 0)

        def past(hh, rows):
            return jnp.where(n_live > 0, acc_ref[hh, rows], 0.0)

        top_rows, bot_rows = slice(0, half), slice(half, ATT_TILE)
        units = ([(hh, bot_rows, ATT_TILE, mask_bot) for hh in heads]
                 + [(hh, top_rows, half, mask_top) for hh in heads])

        def scores(hh, rows, n_keys, mask):
            s = _dot_nt(q_ref[0, hh, rows, :], k_ref[0, hh, pl.ds(start, n_keys), :])
            return jnp.where(mask, s, -jnp.inf)

        accs, s_next = [], scores(*units[0])
        for n, (hh, rows, n_keys, _) in enumerate(units):
            s = s_next
            if n + 1 < len(units):
                s_next = scores(*units[n + 1])
            accs.append(past(hh, rows) + _dot(jnp.exp2(s).astype(BF16),
                                              v_ref[0, hh, pl.ds(start, n_keys), :]))
        finish(accs[:ATT_HEADS], bot_rows)
        finish(accs[ATT_HEADS:], top_rows)

    @pl.when(flag_ref[0] != 1)
    def _online():
        acc_ref[...] = jnp.zeros_like(acc_ref)
        m_ref[...] = jnp.full_like(m_ref, -jnp.inf)

        def update(s, m_old, acc_old, v):
            m_new = jnp.maximum(m_old, jnp.max(s, axis=-1, keepdims=True))
            pv = _dot(jnp.exp2(s - m_new).astype(BF16), v)
            return m_new, jnp.exp2(m_old - m_new) * acc_old + pv

        def body(j, carry):
            off = pl.multiple_of(j * ATT_TILE, ATT_TILE)
            for hh in heads:
                s = _dot_nt(q_ref[0, hh], k_ref[0, hh, pl.ds(off, ATT_TILE), :])
                m_ref[hh], acc_ref[hh] = update(s, m_ref[hh], acc_ref[hh],
                                                v_ref[0, hh, pl.ds(off, ATT_TILE), :])
            return carry

        lax.fori_loop(0, i, body, 0)

        accs = []
        for hh in heads:
            vd = v_ref[0, hh, pl.ds(start, ATT_TILE), :]
            s_top, s_bot = diag_scores(hh)
            _, top = update(s_top, m_ref[hh, 0:half], acc_ref[hh, 0:half], vd[0:half])
            _, bot = update(s_bot, m_ref[hh, half:ATT_TILE], acc_ref[hh, half:ATT_TILE], vd)
            accs.append(jnp.concatenate([top, bot], axis=0))
        finish(accs)


def _out_kernel(y_ref, gate_ref, x_ref, wf_ref, wm_ref, wo_ref, o_ref):
    slabs = [slice(c, c + MXU_COLS) for c in range(0, D_MODEL, MXU_COLS)]
    yf, ym = y_ref[:, 0:WIDTH], y_ref[:, WIDTH:2 * WIDTH]
    merged = []
    for cols in slabs:
        pf = _dot(yf, wf_ref[:, cols])
        pm = _dot(ym, wm_ref[:, cols])
        ga = gate_ref[:, cols].astype(F32)
        gb = gate_ref[:, D_MODEL + cols.start:D_MODEL + cols.stop].astype(F32)
        merged.append((ga * pf + gb * pm).astype(BF16))
    merged = jnp.concatenate(merged, axis=1)
    for cols in slabs:
        o_ref[:, cols] = x_ref[:, cols] + _dot(merged, wo_ref[:, cols])


def _first_live_blocks(cend):
    sub = ATT_TILE // TILE
    c_k = cend[:, sub - 1::sub]
    c_q = jnp.concatenate([cend[:, :1], cend[:, sub - 1:-1:sub]], axis=1)
    n = c_k.shape[1]
    past = jnp.arange(n)[None, :] < jnp.arange(n)[:, None]
    dead = past[None, :, :, None] & (c_q[:, :, None, :] - c_k[:, None, :, :] < EXP2_ZERO_BELOW)
    return jnp.min(jnp.sum(dead.astype(jnp.int32), axis=2), axis=-1)


def _const_spec(shape):
    return pl.BlockSpec(shape, lambda *_: (0,) * len(shape))


def _layer(x2, batch, seq, norm_g, w_in, b_f, b_gate, fox_q_g, fox_k_g, moba_q_g, moba_k_g,
           w_fox, w_moba, w_out):
    rows = batch * seq
    tiles_per_seq = seq // TILE
    n_tiles = rows // TILE
    scale = HEAD_DIM ** -0.5

    w = w_in.astype(BF16)
    w_fl = jnp.pad(w_in[:, W_COLS:], ((0, 0), (0, LANES - N_HEADS))).astype(BF16)

    def logit_bound(gq, gk):
        return 8.0 * LOG2E * BOUND_MARGIN * jnp.max(jnp.abs(gq)) * jnp.max(jnp.abs(gk))

    b_fox, b_moba = logit_bound(fox_q_g, fox_k_g), logit_bound(moba_q_g, moba_k_g)
    fixed_ok = jnp.maximum(b_fox, b_moba) <= FIXED_STABILIZER_MAX_BOUND

    def stabilizer(b):
        u = -(b * (1.0 + 2.0 ** -7)).astype(BF16).astype(F32)
        return jnp.where(fixed_ok, u, 0.0)

    big = jnp.exp2(jnp.ceil(jnp.log2(2.1 * b_moba + 160.0)))
    flag = fixed_ok.astype(jnp.int32).reshape(1)

    def row(*pieces):
        r = jnp.concatenate([jnp.ravel(p).astype(F32) for p in pieces])
        return jnp.pad(r, (0, 2 * D_MODEL - r.shape[0]))

    def lanes(v):
        return jnp.full((LANES,), v, F32)

    q_scale = scale * LOG2E
    params = jnp.stack([
        row(norm_g),
        row(b_gate),
        row(jnp.tile(fox_q_g * q_scale, N_HEADS), jnp.tile(fox_k_g, N_HEADS),
            jnp.tile(moba_q_g * q_scale, N_HEADS), jnp.tile(moba_k_g, N_HEADS)),
        row(b_f, jnp.zeros(LANES - N_HEADS), lanes(big), lanes(stabilizer(b_fox)),
            lanes(stabilizer(b_moba))),
    ] + [jnp.zeros(2 * D_MODEL, F32)] * (SUBLANES - 4))

    sel_fq, sel_fk, sel_mq, tri = _routing_constants()
    rc, rsp, rsm = _rope_tables(seq)

    rope_spec = pl.BlockSpec((TILE, LANES), lambda g: (g % tiles_per_seq, 0))
    head_spec = pl.BlockSpec((1, N_AUG_HEADS, TILE, LANES),
                             lambda g: (g // tiles_per_seq, 0, g % tiles_per_seq, 0))
    qkv_shape = jax.ShapeDtypeStruct((batch, N_AUG_HEADS, seq, LANES), BF16)

    q_all, k_all, v_all, gz, gates, cend = pl.pallas_call(
        functools.partial(_proj_kernel, tiles_per_seq=tiles_per_seq),
        grid=(n_tiles,),
        in_specs=[
            _const_spec((TILE, D_MODEL)),
            pl.BlockSpec((TILE, D_MODEL), lambda g: (jnp.minimum(g + 1, n_tiles - 1), 0)),
            _const_spec(w_in.shape), _const_spec((D_MODEL, LANES)),
            _const_spec((SUBLANES, 2 * D_MODEL)),
            rope_spec, rope_spec, rope_spec,
            _const_spec((TILE, TILE)),
            _const_spec((LANES, WIDTH)), _const_spec((LANES, WIDTH)), _const_spec((LANES, WIDTH)),
        ],
        out_specs=[
            head_spec, head_spec, head_spec,
            pl.BlockSpec((TILE, 2 * WIDTH), lambda g: (g, 0)),
            pl.BlockSpec((TILE, 2 * D_MODEL), lambda g: (g, 0)),
            pl.BlockSpec((1, SUBLANES, LANES), lambda g: (g, 0, 0)),
        ],
        out_shape=[
            qkv_shape, qkv_shape, qkv_shape,
            jax.ShapeDtypeStruct((rows, 2 * WIDTH), BF16),
            jax.ShapeDtypeStruct((rows, 2 * D_MODEL), BF16),
            jax.ShapeDtypeStruct((n_tiles, SUBLANES, LANES), F32),
        ],
        scratch_shapes=[pltpu.VMEM((1, LANES), F32),
                        pltpu.VMEM((LANES, WIDTH), F32),
                        pltpu.VMEM((2, TILE, D_MODEL), BF16)],
        compiler_params=pltpu.CompilerParams(
            dimension_semantics=("arbitrary",), vmem_limit_bytes=VMEM_LIMIT),
        name="proj_epilogue",
    )(x2, x2, w, w_fl, params,
      jnp.asarray(rc), jnp.asarray(rsp), jnp.asarray(rsm),
      jnp.asarray(tri, BF16), jnp.asarray(sel_fq, BF16), jnp.asarray(sel_fk, BF16),
      jnp.asarray(sel_mq, BF16))

    n_groups = N_AUG_HEADS // ATT_HEADS
    group_cols = ATT_HEADS // 2 * LANES
    att_tiles = seq // ATT_TILE
    cend = cend[:, 0, :N_HEADS].reshape(batch, tiles_per_seq, N_HEADS) * LOG2E
    first = jnp.stack([_first_live_blocks(cend), jnp.zeros((batch, att_tiles), jnp.int32)], axis=1)
    kv_spec = pl.BlockSpec((1, ATT_HEADS, seq, LANES), lambda b, g, i, *_: (b, g, 0, 0))
    row_spec = pl.BlockSpec((ATT_TILE, group_cols), lambda b, g, i, *_: (b * att_tiles + i, g))
    y = pl.pallas_call(
        _attn_kernel,
        grid_spec=pltpu.PrefetchScalarGridSpec(
            num_scalar_prefetch=2,
            grid=(batch, n_groups, att_tiles),
            in_specs=[
                pl.BlockSpec((1, ATT_HEADS, ATT_TILE, LANES), lambda b, g, i, *_: (b, g, i, 0)),
                kv_spec, kv_spec, row_spec,
            ],
            out_specs=row_spec,
            scratch_shapes=[pltpu.VMEM((ATT_HEADS, ATT_TILE, LANES), F32),
                            pltpu.VMEM((ATT_HEADS, ATT_TILE, 1), F32)],
        ),
        out_shape=jax.ShapeDtypeStruct((rows, 2 * WIDTH), BF16),
        compiler_params=pltpu.CompilerParams(
            dimension_semantics=("arbitrary", "arbitrary", "arbitrary"),
            vmem_limit_bytes=VMEM_LIMIT),
        name="flash_attn",
    )(flag, first.reshape(-1), q_all, k_all, v_all, gz)

    out = pl.pallas_call(
        _out_kernel,
        grid=(rows // OUT_TILE,),
        in_specs=[
            pl.BlockSpec((OUT_TILE, 2 * WIDTH), lambda g: (g, 0)),
            pl.BlockSpec((OUT_TILE, 2 * D_MODEL), lambda g: (g, 0)),
            pl.BlockSpec((OUT_TILE, D_MODEL), lambda g: (g, 0)),
            _const_spec((WIDTH, D_MODEL)), _const_spec((WIDTH, D_MODEL)),
            _const_spec((D_MODEL, D_MODEL)),
        ],
        out_specs=pl.BlockSpec((OUT_TILE, D_MODEL), lambda g: (g, 0)),
        out_shape=jax.ShapeDtypeStruct((rows, D_MODEL), F32),
        compiler_params=pltpu.CompilerParams(
            dimension_semantics=("arbitrary",), vmem_limit_bytes=VMEM_LIMIT),
        name="merge_out",
    )(y, gates, x2, w_fox.astype(BF16), w_moba.astype(BF16), w_out.astype(BF16))
    return out


def kernel(x, norm_g, w_in, b_f, b_gate, fox_q_g, fox_k_g, moba_q_g, moba_k_g, w_fox, w_moba, w_out):
    batch, seq, d_model = x.shape
    assert d_model == D_MODEL and seq % ATT_TILE == 0 and seq // MOBA_BLOCK <= GROUP
    assert (batch * seq) % OUT_TILE == 0
    x2 = x.reshape(batch * seq, D_MODEL)
    for layer in range(norm_g.shape[0]):
        x2 = _layer(x2, batch, seq, norm_g[layer], w_in[layer], b_f[layer], b_gate[layer],
                    fox_q_g[layer], fox_k_g[layer], moba_q_g[layer], moba_k_g[layer],
                    w_fox[layer], w_moba[layer], w_out[layer])
    return x2.reshape(batch, seq, D_MODEL)
```

```python
import functools

import numpy as np
import jax
import jax.numpy as jnp
from jax import lax
from jax.experimental import pallas as pl
from jax.experimental.pallas import tpu as pltpu

D_MODEL = 1024
HEAD_DIM = 64
N_HEADS = 8
WIDTH = N_HEADS * HEAD_DIM
ROPE_DIM = HEAD_DIM // 4
ROPE_HALF = ROPE_DIM // 2
ROPE_THETA = 500000.0
MOBA_BLOCK = 256
MOBA_TOPK = 3
RMS_EPS = 1e-6

LANES = 128
SUBLANES = 8
MXU_COLS = 256
TILE = MOBA_BLOCK
OUT_TILE = 1024
ATT_TILE = 512
ATT_HEADS = N_HEADS
FIXED_STABILIZER_MAX_BOUND = 55.0
EXP2_ZERO_BELOW = -152.0
BOUND_MARGIN = 1.02
GROUP = 16
N_AUG_HEADS = 2 * N_HEADS
U_COL_FOX = 6
U_COL_MOBA = GROUP
LOG2E = 1.4426950408889634

C_FQ, C_FK, C_FV, C_FZ = 0, 512, 1024, 1536
C_MQ, C_MK, C_MV, C_MZ = 2048, 2560, 3072, 3584
C_GA, C_GB = 4096, 5120
W_COLS = 6144

VMEM_LIMIT = 52 * 1024 * 1024

F32 = jnp.float32
BF16 = jnp.bfloat16


def _dot(a, b):
    return jnp.dot(a, b, preferred_element_type=F32)


def _dot_nt(a, b):
    return lax.dot_general(a, b, (((1,), (1,)), ((), ())), preferred_element_type=F32)


def _split3(v):
    hi = v.astype(BF16).astype(F32)
    r = v - hi
    mid = r.astype(BF16).astype(F32)
    lo = r - mid
    return hi, mid, lo


def _extra_base(h):
    return (h // 2) * LANES + (HEAD_DIM if h % 2 == 0 else 0)


def _routing_constants():
    sel_fq = np.zeros((LANES, WIDTH), np.float32)
    sel_fk = np.zeros((LANES, WIDTH), np.float32)
    sel_mq = np.zeros((LANES, WIDTH), np.float32)
    for h in range(N_HEADS):
        base = _extra_base(h)
        for part in range(3):
            sel_fq[part * 8 + h, base + part] = 1.0
            sel_fq[24, base + 3 + part] = 1.0
            sel_fk[24, base + part] = 1.0
            sel_fk[part * 8 + h, base + 3 + part] = -1.0
        sel_fq[25, base + U_COL_FOX] = 1.0
        sel_fk[24, base + U_COL_FOX] = 1.0
        for n in range(GROUP):
            sel_mq[h * GROUP + n, base + n] = 1.0
    tri = np.tril(np.ones((TILE, TILE), np.float32))
    return sel_fq, sel_fk, sel_mq, tri


def _rope_tables(seq):
    inv_freq = ROPE_THETA ** (-np.arange(0, ROPE_HALF, dtype=np.float32) * 2.0 / ROPE_DIM)
    ang = np.arange(seq, dtype=np.float32)[:, None] * inv_freq[None, :].astype(np.float32)
    cos, sin = np.cos(ang).astype(np.float32), np.sin(ang).astype(np.float32)
    rc = np.ones((seq, LANES), np.float32)
    rsp = np.zeros((seq, LANES), np.float32)
    rsm = np.zeros((seq, LANES), np.float32)
    for off in (0, HEAD_DIM):
        rc[:, off:off + ROPE_HALF] = cos
        rc[:, off + ROPE_HALF:off + ROPE_DIM] = cos
        rsm[:, off:off + ROPE_HALF] = -sin
        rsp[:, off + ROPE_HALF:off + ROPE_DIM] = sin
    return rc, rsp, rsm


def _proj_kernel(x0_ref, xn_ref, w_ref, wfl_ref, prm_ref,
                 rc_ref, rsp_ref, rsm_ref, tri_ref, selfq_ref, selfk_ref, selmq_ref,
                 q_out, k_out, v_out, gz_out, gate_out, cend_out,
                 carry_ref, kmt_ref, h_ref, *, tiles_per_seq):
    step = pl.program_id(0)
    t = step % tiles_per_seq
    slot = step % 2

    ng_ref = prm_ref.at[0:1, 0:D_MODEL]
    bg_ref = prm_ref.at[1:2, :]
    gfq_ref, gfk_ref, gmq_ref, gmk_ref = (
        prm_ref.at[2:3, n * WIDTH:(n + 1) * WIDTH] for n in range(4))
    bf_ref, big_ref, ufox_ref, umoba_ref = (
        prm_ref.at[3:4, n * LANES:(n + 1) * LANES] for n in range(4))

    @pl.when(t == 0)
    def _():
        carry_ref[...] = jnp.zeros_like(carry_ref)
        kmt_ref[...] = jnp.zeros_like(kmt_ref)

    lane = lax.broadcasted_iota(jnp.int32, (TILE, LANES), 1)
    low_half = lane < HEAD_DIM

    def normalized(x):
        ms = jnp.mean(x * x, axis=-1, keepdims=True)
        return (x * lax.rsqrt(ms + RMS_EPS) * ng_ref[...]).astype(BF16)

    @pl.when(step == 0)
    def _():
        h_ref[0] = normalized(x0_ref[...])

    def proj(c0, width):
        slabs = [_dot(h_ref[slot], w_ref[:, c:c + min(MXU_COLS, c0 + width - c)])
                 for c in range(c0, c0 + width, MXU_COLS)]
        return slabs[0] if len(slabs) == 1 else jnp.concatenate(slabs, axis=1)

    def head_norm(a, g_ref):
        tiles = []
        for p in range(WIDTH // LANES):
            ap = a[:, p * LANES:(p + 1) * LANES]
            sq = ap * ap
            s_lo = jnp.sum(jnp.where(low_half, sq, 0.0), axis=-1, keepdims=True)
            s_hi = jnp.sum(jnp.where(low_half, 0.0, sq), axis=-1, keepdims=True)
            inv_lo = lax.rsqrt(s_lo * (1.0 / HEAD_DIM) + RMS_EPS)
            inv_hi = lax.rsqrt(s_hi * (1.0 / HEAD_DIM) + RMS_EPS)
            scale = jnp.where(low_half, inv_lo, inv_hi)
            tiles.append(ap * scale * g_ref[:, p * LANES:(p + 1) * LANES])
        return tiles

    def rope(y):
        return (y * rc_ref[...] + pltpu.roll(y, ROPE_HALF, 1) * rsp_ref[...]
                + pltpu.roll(y, LANES - ROPE_HALF, 1) * rsm_ref[...])

    def split_tiles(a):
        return [a[:, p * LANES:(p + 1) * LANES] for p in range(WIDTH // LANES)]

    def store_heads(out_ref, head0, tiles, extras):
        for p, y in enumerate(tiles):
            e = extras(p)
            out_ref[0, head0 + 2 * p] = jnp.where(low_half, y, e).astype(BF16)
            out_ref[0, head0 + 2 * p + 1] = jnp.where(low_half, e, y).astype(BF16)

    def silu(z):
        hz = 0.5 * z
        return hz + hz * jnp.tanh(hz)

    def sigmoid(z):
        return 0.5 * jnp.tanh(0.5 * z) + 0.5

    fl = _dot(h_ref[slot], wfl_ref[...]) + bf_ref[...]
    mk_raw = proj(C_MK, WIDTH)
    mq_raw = proj(C_MQ, WIDTH)
    gate_chunks = [(C_GA + half * WIDTH, half * WIDTH) for half in range(2)]
    gate_chunks += [(C_GB + half * WIDTH, D_MODEL + half * WIDTH) for half in range(2)]
    gate_raw = [proj(c0, WIDTH) for c0, _ in gate_chunks]

    logf = jnp.minimum(fl, 0.0) - jnp.log(1.0 + jnp.exp(-jnp.abs(fl)))
    logf = jnp.where(lane < N_HEADS, logf, 0.0)
    l_hi, l_mid, l_lo = _split3(logf)
    packed = (l_hi + pltpu.roll(l_mid, 8, 1) + pltpu.roll(l_lo, 16, 1)).astype(BF16)
    cum = _dot(tri_ref[...], packed)
    fq = proj(C_FQ, WIDTH)
    fk = proj(C_FK, WIDTH)

    mk_tiles = [rope(y) for y in head_norm(mk_raw, gmk_ref)]
    row_id = lax.broadcasted_iota(jnp.int32, (LANES, LANES), 0)
    lane_sq = lax.broadcasted_iota(jnp.int32, (LANES, LANES), 1)
    for p, kr in enumerate(mk_tiles):
        km = jnp.mean(kr, axis=0, keepdims=True)
        hit = (((row_id == (2 * p) * GROUP + t) & (lane_sq < HEAD_DIM))
               | ((row_id == (2 * p + 1) * GROUP + t) & (lane_sq >= HEAD_DIM)))
        blk = kmt_ref[:, p * LANES:(p + 1) * LANES]
        kmt_ref[:, p * LANES:(p + 1) * LANES] = jnp.where(hit, km, blk)
    mk_ones = jnp.where((lane % HEAD_DIM == t) | (lane % HEAD_DIM == U_COL_MOBA), 1.0, 0.0)
    store_heads(k_out, N_HEADS, mk_tiles, lambda p: mk_ones)

    mq_tiles = [rope(y) for y in head_norm(mq_raw, gmq_ref)]
    q_full = jnp.concatenate(mq_tiles, axis=1)
    q_hi = q_full.astype(BF16)
    q_lo = (q_full - q_hi.astype(F32)).astype(BF16)
    kmt = kmt_ref[...]
    k_hi = kmt.astype(BF16)
    k_lo = (kmt - k_hi.astype(F32)).astype(BF16)
    gate = _dot_nt(q_hi, k_hi) + _dot_nt(q_hi, k_lo) + _dot_nt(q_lo, k_hi)

    c = cum + pltpu.roll(cum, LANES - 8, 1) + pltpu.roll(cum, LANES - 16, 1)
    c = jnp.where(lane < N_HEADS, c, 0.0) + carry_ref[...]
    carry_ref[...] = c[TILE - 1:TILE, :]
    cend_out[0] = jnp.broadcast_to(c[TILE - 1:TILE, :], (SUBLANES, LANES))
    c_hi, c_mid, c_lo = _split3(c * LOG2E)
    cparts = (c_hi + pltpu.roll(c_mid, 8, 1) + pltpu.roll(c_lo, 16, 1)
              + jnp.where(lane == 24, 1.0, 0.0)
              + jnp.where(lane == 25, ufox_ref[...], 0.0)).astype(BF16)
    ex_fq = _dot(cparts, selfq_ref[...])
    ex_fk = _dot(cparts, selfk_ref[...])
    fz = proj(C_FZ, WIDTH)
    mz = proj(C_MZ, WIDTH)
    fv = proj(C_FV, WIDTH)
    mv = proj(C_MV, WIDTH)

    v_ones = jnp.where(lane % HEAD_DIM == 0, 1.0, 0.0)
    store_heads(v_out, 0, split_tiles(fv), lambda p: v_ones)
    gz_out[:, 0:WIDTH] = silu(fz).astype(BF16)

    store_heads(q_out, 0, head_norm(fq, gfq_ref), lambda p: ex_fq[:, p * LANES:(p + 1) * LANES])
    store_heads(k_out, 0, head_norm(fk, gfk_ref), lambda p: ex_fk[:, p * LANES:(p + 1) * LANES])

    blk_id = lane % GROUP
    past = blk_id < t
    g = jnp.where(past, gate, -jnp.inf)
    beaten = jnp.zeros((TILE, LANES), jnp.int32)
    for d in range(1, GROUP):
        lower = jnp.where(blk_id >= d, pltpu.roll(g, d, 1), -jnp.inf)
        upper = jnp.where(blk_id < GROUP - d, pltpu.roll(g, LANES - d, 1), -jnp.inf)
        beaten = beaten + jnp.where(lower >= g, 1, 0) + jnp.where(upper > g, 1, 0)
    keep = (past & (beaten < MOBA_TOPK)) | (blk_id == t)
    maskvals = jnp.where(keep, 0.0, -big_ref[...]).astype(BF16)
    ex_mq = _dot(maskvals, selmq_ref[...])

    store_heads(v_out, N_HEADS, split_tiles(mv), lambda p: v_ones)
    gz_out[:, WIDTH:2 * WIDTH] = silu(mz).astype(BF16)

    is_u_lane = lane % HEAD_DIM == U_COL_MOBA
    store_heads(q_out, N_HEADS, mq_tiles,
                lambda p: jnp.where(is_u_lane, umoba_ref[...], ex_mq[:, p * LANES:(p + 1) * LANES]))
    for raw, (_, o0) in zip(gate_raw, gate_chunks):
        gate_out[:, o0:o0 + WIDTH] = sigmoid(raw + bg_ref[:, o0:o0 + WIDTH]).astype(BF16)

    h_ref[1 - slot] = normalized(xn_ref[...])


def _attn_kernel(flag_ref, first_ref, q_ref, k_ref, v_ref, gz_ref, o_ref, acc_ref, m_ref):
    b, group, i = pl.program_id(0), pl.program_id(1), pl.program_id(2)
    half = ATT_TILE // 2
    start = pl.multiple_of(i * ATT_TILE, ATT_TILE)
    heads = range(ATT_HEADS)

    row = lax.broadcasted_iota(jnp.int32, (half, ATT_TILE), 0)
    col = lax.broadcasted_iota(jnp.int32, (half, ATT_TILE), 1)
    mask_top = (lax.broadcasted_iota(jnp.int32, (half, half), 1)
                <= lax.broadcasted_iota(jnp.int32, (half, half), 0))
    mask_bot = col <= row + half

    def diag_scores(hh):
        kd = k_ref[0, hh, pl.ds(start, ATT_TILE), :]
        s_top = jnp.where(mask_top, _dot_nt(q_ref[0, hh, 0:half, :], kd[0:half]), -jnp.inf)
        s_bot = jnp.where(mask_bot, _dot_nt(q_ref[0, hh, half:ATT_TILE, :], kd), -jnp.inf)
        return s_top, s_bot

    def finish(accs, rows=slice(0, ATT_TILE)):
        lane = lax.broadcasted_iota(jnp.int32, accs[0].shape, 1)
        for pp in range(ATT_HEADS // 2):
            acc_e, acc_o = accs[2 * pp], accs[2 * pp + 1]
            o_e = acc_e * (1.0 / acc_e[:, HEAD_DIM:HEAD_DIM + 1])
            o_o = acc_o * (1.0 / acc_o[:, 0:1])
            o = jnp.where(lane < HEAD_DIM, o_e, o_o)
            cols = slice(pp * LANES, (pp + 1) * LANES)
            o_ref[rows, cols] = (o * gz_ref[rows, cols].astype(F32)).astype(BF16)

    @pl.when(flag_ref[0] == 1)
    def _fixed_stabilizer():
        def add_blocks(j, n_blocks, is_first):
            def block_scores(hh, jj):
                off = pl.multiple_of((j + jj) * ATT_TILE, ATT_TILE)
                return _dot_nt(q_ref[0, hh], k_ref[0, hh, pl.ds(off, ATT_TILE), :])

            ahead = n_blocks == 1
            s_next = block_scores(0, 0) if ahead else None
            for hh in heads:
                pv = None
                for jj in range(n_blocks):
                    off = pl.multiple_of((j + jj) * ATT_TILE, ATT_TILE)
                    s = s_next if ahead else block_scores(hh, jj)
                    if ahead and hh + 1 < ATT_HEADS:
                        s_next = block_scores(hh + 1, 0)
                    d = _dot(jnp.exp2(s).astype(BF16), v_ref[0, hh, pl.ds(off, ATT_TILE), :])
                    pv = d if pv is None else pv + d
                if is_first is True:
                    acc_ref[hh] = pv
                else:
                    acc_ref[hh] = pv + jnp.where(is_first, 0.0, acc_ref[hh])

        @pl.when(i == 0)
        def _():
            def zero(hh, carry):
                acc_ref[hh] = jnp.zeros((ATT_TILE, LANES), F32)
                return carry
            lax.fori_loop(0, ATT_HEADS, zero, 0)

        first = first_ref[(b * pl.num_programs(1) + group) * pl.num_programs(2) + i]
        n_live = i - first
        odd = n_live & 1

        @pl.when(odd == 1)
        def _():
            add_blocks(first, 1, True)

        def pair(p, carry):
            add_blocks(first + odd + 2 * p, 2, (odd == 0) & (p == 0))
            return carry

        lax.fori_loop(0, n_live >> 1, pair, 0)

        def past(hh, rows):
            return jnp.where(n_live > 0, acc_ref[hh, rows], 0.0)

        top_rows, bot_rows = slice(0, half), slice(half, ATT_TILE)
        units = ([(hh, bot_rows, ATT_TILE, mask_bot) for hh in heads]
                 + [(hh, top_rows, half, mask_top) for hh in heads])

        def scores(hh, rows, n_keys, mask):
            s = _dot_nt(q_ref[0, hh, rows, :], k_ref[0, hh, pl.ds(start, n_keys), :])
            return jnp.where(mask, s, -jnp.inf)

        accs, s_next = [], scores(*units[0])
        for n, (hh, rows, n_keys, _) in enumerate(units):
            s = s_next
            if n + 1 < len(units):
                s_next = scores(*units[n + 1])
            accs.append(past(hh, rows) + _dot(jnp.exp2(s).astype(BF16),
                                              v_ref[0, hh, pl.ds(start, n_keys), :]))
        finish(accs[:ATT_HEADS], bot_rows)
        finish(accs[ATT_HEADS:], top_rows)

    @pl.when(flag_ref[0] != 1)
    def _online():
        acc_ref[...] = jnp.zeros_like(acc_ref)
        m_ref[...] = jnp.full_like(m_ref, -jnp.inf)

        def update(s, m_old, acc_old, v):
            m_new = jnp.maximum(m_old, jnp.max(s, axis=-1, keepdims=True))
            pv = _dot(jnp.exp2(s - m_new).astype(BF16), v)
            return m_new, jnp.exp2(m_old - m_new) * acc_old + pv

        def body(j, carry):
            off = pl.multiple_of(j * ATT_TILE, ATT_TILE)
            for hh in heads:
                s = _dot_nt(q_ref[0, hh], k_ref[0, hh, pl.ds(off, ATT_TILE), :])
                m_ref[hh], acc_ref[hh] = update(s, m_ref[hh], acc_ref[hh],
                                                v_ref[0, hh, pl.ds(off, ATT_TILE), :])
            return carry

        lax.fori_loop(0, i, body, 0)

        accs = []
        for hh in heads:
            vd = v_ref[0, hh, pl.ds(start, ATT_TILE), :]
            s_top, s_bot = diag_scores(hh)
            _, top = update(s_top, m_ref[hh, 0:half], acc_ref[hh, 0:half], vd[0:half])
            _, bot = update(s_bot, m_ref[hh, half:ATT_TILE], acc_ref[hh, half:ATT_TILE], vd)
            accs.append(jnp.concatenate([top, bot], axis=0))
        finish(accs)


def _out_kernel(y_ref, gate_ref, x_ref, w_ref, o_ref):
    wf_ref, wm_ref = w_ref.at[0:WIDTH], w_ref.at[WIDTH:2 * WIDTH]
    wo_ref = w_ref.at[2 * WIDTH:2 * WIDTH + D_MODEL]
    slabs = [slice(c, c + MXU_COLS) for c in range(0, D_MODEL, MXU_COLS)]
    yf, ym = y_ref[:, 0:WIDTH], y_ref[:, WIDTH:2 * WIDTH]
    merged = []
    for cols in slabs:
        pf = _dot(yf, wf_ref[:, cols])
        pm = _dot(ym, wm_ref[:, cols])
        ga = gate_ref[:, cols].astype(F32)
        gb = gate_ref[:, D_MODEL + cols.start:D_MODEL + cols.stop].astype(F32)
        merged.append((ga * pf + gb * pm).astype(BF16))
    merged = jnp.concatenate(merged, axis=1)
    for cols in slabs:
        o_ref[:, cols] = x_ref[:, cols] + _dot(merged, wo_ref[:, cols])


def _first_live_blocks(cend):
    sub = ATT_TILE // TILE
    c_k = cend[:, sub - 1::sub]
    c_q = jnp.concatenate([cend[:, :1], cend[:, sub - 1:-1:sub]], axis=1)
    n = c_k.shape[1]
    past = jnp.arange(n)[None, :] < jnp.arange(n)[:, None]
    dead = past[None, :, :, None] & (c_q[:, :, None, :] - c_k[:, None, :, :] < EXP2_ZERO_BELOW)
    return jnp.min(jnp.sum(dead.astype(jnp.int32), axis=2), axis=-1)


def _const_spec(shape):
    return pl.BlockSpec(shape, lambda *_: (0,) * len(shape))


def _layer(x2, batch, seq, norm_g, w_in, b_f, b_gate, fox_q_g, fox_k_g, moba_q_g, moba_k_g,
           w_fox, w_moba, w_out):
    rows = batch * seq
    tiles_per_seq = seq // TILE
    n_tiles = rows // TILE
    scale = HEAD_DIM ** -0.5

    w = w_in.astype(BF16)
    w_fl = jnp.pad(w_in[:, W_COLS:], ((0, 0), (0, LANES - N_HEADS))).astype(BF16)

    gains = jnp.stack([fox_q_g, fox_k_g, moba_q_g, moba_k_g]).astype(F32)
    gain_max = jnp.max(jnp.abs(gains), axis=1)
    b_fox = 8.0 * LOG2E * BOUND_MARGIN * gain_max[0] * gain_max[1]
    b_moba = 8.0 * LOG2E * BOUND_MARGIN * gain_max[2] * gain_max[3]
    fixed_ok = jnp.maximum(b_fox, b_moba) <= FIXED_STABILIZER_MAX_BOUND

    def stabilizer(b):
        u = -(b * (1.0 + 2.0 ** -7)).astype(BF16).astype(F32)
        return jnp.where(fixed_ok, u, 0.0)

    big = jnp.exp2(jnp.ceil(jnp.log2(2.1 * b_moba + 160.0)))
    flag = fixed_ok.astype(jnp.int32).reshape(1)

    def row(*pieces):
        r = jnp.concatenate([jnp.ravel(p).astype(F32) for p in pieces])
        return jnp.pad(r, (0, 2 * D_MODEL - r.shape[0]))

    def lanes(v):
        return jnp.full((LANES,), v, F32)

    q_scale = scale * LOG2E
    params = jnp.stack([
        row(norm_g),
        row(b_gate),
        row(jnp.tile(gains * jnp.array([q_scale, 1.0, q_scale, 1.0], F32)[:, None], (1, N_HEADS))),
        row(b_f, jnp.zeros(LANES - N_HEADS), lanes(big), lanes(stabilizer(b_fox)),
            lanes(stabilizer(b_moba))),
    ] + [jnp.zeros(2 * D_MODEL, F32)] * (SUBLANES - 4))

    sel_fq, sel_fk, sel_mq, tri = _routing_constants()
    rc, rsp, rsm = _rope_tables(seq)

    rope_spec = pl.BlockSpec((TILE, LANES), lambda g: (g % tiles_per_seq, 0))
    head_spec = pl.BlockSpec((1, N_AUG_HEADS, TILE, LANES),
                             lambda g: (g // tiles_per_seq, 0, g % tiles_per_seq, 0))
    qkv_shape = jax.ShapeDtypeStruct((batch, N_AUG_HEADS, seq, LANES), BF16)

    q_all, k_all, v_all, gz, gates, cend = pl.pallas_call(
        functools.partial(_proj_kernel, tiles_per_seq=tiles_per_seq),
        grid=(n_tiles,),
        in_specs=[
            _const_spec((TILE, D_MODEL)),
            pl.BlockSpec((TILE, D_MODEL), lambda g: (jnp.minimum(g + 1, n_tiles - 1), 0)),
            _const_spec(w_in.shape), _const_spec((D_MODEL, LANES)),
            _const_spec((SUBLANES, 2 * D_MODEL)),
            rope_spec, rope_spec, rope_spec,
            _const_spec((TILE, TILE)),
            _const_spec((LANES, WIDTH)), _const_spec((LANES, WIDTH)), _const_spec((LANES, WIDTH)),
        ],
        out_specs=[
            head_spec, head_spec, head_spec,
            pl.BlockSpec((TILE, 2 * WIDTH), lambda g: (g, 0)),
            pl.BlockSpec((TILE, 2 * D_MODEL), lambda g: (g, 0)),
            pl.BlockSpec((1, SUBLANES, LANES), lambda g: (g, 0, 0)),
        ],
        out_shape=[
            qkv_shape, qkv_shape, qkv_shape,
            jax.ShapeDtypeStruct((rows, 2 * WIDTH), BF16),
            jax.ShapeDtypeStruct((rows, 2 * D_MODEL), BF16),
            jax.ShapeDtypeStruct((n_tiles, SUBLANES, LANES), F32),
        ],
        scratch_shapes=[pltpu.VMEM((1, LANES), F32),
                        pltpu.VMEM((LANES, WIDTH), F32),
                        pltpu.VMEM((2, TILE, D_MODEL), BF16)],
        compiler_params=pltpu.CompilerParams(
            dimension_semantics=("arbitrary",), vmem_limit_bytes=VMEM_LIMIT),
        name="proj_epilogue",
    )(x2, x2, w, w_fl, params,
      jnp.asarray(rc), jnp.asarray(rsp), jnp.asarray(rsm),
      jnp.asarray(tri, BF16), jnp.asarray(sel_fq, BF16), jnp.asarray(sel_fk, BF16),
      jnp.asarray(sel_mq, BF16))

    n_groups = N_AUG_HEADS // ATT_HEADS
    group_cols = ATT_HEADS // 2 * LANES
    att_tiles = seq // ATT_TILE
    cend = cend[:, 0, :N_HEADS].reshape(batch, tiles_per_seq, N_HEADS) * LOG2E
    first = jnp.stack([_first_live_blocks(cend), jnp.zeros((batch, att_tiles), jnp.int32)], axis=1)
    kv_spec = pl.BlockSpec((1, ATT_HEADS, seq, LANES), lambda b, g, i, *_: (b, g, 0, 0))
    row_spec = pl.BlockSpec((ATT_TILE, group_cols), lambda b, g, i, *_: (b * att_tiles + i, g))
    y = pl.pallas_call(
        _attn_kernel,
        grid_spec=pltpu.PrefetchScalarGridSpec(
            num_scalar_prefetch=2,
            grid=(batch, n_groups, att_tiles),
            in_specs=[
                pl.BlockSpec((1, ATT_HEADS, ATT_TILE, LANES), lambda b, g, i, *_: (b, g, i, 0)),
                kv_spec, kv_spec, row_spec,
            ],
            out_specs=row_spec,
            scratch_shapes=[pltpu.VMEM((ATT_HEADS, ATT_TILE, LANES), F32),
                            pltpu.VMEM((ATT_HEADS, ATT_TILE, 1), F32)],
        ),
        out_shape=jax.ShapeDtypeStruct((rows, 2 * WIDTH), BF16),
        compiler_params=pltpu.CompilerParams(
            dimension_semantics=("arbitrary", "arbitrary", "arbitrary"),
            vmem_limit_bytes=VMEM_LIMIT),
        name="flash_attn",
    )(flag, first.reshape(-1), q_all, k_all, v_all, gz)

    out = pl.pallas_call(
        _out_kernel,
        grid=(rows // OUT_TILE,),
        in_specs=[
            pl.BlockSpec((OUT_TILE, 2 * WIDTH), lambda g: (g, 0)),
            pl.BlockSpec((OUT_TILE, 2 * D_MODEL), lambda g: (g, 0)),
            pl.BlockSpec((OUT_TILE, D_MODEL), lambda g: (g, 0)),
            _const_spec((2 * WIDTH + D_MODEL, D_MODEL)),
        ],
        out_specs=pl.BlockSpec((OUT_TILE, D_MODEL), lambda g: (g, 0)),
        out_shape=jax.ShapeDtypeStruct((rows, D_MODEL), F32),
        compiler_params=pltpu.CompilerParams(
            dimension_semantics=("arbitrary",), vmem_limit_bytes=VMEM_LIMIT),
        name="merge_out",
    )(y, gates, x2, jnp.concatenate([w_fox, w_moba, w_out], axis=0).astype(BF16))
    return out


def kernel(x, norm_g, w_in, b_f, b_gate, fox_q_g, fox_k_g, moba_q_g, moba_k_g, w_fox, w_moba, w_out):
    batch, seq, d_model = x.shape
    assert d_model == D_MODEL and seq % ATT_TILE == 0 and seq // MOBA_BLOCK <= GROUP
    assert (batch * seq) % OUT_TILE == 0
    x2 = x.reshape(batch * seq, D_MODEL)
    for layer in range(norm_g.shape[0]):
        x2 = _layer(x2, batch, seq, norm_g[layer], w_in[layer], b_f[layer], b_gate[layer],
                    fox_q_g[layer], fox_k_g[layer], moba_q_g[layer], moba_k_g[layer],
                    w_fox[layer], w_moba[layer], w_out[layer])
    return x2.reshape(batch, seq, D_MODEL)
```

```python
import functools

import numpy as np
import jax
import jax.numpy as jnp
from jax import lax
from jax.experimental import pallas as pl
from jax.experimental.pallas import tpu as pltpu

D_MODEL = 1024
HEAD_DIM = 64
N_HEADS = 8
WIDTH = N_HEADS * HEAD_DIM
ROPE_DIM = HEAD_DIM // 4
ROPE_HALF = ROPE_DIM // 2
ROPE_THETA = 500000.0
MOBA_BLOCK = 256
MOBA_TOPK = 3
RMS_EPS = 1e-6

LANES = 128
SUBLANES = 8
MXU_COLS = 256
TILE = MOBA_BLOCK
OUT_TILE = 1024
ATT_TILE = 512
ATT_HEADS = N_HEADS
FIXED_STABILIZER_MAX_BOUND = 55.0
EXP2_ZERO_BELOW = -152.0
BOUND_MARGIN = 1.02
GROUP = 16
N_AUG_HEADS = 2 * N_HEADS
U_COL_FOX = 6
U_COL_MOBA = GROUP
LOG2E = 1.4426950408889634

C_FQ, C_FK, C_FV, C_FZ = 0, 512, 1024, 1536
C_MQ, C_MK, C_MV, C_MZ = 2048, 2560, 3072, 3584
C_GA, C_GB = 4096, 5120
W_COLS = 6144

VMEM_LIMIT = 52 * 1024 * 1024

F32 = jnp.float32
BF16 = jnp.bfloat16


def _dot(a, b):
    return jnp.dot(a, b, preferred_element_type=F32)


def _dot_nt(a, b):
    return lax.dot_general(a, b, (((1,), (1,)), ((), ())), preferred_element_type=F32)


def _split3(v):
    hi = v.astype(BF16).astype(F32)
    r = v - hi
    mid = r.astype(BF16).astype(F32)
    lo = r - mid
    return hi, mid, lo


def _extra_base(h):
    return (h // 2) * LANES + (HEAD_DIM if h % 2 == 0 else 0)


def _routing_constants():
    sel_fq = np.zeros((LANES, WIDTH), np.float32)
    sel_fk = np.zeros((LANES, WIDTH), np.float32)
    sel_mq = np.zeros((LANES, WIDTH), np.float32)
    for h in range(N_HEADS):
        base = _extra_base(h)
        for part in range(3):
            sel_fq[part * 8 + h, base + part] = 1.0
            sel_fq[24, base + 3 + part] = 1.0
            sel_fk[24, base + part] = 1.0
            sel_fk[part * 8 + h, base + 3 + part] = -1.0
        sel_fq[25, base + U_COL_FOX] = 1.0
        sel_fk[24, base + U_COL_FOX] = 1.0
        for n in range(GROUP):
            sel_mq[h * GROUP + n, base + n] = 1.0
    tri = np.tril(np.ones((TILE, TILE), np.float32))
    return sel_fq, sel_fk, sel_mq, tri


def _rope_tables(seq):
    inv_freq = ROPE_THETA ** (-np.arange(0, ROPE_HALF, dtype=np.float32) * 2.0 / ROPE_DIM)
    ang = np.arange(seq, dtype=np.float32)[:, None] * inv_freq[None, :].astype(np.float32)
    cos, sin = np.cos(ang).astype(np.float32), np.sin(ang).astype(np.float32)
    rc = np.ones((seq, LANES), np.float32)
    rsp = np.zeros((seq, LANES), np.float32)
    rsm = np.zeros((seq, LANES), np.float32)
    for off in (0, HEAD_DIM):
        rc[:, off:off + ROPE_HALF] = cos
        rc[:, off + ROPE_HALF:off + ROPE_DIM] = cos
        rsm[:, off:off + ROPE_HALF] = -sin
        rsp[:, off + ROPE_HALF:off + ROPE_DIM] = sin
    return rc, rsp, rsm


def _proj_kernel(x0_ref, xn_ref, w_ref, wfl_ref, prm_ref,
                 rc_ref, rsp_ref, rsm_ref, tri_ref, selfq_ref, selfk_ref, selmq_ref,
                 q_out, k_out, v_out, gz_out, gate_out, cend_out,
                 carry_ref, kmt_ref, h_ref, *, tiles_per_seq):
    step = pl.program_id(0)
    t = step % tiles_per_seq
    slot = step % 2

    ng_ref = prm_ref.at[0:1, 0:D_MODEL]
    bg_ref = prm_ref.at[1:2, :]
    gfq_ref, gfk_ref, gmq_ref, gmk_ref = (
        prm_ref.at[2:3, n * WIDTH:(n + 1) * WIDTH] for n in range(4))
    bf_ref, big_ref, ufox_ref, umoba_ref = (
        prm_ref.at[3:4, n * LANES:(n + 1) * LANES] for n in range(4))

    @pl.when(t == 0)
    def _():
        carry_ref[...] = jnp.zeros_like(carry_ref)
        kmt_ref[...] = jnp.zeros_like(kmt_ref)

    lane = lax.broadcasted_iota(jnp.int32, (TILE, LANES), 1)
    low_half = lane < HEAD_DIM

    def normalized(x):
        ms = jnp.mean(x * x, axis=-1, keepdims=True)
        return (x * lax.rsqrt(ms + RMS_EPS) * ng_ref[...]).astype(BF16)

    @pl.when(step == 0)
    def _():
        h_ref[0] = normalized(x0_ref[...])

    def proj(c0, width):
        slabs = [_dot(h_ref[slot], w_ref[:, c:c + min(MXU_COLS, c0 + width - c)])
                 for c in range(c0, c0 + width, MXU_COLS)]
        return slabs[0] if len(slabs) == 1 else jnp.concatenate(slabs, axis=1)

    def head_norm(a, g_ref):
        tiles = []
        for p in range(WIDTH // LANES):
            ap = a[:, p * LANES:(p + 1) * LANES]
            sq = ap * ap
            s_lo = jnp.sum(jnp.where(low_half, sq, 0.0), axis=-1, keepdims=True)
            s_hi = jnp.sum(jnp.where(low_half, 0.0, sq), axis=-1, keepdims=True)
            inv_lo = lax.rsqrt(s_lo * (1.0 / HEAD_DIM) + RMS_EPS)
            inv_hi = lax.rsqrt(s_hi * (1.0 / HEAD_DIM) + RMS_EPS)
            scale = jnp.where(low_half, inv_lo, inv_hi)
            tiles.append(ap * scale * g_ref[:, p * LANES:(p + 1) * LANES])
        return tiles

    def rope(y):
        return (y * rc_ref[...] + pltpu.roll(y, ROPE_HALF, 1) * rsp_ref[...]
                + pltpu.roll(y, LANES - ROPE_HALF, 1) * rsm_ref[...])

    def split_tiles(a):
        return [a[:, p * LANES:(p + 1) * LANES] for p in range(WIDTH // LANES)]

    def store_heads(out_ref, head0, tiles, extras):
        for p, y in enumerate(tiles):
            e = extras(p)
            out_ref[0, head0 + 2 * p] = jnp.where(low_half, y, e).astype(BF16)
            out_ref[0, head0 + 2 * p + 1] = jnp.where(low_half, e, y).astype(BF16)

    def silu(z):
        hz = 0.5 * z
        return hz + hz * jnp.tanh(hz)

    def sigmoid(z):
        return 0.5 * jnp.tanh(0.5 * z) + 0.5

    fl = _dot(h_ref[slot], wfl_ref[...]) + bf_ref[...]
    mk_raw = proj(C_MK, WIDTH)
    mq_raw = proj(C_MQ, WIDTH)
    gate_chunks = [(C_GA + half * WIDTH, half * WIDTH) for half in range(2)]
    gate_chunks += [(C_GB + half * WIDTH, D_MODEL + half * WIDTH) for half in range(2)]
    gate_raw = [proj(c0, WIDTH) for c0, _ in gate_chunks]

    logf = jnp.minimum(fl, 0.0) - jnp.log(1.0 + jnp.exp(-jnp.abs(fl)))
    logf = jnp.where(lane < N_HEADS, logf, 0.0)
    l_hi, l_mid, l_lo = _split3(logf)
    packed = (l_hi + pltpu.roll(l_mid, 8, 1) + pltpu.roll(l_lo, 16, 1)).astype(BF16)
    cum = _dot(tri_ref[...], packed)
    fq = proj(C_FQ, WIDTH)
    fk = proj(C_FK, WIDTH)

    mk_tiles = [rope(y) for y in head_norm(mk_raw, gmk_ref)]
    row_id = lax.broadcasted_iota(jnp.int32, (LANES, LANES), 0)
    lane_sq = lax.broadcasted_iota(jnp.int32, (LANES, LANES), 1)
    for p, kr in enumerate(mk_tiles):
        km = jnp.mean(kr, axis=0, keepdims=True)
        hit = (((row_id == (2 * p) * GROUP + t) & (lane_sq < HEAD_DIM))
               | ((row_id == (2 * p + 1) * GROUP + t) & (lane_sq >= HEAD_DIM)))
        blk = kmt_ref[:, p * LANES:(p + 1) * LANES]
        kmt_ref[:, p * LANES:(p + 1) * LANES] = jnp.where(hit, km, blk)
    mk_ones = jnp.where((lane % HEAD_DIM == t) | (lane % HEAD_DIM == U_COL_MOBA), 1.0, 0.0)
    store_heads(k_out, N_HEADS, mk_tiles, lambda p: mk_ones)

    mq_tiles = [rope(y) for y in head_norm(mq_raw, gmq_ref)]
    q_full = jnp.concatenate(mq_tiles, axis=1)
    q_hi = q_full.astype(BF16)
    q_lo = (q_full - q_hi.astype(F32)).astype(BF16)
    kmt = kmt_ref[...]
    k_hi = kmt.astype(BF16)
    k_lo = (kmt - k_hi.astype(F32)).astype(BF16)
    gate = _dot_nt(q_hi, k_hi) + _dot_nt(q_hi, k_lo) + _dot_nt(q_lo, k_hi)

    c = cum + pltpu.roll(cum, LANES - 8, 1) + pltpu.roll(cum, LANES - 16, 1)
    c = jnp.where(lane < N_HEADS, c, 0.0) + carry_ref[...]
    carry_ref[...] = c[TILE - 1:TILE, :]
    cend_out[0] = jnp.broadcast_to(c[TILE - 1:TILE, :], (SUBLANES, LANES))
    c_hi, c_mid, c_lo = _split3(c * LOG2E)
    cparts = (c_hi + pltpu.roll(c_mid, 8, 1) + pltpu.roll(c_lo, 16, 1)
              + jnp.where(lane == 24, 1.0, 0.0)
              + jnp.where(lane == 25, ufox_ref[...], 0.0)).astype(BF16)
    ex_fq = _dot(cparts, selfq_ref[...])
    ex_fk = _dot(cparts, selfk_ref[...])
    fz = proj(C_FZ, WIDTH)
    mz = proj(C_MZ, WIDTH)
    fv = proj(C_FV, WIDTH)
    mv = proj(C_MV, WIDTH)

    v_ones = jnp.where(lane % HEAD_DIM == 0, 1.0, 0.0)
    store_heads(v_out, 0, split_tiles(fv), lambda p: v_ones)
    gz_out[:, 0:WIDTH] = silu(fz).astype(BF16)

    store_heads(q_out, 0, head_norm(fq, gfq_ref), lambda p: ex_fq[:, p * LANES:(p + 1) * LANES])
    store_heads(k_out, 0, head_norm(fk, gfk_ref), lambda p: ex_fk[:, p * LANES:(p + 1) * LANES])

    blk_id = lane % GROUP
    past = blk_id < t
    g = jnp.where(past, gate, -jnp.inf)
    beaten = jnp.zeros((TILE, LANES), jnp.int32)
    for d in range(1, GROUP):
        lower = jnp.where(blk_id >= d, pltpu.roll(g, d, 1), -jnp.inf)
        upper = jnp.where(blk_id < GROUP - d, pltpu.roll(g, LANES - d, 1), -jnp.inf)
        beaten = beaten + jnp.where(lower >= g, 1, 0) + jnp.where(upper > g, 1, 0)
    keep = (past & (beaten < MOBA_TOPK)) | (blk_id == t)
    maskvals = jnp.where(keep, 0.0, -big_ref[...]).astype(BF16)
    ex_mq = _dot(maskvals, selmq_ref[...])

    store_heads(v_out, N_HEADS, split_tiles(mv), lambda p: v_ones)
    gz_out[:, WIDTH:2 * WIDTH] = silu(mz).astype(BF16)

    is_u_lane = lane % HEAD_DIM == U_COL_MOBA
    store_heads(q_out, N_HEADS, mq_tiles,
                lambda p: jnp.where(is_u_lane, umoba_ref[...], ex_mq[:, p * LANES:(p + 1) * LANES]))
    for raw, (_, o0) in zip(gate_raw, gate_chunks):
        gate_out[:, o0:o0 + WIDTH] = sigmoid(raw + bg_ref[:, o0:o0 + WIDTH]).astype(BF16)

    h_ref[1 - slot] = normalized(xn_ref[...])


def _attn_kernel(flag_ref, first_ref, q_ref, k_ref, v_ref, gz_ref, o_ref, acc_ref, m_ref):
    b, group, i = pl.program_id(0), pl.program_id(1), pl.program_id(2)
    half = ATT_TILE // 2
    start = pl.multiple_of(i * ATT_TILE, ATT_TILE)
    heads = range(ATT_HEADS)

    row = lax.broadcasted_iota(jnp.int32, (half, ATT_TILE), 0)
    col = lax.broadcasted_iota(jnp.int32, (half, ATT_TILE), 1)
    mask_top = (lax.broadcasted_iota(jnp.int32, (half, half), 1)
                <= lax.broadcasted_iota(jnp.int32, (half, half), 0))
    mask_bot = col <= row + half

    def diag_scores(hh):
        kd = k_ref[0, hh, pl.ds(start, ATT_TILE), :]
        s_top = jnp.where(mask_top, _dot_nt(q_ref[0, hh, 0:half, :], kd[0:half]), -jnp.inf)
        s_bot = jnp.where(mask_bot, _dot_nt(q_ref[0, hh, half:ATT_TILE, :], kd), -jnp.inf)
        return s_top, s_bot

    def finish(accs, rows=slice(0, ATT_TILE)):
        lane = lax.broadcasted_iota(jnp.int32, accs[0].shape, 1)
        for pp in range(ATT_HEADS // 2):
            acc_e, acc_o = accs[2 * pp], accs[2 * pp + 1]
            o_e = acc_e * (1.0 / acc_e[:, HEAD_DIM:HEAD_DIM + 1])
            o_o = acc_o * (1.0 / acc_o[:, 0:1])
            o = jnp.where(lane < HEAD_DIM, o_e, o_o)
            cols = slice(pp * LANES, (pp + 1) * LANES)
            o_ref[rows, cols] = (o * gz_ref[rows, cols].astype(F32)).astype(BF16)

    @pl.when(flag_ref[0] == 1)
    def _fixed_stabilizer():
        def add_blocks(j, n_blocks, is_first):
            def block_scores(hh, jj):
                off = pl.multiple_of((j + jj) * ATT_TILE, ATT_TILE)
                return _dot_nt(q_ref[0, hh], k_ref[0, hh, pl.ds(off, ATT_TILE), :])

            ahead = n_blocks == 1
            s_next = block_scores(0, 0) if ahead else None
            for hh in heads:
                pv = None
                for jj in range(n_blocks):
                    off = pl.multiple_of((j + jj) * ATT_TILE, ATT_TILE)
                    s = s_next if ahead else block_scores(hh, jj)
                    if ahead and hh + 1 < ATT_HEADS:
                        s_next = block_scores(hh + 1, 0)
                    d = _dot(jnp.exp2(s).astype(BF16), v_ref[0, hh, pl.ds(off, ATT_TILE), :])
                    pv = d if pv is None else pv + d
                if is_first is True:
                    acc_ref[hh] = pv
                else:
                    acc_ref[hh] = pv + jnp.where(is_first, 0.0, acc_ref[hh])

        @pl.when(i == 0)
        def _():
            def zero(hh, carry):
                acc_ref[hh] = jnp.zeros((ATT_TILE, LANES), F32)
                return carry
            lax.fori_loop(0, ATT_HEADS, zero, 0)

        first = first_ref[(b * pl.num_programs(1) + group) * pl.num_programs(2) + i]
        n_live = i - first
        odd = n_live & 1

        @pl.when(odd == 1)
        def _():
            add_blocks(first, 1, True)

        def pair(p, carry):
            add_blocks(first + odd + 2 * p, 2, (odd == 0) & (p == 0))
            return carry

        n_pairs = n_live >> 1
        lax.fori_loop(0, n_pairs - 1, pair, 0)

        def past(hh, rows):
            return jnp.where(n_live > 0, acc_ref[hh, rows], 0.0)

        top_rows, bot_rows = slice(0, half), slice(half, ATT_TILE)
        units = ([(hh, bot_rows, ATT_TILE, mask_bot) for hh in heads]
                 + [(hh, top_rows, half, mask_top) for hh in heads])

        def scores(hh, rows, n_keys, mask):
            s = _dot_nt(q_ref[0, hh, rows, :], k_ref[0, hh, pl.ds(start, n_keys), :])
            return jnp.where(mask, s, -jnp.inf)

        def diagonal():
            accs, s_next = [], scores(*units[0])
            for n, (hh, rows, n_keys, _) in enumerate(units):
                s = s_next
                if n + 1 < len(units):
                    s_next = scores(*units[n + 1])
                accs.append(past(hh, rows) + _dot(jnp.exp2(s).astype(BF16),
                                                  v_ref[0, hh, pl.ds(start, n_keys), :]))
            finish(accs[:ATT_HEADS], bot_rows)
            finish(accs[ATT_HEADS:], top_rows)

        @pl.when(n_pairs > 0)
        def _():
            add_blocks(i - 2, 2, (odd == 0) & (n_pairs == 1))
            diagonal()

        @pl.when(n_pairs == 0)
        def _():
            diagonal()

    @pl.when(flag_ref[0] != 1)
    def _online():
        acc_ref[...] = jnp.zeros_like(acc_ref)
        m_ref[...] = jnp.full_like(m_ref, -jnp.inf)

        def update(s, m_old, acc_old, v):
            m_new = jnp.maximum(m_old, jnp.max(s, axis=-1, keepdims=True))
            pv = _dot(jnp.exp2(s - m_new).astype(BF16), v)
            return m_new, jnp.exp2(m_old - m_new) * acc_old + pv

        def body(j, carry):
            off = pl.multiple_of(j * ATT_TILE, ATT_TILE)
            for hh in heads:
                s = _dot_nt(q_ref[0, hh], k_ref[0, hh, pl.ds(off, ATT_TILE), :])
                m_ref[hh], acc_ref[hh] = update(s, m_ref[hh], acc_ref[hh],
                                                v_ref[0, hh, pl.ds(off, ATT_TILE), :])
            return carry

        lax.fori_loop(0, i, body, 0)

        accs = []
        for hh in heads:
            vd = v_ref[0, hh, pl.ds(start, ATT_TILE), :]
            s_top, s_bot = diag_scores(hh)
            _, top = update(s_top, m_ref[hh, 0:half], acc_ref[hh, 0:half], vd[0:half])
            _, bot = update(s_bot, m_ref[hh, half:ATT_TILE], acc_ref[hh, half:ATT_TILE], vd)
            accs.append(jnp.concatenate([top, bot], axis=0))
        finish(accs)


def _out_kernel(y_ref, gate_ref, x_ref, w_ref, o_ref):
    wf_ref, wm_ref = w_ref.at[0:WIDTH], w_ref.at[WIDTH:2 * WIDTH]
    wo_ref = w_ref.at[2 * WIDTH:2 * WIDTH + D_MODEL]
    slabs = [slice(c, c + MXU_COLS) for c in range(0, D_MODEL, MXU_COLS)]
    yf, ym = y_ref[:, 0:WIDTH], y_ref[:, WIDTH:2 * WIDTH]
    merged = []
    for cols in slabs:
        pf = _dot(yf, wf_ref[:, cols])
        pm = _dot(ym, wm_ref[:, cols])
        ga = gate_ref[:, cols].astype(F32)
        gb = gate_ref[:, D_MODEL + cols.start:D_MODEL + cols.stop].astype(F32)
        merged.append((ga * pf + gb * pm).astype(BF16))
    merged = jnp.concatenate(merged, axis=1)
    for cols in slabs:
        o_ref[:, cols] = x_ref[:, cols] + _dot(merged, wo_ref[:, cols])


def _first_live_blocks(cend):
    sub = ATT_TILE // TILE
    c_k = cend[:, sub - 1::sub]
    c_q = jnp.concatenate([cend[:, :1], cend[:, sub - 1:-1:sub]], axis=1)
    n = c_k.shape[1]
    past = jnp.arange(n)[None, :] < jnp.arange(n)[:, None]
    dead = past[None, :, :, None] & (c_q[:, :, None, :] - c_k[:, None, :, :] < EXP2_ZERO_BELOW)
    return jnp.min(jnp.sum(dead.astype(jnp.int32), axis=2), axis=-1)


def _const_spec(shape):
    return pl.BlockSpec(shape, lambda *_: (0,) * len(shape))


def _layer(x2, batch, seq, norm_g, w_in, b_f, b_gate, fox_q_g, fox_k_g, moba_q_g, moba_k_g,
           w_fox, w_moba, w_out):
    rows = batch * seq
    tiles_per_seq = seq // TILE
    n_tiles = rows // TILE
    scale = HEAD_DIM ** -0.5

    w = w_in.astype(BF16)
    w_fl = jnp.pad(w_in[:, W_COLS:], ((0, 0), (0, LANES - N_HEADS))).astype(BF16)

    gains = jnp.stack([fox_q_g, fox_k_g, moba_q_g, moba_k_g]).astype(F32)
    gain_max = jnp.max(jnp.abs(gains), axis=1)
    b_fox = 8.0 * LOG2E * BOUND_MARGIN * gain_max[0] * gain_max[1]
    b_moba = 8.0 * LOG2E * BOUND_MARGIN * gain_max[2] * gain_max[3]
    fixed_ok = jnp.maximum(b_fox, b_moba) <= FIXED_STABILIZER_MAX_BOUND

    def stabilizer(b):
        u = -(b * (1.0 + 2.0 ** -7)).astype(BF16).astype(F32)
        return jnp.where(fixed_ok, u, 0.0)

    big = jnp.exp2(jnp.ceil(jnp.log2(2.1 * b_moba + 160.0)))
    flag = fixed_ok.astype(jnp.int32).reshape(1)

    def row(*pieces):
        r = jnp.concatenate([jnp.ravel(p).astype(F32) for p in pieces])
        return jnp.pad(r, (0, 2 * D_MODEL - r.shape[0]))

    def lanes(v):
        return jnp.full((LANES,), v, F32)

    q_scale = scale * LOG2E
    params = jnp.stack([
        row(norm_g),
        row(b_gate),
        row(jnp.tile(gains * jnp.array([q_scale, 1.0, q_scale, 1.0], F32)[:, None], (1, N_HEADS))),
        row(b_f, jnp.zeros(LANES - N_HEADS), lanes(big), lanes(stabilizer(b_fox)),
            lanes(stabilizer(b_moba))),
    ] + [jnp.zeros(2 * D_MODEL, F32)] * (SUBLANES - 4))

    sel_fq, sel_fk, sel_mq, tri = _routing_constants()
    rc, rsp, rsm = _rope_tables(seq)

    rope_spec = pl.BlockSpec((TILE, LANES), lambda g: (g % tiles_per_seq, 0))
    head_spec = pl.BlockSpec((1, N_AUG_HEADS, TILE, LANES),
                             lambda g: (g // tiles_per_seq, 0, g % tiles_per_seq, 0))
    qkv_shape = jax.ShapeDtypeStruct((batch, N_AUG_HEADS, seq, LANES), BF16)

    q_all, k_all, v_all, gz, gates, cend = pl.pallas_call(
        functools.partial(_proj_kernel, tiles_per_seq=tiles_per_seq),
        grid=(n_tiles,),
        in_specs=[
            _const_spec((TILE, D_MODEL)),
            pl.BlockSpec((TILE, D_MODEL), lambda g: (jnp.minimum(g + 1, n_tiles - 1), 0)),
            _const_spec(w_in.shape), _const_spec((D_MODEL, LANES)),
            _const_spec((SUBLANES, 2 * D_MODEL)),
            rope_spec, rope_spec, rope_spec,
            _const_spec((TILE, TILE)),
            _const_spec((LANES, WIDTH)), _const_spec((LANES, WIDTH)), _const_spec((LANES, WIDTH)),
        ],
        out_specs=[
            head_spec, head_spec, head_spec,
            pl.BlockSpec((TILE, 2 * WIDTH), lambda g: (g, 0)),
            pl.BlockSpec((TILE, 2 * D_MODEL), lambda g: (g, 0)),
            pl.BlockSpec((1, SUBLANES, LANES), lambda g: (g, 0, 0)),
        ],
        out_shape=[
            qkv_shape, qkv_shape, qkv_shape,
            jax.ShapeDtypeStruct((rows, 2 * WIDTH), BF16),
            jax.ShapeDtypeStruct((rows, 2 * D_MODEL), BF16),
            jax.ShapeDtypeStruct((n_tiles, SUBLANES, LANES), F32),
        ],
        scratch_shapes=[pltpu.VMEM((1, LANES), F32),
                        pltpu.VMEM((LANES, WIDTH), F32),
                        pltpu.VMEM((2, TILE, D_MODEL), BF16)],
        compiler_params=pltpu.CompilerParams(
            dimension_semantics=("arbitrary",), vmem_limit_bytes=VMEM_LIMIT),
        name="proj_epilogue",
    )(x2, x2, w, w_fl, params,
      jnp.asarray(rc), jnp.asarray(rsp), jnp.asarray(rsm),
      jnp.asarray(tri, BF16), jnp.asarray(sel_fq, BF16), jnp.asarray(sel_fk, BF16),
      jnp.asarray(sel_mq, BF16))

    n_groups = N_AUG_HEADS // ATT_HEADS
    group_cols = ATT_HEADS // 2 * LANES
    att_tiles = seq // ATT_TILE
    cend = cend[:, 0, :N_HEADS].reshape(batch, tiles_per_seq, N_HEADS) * LOG2E
    first = jnp.stack([_first_live_blocks(cend), jnp.zeros((batch, att_tiles), jnp.int32)], axis=1)
    kv_spec = pl.BlockSpec((1, ATT_HEADS, seq, LANES), lambda b, g, i, *_: (b, g, 0, 0))
    row_spec = pl.BlockSpec((ATT_TILE, group_cols), lambda b, g, i, *_: (b * att_tiles + i, g))
    y = pl.pallas_call(
        _attn_kernel,
        grid_spec=pltpu.PrefetchScalarGridSpec(
            num_scalar_prefetch=2,
            grid=(batch, n_groups, att_tiles),
            in_specs=[
                pl.BlockSpec((1, ATT_HEADS, ATT_TILE, LANES), lambda b, g, i, *_: (b, g, i, 0)),
                kv_spec, kv_spec, row_spec,
            ],
            out_specs=row_spec,
            scratch_shapes=[pltpu.VMEM((ATT_HEADS, ATT_TILE, LANES), F32),
                            pltpu.VMEM((ATT_HEADS, ATT_TILE, 1), F32)],
        ),
        out_shape=jax.ShapeDtypeStruct((rows, 2 * WIDTH), BF16),
        compiler_params=pltpu.CompilerParams(
            dimension_semantics=("arbitrary", "arbitrary", "arbitrary"),
            vmem_limit_bytes=VMEM_LIMIT),
        name="flash_attn",
    )(flag, first.reshape(-1), q_all, k_all, v_all, gz)

    out = pl.pallas_call(
        _out_kernel,
        grid=(rows // OUT_TILE,),
        in_specs=[
            pl.BlockSpec((OUT_TILE, 2 * WIDTH), lambda g: (g, 0)),
            pl.BlockSpec((OUT_TILE, 2 * D_MODEL), lambda g: (g, 0)),
            pl.BlockSpec((OUT_TILE, D_MODEL), lambda g: (g, 0)),
            _const_spec((2 * WIDTH + D_MODEL, D_MODEL)),
        ],
        out_specs=pl.BlockSpec((OUT_TILE, D_MODEL), lambda g: (g, 0)),
        out_shape=jax.ShapeDtypeStruct((rows, D_MODEL), F32),
        compiler_params=pltpu.CompilerParams(
            dimension_semantics=("arbitrary",), vmem_limit_bytes=VMEM_LIMIT),
        name="merge_out",
    )(y, gates, x2, jnp.concatenate([w_fox, w_moba, w_out], axis=0).astype(BF16))
    return out


def kernel(x, norm_g, w_in, b_f, b_gate, fox_q_g, fox_k_g, moba_q_g, moba_k_g, w_fox, w_moba, w_out):
    batch, seq, d_model = x.shape
    assert d_model == D_MODEL and seq % ATT_TILE == 0 and seq // MOBA_BLOCK <= GROUP
    assert (batch * seq) % OUT_TILE == 0
    x2 = x.reshape(batch * seq, D_MODEL)
    for layer in range(norm_g.shape[0]):
        x2 = _layer(x2, batch, seq, norm_g[layer], w_in[layer], b_f[layer], b_gate[layer],
                    fox_q_g[layer], fox_k_g[layer], moba_q_g[layer], moba_k_g[layer],
                    w_fox[layer], w_moba[layer], w_out[layer])
    return x2.reshape(batch, seq, D_MODEL)
```

```python
import functools

import numpy as np
import jax
import jax.numpy as jnp
from jax import lax
from jax.experimental import pallas as pl
from jax.experimental.pallas import tpu as pltpu

D_MODEL = 1024
HEAD_DIM = 64
N_HEADS = 8
WIDTH = N_HEADS * HEAD_DIM
ROPE_DIM = HEAD_DIM // 4
ROPE_HALF = ROPE_DIM // 2
ROPE_THETA = 500000.0
MOBA_BLOCK = 256
MOBA_TOPK = 3
RMS_EPS = 1e-6

LANES = 128
SUBLANES = 8
MXU_COLS = 256
TILE = MOBA_BLOCK
OUT_TILE = 1024
ATT_TILE = 512
ATT_HEADS = N_HEADS
FIXED_STABILIZER_MAX_BOUND = 55.0
EXP2_ZERO_BELOW = -152.0
BOUND_MARGIN = 1.02
GROUP = 16
N_AUG_HEADS = 2 * N_HEADS
U_COL_FOX = 6
U_COL_MOBA = GROUP
LOG2E = 1.4426950408889634

C_FQ, C_FK, C_FV, C_FZ = 0, 512, 1024, 1536
C_MQ, C_MK, C_MV, C_MZ = 2048, 2560, 3072, 3584
C_GA, C_GB = 4096, 5120
W_COLS = 6144

VMEM_LIMIT = 52 * 1024 * 1024

F32 = jnp.float32
BF16 = jnp.bfloat16


def _dot(a, b):
    return jnp.dot(a, b, preferred_element_type=F32)


def _dot_nt(a, b):
    return lax.dot_general(a, b, (((1,), (1,)), ((), ())), preferred_element_type=F32)


def _split3(v):
    hi = v.astype(BF16).astype(F32)
    r = v - hi
    mid = r.astype(BF16).astype(F32)
    lo = r - mid
    return hi, mid, lo


def _extra_base(h):
    return (h // 2) * LANES + (HEAD_DIM if h % 2 == 0 else 0)


def _routing_constants():
    sel_fq = np.zeros((LANES, WIDTH), np.float32)
    sel_fk = np.zeros((LANES, WIDTH), np.float32)
    sel_mq = np.zeros((LANES, WIDTH), np.float32)
    for h in range(N_HEADS):
        base = _extra_base(h)
        for part in range(3):
            sel_fq[part * 8 + h, base + part] = 1.0
            sel_fq[24, base + 3 + part] = 1.0
            sel_fk[24, base + part] = 1.0
            sel_fk[part * 8 + h, base + 3 + part] = -1.0
        sel_fq[25, base + U_COL_FOX] = 1.0
        sel_fk[24, base + U_COL_FOX] = 1.0
        for n in range(GROUP):
            sel_mq[h * GROUP + n, base + n] = 1.0
    tri = np.tril(np.ones((TILE, TILE), np.float32))
    return sel_fq, sel_fk, sel_mq, tri


def _rope_tables(seq):
    inv_freq = ROPE_THETA ** (-np.arange(0, ROPE_HALF, dtype=np.float32) * 2.0 / ROPE_DIM)
    ang = np.arange(seq, dtype=np.float32)[:, None] * inv_freq[None, :].astype(np.float32)
    cos, sin = np.cos(ang).astype(np.float32), np.sin(ang).astype(np.float32)
    rc = np.ones((seq, LANES), np.float32)
    rsp = np.zeros((seq, LANES), np.float32)
    rsm = np.zeros((seq, LANES), np.float32)
    for off in (0, HEAD_DIM):
        rc[:, off:off + ROPE_HALF] = cos
        rc[:, off + ROPE_HALF:off + ROPE_DIM] = cos
        rsm[:, off:off + ROPE_HALF] = -sin
        rsp[:, off + ROPE_HALF:off + ROPE_DIM] = sin
    return rc, rsp, rsm


def _proj_kernel(x0_ref, xn_ref, w_ref, wfl_ref, prm_ref,
                 rc_ref, rsp_ref, rsm_ref, tri_ref, selfq_ref, selfk_ref, selmq_ref,
                 q_out, k_out, v_out, gz_out, gate_out, cend_out,
                 carry_ref, kmt_ref, h_ref, *, tiles_per_seq):
    step = pl.program_id(0)
    t = step % tiles_per_seq
    slot = step % 2

    ng_ref = prm_ref.at[0:1, 0:D_MODEL]
    bg_ref = prm_ref.at[1:2, :]
    gfq_ref, gfk_ref, gmq_ref, gmk_ref = (
        prm_ref.at[2:3, n * WIDTH:(n + 1) * WIDTH] for n in range(4))
    bf_ref, big_ref, ufox_ref, umoba_ref = (
        prm_ref.at[3:4, n * LANES:(n + 1) * LANES] for n in range(4))

    @pl.when(t == 0)
    def _():
        carry_ref[...] = jnp.zeros_like(carry_ref)
        kmt_ref[...] = jnp.zeros_like(kmt_ref)

    lane = lax.broadcasted_iota(jnp.int32, (TILE, LANES), 1)
    low_half = lane < HEAD_DIM

    def normalized(x):
        ms = jnp.mean(x * x, axis=-1, keepdims=True)
        return (x * lax.rsqrt(ms + RMS_EPS) * ng_ref[...]).astype(BF16)

    @pl.when(step == 0)
    def _():
        h_ref[0] = normalized(x0_ref[...])

    def proj(c0, width):
        slabs = [_dot(h_ref[slot], w_ref[:, c:c + min(MXU_COLS, c0 + width - c)])
                 for c in range(c0, c0 + width, MXU_COLS)]
        return slabs[0] if len(slabs) == 1 else jnp.concatenate(slabs, axis=1)

    def head_norm(a, g_ref):
        tiles = []
        for p in range(WIDTH // LANES):
            ap = a[:, p * LANES:(p + 1) * LANES]
            sq = ap * ap
            s_lo = jnp.sum(jnp.where(low_half, sq, 0.0), axis=-1, keepdims=True)
            s_hi = jnp.sum(jnp.where(low_half, 0.0, sq), axis=-1, keepdims=True)
            inv_lo = lax.rsqrt(s_lo * (1.0 / HEAD_DIM) + RMS_EPS)
            inv_hi = lax.rsqrt(s_hi * (1.0 / HEAD_DIM) + RMS_EPS)
            scale = jnp.where(low_half, inv_lo, inv_hi)
            tiles.append(ap * scale * g_ref[:, p * LANES:(p + 1) * LANES])
        return tiles

    def rope(y):
        return (y * rc_ref[...] + pltpu.roll(y, ROPE_HALF, 1) * rsp_ref[...]
                + pltpu.roll(y, LANES - ROPE_HALF, 1) * rsm_ref[...])

    def split_tiles(a):
        return [a[:, p * LANES:(p + 1) * LANES] for p in range(WIDTH // LANES)]

    def store_heads(out_ref, head0, tiles, extras):
        for p, y in enumerate(tiles):
            e = extras(p)
            out_ref[0, head0 + 2 * p] = jnp.where(low_half, y, e).astype(BF16)
            out_ref[0, head0 + 2 * p + 1] = jnp.where(low_half, e, y).astype(BF16)

    def silu(z):
        hz = 0.5 * z
        return hz + hz * jnp.tanh(hz)

    def sigmoid(z):
        return 0.5 * jnp.tanh(0.5 * z) + 0.5

    fl = _dot(h_ref[slot], wfl_ref[...]) + bf_ref[...]
    mk_raw = proj(C_MK, WIDTH)
    mq_raw = proj(C_MQ, WIDTH)
    gate_chunks = [(C_GA + half * WIDTH, half * WIDTH) for half in range(2)]
    gate_chunks += [(C_GB + half * WIDTH, D_MODEL + half * WIDTH) for half in range(2)]
    gate_raw = [proj(c0, WIDTH) for c0, _ in gate_chunks]

    logf = jnp.minimum(fl, 0.0) - jnp.log(1.0 + jnp.exp(-jnp.abs(fl)))
    logf = jnp.where(lane < N_HEADS, logf, 0.0)
    l_hi, l_mid, l_lo = _split3(logf)
    packed = (l_hi + pltpu.roll(l_mid, 8, 1) + pltpu.roll(l_lo, 16, 1)).astype(BF16)
    cum = _dot(tri_ref[...], packed)
    fq = proj(C_FQ, WIDTH)
    fk = proj(C_FK, WIDTH)

    mk_tiles = [rope(y) for y in head_norm(mk_raw, gmk_ref)]
    row_id = lax.broadcasted_iota(jnp.int32, (LANES, LANES), 0)
    lane_sq = lax.broadcasted_iota(jnp.int32, (LANES, LANES), 1)
    for p, kr in enumerate(mk_tiles):
        km = jnp.mean(kr, axis=0, keepdims=True)
        hit = (((row_id == (2 * p) * GROUP + t) & (lane_sq < HEAD_DIM))
               | ((row_id == (2 * p + 1) * GROUP + t) & (lane_sq >= HEAD_DIM)))
        blk = kmt_ref[:, p * LANES:(p + 1) * LANES]
        kmt_ref[:, p * LANES:(p + 1) * LANES] = jnp.where(hit, km, blk)
    mk_ones = jnp.where((lane % HEAD_DIM == t) | (lane % HEAD_DIM == U_COL_MOBA), 1.0, 0.0)
    store_heads(k_out, N_HEADS, mk_tiles, lambda p: mk_ones)

    mq_tiles = [rope(y) for y in head_norm(mq_raw, gmq_ref)]
    q_full = jnp.concatenate(mq_tiles, axis=1)
    q_hi = q_full.astype(BF16)
    q_lo = (q_full - q_hi.astype(F32)).astype(BF16)
    kmt = kmt_ref[...]
    k_hi = kmt.astype(BF16)
    k_lo = (kmt - k_hi.astype(F32)).astype(BF16)
    gate = _dot_nt(q_hi, k_hi) + _dot_nt(q_hi, k_lo) + _dot_nt(q_lo, k_hi)

    c = cum + pltpu.roll(cum, LANES - 8, 1) + pltpu.roll(cum, LANES - 16, 1)
    c = jnp.where(lane < N_HEADS, c, 0.0) + carry_ref[...]
    carry_ref[...] = c[TILE - 1:TILE, :]
    cend_out[0] = jnp.broadcast_to(c[TILE - 1:TILE, :], (SUBLANES, LANES))
    c_hi, c_mid, c_lo = _split3(c * LOG2E)
    cparts = (c_hi + pltpu.roll(c_mid, 8, 1) + pltpu.roll(c_lo, 16, 1)
              + jnp.where(lane == 24, 1.0, 0.0)
              + jnp.where(lane == 25, ufox_ref[...], 0.0)).astype(BF16)
    ex_fq = _dot(cparts, selfq_ref[...])
    ex_fk = _dot(cparts, selfk_ref[...])
    fz = proj(C_FZ, WIDTH)
    mz = proj(C_MZ, WIDTH)
    fv = proj(C_FV, WIDTH)
    mv = proj(C_MV, WIDTH)

    v_ones = jnp.where(lane % HEAD_DIM == 0, 1.0, 0.0)
    store_heads(v_out, 0, split_tiles(fv), lambda p: v_ones)
    gz_out[:, 0:WIDTH] = silu(fz).astype(BF16)

    store_heads(q_out, 0, head_norm(fq, gfq_ref), lambda p: ex_fq[:, p * LANES:(p + 1) * LANES])
    store_heads(k_out, 0, head_norm(fk, gfk_ref), lambda p: ex_fk[:, p * LANES:(p + 1) * LANES])

    blk_id = lane % GROUP
    past = blk_id < t
    g = jnp.where(past, gate, -jnp.inf)
    beaten = jnp.zeros((TILE, LANES), jnp.int32)
    for d in range(1, GROUP):
        lower = jnp.where(blk_id >= d, pltpu.roll(g, d, 1), -jnp.inf)
        upper = jnp.where(blk_id < GROUP - d, pltpu.roll(g, LANES - d, 1), -jnp.inf)
        beaten = beaten + jnp.where(lower >= g, 1, 0) + jnp.where(upper > g, 1, 0)
    keep = (past & (beaten < MOBA_TOPK)) | (blk_id == t)
    maskvals = jnp.where(keep, 0.0, -big_ref[...]).astype(BF16)
    ex_mq = _dot(maskvals, selmq_ref[...])

    store_heads(v_out, N_HEADS, split_tiles(mv), lambda p: v_ones)
    gz_out[:, WIDTH:2 * WIDTH] = silu(mz).astype(BF16)

    is_u_lane = lane % HEAD_DIM == U_COL_MOBA
    store_heads(q_out, N_HEADS, mq_tiles,
                lambda p: jnp.where(is_u_lane, umoba_ref[...], ex_mq[:, p * LANES:(p + 1) * LANES]))
    for raw, (_, o0) in zip(gate_raw, gate_chunks):
        gate_out[:, o0:o0 + WIDTH] = sigmoid(raw + bg_ref[:, o0:o0 + WIDTH]).astype(BF16)

    h_ref[1 - slot] = normalized(xn_ref[...])


def _attn_kernel(flag_ref, first_ref, q_ref, k_ref, v_ref, gz_ref, o_ref, acc_ref, m_ref):
    b, group, i = pl.program_id(0), pl.program_id(1), pl.program_id(2)
    half = ATT_TILE // 2
    start = pl.multiple_of(i * ATT_TILE, ATT_TILE)
    heads = range(ATT_HEADS)

    row = lax.broadcasted_iota(jnp.int32, (half, ATT_TILE), 0)
    col = lax.broadcasted_iota(jnp.int32, (half, ATT_TILE), 1)
    mask_top = (lax.broadcasted_iota(jnp.int32, (half, half), 1)
                <= lax.broadcasted_iota(jnp.int32, (half, half), 0))
    mask_bot = col <= row + half

    def diag_scores(hh):
        kd = k_ref[0, hh, pl.ds(start, ATT_TILE), :]
        s_top = jnp.where(mask_top, _dot_nt(q_ref[0, hh, 0:half, :], kd[0:half]), -jnp.inf)
        s_bot = jnp.where(mask_bot, _dot_nt(q_ref[0, hh, half:ATT_TILE, :], kd), -jnp.inf)
        return s_top, s_bot

    def finish(accs, rows=slice(0, ATT_TILE)):
        lane = lax.broadcasted_iota(jnp.int32, accs[0].shape, 1)
        for pp in range(ATT_HEADS // 2):
            acc_e, acc_o = accs[2 * pp], accs[2 * pp + 1]
            o_e = acc_e * (1.0 / acc_e[:, HEAD_DIM:HEAD_DIM + 1])
            o_o = acc_o * (1.0 / acc_o[:, 0:1])
            o = jnp.where(lane < HEAD_DIM, o_e, o_o)
            cols = slice(pp * LANES, (pp + 1) * LANES)
            o_ref[rows, cols] = (o * gz_ref[rows, cols].astype(F32)).astype(BF16)

    @pl.when(flag_ref[0] == 1)
    def _fixed_stabilizer():
        def add_blocks(j, n_blocks, is_first):
            def block_scores(hh, jj):
                off = pl.multiple_of((j + jj) * ATT_TILE, ATT_TILE)
                return _dot_nt(q_ref[0, hh], k_ref[0, hh, pl.ds(off, ATT_TILE), :])

            ahead = n_blocks == 1
            s_next = block_scores(0, 0) if ahead else None
            for hh in heads:
                pv = None
                for jj in range(n_blocks):
                    off = pl.multiple_of((j + jj) * ATT_TILE, ATT_TILE)
                    s = s_next if ahead else block_scores(hh, jj)
                    if ahead and hh + 1 < ATT_HEADS:
                        s_next = block_scores(hh + 1, 0)
                    d = _dot(jnp.exp2(s).astype(BF16), v_ref[0, hh, pl.ds(off, ATT_TILE), :])
                    pv = d if pv is None else pv + d
                if is_first is True:
                    acc_ref[hh] = pv
                else:
                    acc_ref[hh] = pv + jnp.where(is_first, 0.0, acc_ref[hh])

        @pl.when(i == 0)
        def _():
            def zero(hh, carry):
                acc_ref[hh] = jnp.zeros((ATT_TILE, LANES), F32)
                return carry
            lax.fori_loop(0, ATT_HEADS, zero, 0)

        first = first_ref[(b * pl.num_programs(1) + group) * pl.num_programs(2) + i]
        n_live = i - first
        odd = n_live & 1

        @pl.when(odd == 1)
        def _():
            add_blocks(first, 1, True)

        def pair(p, carry):
            add_blocks(first + odd + 2 * p, 2, (odd == 0) & (p == 0))
            return carry

        lax.fori_loop(0, n_live >> 1, pair, 0)

        def past(hh, rows):
            return jnp.where(n_live > 0, acc_ref[hh, rows], 0.0)

        top_rows, bot_rows = slice(0, half), slice(half, ATT_TILE)
        units = ([(hh, bot_rows, ATT_TILE, mask_bot) for hh in heads]
                 + [(hh, top_rows, half, mask_top) for hh in heads])

        def scores(hh, rows, n_keys, mask):
            s = _dot_nt(q_ref[0, hh, rows, :], k_ref[0, hh, pl.ds(start, n_keys), :])
            return jnp.where(mask, s, -jnp.inf)

        accs, s_next = [], scores(*units[0])
        for n, (hh, rows, n_keys, _) in enumerate(units):
            s = s_next
            if n + 1 < len(units):
                s_next = scores(*units[n + 1])
            accs.append(past(hh, rows) + _dot(jnp.exp2(s).astype(BF16),
                                              v_ref[0, hh, pl.ds(start, n_keys), :]))
        finish(accs[:ATT_HEADS], bot_rows)
        finish(accs[ATT_HEADS:], top_rows)

    @pl.when(flag_ref[0] != 1)
    def _online():
        acc_ref[...] = jnp.zeros_like(acc_ref)
        m_ref[...] = jnp.full_like(m_ref, -jnp.inf)

        def update(s, m_old, acc_old, v):
            m_new = jnp.maximum(m_old, jnp.max(s, axis=-1, keepdims=True))
            pv = _dot(jnp.exp2(s - m_new).astype(BF16), v)
            return m_new, jnp.exp2(m_old - m_new) * acc_old + pv

        def body(j, carry):
            off = pl.multiple_of(j * ATT_TILE, ATT_TILE)

            def head(hh, c):
                s = _dot_nt(q_ref[0, hh], k_ref[0, hh, pl.ds(off, ATT_TILE), :])
                m_ref[hh], acc_ref[hh] = update(s, m_ref[hh], acc_ref[hh],
                                                v_ref[0, hh, pl.ds(off, ATT_TILE), :])
                return c

            return lax.fori_loop(0, ATT_HEADS, head, carry)

        lax.fori_loop(0, i, body, 0)

        def diag_head(hh, c):
            vd = v_ref[0, hh, pl.ds(start, ATT_TILE), :]
            s_top, s_bot = diag_scores(hh)
            _, top = update(s_top, m_ref[hh, 0:half], acc_ref[hh, 0:half], vd[0:half])
            _, bot = update(s_bot, m_ref[hh, half:ATT_TILE], acc_ref[hh, half:ATT_TILE], vd)
            acc_ref[hh, 0:half] = top
            acc_ref[hh, half:ATT_TILE] = bot
            return c

        lax.fori_loop(0, ATT_HEADS, diag_head, 0)
        finish([acc_ref[hh] for hh in heads])


def _out_kernel(y_ref, gate_ref, x_ref, w_ref, o_ref):
    wf_ref, wm_ref = w_ref.at[0:WIDTH], w_ref.at[WIDTH:2 * WIDTH]
    wo_ref = w_ref.at[2 * WIDTH:2 * WIDTH + D_MODEL]
    slabs = [slice(c, c + MXU_COLS) for c in range(0, D_MODEL, MXU_COLS)]
    yf, ym = y_ref[:, 0:WIDTH], y_ref[:, WIDTH:2 * WIDTH]
    merged = []
    for cols in slabs:
        pf = _dot(yf, wf_ref[:, cols])
        pm = _dot(ym, wm_ref[:, cols])
        ga = gate_ref[:, cols].astype(F32)
        gb = gate_ref[:, D_MODEL + cols.start:D_MODEL + cols.stop].astype(F32)
        merged.append((ga * pf + gb * pm).astype(BF16))
    merged = jnp.concatenate(merged, axis=1)
    for cols in slabs:
        o_ref[:, cols] = x_ref[:, cols] + _dot(merged, wo_ref[:, cols])


def _first_live_blocks(cend):
    sub = ATT_TILE // TILE
    c_k = cend[:, sub - 1::sub]
    c_q = jnp.concatenate([cend[:, :1], cend[:, sub - 1:-1:sub]], axis=1)
    n = c_k.shape[1]
    past = jnp.arange(n)[None, :] < jnp.arange(n)[:, None]
    dead = past[None, :, :, None] & (c_q[:, :, None, :] - c_k[:, None, :, :] < EXP2_ZERO_BELOW)
    return jnp.min(jnp.sum(dead.astype(jnp.int32), axis=2), axis=-1)


def _const_spec(shape):
    return pl.BlockSpec(shape, lambda *_: (0,) * len(shape))


def _layer(x2, batch, seq, norm_g, w_in, b_f, b_gate, fox_q_g, fox_k_g, moba_q_g, moba_k_g,
           w_fox, w_moba, w_out):
    rows = batch * seq
    tiles_per_seq = seq // TILE
    n_tiles = rows // TILE
    scale = HEAD_DIM ** -0.5

    w = w_in.astype(BF16)
    w_fl = jnp.pad(w_in[:, W_COLS:], ((0, 0), (0, LANES - N_HEADS))).astype(BF16)

    gains = jnp.stack([fox_q_g, fox_k_g, moba_q_g, moba_k_g]).astype(F32)
    gain_max = jnp.max(jnp.abs(gains), axis=1)
    b_fox = 8.0 * LOG2E * BOUND_MARGIN * gain_max[0] * gain_max[1]
    b_moba = 8.0 * LOG2E * BOUND_MARGIN * gain_max[2] * gain_max[3]
    fixed_ok = jnp.maximum(b_fox, b_moba) <= FIXED_STABILIZER_MAX_BOUND

    def stabilizer(b):
        u = -(b * (1.0 + 2.0 ** -7)).astype(BF16).astype(F32)
        return jnp.where(fixed_ok, u, 0.0)

    big = jnp.exp2(jnp.ceil(jnp.log2(2.1 * b_moba + 160.0)))
    flag = fixed_ok.astype(jnp.int32).reshape(1)

    def row(*pieces):
        r = jnp.concatenate([jnp.ravel(p).astype(F32) for p in pieces])
        return jnp.pad(r, (0, 2 * D_MODEL - r.shape[0]))

    def lanes(v):
        return jnp.full((LANES,), v, F32)

    q_scale = scale * LOG2E
    params = jnp.stack([
        row(norm_g),
        row(b_gate),
        row(jnp.tile(gains * jnp.array([q_scale, 1.0, q_scale, 1.0], F32)[:, None], (1, N_HEADS))),
        row(b_f, jnp.zeros(LANES - N_HEADS), lanes(big), lanes(stabilizer(b_fox)),
            lanes(stabilizer(b_moba))),
    ] + [jnp.zeros(2 * D_MODEL, F32)] * (SUBLANES - 4))

    sel_fq, sel_fk, sel_mq, tri = _routing_constants()
    rc, rsp, rsm = _rope_tables(seq)

    rope_spec = pl.BlockSpec((TILE, LANES), lambda g: (g % tiles_per_seq, 0))
    head_spec = pl.BlockSpec((1, N_AUG_HEADS, TILE, LANES),
                             lambda g: (g // tiles_per_seq, 0, g % tiles_per_seq, 0))
    qkv_shape = jax.ShapeDtypeStruct((batch, N_AUG_HEADS, seq, LANES), BF16)

    q_all, k_all, v_all, gz, gates, cend = pl.pallas_call(
        functools.partial(_proj_kernel, tiles_per_seq=tiles_per_seq),
        grid=(n_tiles,),
        in_specs=[
            _const_spec((TILE, D_MODEL)),
            pl.BlockSpec((TILE, D_MODEL), lambda g: (jnp.minimum(g + 1, n_tiles - 1), 0)),
            _const_spec(w_in.shape), _const_spec((D_MODEL, LANES)),
            _const_spec((SUBLANES, 2 * D_MODEL)),
            rope_spec, rope_spec, rope_spec,
            _const_spec((TILE, TILE)),
            _const_spec((LANES, WIDTH)), _const_spec((LANES, WIDTH)), _const_spec((LANES, WIDTH)),
        ],
        out_specs=[
            head_spec, head_spec, head_spec,
            pl.BlockSpec((TILE, 2 * WIDTH), lambda g: (g, 0)),
            pl.BlockSpec((TILE, 2 * D_MODEL), lambda g: (g, 0)),
            pl.BlockSpec((1, SUBLANES, LANES), lambda g: (g, 0, 0)),
        ],
        out_shape=[
            qkv_shape, qkv_shape, qkv_shape,
            jax.ShapeDtypeStruct((rows, 2 * WIDTH), BF16),
            jax.ShapeDtypeStruct((rows, 2 * D_MODEL), BF16),
            jax.ShapeDtypeStruct((n_tiles, SUBLANES, LANES), F32),
        ],
        scratch_shapes=[pltpu.VMEM((1, LANES), F32),
                        pltpu.VMEM((LANES, WIDTH), F32),
                        pltpu.VMEM((2, TILE, D_MODEL), BF16)],
        compiler_params=pltpu.CompilerParams(
            dimension_semantics=("arbitrary",), vmem_limit_bytes=VMEM_LIMIT),
        name="proj_epilogue",
    )(x2, x2, w, w_fl, params,
      jnp.asarray(rc), jnp.asarray(rsp), jnp.asarray(rsm),
      jnp.asarray(tri, BF16), jnp.asarray(sel_fq, BF16), jnp.asarray(sel_fk, BF16),
      jnp.asarray(sel_mq, BF16))

    n_groups = N_AUG_HEADS // ATT_HEADS
    group_cols = ATT_HEADS // 2 * LANES
    att_tiles = seq // ATT_TILE
    cend = cend[:, 0, :N_HEADS].reshape(batch, tiles_per_seq, N_HEADS) * LOG2E
    first = jnp.stack([_first_live_blocks(cend), jnp.zeros((batch, att_tiles), jnp.int32)], axis=1)
    kv_spec = pl.BlockSpec((1, ATT_HEADS, seq, LANES), lambda b, g, i, *_: (b, g, 0, 0))
    row_spec = pl.BlockSpec((ATT_TILE, group_cols), lambda b, g, i, *_: (b * att_tiles + i, g))
    y = pl.pallas_call(
        _attn_kernel,
        grid_spec=pltpu.PrefetchScalarGridSpec(
            num_scalar_prefetch=2,
            grid=(batch, n_groups, att_tiles),
            in_specs=[
                pl.BlockSpec((1, ATT_HEADS, ATT_TILE, LANES), lambda b, g, i, *_: (b, g, i, 0)),
                kv_spec, kv_spec, row_spec,
            ],
            out_specs=row_spec,
            scratch_shapes=[pltpu.VMEM((ATT_HEADS, ATT_TILE, LANES), F32),
                            pltpu.VMEM((ATT_HEADS, ATT_TILE, 1), F32)],
        ),
        out_shape=jax.ShapeDtypeStruct((rows, 2 * WIDTH), BF16),
        compiler_params=pltpu.CompilerParams(
            dimension_semantics=("arbitrary", "arbitrary", "arbitrary"),
            vmem_limit_bytes=VMEM_LIMIT),
        name="flash_attn",
    )(flag, first.reshape(-1), q_all, k_all, v_all, gz)

    out = pl.pallas_call(
        _out_kernel,
        grid=(rows // OUT_TILE,),
        in_specs=[
            pl.BlockSpec((OUT_TILE, 2 * WIDTH), lambda g: (g, 0)),
            pl.BlockSpec((OUT_TILE, 2 * D_MODEL), lambda g: (g, 0)),
            pl.BlockSpec((OUT_TILE, D_MODEL), lambda g: (g, 0)),
            _const_spec((2 * WIDTH + D_MODEL, D_MODEL)),
        ],
        out_specs=pl.BlockSpec((OUT_TILE, D_MODEL), lambda g: (g, 0)),
        out_shape=jax.ShapeDtypeStruct((rows, D_MODEL), F32),
        compiler_params=pltpu.CompilerParams(
            dimension_semantics=("arbitrary",), vmem_limit_bytes=VMEM_LIMIT),
        name="merge_out",
    )(y, gates, x2, jnp.concatenate([w_fox, w_moba, w_out], axis=0).astype(BF16))
    return out


def kernel(x, norm_g, w_in, b_f, b_gate, fox_q_g, fox_k_g, moba_q_g, moba_k_g, w_fox, w_moba, w_out):
    batch, seq, d_model = x.shape
    assert d_model == D_MODEL and seq % ATT_TILE == 0 and seq // MOBA_BLOCK <= GROUP
    assert (batch * seq) % OUT_TILE == 0
    x2 = x.reshape(batch * seq, D_MODEL)
    for layer in range(norm_g.shape[0]):
        x2 = _layer(x2, batch, seq, norm_g[layer], w_in[layer], b_f[layer], b_gate[layer],
                    fox_q_g[layer], fox_k_g[layer], moba_q_g[layer], moba_k_g[layer],
                    w_fox[layer], w_moba[layer], w_out[layer])
    return x2.reshape(batch, seq, D_MODEL)
```

```python
import functools

import numpy as np
import jax
import jax.numpy as jnp
from jax import lax
from jax.experimental import pallas as pl
from jax.experimental.pallas import tpu as pltpu

D_MODEL = 1024
HEAD_DIM = 64
N_HEADS = 8
WIDTH = N_HEADS * HEAD_DIM
ROPE_DIM = HEAD_DIM // 4
ROPE_HALF = ROPE_DIM // 2
ROPE_THETA = 500000.0
MOBA_BLOCK = 256
MOBA_TOPK = 3
RMS_EPS = 1e-6

LANES = 128
SUBLANES = 8
MXU_COLS = 256
TILE = MOBA_BLOCK
OUT_TILE = 1024
ATT_TILE = 512
ATT_HEADS = N_HEADS
FIXED_STABILIZER_MAX_BOUND = 55.0
EXP2_ZERO_BELOW = -152.0
BOUND_MARGIN = 1.02
GROUP = 16
N_AUG_HEADS = 2 * N_HEADS
U_COL_FOX = 6
U_COL_MOBA = GROUP
LOG2E = 1.4426950408889634

C_FQ, C_FK, C_FV, C_FZ = 0, 512, 1024, 1536
C_MQ, C_MK, C_MV, C_MZ = 2048, 2560, 3072, 3584
C_GA, C_GB = 4096, 5120
W_COLS = 6144

VMEM_LIMIT = 52 * 1024 * 1024

F32 = jnp.float32
BF16 = jnp.bfloat16


def _dot(a, b):
    return jnp.dot(a, b, preferred_element_type=F32)


def _dot_nt(a, b):
    return lax.dot_general(a, b, (((1,), (1,)), ((), ())), preferred_element_type=F32)


def _split3(v):
    hi = v.astype(BF16).astype(F32)
    r = v - hi
    mid = r.astype(BF16).astype(F32)
    lo = r - mid
    return hi, mid, lo


def _extra_base(h):
    return (h // 2) * LANES + (HEAD_DIM if h % 2 == 0 else 0)


def _routing_constants():
    sel_fq = np.zeros((LANES, WIDTH), np.float32)
    sel_fk = np.zeros((LANES, WIDTH), np.float32)
    sel_mq = np.zeros((LANES, WIDTH), np.float32)
    for h in range(N_HEADS):
        base = _extra_base(h)
        for part in range(3):
            sel_fq[part * 8 + h, base + part] = 1.0
            sel_fq[24, base + 3 + part] = 1.0
            sel_fk[24, base + part] = 1.0
            sel_fk[part * 8 + h, base + 3 + part] = -1.0
        sel_fq[25, base + U_COL_FOX] = 1.0
        sel_fk[24, base + U_COL_FOX] = 1.0
        for n in range(GROUP):
            sel_mq[h * GROUP + n, base + n] = 1.0
    tri = np.tril(np.ones((TILE, TILE), np.float32))
    return sel_fq, sel_fk, sel_mq, tri


def _rope_tables(seq):
    inv_freq = ROPE_THETA ** (-np.arange(0, ROPE_HALF, dtype=np.float32) * 2.0 / ROPE_DIM)
    ang = np.arange(seq, dtype=np.float32)[:, None] * inv_freq[None, :].astype(np.float32)
    cos, sin = np.cos(ang).astype(np.float32), np.sin(ang).astype(np.float32)
    rc = np.ones((seq, LANES), np.float32)
    rsp = np.zeros((seq, LANES), np.float32)
    rsm = np.zeros((seq, LANES), np.float32)
    for off in (0, HEAD_DIM):
        rc[:, off:off + ROPE_HALF] = cos
        rc[:, off + ROPE_HALF:off + ROPE_DIM] = cos
        rsm[:, off:off + ROPE_HALF] = -sin
        rsp[:, off + ROPE_HALF:off + ROPE_DIM] = sin
    return rc, rsp, rsm


def _proj_kernel(x0_ref, xn_ref, w_ref, wfl_ref, prm_ref,
                 rc_ref, rsp_ref, rsm_ref, tri_ref, selfq_ref, selfk_ref, selmq_ref,
                 q_out, k_out, v_out, gz_out, gate_out, cend_out,
                 carry_ref, kmt_ref, h_ref, *, tiles_per_seq):
    step = pl.program_id(0)
    t = step % tiles_per_seq
    slot = step % 2

    ng_ref = prm_ref.at[0:1, 0:D_MODEL]
    bg_ref = prm_ref.at[1:2, :]
    gfq_ref, gfk_ref, gmq_ref, gmk_ref = (
        prm_ref.at[2:3, n * WIDTH:(n + 1) * WIDTH] for n in range(4))
    bf_ref, big_ref, ufox_ref, umoba_ref = (
        prm_ref.at[3:4, n * LANES:(n + 1) * LANES] for n in range(4))

    @pl.when(t == 0)
    def _():
        carry_ref[...] = jnp.zeros_like(carry_ref)
        kmt_ref[...] = jnp.zeros_like(kmt_ref)

    lane = lax.broadcasted_iota(jnp.int32, (TILE, LANES), 1)
    low_half = lane < HEAD_DIM

    def normalized(x):
        ms = jnp.mean(x * x, axis=-1, keepdims=True)
        return (x * lax.rsqrt(ms + RMS_EPS) * ng_ref[...]).astype(BF16)

    @pl.when(step == 0)
    def _():
        h_ref[0] = normalized(x0_ref[...])

    def proj(c0, width):
        slabs = [_dot(h_ref[slot], w_ref[:, c:c + min(MXU_COLS, c0 + width - c)])
                 for c in range(c0, c0 + width, MXU_COLS)]
        return slabs[0] if len(slabs) == 1 else jnp.concatenate(slabs, axis=1)

    def head_norm(a, g_ref):
        tiles = []
        for p in range(WIDTH // LANES):
            ap = a[:, p * LANES:(p + 1) * LANES]
            sq = ap * ap
            s_lo = jnp.sum(jnp.where(low_half, sq, 0.0), axis=-1, keepdims=True)
            s_hi = jnp.sum(jnp.where(low_half, 0.0, sq), axis=-1, keepdims=True)
            inv_lo = lax.rsqrt(s_lo * (1.0 / HEAD_DIM) + RMS_EPS)
            inv_hi = lax.rsqrt(s_hi * (1.0 / HEAD_DIM) + RMS_EPS)
            scale = jnp.where(low_half, inv_lo, inv_hi)
            tiles.append(ap * scale * g_ref[:, p * LANES:(p + 1) * LANES])
        return tiles

    def rope(y):
        return (y * rc_ref[...] + pltpu.roll(y, ROPE_HALF, 1) * rsp_ref[...]
                + pltpu.roll(y, LANES - ROPE_HALF, 1) * rsm_ref[...])

    def split_tiles(a):
        return [a[:, p * LANES:(p + 1) * LANES] for p in range(WIDTH // LANES)]

    def store_heads(out_ref, head0, tiles, extras):
        for p, y in enumerate(tiles):
            e = extras(p)
            out_ref[0, head0 + 2 * p] = jnp.where(low_half, y, e).astype(BF16)
            out_ref[0, head0 + 2 * p + 1] = jnp.where(low_half, e, y).astype(BF16)

    def silu(z):
        hz = 0.5 * z
        return hz + hz * jnp.tanh(hz)

    def sigmoid(z):
        return 0.5 * jnp.tanh(0.5 * z) + 0.5

    fl = _dot(h_ref[slot], wfl_ref[...]) + bf_ref[...]
    mk_raw = proj(C_MK, WIDTH)
    mq_raw = proj(C_MQ, WIDTH)
    gate_chunks = [(C_GA + half * WIDTH, half * WIDTH) for half in range(2)]
    gate_chunks += [(C_GB + half * WIDTH, D_MODEL + half * WIDTH) for half in range(2)]
    gate_raw = [proj(c0, WIDTH) for c0, _ in gate_chunks]

    logf = jnp.minimum(fl, 0.0) - jnp.log(1.0 + jnp.exp(-jnp.abs(fl)))
    logf = jnp.where(lane < N_HEADS, logf, 0.0)
    l_hi, l_mid, l_lo = _split3(logf)
    packed = (l_hi + pltpu.roll(l_mid, 8, 1) + pltpu.roll(l_lo, 16, 1)).astype(BF16)
    cum = _dot(tri_ref[...], packed)
    fq = proj(C_FQ, WIDTH)
    fk = proj(C_FK, WIDTH)

    mk_tiles = [rope(y) for y in head_norm(mk_raw, gmk_ref)]
    row_id = lax.broadcasted_iota(jnp.int32, (LANES, LANES), 0)
    lane_sq = lax.broadcasted_iota(jnp.int32, (LANES, LANES), 1)
    for p, kr in enumerate(mk_tiles):
        km = jnp.mean(kr, axis=0, keepdims=True)
        hit = (((row_id == (2 * p) * GROUP + t) & (lane_sq < HEAD_DIM))
               | ((row_id == (2 * p + 1) * GROUP + t) & (lane_sq >= HEAD_DIM)))
        blk = kmt_ref[:, p * LANES:(p + 1) * LANES]
        kmt_ref[:, p * LANES:(p + 1) * LANES] = jnp.where(hit, km, blk)
    mk_ones = jnp.where((lane % HEAD_DIM == t) | (lane % HEAD_DIM == U_COL_MOBA), 1.0, 0.0)
    store_heads(k_out, N_HEADS, mk_tiles, lambda p: mk_ones)

    mq_tiles = [rope(y) for y in head_norm(mq_raw, gmq_ref)]
    q_full = jnp.concatenate(mq_tiles, axis=1)
    q_hi = q_full.astype(BF16)
    q_lo = (q_full - q_hi.astype(F32)).astype(BF16)
    kmt = kmt_ref[...]
    k_hi = kmt.astype(BF16)
    k_lo = (kmt - k_hi.astype(F32)).astype(BF16)
    gate = _dot_nt(q_hi, k_hi) + _dot_nt(q_hi, k_lo) + _dot_nt(q_lo, k_hi)

    c = cum + pltpu.roll(cum, LANES - 8, 1) + pltpu.roll(cum, LANES - 16, 1)
    c = jnp.where(lane < N_HEADS, c, 0.0) + carry_ref[...]
    carry_ref[...] = c[TILE - 1:TILE, :]
    cend_out[0] = jnp.broadcast_to(c[TILE - 1:TILE, :], (SUBLANES, LANES))
    c_hi, c_mid, c_lo = _split3(c * LOG2E)
    cparts = (c_hi + pltpu.roll(c_mid, 8, 1) + pltpu.roll(c_lo, 16, 1)
              + jnp.where(lane == 24, 1.0, 0.0)
              + jnp.where(lane == 25, ufox_ref[...], 0.0)).astype(BF16)
    ex_fq = _dot(cparts, selfq_ref[...])
    ex_fk = _dot(cparts, selfk_ref[...])
    fz = proj(C_FZ, WIDTH)
    mz = proj(C_MZ, WIDTH)
    fv = proj(C_FV, WIDTH)
    mv = proj(C_MV, WIDTH)

    v_ones = jnp.where(lane % HEAD_DIM == 0, 1.0, 0.0)
    store_heads(v_out, 0, split_tiles(fv), lambda p: v_ones)
    gz_out[:, 0:WIDTH] = silu(fz).astype(BF16)

    store_heads(q_out, 0, head_norm(fq, gfq_ref), lambda p: ex_fq[:, p * LANES:(p + 1) * LANES])
    store_heads(k_out, 0, head_norm(fk, gfk_ref), lambda p: ex_fk[:, p * LANES:(p + 1) * LANES])

    blk_id = lane % GROUP
    past = blk_id < t
    g = jnp.where(past, gate, -jnp.inf)
    beaten = jnp.zeros((TILE, LANES), jnp.int32)
    for d in range(1, GROUP):
        lower = jnp.where(blk_id >= d, pltpu.roll(g, d, 1), -jnp.inf)
        upper = jnp.where(blk_id < GROUP - d, pltpu.roll(g, LANES - d, 1), -jnp.inf)
        beaten = beaten + jnp.where(lower >= g, 1, 0) + jnp.where(upper > g, 1, 0)
    keep = (past & (beaten < MOBA_TOPK)) | (blk_id == t)
    maskvals = jnp.where(keep, 0.0, -big_ref[...]).astype(BF16)
    ex_mq = _dot(maskvals, selmq_ref[...])

    store_heads(v_out, N_HEADS, split_tiles(mv), lambda p: v_ones)
    gz_out[:, WIDTH:2 * WIDTH] = silu(mz).astype(BF16)

    is_u_lane = lane % HEAD_DIM == U_COL_MOBA
    store_heads(q_out, N_HEADS, mq_tiles,
                lambda p: jnp.where(is_u_lane, umoba_ref[...], ex_mq[:, p * LANES:(p + 1) * LANES]))
    for raw, (_, o0) in zip(gate_raw, gate_chunks):
        gate_out[:, o0:o0 + WIDTH] = sigmoid(raw + bg_ref[:, o0:o0 + WIDTH]).astype(BF16)

    h_ref[1 - slot] = normalized(xn_ref[...])


def _attn_kernel(flag_ref, first_ref, q_ref, k_hbm, v_hbm, gz_ref, o_ref,
                 acc_ref, m_ref, k_buf, v_buf, kv_sem, *, att_tiles):
    b, group, i = pl.program_id(0), pl.program_id(1), pl.program_id(2)
    half = ATT_TILE // 2
    start = pl.multiple_of(i * ATT_TILE, ATT_TILE)
    heads = range(ATT_HEADS)

    n_groups = pl.num_programs(1)
    this = b * n_groups + group
    slot = this % 2

    def kv_copy(which, chunk, into_slot, which_buf):
        src, dst = ((k_hbm, k_buf), (v_hbm, v_buf))[which_buf]
        keys = pl.ds(pl.multiple_of(chunk * ATT_TILE, ATT_TILE), ATT_TILE)
        return pltpu.make_async_copy(
            src.at[which // n_groups, pl.ds((which % n_groups) * ATT_HEADS, ATT_HEADS), keys, :],
            dst.at[into_slot, :, keys, :], kv_sem.at[into_slot, chunk, which_buf])

    def start_group(which, into_slot):
        for chunk in range(att_tiles):
            for which_buf in range(2):
                kv_copy(which, chunk, into_slot, which_buf).start()

    @pl.when((this == 0) & (i == 0))
    def _():
        start_group(0, 0)

    @pl.when((i == 0) & (this + 1 < pl.num_programs(0) * n_groups))
    def _():
        start_group(this + 1, 1 - slot)

    for which_buf in range(2):
        kv_copy(this, i, slot, which_buf).wait()
    k_ref, v_ref = k_buf.at[slot], v_buf.at[slot]

    row = lax.broadcasted_iota(jnp.int32, (half, ATT_TILE), 0)
    col = lax.broadcasted_iota(jnp.int32, (half, ATT_TILE), 1)
    mask_top = (lax.broadcasted_iota(jnp.int32, (half, half), 1)
                <= lax.broadcasted_iota(jnp.int32, (half, half), 0))
    mask_bot = col <= row + half

    def diag_scores(hh):
        kd = k_ref[hh, pl.ds(start, ATT_TILE), :]
        s_top = jnp.where(mask_top, _dot_nt(q_ref[0, hh, 0:half, :], kd[0:half]), -jnp.inf)
        s_bot = jnp.where(mask_bot, _dot_nt(q_ref[0, hh, half:ATT_TILE, :], kd), -jnp.inf)
        return s_top, s_bot

    def finish(accs, rows=slice(0, ATT_TILE)):
        lane = lax.broadcasted_iota(jnp.int32, accs[0].shape, 1)
        for pp in range(ATT_HEADS // 2):
            acc_e, acc_o = accs[2 * pp], accs[2 * pp + 1]
            o_e = acc_e * (1.0 / acc_e[:, HEAD_DIM:HEAD_DIM + 1])
            o_o = acc_o * (1.0 / acc_o[:, 0:1])
            o = jnp.where(lane < HEAD_DIM, o_e, o_o)
            cols = slice(pp * LANES, (pp + 1) * LANES)
            o_ref[rows, cols] = (o * gz_ref[rows, cols].astype(F32)).astype(BF16)

    @pl.when(flag_ref[0] == 1)
    def _fixed_stabilizer():
        def add_blocks(j, n_blocks, is_first):
            def block_scores(hh, jj):
                off = pl.multiple_of((j + jj) * ATT_TILE, ATT_TILE)
                return _dot_nt(q_ref[0, hh], k_ref[hh, pl.ds(off, ATT_TILE), :])

            ahead = n_blocks == 1
            s_next = block_scores(0, 0) if ahead else None
            for hh in heads:
                pv = None
                for jj in range(n_blocks):
                    off = pl.multiple_of((j + jj) * ATT_TILE, ATT_TILE)
                    s = s_next if ahead else block_scores(hh, jj)
                    if ahead and hh + 1 < ATT_HEADS:
                        s_next = block_scores(hh + 1, 0)
                    d = _dot(jnp.exp2(s).astype(BF16), v_ref[hh, pl.ds(off, ATT_TILE), :])
                    pv = d if pv is None else pv + d
                if is_first is True:
                    acc_ref[hh] = pv
                else:
                    acc_ref[hh] = pv + jnp.where(is_first, 0.0, acc_ref[hh])

        @pl.when(i == 0)
        def _():
            def zero(hh, carry):
                acc_ref[hh] = jnp.zeros((ATT_TILE, LANES), F32)
                return carry
            lax.fori_loop(0, ATT_HEADS, zero, 0)

        first = first_ref[(b * pl.num_programs(1) + group) * pl.num_programs(2) + i]
        n_live = i - first
        odd = n_live & 1

        @pl.when(odd == 1)
        def _():
            add_blocks(first, 1, True)

        def pair(p, carry):
            add_blocks(first + odd + 2 * p, 2, (odd == 0) & (p == 0))
            return carry

        lax.fori_loop(0, n_live >> 1, pair, 0)

        def past(hh, rows):
            return jnp.where(n_live > 0, acc_ref[hh, rows], 0.0)

        top_rows, bot_rows = slice(0, half), slice(half, ATT_TILE)
        units = ([(hh, bot_rows, ATT_TILE, mask_bot) for hh in heads]
                 + [(hh, top_rows, half, mask_top) for hh in heads])

        def scores(hh, rows, n_keys, mask):
            s = _dot_nt(q_ref[0, hh, rows, :], k_ref[hh, pl.ds(start, n_keys), :])
            return jnp.where(mask, s, -jnp.inf)

        accs, s_next = [], scores(*units[0])
        for n, (hh, rows, n_keys, _) in enumerate(units):
            s = s_next
            if n + 1 < len(units):
                s_next = scores(*units[n + 1])
            accs.append(past(hh, rows) + _dot(jnp.exp2(s).astype(BF16),
                                              v_ref[hh, pl.ds(start, n_keys), :]))
        finish(accs[:ATT_HEADS], bot_rows)
        finish(accs[ATT_HEADS:], top_rows)

    @pl.when(flag_ref[0] != 1)
    def _online():
        acc_ref[...] = jnp.zeros_like(acc_ref)
        m_ref[...] = jnp.full_like(m_ref, -jnp.inf)

        def update(s, m_old, acc_old, v):
            m_new = jnp.maximum(m_old, jnp.max(s, axis=-1, keepdims=True))
            pv = _dot(jnp.exp2(s - m_new).astype(BF16), v)
            return m_new, jnp.exp2(m_old - m_new) * acc_old + pv

        def body(j, carry):
            off = pl.multiple_of(j * ATT_TILE, ATT_TILE)

            def head(hh, c):
                s = _dot_nt(q_ref[0, hh], k_ref[hh, pl.ds(off, ATT_TILE), :])
                m_ref[hh], acc_ref[hh] = update(s, m_ref[hh], acc_ref[hh],
                                                v_ref[hh, pl.ds(off, ATT_TILE), :])
                return c

            return lax.fori_loop(0, ATT_HEADS, head, carry)

        lax.fori_loop(0, i, body, 0)

        def diag_head(hh, c):
            vd = v_ref[hh, pl.ds(start, ATT_TILE), :]
            s_top, s_bot = diag_scores(hh)
            _, top = update(s_top, m_ref[hh, 0:half], acc_ref[hh, 0:half], vd[0:half])
            _, bot = update(s_bot, m_ref[hh, half:ATT_TILE], acc_ref[hh, half:ATT_TILE], vd)
            acc_ref[hh, 0:half] = top
            acc_ref[hh, half:ATT_TILE] = bot
            return c

        lax.fori_loop(0, ATT_HEADS, diag_head, 0)
        finish([acc_ref[hh] for hh in heads])


def _out_kernel(y_ref, gate_ref, x_ref, w_ref, o_ref):
    wf_ref, wm_ref = w_ref.at[0:WIDTH], w_ref.at[WIDTH:2 * WIDTH]
    wo_ref = w_ref.at[2 * WIDTH:2 * WIDTH + D_MODEL]
    slabs = [slice(c, c + MXU_COLS) for c in range(0, D_MODEL, MXU_COLS)]
    yf, ym = y_ref[:, 0:WIDTH], y_ref[:, WIDTH:2 * WIDTH]
    merged = []
    for cols in slabs:
        pf = _dot(yf, wf_ref[:, cols])
        pm = _dot(ym, wm_ref[:, cols])
        ga = gate_ref[:, cols].astype(F32)
        gb = gate_ref[:, D_MODEL + cols.start:D_MODEL + cols.stop].astype(F32)
        merged.append((ga * pf + gb * pm).astype(BF16))
    merged = jnp.concatenate(merged, axis=1)
    for cols in slabs:
        o_ref[:, cols] = x_ref[:, cols] + _dot(merged, wo_ref[:, cols])


def _first_live_blocks(cend):
    sub = ATT_TILE // TILE
    c_k = cend[:, sub - 1::sub]
    c_q = jnp.concatenate([cend[:, :1], cend[:, sub - 1:-1:sub]], axis=1)
    n = c_k.shape[1]
    past = jnp.arange(n)[None, :] < jnp.arange(n)[:, None]
    dead = past[None, :, :, None] & (c_q[:, :, None, :] - c_k[:, None, :, :] < EXP2_ZERO_BELOW)
    return jnp.min(jnp.sum(dead.astype(jnp.int32), axis=2), axis=-1)


def _const_spec(shape):
    return pl.BlockSpec(shape, lambda *_: (0,) * len(shape))


def _layer(x2, batch, seq, norm_g, w_in, b_f, b_gate, fox_q_g, fox_k_g, moba_q_g, moba_k_g,
           w_fox, w_moba, w_out):
    rows = batch * seq
    tiles_per_seq = seq // TILE
    n_tiles = rows // TILE
    scale = HEAD_DIM ** -0.5

    w = w_in.astype(BF16)
    w_fl = jnp.pad(w_in[:, W_COLS:], ((0, 0), (0, LANES - N_HEADS))).astype(BF16)

    gains = jnp.stack([fox_q_g, fox_k_g, moba_q_g, moba_k_g]).astype(F32)
    gain_max = jnp.max(jnp.abs(gains), axis=1)
    b_fox = 8.0 * LOG2E * BOUND_MARGIN * gain_max[0] * gain_max[1]
    b_moba = 8.0 * LOG2E * BOUND_MARGIN * gain_max[2] * gain_max[3]
    fixed_ok = jnp.maximum(b_fox, b_moba) <= FIXED_STABILIZER_MAX_BOUND

    def stabilizer(b):
        u = -(b * (1.0 + 2.0 ** -7)).astype(BF16).astype(F32)
        return jnp.where(fixed_ok, u, 0.0)

    big = jnp.exp2(jnp.ceil(jnp.log2(2.1 * b_moba + 160.0)))
    flag = fixed_ok.astype(jnp.int32).reshape(1)

    def row(*pieces):
        r = jnp.concatenate([jnp.ravel(p).astype(F32) for p in pieces])
        return jnp.pad(r, (0, 2 * D_MODEL - r.shape[0]))

    def lanes(v):
        return jnp.full((LANES,), v, F32)

    q_scale = scale * LOG2E
    params = jnp.stack([
        row(norm_g),
        row(b_gate),
        row(jnp.tile(gains * jnp.array([q_scale, 1.0, q_scale, 1.0], F32)[:, None], (1, N_HEADS))),
        row(b_f, jnp.zeros(LANES - N_HEADS), lanes(big), lanes(stabilizer(b_fox)),
            lanes(stabilizer(b_moba))),
    ] + [jnp.zeros(2 * D_MODEL, F32)] * (SUBLANES - 4))

    sel_fq, sel_fk, sel_mq, tri = _routing_constants()
    rc, rsp, rsm = _rope_tables(seq)

    rope_spec = pl.BlockSpec((TILE, LANES), lambda g: (g % tiles_per_seq, 0))
    head_spec = pl.BlockSpec((1, N_AUG_HEADS, TILE, LANES),
                             lambda g: (g // tiles_per_seq, 0, g % tiles_per_seq, 0))
    qkv_shape = jax.ShapeDtypeStruct((batch, N_AUG_HEADS, seq, LANES), BF16)

    q_all, k_all, v_all, gz, gates, cend = pl.pallas_call(
        functools.partial(_proj_kernel, tiles_per_seq=tiles_per_seq),
        grid=(n_tiles,),
        in_specs=[
            _const_spec((TILE, D_MODEL)),
            pl.BlockSpec((TILE, D_MODEL), lambda g: (jnp.minimum(g + 1, n_tiles - 1), 0)),
            _const_spec(w_in.shape), _const_spec((D_MODEL, LANES)),
            _const_spec((SUBLANES, 2 * D_MODEL)),
            rope_spec, rope_spec, rope_spec,
            _const_spec((TILE, TILE)),
            _const_spec((LANES, WIDTH)), _const_spec((LANES, WIDTH)), _const_spec((LANES, WIDTH)),
        ],
        out_specs=[
            head_spec, head_spec, head_spec,
            pl.BlockSpec((TILE, 2 * WIDTH), lambda g: (g, 0)),
            pl.BlockSpec((TILE, 2 * D_MODEL), lambda g: (g, 0)),
            pl.BlockSpec((1, SUBLANES, LANES), lambda g: (g, 0, 0)),
        ],
        out_shape=[
            qkv_shape, qkv_shape, qkv_shape,
            jax.ShapeDtypeStruct((rows, 2 * WIDTH), BF16),
            jax.ShapeDtypeStruct((rows, 2 * D_MODEL), BF16),
            jax.ShapeDtypeStruct((n_tiles, SUBLANES, LANES), F32),
        ],
        scratch_shapes=[pltpu.VMEM((1, LANES), F32),
                        pltpu.VMEM((LANES, WIDTH), F32),
                        pltpu.VMEM((2, TILE, D_MODEL), BF16)],
        compiler_params=pltpu.CompilerParams(
            dimension_semantics=("arbitrary",), vmem_limit_bytes=VMEM_LIMIT),
        name="proj_epilogue",
    )(x2, x2, w, w_fl, params,
      jnp.asarray(rc), jnp.asarray(rsp), jnp.asarray(rsm),
      jnp.asarray(tri, BF16), jnp.asarray(sel_fq, BF16), jnp.asarray(sel_fk, BF16),
      jnp.asarray(sel_mq, BF16))

    n_groups = N_AUG_HEADS // ATT_HEADS
    group_cols = ATT_HEADS // 2 * LANES
    att_tiles = seq // ATT_TILE
    cend = cend[:, 0, :N_HEADS].reshape(batch, tiles_per_seq, N_HEADS) * LOG2E
    first = jnp.stack([_first_live_blocks(cend), jnp.zeros((batch, att_tiles), jnp.int32)], axis=1)
    kv_spec = pl.BlockSpec(memory_space=pl.ANY)
    kv_slots = pltpu.VMEM((2, ATT_HEADS, seq, LANES), BF16)
    row_spec = pl.BlockSpec((ATT_TILE, group_cols), lambda b, g, i, *_: (b * att_tiles + i, g))
    y = pl.pallas_call(
        functools.partial(_attn_kernel, att_tiles=att_tiles),
        grid_spec=pltpu.PrefetchScalarGridSpec(
            num_scalar_prefetch=2,
            grid=(batch, n_groups, att_tiles),
            in_specs=[
                pl.BlockSpec((1, ATT_HEADS, ATT_TILE, LANES), lambda b, g, i, *_: (b, g, i, 0)),
                kv_spec, kv_spec, row_spec,
            ],
            out_specs=row_spec,
            scratch_shapes=[pltpu.VMEM((ATT_HEADS, ATT_TILE, LANES), F32),
                            pltpu.VMEM((ATT_HEADS, ATT_TILE, 1), F32),
                            kv_slots, kv_slots,
                            pltpu.SemaphoreType.DMA((2, att_tiles, 2))],
        ),
        out_shape=jax.ShapeDtypeStruct((rows, 2 * WIDTH), BF16),
        compiler_params=pltpu.CompilerParams(
            dimension_semantics=("arbitrary", "arbitrary", "arbitrary"),
            vmem_limit_bytes=VMEM_LIMIT),
        name="flash_attn",
    )(flag, first.reshape(-1), q_all, k_all, v_all, gz)

    out = pl.pallas_call(
        _out_kernel,
        grid=(rows // OUT_TILE,),
        in_specs=[
            pl.BlockSpec((OUT_TILE, 2 * WIDTH), lambda g: (g, 0)),
            pl.BlockSpec((OUT_TILE, 2 * D_MODEL), lambda g: (g, 0)),
            pl.BlockSpec((OUT_TILE, D_MODEL), lambda g: (g, 0)),
            _const_spec((2 * WIDTH + D_MODEL, D_MODEL)),
        ],
        out_specs=pl.BlockSpec((OUT_TILE, D_MODEL), lambda g: (g, 0)),
        out_shape=jax.ShapeDtypeStruct((rows, D_MODEL), F32),
        compiler_params=pltpu.CompilerParams(
            dimension_semantics=("arbitrary",), vmem_limit_bytes=VMEM_LIMIT),
        name="merge_out",
    )(y, gates, x2, jnp.concatenate([w_fox, w_moba, w_out], axis=0).astype(BF16))
    return out


def kernel(x, norm_g, w_in, b_f, b_gate, fox_q_g, fox_k_g, moba_q_g, moba_k_g, w_fox, w_moba, w_out):
    batch, seq, d_model = x.shape
    assert d_model == D_MODEL and seq % ATT_TILE == 0 and seq // MOBA_BLOCK <= GROUP
    assert (batch * seq) % OUT_TILE == 0
    x2 = x.reshape(batch * seq, D_MODEL)
    for layer in range(norm_g.shape[0]):
        x2 = _layer(x2, batch, seq, norm_g[layer], w_in[layer], b_f[layer], b_gate[layer],
                    fox_q_g[layer], fox_k_g[layer], moba_q_g[layer], moba_k_g[layer],
                    w_fox[layer], w_moba[layer], w_out[layer])
    return x2.reshape(batch, seq, D_MODEL)
```

```python
import functools

import numpy as np
import jax
import jax.numpy as jnp
from jax import lax
from jax.experimental import pallas as pl
from jax.experimental.pallas import tpu as pltpu

D_MODEL = 1024
HEAD_DIM = 64
N_HEADS = 8
WIDTH = N_HEADS * HEAD_DIM
ROPE_DIM = HEAD_DIM // 4
ROPE_HALF = ROPE_DIM // 2
ROPE_THETA = 500000.0
MOBA_BLOCK = 256
MOBA_TOPK = 3
RMS_EPS = 1e-6

LANES = 128
SUBLANES = 8
MXU_COLS = 256
TILE = MOBA_BLOCK
OUT_TILE = 1024
ATT_TILE = 512
ATT_HEADS = N_HEADS
FIXED_STABILIZER_MAX_BOUND = 55.0
EXP2_ZERO_BELOW = -152.0
BOUND_MARGIN = 1.02
GROUP = 16
N_AUG_HEADS = 2 * N_HEADS
U_COL_FOX = 6
U_COL_MOBA = GROUP
LOG2E = 1.4426950408889634

C_FQ, C_FK, C_FV, C_FZ = 0, 512, 1024, 1536
C_MQ, C_MK, C_MV, C_MZ = 2048, 2560, 3072, 3584
C_GA, C_GB = 4096, 5120
W_COLS = 6144

VMEM_LIMIT = 52 * 1024 * 1024

F32 = jnp.float32
BF16 = jnp.bfloat16


def _dot(a, b):
    return jnp.dot(a, b, preferred_element_type=F32)


def _dot_nt(a, b):
    return lax.dot_general(a, b, (((1,), (1,)), ((), ())), preferred_element_type=F32)


def _split3(v):
    hi = v.astype(BF16).astype(F32)
    r = v - hi
    mid = r.astype(BF16).astype(F32)
    lo = r - mid
    return hi, mid, lo


def _extra_base(h):
    return (h // 2) * LANES + (HEAD_DIM if h % 2 == 0 else 0)


def _routing_constants():
    sel_fq = np.zeros((LANES, WIDTH), np.float32)
    sel_fk = np.zeros((LANES, WIDTH), np.float32)
    sel_mq = np.zeros((LANES, WIDTH), np.float32)
    for h in range(N_HEADS):
        base = _extra_base(h)
        for part in range(3):
            sel_fq[part * 8 + h, base + part] = 1.0
            sel_fq[24, base + 3 + part] = 1.0
            sel_fk[24, base + part] = 1.0
            sel_fk[part * 8 + h, base + 3 + part] = -1.0
        sel_fq[25, base + U_COL_FOX] = 1.0
        sel_fk[24, base + U_COL_FOX] = 1.0
        for n in range(GROUP):
            sel_mq[h * GROUP + n, base + n] = 1.0
    tri = np.tril(np.ones((TILE, TILE), np.float32))
    return sel_fq, sel_fk, sel_mq, tri


def _rope_tables(seq):
    inv_freq = ROPE_THETA ** (-np.arange(0, ROPE_HALF, dtype=np.float32) * 2.0 / ROPE_DIM)
    ang = np.arange(seq, dtype=np.float32)[:, None] * inv_freq[None, :].astype(np.float32)
    cos, sin = np.cos(ang).astype(np.float32), np.sin(ang).astype(np.float32)
    rc = np.ones((seq, LANES), np.float32)
    rsp = np.zeros((seq, LANES), np.float32)
    rsm = np.zeros((seq, LANES), np.float32)
    for off in (0, HEAD_DIM):
        rc[:, off:off + ROPE_HALF] = cos
        rc[:, off + ROPE_HALF:off + ROPE_DIM] = cos
        rsm[:, off:off + ROPE_HALF] = -sin
        rsp[:, off + ROPE_HALF:off + ROPE_DIM] = sin
    return rc, rsp, rsm


def _proj_kernel(x0_ref, xn_ref, w_ref, wfl_ref, prm_ref,
                 rc_ref, rsp_ref, rsm_ref, tri_ref, selfq_ref, selfk_ref, selmq_ref,
                 q_out, k_out, v_out, gz_out, gate_out, cend_out,
                 carry_ref, kmt_ref, h_ref, *, tiles_per_seq):
    step = pl.program_id(0)
    t = step % tiles_per_seq
    slot = step % 2

    ng_ref = prm_ref.at[0:1, 0:D_MODEL]
    bg_ref = prm_ref.at[1:2, :]
    gfq_ref, gfk_ref, gmq_ref, gmk_ref = (
        prm_ref.at[2:3, n * WIDTH:(n + 1) * WIDTH] for n in range(4))
    bf_ref, big_ref, ufox_ref, umoba_ref = (
        prm_ref.at[3:4, n * LANES:(n + 1) * LANES] for n in range(4))

    @pl.when(t == 0)
    def _():
        carry_ref[...] = jnp.zeros_like(carry_ref)
        kmt_ref[...] = jnp.zeros_like(kmt_ref)

    lane = lax.broadcasted_iota(jnp.int32, (TILE, LANES), 1)
    low_half = lane < HEAD_DIM

    def normalized(x):
        ms = jnp.mean(x * x, axis=-1, keepdims=True)
        return (x * lax.rsqrt(ms + RMS_EPS) * ng_ref[...]).astype(BF16)

    @pl.when(step == 0)
    def _():
        h_ref[0] = normalized(x0_ref[...])

    def proj(c0, width):
        slabs = [_dot(h_ref[slot], w_ref[:, c:c + min(MXU_COLS, c0 + width - c)])
                 for c in range(c0, c0 + width, MXU_COLS)]
        return slabs[0] if len(slabs) == 1 else jnp.concatenate(slabs, axis=1)

    def head_norm(a, g_ref):
        tiles = []
        for p in range(WIDTH // LANES):
            ap = a[:, p * LANES:(p + 1) * LANES]
            sq = ap * ap
            s_lo = jnp.sum(jnp.where(low_half, sq, 0.0), axis=-1, keepdims=True)
            s_hi = jnp.sum(jnp.where(low_half, 0.0, sq), axis=-1, keepdims=True)
            inv_lo = lax.rsqrt(s_lo * (1.0 / HEAD_DIM) + RMS_EPS)
            inv_hi = lax.rsqrt(s_hi * (1.0 / HEAD_DIM) + RMS_EPS)
            scale = jnp.where(low_half, inv_lo, inv_hi)
            tiles.append(ap * scale * g_ref[:, p * LANES:(p + 1) * LANES])
        return tiles

    def rope(y):
        return (y * rc_ref[...] + pltpu.roll(y, ROPE_HALF, 1) * rsp_ref[...]
                + pltpu.roll(y, LANES - ROPE_HALF, 1) * rsm_ref[...])

    def split_tiles(a):
        return [a[:, p * LANES:(p + 1) * LANES] for p in range(WIDTH // LANES)]

    def store_heads(out_ref, head0, tiles, extras):
        for p, y in enumerate(tiles):
            e = extras(p)
            out_ref[0, head0 + 2 * p] = jnp.where(low_half, y, e).astype(BF16)
            out_ref[0, head0 + 2 * p + 1] = jnp.where(low_half, e, y).astype(BF16)

    def silu(z):
        hz = 0.5 * z
        return hz + hz * jnp.tanh(hz)

    def sigmoid(z):
        return 0.5 * jnp.tanh(0.5 * z) + 0.5

    fl = _dot(h_ref[slot], wfl_ref[...]) + bf_ref[...]
    mk_raw = proj(C_MK, WIDTH)
    mq_raw = proj(C_MQ, WIDTH)
    gate_chunks = [(C_GA + half * WIDTH, half * WIDTH) for half in range(2)]
    gate_chunks += [(C_GB + half * WIDTH, D_MODEL + half * WIDTH) for half in range(2)]
    gate_raw = [proj(c0, WIDTH) for c0, _ in gate_chunks]

    logf = jnp.minimum(fl, 0.0) - jnp.log(1.0 + jnp.exp(-jnp.abs(fl)))
    logf = jnp.where(lane < N_HEADS, logf, 0.0)
    l_hi, l_mid, l_lo = _split3(logf)
    packed = (l_hi + pltpu.roll(l_mid, 8, 1) + pltpu.roll(l_lo, 16, 1)).astype(BF16)
    cum = _dot(tri_ref[...], packed)
    fq = proj(C_FQ, WIDTH)
    fk = proj(C_FK, WIDTH)

    mk_tiles = [rope(y) for y in head_norm(mk_raw, gmk_ref)]
    row_id = lax.broadcasted_iota(jnp.int32, (LANES, LANES), 0)
    lane_sq = lax.broadcasted_iota(jnp.int32, (LANES, LANES), 1)
    for p, kr in enumerate(mk_tiles):
        km = jnp.mean(kr, axis=0, keepdims=True)
        hit = (((row_id == (2 * p) * GROUP + t) & (lane_sq < HEAD_DIM))
               | ((row_id == (2 * p + 1) * GROUP + t) & (lane_sq >= HEAD_DIM)))
        blk = kmt_ref[:, p * LANES:(p + 1) * LANES]
        kmt_ref[:, p * LANES:(p + 1) * LANES] = jnp.where(hit, km, blk)
    mk_ones = jnp.where((lane % HEAD_DIM == t) | (lane % HEAD_DIM == U_COL_MOBA), 1.0, 0.0)
    store_heads(k_out, N_HEADS, mk_tiles, lambda p: mk_ones)

    mq_tiles = [rope(y) for y in head_norm(mq_raw, gmq_ref)]
    q_full = jnp.concatenate(mq_tiles, axis=1)
    q_hi = q_full.astype(BF16)
    q_lo = (q_full - q_hi.astype(F32)).astype(BF16)
    kmt = kmt_ref[...]
    k_hi = kmt.astype(BF16)
    k_lo = (kmt - k_hi.astype(F32)).astype(BF16)
    gate = _dot_nt(q_hi, k_hi) + _dot_nt(q_hi, k_lo) + _dot_nt(q_lo, k_hi)

    c = cum + pltpu.roll(cum, LANES - 8, 1) + pltpu.roll(cum, LANES - 16, 1)
    c = jnp.where(lane < N_HEADS, c, 0.0) + carry_ref[...]
    carry_ref[...] = c[TILE - 1:TILE, :]
    cend_out[0] = jnp.broadcast_to(c[TILE - 1:TILE, :], (SUBLANES, LANES))
    c_hi, c_mid, c_lo = _split3(c * LOG2E)
    cparts = (c_hi + pltpu.roll(c_mid, 8, 1) + pltpu.roll(c_lo, 16, 1)
              + jnp.where(lane == 24, 1.0, 0.0)
              + jnp.where(lane == 25, ufox_ref[...], 0.0)).astype(BF16)
    ex_fq = _dot(cparts, selfq_ref[...])
    ex_fk = _dot(cparts, selfk_ref[...])
    fz = proj(C_FZ, WIDTH)
    mz = proj(C_MZ, WIDTH)
    fv = proj(C_FV, WIDTH)
    mv = proj(C_MV, WIDTH)

    v_ones = jnp.where(lane % HEAD_DIM == 0, 1.0, 0.0)
    store_heads(v_out, 0, split_tiles(fv), lambda p: v_ones)
    gz_out[:, 0:WIDTH] = silu(fz).astype(BF16)

    store_heads(q_out, 0, head_norm(fq, gfq_ref), lambda p: ex_fq[:, p * LANES:(p + 1) * LANES])
    store_heads(k_out, 0, head_norm(fk, gfk_ref), lambda p: ex_fk[:, p * LANES:(p + 1) * LANES])

    blk_id = lane % GROUP
    past = blk_id < t
    g = jnp.where(past, gate, -jnp.inf)
    beaten = jnp.zeros((TILE, LANES), jnp.int32)
    for d in range(1, GROUP):
        lower = jnp.where(blk_id >= d, pltpu.roll(g, d, 1), -jnp.inf)
        upper = jnp.where(blk_id < GROUP - d, pltpu.roll(g, LANES - d, 1), -jnp.inf)
        beaten = beaten + jnp.where(lower >= g, 1, 0) + jnp.where(upper > g, 1, 0)
    keep = (past & (beaten < MOBA_TOPK)) | (blk_id == t)
    maskvals = jnp.where(keep, 0.0, -big_ref[...]).astype(BF16)
    ex_mq = _dot(maskvals, selmq_ref[...])

    store_heads(v_out, N_HEADS, split_tiles(mv), lambda p: v_ones)
    gz_out[:, WIDTH:2 * WIDTH] = silu(mz).astype(BF16)

    is_u_lane = lane % HEAD_DIM == U_COL_MOBA
    store_heads(q_out, N_HEADS, mq_tiles,
                lambda p: jnp.where(is_u_lane, umoba_ref[...], ex_mq[:, p * LANES:(p + 1) * LANES]))
    for raw, (_, o0) in zip(gate_raw, gate_chunks):
        gate_out[:, o0:o0 + WIDTH] = sigmoid(raw + bg_ref[:, o0:o0 + WIDTH]).astype(BF16)

    h_ref[1 - slot] = normalized(xn_ref[...])


def _attn_kernel(flag_ref, first_ref, q_ref, k_hbm, v_hbm, gz_ref, o_ref,
                 acc_ref, m_ref, k_buf, v_buf, kv_sem, *, att_tiles):
    b, group, i = pl.program_id(0), pl.program_id(1), pl.program_id(2)
    half = ATT_TILE // 2
    start = pl.multiple_of(i * ATT_TILE, ATT_TILE)
    heads = range(ATT_HEADS)

    n_groups = pl.num_programs(1)
    this = b * n_groups + group
    slot = this % 2

    def kv_copy(which, chunk, into_slot, which_buf):
        src, dst = ((k_hbm, k_buf), (v_hbm, v_buf))[which_buf]
        keys = pl.ds(pl.multiple_of(chunk * ATT_TILE, ATT_TILE), ATT_TILE)
        return pltpu.make_async_copy(
            src.at[which // n_groups, pl.ds((which % n_groups) * ATT_HEADS, ATT_HEADS), keys, :],
            dst.at[into_slot, :, keys, :], kv_sem.at[into_slot, chunk, which_buf])

    def start_group(which, into_slot, priority):
        for chunk in range(att_tiles):
            for which_buf in range(2):
                kv_copy(which, chunk, into_slot, which_buf).start(priority=priority)

    @pl.when((this == 0) & (i == 0))
    def _():
        start_group(0, 0, 0)

    @pl.when((i == att_tiles // 2) & (this + 1 < pl.num_programs(0) * n_groups))
    def _():
        start_group(this + 1, 1 - slot, 1)

    for which_buf in range(2):
        kv_copy(this, i, slot, which_buf).wait()
    k_ref, v_ref = k_buf.at[slot], v_buf.at[slot]

    row = lax.broadcasted_iota(jnp.int32, (half, ATT_TILE), 0)
    col = lax.broadcasted_iota(jnp.int32, (half, ATT_TILE), 1)
    mask_top = (lax.broadcasted_iota(jnp.int32, (half, half), 1)
                <= lax.broadcasted_iota(jnp.int32, (half, half), 0))
    mask_bot = col <= row + half

    def diag_scores(hh):
        kd = k_ref[hh, pl.ds(start, ATT_TILE), :]
        s_top = jnp.where(mask_top, _dot_nt(q_ref[0, hh, 0:half, :], kd[0:half]), -jnp.inf)
        s_bot = jnp.where(mask_bot, _dot_nt(q_ref[0, hh, half:ATT_TILE, :], kd), -jnp.inf)
        return s_top, s_bot

    def finish(accs, rows=slice(0, ATT_TILE)):
        lane = lax.broadcasted_iota(jnp.int32, accs[0].shape, 1)
        for pp in range(ATT_HEADS // 2):
            acc_e, acc_o = accs[2 * pp], accs[2 * pp + 1]
            o_e = acc_e * (1.0 / acc_e[:, HEAD_DIM:HEAD_DIM + 1])
            o_o = acc_o * (1.0 / acc_o[:, 0:1])
            o = jnp.where(lane < HEAD_DIM, o_e, o_o)
            cols = slice(pp * LANES, (pp + 1) * LANES)
            o_ref[rows, cols] = (o * gz_ref[rows, cols].astype(F32)).astype(BF16)

    @pl.when(flag_ref[0] == 1)
    def _fixed_stabilizer():
        def add_blocks(j, n_blocks, is_first):
            def block_scores(hh, jj):
                off = pl.multiple_of((j + jj) * ATT_TILE, ATT_TILE)
                return _dot_nt(q_ref[0, hh], k_ref[hh, pl.ds(off, ATT_TILE), :])

            ahead = n_blocks == 1
            s_next = block_scores(0, 0) if ahead else None
            for hh in heads:
                pv = None
                for jj in range(n_blocks):
                    off = pl.multiple_of((j + jj) * ATT_TILE, ATT_TILE)
                    s = s_next if ahead else block_scores(hh, jj)
                    if ahead and hh + 1 < ATT_HEADS:
                        s_next = block_scores(hh + 1, 0)
                    d = _dot(jnp.exp2(s).astype(BF16), v_ref[hh, pl.ds(off, ATT_TILE), :])
                    pv = d if pv is None else pv + d
                if is_first is True:
                    acc_ref[hh] = pv
                else:
                    acc_ref[hh] = pv + jnp.where(is_first, 0.0, acc_ref[hh])

        @pl.when(i == 0)
        def _():
            def zero(hh, carry):
                acc_ref[hh] = jnp.zeros((ATT_TILE, LANES), F32)
                return carry
            lax.fori_loop(0, ATT_HEADS, zero, 0)

        first = first_ref[(b * pl.num_programs(1) + group) * pl.num_programs(2) + i]
        n_live = i - first
        odd = n_live & 1

        @pl.when(odd == 1)
        def _():
            add_blocks(first, 1, True)

        def pair(p, carry):
            add_blocks(first + odd + 2 * p, 2, (odd == 0) & (p == 0))
            return carry

        lax.fori_loop(0, n_live >> 1, pair, 0)

        def past(hh, rows):
            return jnp.where(n_live > 0, acc_ref[hh, rows], 0.0)

        top_rows, bot_rows = slice(0, half), slice(half, ATT_TILE)
        units = ([(hh, bot_rows, ATT_TILE, mask_bot) for hh in heads]
                 + [(hh, top_rows, half, mask_top) for hh in heads])

        def scores(hh, rows, n_keys, mask):
            s = _dot_nt(q_ref[0, hh, rows, :], k_ref[hh, pl.ds(start, n_keys), :])
            return jnp.where(mask, s, -jnp.inf)

        accs, s_next = [], scores(*units[0])
        for n, (hh, rows, n_keys, _) in enumerate(units):
            s = s_next
            if n + 1 < len(units):
                s_next = scores(*units[n + 1])
            accs.append(past(hh, rows) + _dot(jnp.exp2(s).astype(BF16),
                                              v_ref[hh, pl.ds(start, n_keys), :]))
        finish(accs[:ATT_HEADS], bot_rows)
        finish(accs[ATT_HEADS:], top_rows)

    @pl.when(flag_ref[0] != 1)
    def _online():
        acc_ref[...] = jnp.zeros_like(acc_ref)
        m_ref[...] = jnp.full_like(m_ref, -jnp.inf)

        def update(s, m_old, acc_old, v):
            m_new = jnp.maximum(m_old, jnp.max(s, axis=-1, keepdims=True))
            pv = _dot(jnp.exp2(s - m_new).astype(BF16), v)
            return m_new, jnp.exp2(m_old - m_new) * acc_old + pv

        def body(j, carry):
            off = pl.multiple_of(j * ATT_TILE, ATT_TILE)

            def head(hh, c):
                s = _dot_nt(q_ref[0, hh], k_ref[hh, pl.ds(off, ATT_TILE), :])
                m_ref[hh], acc_ref[hh] = update(s, m_ref[hh], acc_ref[hh],
                                                v_ref[hh, pl.ds(off, ATT_TILE), :])
                return c

            return lax.fori_loop(0, ATT_HEADS, head, carry)

        lax.fori_loop(0, i, body, 0)

        def diag_head(hh, c):
            vd = v_ref[hh, pl.ds(start, ATT_TILE), :]
            s_top, s_bot = diag_scores(hh)
            _, top = update(s_top, m_ref[hh, 0:half], acc_ref[hh, 0:half], vd[0:half])
            _, bot = update(s_bot, m_ref[hh, half:ATT_TILE], acc_ref[hh, half:ATT_TILE], vd)
            acc_ref[hh, 0:half] = top
            acc_ref[hh, half:ATT_TILE] = bot
            return c

        lax.fori_loop(0, ATT_HEADS, diag_head, 0)
        finish([acc_ref[hh] for hh in heads])


def _out_kernel(y_ref, gate_ref, x_ref, w_ref, o_ref):
    wf_ref, wm_ref = w_ref.at[0:WIDTH], w_ref.at[WIDTH:2 * WIDTH]
    wo_ref = w_ref.at[2 * WIDTH:2 * WIDTH + D_MODEL]
    slabs = [slice(c, c + MXU_COLS) for c in range(0, D_MODEL, MXU_COLS)]
    yf, ym = y_ref[:, 0:WIDTH], y_ref[:, WIDTH:2 * WIDTH]
    merged = []
    for cols in slabs:
        pf = _dot(yf, wf_ref[:, cols])
        pm = _dot(ym, wm_ref[:, cols])
        ga = gate_ref[:, cols].astype(F32)
        gb = gate_ref[:, D_MODEL + cols.start:D_MODEL + cols.stop].astype(F32)
        merged.append((ga * pf + gb * pm).astype(BF16))
    merged = jnp.concatenate(merged, axis=1)
    for cols in slabs:
        o_ref[:, cols] = x_ref[:, cols] + _dot(merged, wo_ref[:, cols])


def _first_live_blocks(cend):
    sub = ATT_TILE // TILE
    c_k = cend[:, sub - 1::sub]
    c_q = jnp.concatenate([cend[:, :1], cend[:, sub - 1:-1:sub]], axis=1)
    n = c_k.shape[1]
    past = jnp.arange(n)[None, :] < jnp.arange(n)[:, None]
    dead = past[None, :, :, None] & (c_q[:, :, None, :] - c_k[:, None, :, :] < EXP2_ZERO_BELOW)
    return jnp.min(jnp.sum(dead.astype(jnp.int32), axis=2), axis=-1)


def _const_spec(shape):
    return pl.BlockSpec(shape, lambda *_: (0,) * len(shape))


def _layer(x2, batch, seq, norm_g, w_in, b_f, b_gate, fox_q_g, fox_k_g, moba_q_g, moba_k_g,
           w_fox, w_moba, w_out):
    rows = batch * seq
    tiles_per_seq = seq // TILE
    n_tiles = rows // TILE
    scale = HEAD_DIM ** -0.5

    w = w_in.astype(BF16)
    w_fl = jnp.pad(w_in[:, W_COLS:], ((0, 0), (0, LANES - N_HEADS))).astype(BF16)

    gains = jnp.stack([fox_q_g, fox_k_g, moba_q_g, moba_k_g]).astype(F32)
    gain_max = jnp.max(jnp.abs(gains), axis=1)
    b_fox = 8.0 * LOG2E * BOUND_MARGIN * gain_max[0] * gain_max[1]
    b_moba = 8.0 * LOG2E * BOUND_MARGIN * gain_max[2] * gain_max[3]
    fixed_ok = jnp.maximum(b_fox, b_moba) <= FIXED_STABILIZER_MAX_BOUND

    def stabilizer(b):
        u = -(b * (1.0 + 2.0 ** -7)).astype(BF16).astype(F32)
        return jnp.where(fixed_ok, u, 0.0)

    big = jnp.exp2(jnp.ceil(jnp.log2(2.1 * b_moba + 160.0)))
    flag = fixed_ok.astype(jnp.int32).reshape(1)

    def row(*pieces):
        r = jnp.concatenate([jnp.ravel(p).astype(F32) for p in pieces])
        return jnp.pad(r, (0, 2 * D_MODEL - r.shape[0]))

    def lanes(v):
        return jnp.full((LANES,), v, F32)

    q_scale = scale * LOG2E
    params = jnp.stack([
        row(norm_g),
        row(b_gate),
        row(jnp.tile(gains * jnp.array([q_scale, 1.0, q_scale, 1.0], F32)[:, None], (1, N_HEADS))),
        row(b_f, jnp.zeros(LANES - N_HEADS), lanes(big), lanes(stabilizer(b_fox)),
            lanes(stabilizer(b_moba))),
    ] + [jnp.zeros(2 * D_MODEL, F32)] * (SUBLANES - 4))

    sel_fq, sel_fk, sel_mq, tri = _routing_constants()
    rc, rsp, rsm = _rope_tables(seq)

    rope_spec = pl.BlockSpec((TILE, LANES), lambda g: (g % tiles_per_seq, 0))
    head_spec = pl.BlockSpec((1, N_AUG_HEADS, TILE, LANES),
                             lambda g: (g // tiles_per_seq, 0, g % tiles_per_seq, 0))
    qkv_shape = jax.ShapeDtypeStruct((batch, N_AUG_HEADS, seq, LANES), BF16)

    q_all, k_all, v_all, gz, gates, cend = pl.pallas_call(
        functools.partial(_proj_kernel, tiles_per_seq=tiles_per_seq),
        grid=(n_tiles,),
        in_specs=[
            _const_spec((TILE, D_MODEL)),
            pl.BlockSpec((TILE, D_MODEL), lambda g: (jnp.minimum(g + 1, n_tiles - 1), 0)),
            _const_spec(w_in.shape), _const_spec((D_MODEL, LANES)),
            _const_spec((SUBLANES, 2 * D_MODEL)),
            rope_spec, rope_spec, rope_spec,
            _const_spec((TILE, TILE)),
            _const_spec((LANES, WIDTH)), _const_spec((LANES, WIDTH)), _const_spec((LANES, WIDTH)),
        ],
        out_specs=[
            head_spec, head_spec, head_spec,
            pl.BlockSpec((TILE, 2 * WIDTH), lambda g: (g, 0)),
            pl.BlockSpec((TILE, 2 * D_MODEL), lambda g: (g, 0)),
            pl.BlockSpec((1, SUBLANES, LANES), lambda g: (g, 0, 0)),
        ],
        out_shape=[
            qkv_shape, qkv_shape, qkv_shape,
            jax.ShapeDtypeStruct((rows, 2 * WIDTH), BF16),
            jax.ShapeDtypeStruct((rows, 2 * D_MODEL), BF16),
            jax.ShapeDtypeStruct((n_tiles, SUBLANES, LANES), F32),
        ],
        scratch_shapes=[pltpu.VMEM((1, LANES), F32),
                        pltpu.VMEM((LANES, WIDTH), F32),
                        pltpu.VMEM((2, TILE, D_MODEL), BF16)],
        compiler_params=pltpu.CompilerParams(
            dimension_semantics=("arbitrary",), vmem_limit_bytes=VMEM_LIMIT),
        name="proj_epilogue",
    )(x2, x2, w, w_fl, params,
      jnp.asarray(rc), jnp.asarray(rsp), jnp.asarray(rsm),
      jnp.asarray(tri, BF16), jnp.asarray(sel_fq, BF16), jnp.asarray(sel_fk, BF16),
      jnp.asarray(sel_mq, BF16))

    n_groups = N_AUG_HEADS // ATT_HEADS
    group_cols = ATT_HEADS // 2 * LANES
    att_tiles = seq // ATT_TILE
    cend = cend[:, 0, :N_HEADS].reshape(batch, tiles_per_seq, N_HEADS) * LOG2E
    first = jnp.stack([_first_live_blocks(cend), jnp.zeros((batch, att_tiles), jnp.int32)], axis=1)
    kv_spec = pl.BlockSpec(memory_space=pl.ANY)
    kv_slots = pltpu.VMEM((2, ATT_HEADS, seq, LANES), BF16)
    row_spec = pl.BlockSpec((ATT_TILE, group_cols), lambda b, g, i, *_: (b * att_tiles + i, g))
    y = pl.pallas_call(
        functools.partial(_attn_kernel, att_tiles=att_tiles),
        grid_spec=pltpu.PrefetchScalarGridSpec(
            num_scalar_prefetch=2,
            grid=(batch, n_groups, att_tiles),
            in_specs=[
                pl.BlockSpec((1, ATT_HEADS, ATT_TILE, LANES), lambda b, g, i, *_: (b, g, i, 0)),
                kv_spec, kv_spec, row_spec,
            ],
            out_specs=row_spec,
            scratch_shapes=[pltpu.VMEM((ATT_HEADS, ATT_TILE, LANES), F32),
                            pltpu.VMEM((ATT_HEADS, ATT_TILE, 1), F32),
                            kv_slots, kv_slots,
                            pltpu.SemaphoreType.DMA((2, att_tiles, 2))],
        ),
        out_shape=jax.ShapeDtypeStruct((rows, 2 * WIDTH), BF16),
        compiler_params=pltpu.CompilerParams(
            dimension_semantics=("arbitrary", "arbitrary", "arbitrary"),
            vmem_limit_bytes=VMEM_LIMIT),
        name="flash_attn",
    )(flag, first.reshape(-1), q_all, k_all, v_all, gz)

    out = pl.pallas_call(
        _out_kernel,
        grid=(rows // OUT_TILE,),
        in_specs=[
            pl.BlockSpec((OUT_TILE, 2 * WIDTH), lambda g: (g, 0)),
            pl.BlockSpec((OUT_TILE, 2 * D_MODEL), lambda g: (g, 0)),
            pl.BlockSpec((OUT_TILE, D_MODEL), lambda g: (g, 0)),
            _const_spec((2 * WIDTH + D_MODEL, D_MODEL)),
        ],
        out_specs=pl.BlockSpec((OUT_TILE, D_MODEL), lambda g: (g, 0)),
        out_shape=jax.ShapeDtypeStruct((rows, D_MODEL), F32),
        compiler_params=pltpu.CompilerParams(
            dimension_semantics=("arbitrary",), vmem_limit_bytes=VMEM_LIMIT),
        name="merge_out",
    )(y, gates, x2, jnp.concatenate([w_fox, w_moba, w_out], axis=0).astype(BF16))
    return out


def kernel(x, norm_g, w_in, b_f, b_gate, fox_q_g, fox_k_g, moba_q_g, moba_k_g, w_fox, w_moba, w_out):
    batch, seq, d_model = x.shape
    assert d_model == D_MODEL and seq % ATT_TILE == 0 and seq // MOBA_BLOCK <= GROUP
    assert (batch * seq) % OUT_TILE == 0
    x2 = x.reshape(batch * seq, D_MODEL)
    for layer in range(norm_g.shape[0]):
        x2 = _layer(x2, batch, seq, norm_g[layer], w_in[layer], b_f[layer], b_gate[layer],
                    fox_q_g[layer], fox_k_g[layer], moba_q_g[layer], moba_k_g[layer],
                    w_fox[layer], w_moba[layer], w_out[layer])
    return x2.reshape(batch, seq, D_MODEL)
```

```python
import functools

import numpy as np
import jax
import jax.numpy as jnp
from jax import lax
from jax.experimental import pallas as pl
from jax.experimental.pallas import tpu as pltpu

D_MODEL = 1024
HEAD_DIM = 64
N_HEADS = 8
WIDTH = N_HEADS * HEAD_DIM
ROPE_DIM = HEAD_DIM // 4
ROPE_HALF = ROPE_DIM // 2
ROPE_THETA = 500000.0
MOBA_BLOCK = 256
MOBA_TOPK = 3
RMS_EPS = 1e-6

LANES = 128
SUBLANES = 8
MXU_COLS = 256
TILE = MOBA_BLOCK
OUT_TILE = 1024
ATT_TILE = 512
ATT_HEADS = N_HEADS
FIXED_STABILIZER_MAX_BOUND = 55.0
EXP2_ZERO_BELOW = -152.0
BOUND_MARGIN = 1.02
GROUP = 16
N_AUG_HEADS = 2 * N_HEADS
U_COL_FOX = 6
U_COL_MOBA = GROUP
LOG2E = 1.4426950408889634

C_FQ, C_FK, C_FV, C_FZ = 0, 512, 1024, 1536
C_MQ, C_MK, C_MV, C_MZ = 2048, 2560, 3072, 3584
C_GA, C_GB = 4096, 5120
W_COLS = 6144

VMEM_LIMIT = 52 * 1024 * 1024

F32 = jnp.float32
BF16 = jnp.bfloat16


def _dot(a, b):
    return jnp.dot(a, b, preferred_element_type=F32)


def _dot_nt(a, b):
    return lax.dot_general(a, b, (((1,), (1,)), ((), ())), preferred_element_type=F32)


def _split3(v):
    hi = v.astype(BF16).astype(F32)
    r = v - hi
    mid = r.astype(BF16).astype(F32)
    lo = r - mid
    return hi, mid, lo


def _extra_base(h):
    return (h // 2) * LANES + (HEAD_DIM if h % 2 == 0 else 0)


def _routing_constants():
    sel_fq = np.zeros((LANES, WIDTH), np.float32)
    sel_fk = np.zeros((LANES, WIDTH), np.float32)
    sel_mq = np.zeros((LANES, WIDTH), np.float32)
    for h in range(N_HEADS):
        base = _extra_base(h)
        for part in range(3):
            sel_fq[part * 8 + h, base + part] = 1.0
            sel_fq[24, base + 3 + part] = 1.0
            sel_fk[24, base + part] = 1.0
            sel_fk[part * 8 + h, base + 3 + part] = -1.0
        sel_fq[25, base + U_COL_FOX] = 1.0
        sel_fk[24, base + U_COL_FOX] = 1.0
        for n in range(GROUP):
            sel_mq[h * GROUP + n, base + n] = 1.0
    tri = np.tril(np.ones((TILE, TILE), np.float32))
    return sel_fq, sel_fk, sel_mq, tri


def _rope_tables(seq):
    inv_freq = ROPE_THETA ** (-np.arange(0, ROPE_HALF, dtype=np.float32) * 2.0 / ROPE_DIM)
    ang = np.arange(seq, dtype=np.float32)[:, None] * inv_freq[None, :].astype(np.float32)
    cos, sin = np.cos(ang).astype(np.float32), np.sin(ang).astype(np.float32)
    rc = np.ones((seq, LANES), np.float32)
    rsp = np.zeros((seq, LANES), np.float32)
    rsm = np.zeros((seq, LANES), np.float32)
    for off in (0, HEAD_DIM):
        rc[:, off:off + ROPE_HALF] = cos
        rc[:, off + ROPE_HALF:off + ROPE_DIM] = cos
        rsm[:, off:off + ROPE_HALF] = -sin
        rsp[:, off + ROPE_HALF:off + ROPE_DIM] = sin
    return rc, rsp, rsm


def _proj_kernel(x0_ref, xn_ref, w_ref, wfl_ref, prm_ref,
                 rc_ref, rsp_ref, rsm_ref, tri_ref, selfq_ref, selfk_ref, selmq_ref,
                 q_out, k_out, v_out, gz_out, gate_out, cend_out,
                 carry_ref, kmt_ref, h_ref, *, tiles_per_seq):
    step = pl.program_id(0)
    t = step % tiles_per_seq
    slot = step % 2

    ng_ref = prm_ref.at[0:1, 0:D_MODEL]
    bg_ref = prm_ref.at[1:2, :]
    gfq_ref, gfk_ref, gmq_ref, gmk_ref = (
        prm_ref.at[2:3, n * WIDTH:(n + 1) * WIDTH] for n in range(4))
    bf_ref, big_ref, ufox_ref, umoba_ref = (
        prm_ref.at[3:4, n * LANES:(n + 1) * LANES] for n in range(4))

    @pl.when(t == 0)
    def _():
        carry_ref[...] = jnp.zeros_like(carry_ref)
        kmt_ref[...] = jnp.zeros_like(kmt_ref)

    lane = lax.broadcasted_iota(jnp.int32, (TILE, LANES), 1)
    low_half = lane < HEAD_DIM

    def normalized(x):
        ms = jnp.mean(x * x, axis=-1, keepdims=True)
        return (x * lax.rsqrt(ms + RMS_EPS) * ng_ref[...]).astype(BF16)

    @pl.when(step == 0)
    def _():
        h_ref[0] = normalized(x0_ref[...])

    def proj(c0, width):
        slabs = [_dot(h_ref[slot], w_ref[:, c:c + min(MXU_COLS, c0 + width - c)])
                 for c in range(c0, c0 + width, MXU_COLS)]
        return slabs[0] if len(slabs) == 1 else jnp.concatenate(slabs, axis=1)

    def head_norm(a, g_ref):
        tiles = []
        for p in range(WIDTH // LANES):
            ap = a[:, p * LANES:(p + 1) * LANES]
            sq = ap * ap
            s_lo = jnp.sum(jnp.where(low_half, sq, 0.0), axis=-1, keepdims=True)
            s_hi = jnp.sum(jnp.where(low_half, 0.0, sq), axis=-1, keepdims=True)
            inv_lo = lax.rsqrt(s_lo * (1.0 / HEAD_DIM) + RMS_EPS)
            inv_hi = lax.rsqrt(s_hi * (1.0 / HEAD_DIM) + RMS_EPS)
            scale = jnp.where(low_half, inv_lo, inv_hi)
            tiles.append(ap * scale * g_ref[:, p * LANES:(p + 1) * LANES])
        return tiles

    def rope(y):
        return (y * rc_ref[...] + pltpu.roll(y, ROPE_HALF, 1) * rsp_ref[...]
                + pltpu.roll(y, LANES - ROPE_HALF, 1) * rsm_ref[...])

    def split_tiles(a):
        return [a[:, p * LANES:(p + 1) * LANES] for p in range(WIDTH // LANES)]

    def store_heads(out_ref, head0, tiles, extras):
        for p, y in enumerate(tiles):
            e = extras(p)
            out_ref[0, head0 + 2 * p] = jnp.where(low_half, y, e).astype(BF16)
            out_ref[0, head0 + 2 * p + 1] = jnp.where(low_half, e, y).astype(BF16)

    def silu(z):
        hz = 0.5 * z
        return hz + hz * jnp.tanh(hz)

    def sigmoid(z):
        return 0.5 * jnp.tanh(0.5 * z) + 0.5

    fl = _dot(h_ref[slot], wfl_ref[...]) + bf_ref[...]
    mk_raw = proj(C_MK, WIDTH)
    mq_raw = proj(C_MQ, WIDTH)
    gate_chunks = [(C_GA + half * WIDTH, half * WIDTH) for half in range(2)]
    gate_chunks += [(C_GB + half * WIDTH, D_MODEL + half * WIDTH) for half in range(2)]
    gate_raw = [proj(c0, WIDTH) for c0, _ in gate_chunks]

    logf = jnp.minimum(fl, 0.0) - jnp.log(1.0 + jnp.exp(-jnp.abs(fl)))
    logf = jnp.where(lane < N_HEADS, logf, 0.0)
    l_hi, l_mid, l_lo = _split3(logf)
    packed = (l_hi + pltpu.roll(l_mid, 8, 1) + pltpu.roll(l_lo, 16, 1)).astype(BF16)
    cum = _dot(tri_ref[...], packed)
    fq = proj(C_FQ, WIDTH)
    fk = proj(C_FK, WIDTH)

    mk_tiles = [rope(y) for y in head_norm(mk_raw, gmk_ref)]
    row_id = lax.broadcasted_iota(jnp.int32, (LANES, LANES), 0)
    lane_sq = lax.broadcasted_iota(jnp.int32, (LANES, LANES), 1)
    for p, kr in enumerate(mk_tiles):
        km = jnp.mean(kr, axis=0, keepdims=True)
        hit = (((row_id == (2 * p) * GROUP + t) & (lane_sq < HEAD_DIM))
               | ((row_id == (2 * p + 1) * GROUP + t) & (lane_sq >= HEAD_DIM)))
        blk = kmt_ref[:, p * LANES:(p + 1) * LANES]
        kmt_ref[:, p * LANES:(p + 1) * LANES] = jnp.where(hit, km, blk)
    mk_ones = jnp.where((lane % HEAD_DIM == t) | (lane % HEAD_DIM == U_COL_MOBA), 1.0, 0.0)
    store_heads(k_out, N_HEADS, mk_tiles, lambda p: mk_ones)

    mq_tiles = [rope(y) for y in head_norm(mq_raw, gmq_ref)]
    q_full = jnp.concatenate(mq_tiles, axis=1)
    q_hi = q_full.astype(BF16)
    q_lo = (q_full - q_hi.astype(F32)).astype(BF16)
    kmt = kmt_ref[...]
    k_hi = kmt.astype(BF16)
    k_lo = (kmt - k_hi.astype(F32)).astype(BF16)
    gate = _dot_nt(q_hi, k_hi) + _dot_nt(q_hi, k_lo) + _dot_nt(q_lo, k_hi)

    c = cum + pltpu.roll(cum, LANES - 8, 1) + pltpu.roll(cum, LANES - 16, 1)
    c = jnp.where(lane < N_HEADS, c, 0.0) + carry_ref[...]
    carry_ref[...] = c[TILE - 1:TILE, :]
    cend_out[0] = jnp.broadcast_to(c[TILE - 1:TILE, :], (SUBLANES, LANES))
    c_hi, c_mid, c_lo = _split3(c * LOG2E)
    cparts = (c_hi + pltpu.roll(c_mid, 8, 1) + pltpu.roll(c_lo, 16, 1)
              + jnp.where(lane == 24, 1.0, 0.0)
              + jnp.where(lane == 25, ufox_ref[...], 0.0)).astype(BF16)
    ex_fq = _dot(cparts, selfq_ref[...])
    ex_fk = _dot(cparts, selfk_ref[...])
    fz = proj(C_FZ, WIDTH)
    mz = proj(C_MZ, WIDTH)
    fv = proj(C_FV, WIDTH)
    mv = proj(C_MV, WIDTH)

    v_ones = jnp.where(lane % HEAD_DIM == 0, 1.0, 0.0)
    store_heads(v_out, 0, split_tiles(fv), lambda p: v_ones)
    gz_out[:, 0:WIDTH] = silu(fz).astype(BF16)

    store_heads(q_out, 0, head_norm(fq, gfq_ref), lambda p: ex_fq[:, p * LANES:(p + 1) * LANES])
    store_heads(k_out, 0, head_norm(fk, gfk_ref), lambda p: ex_fk[:, p * LANES:(p + 1) * LANES])

    blk_id = lane % GROUP
    past = blk_id < t
    g = jnp.where(past, gate, -jnp.inf)
    beaten = jnp.zeros((TILE, LANES), jnp.int32)
    for d in range(1, GROUP):
        lower = jnp.where(blk_id >= d, pltpu.roll(g, d, 1), -jnp.inf)
        upper = jnp.where(blk_id < GROUP - d, pltpu.roll(g, LANES - d, 1), -jnp.inf)
        beaten = beaten + jnp.where(lower >= g, 1, 0) + jnp.where(upper > g, 1, 0)
    keep = (past & (beaten < MOBA_TOPK)) | (blk_id == t)
    maskvals = jnp.where(keep, 0.0, -big_ref[...]).astype(BF16)
    ex_mq = _dot(maskvals, selmq_ref[...])

    store_heads(v_out, N_HEADS, split_tiles(mv), lambda p: v_ones)
    gz_out[:, WIDTH:2 * WIDTH] = silu(mz).astype(BF16)

    is_u_lane = lane % HEAD_DIM == U_COL_MOBA
    store_heads(q_out, N_HEADS, mq_tiles,
                lambda p: jnp.where(is_u_lane, umoba_ref[...], ex_mq[:, p * LANES:(p + 1) * LANES]))
    for raw, (_, o0) in zip(gate_raw, gate_chunks):
        gate_out[:, o0:o0 + WIDTH] = sigmoid(raw + bg_ref[:, o0:o0 + WIDTH]).astype(BF16)

    h_ref[1 - slot] = normalized(xn_ref[...])


def _attn_kernel(flag_ref, first_ref, q_ref, k_ref, v_ref, gz_ref, o_ref, acc_ref, m_ref):
    b, group, i = pl.program_id(0), pl.program_id(1), pl.program_id(2)
    half = ATT_TILE // 2
    start = pl.multiple_of(i * ATT_TILE, ATT_TILE)
    heads = range(ATT_HEADS)

    def mask_top():
        return (lax.broadcasted_iota(jnp.int32, (half, half), 1)
                <= lax.broadcasted_iota(jnp.int32, (half, half), 0))

    def mask_bot():
        return (lax.broadcasted_iota(jnp.int32, (half, ATT_TILE), 1)
                <= lax.broadcasted_iota(jnp.int32, (half, ATT_TILE), 0) + half)

    def diag_scores(hh):
        kd = k_ref[0, hh, pl.ds(start, ATT_TILE), :]
        s_top = jnp.where(mask_top(), _dot_nt(q_ref[0, hh, 0:half, :], kd[0:half]), -jnp.inf)
        s_bot = jnp.where(mask_bot(), _dot_nt(q_ref[0, hh, half:ATT_TILE, :], kd), -jnp.inf)
        return s_top, s_bot

    def finish(accs, rows=slice(0, ATT_TILE)):
        lane = lax.broadcasted_iota(jnp.int32, accs[0].shape, 1)
        for pp in range(ATT_HEADS // 2):
            acc_e, acc_o = accs[2 * pp], accs[2 * pp + 1]
            o_e = acc_e * (1.0 / acc_e[:, HEAD_DIM:HEAD_DIM + 1])
            o_o = acc_o * (1.0 / acc_o[:, 0:1])
            o = jnp.where(lane < HEAD_DIM, o_e, o_o)
            cols = slice(pp * LANES, (pp + 1) * LANES)
            o_ref[rows, cols] = (o * gz_ref[rows, cols].astype(F32)).astype(BF16)

    @pl.when(flag_ref[0] == 1)
    def _fixed_stabilizer():
        def add_blocks(j, n_blocks, is_first):
            def block_scores(hh, jj):
                off = pl.multiple_of((j + jj) * ATT_TILE, ATT_TILE)
                return _dot_nt(q_ref[0, hh], k_ref[0, hh, pl.ds(off, ATT_TILE), :])

            ahead = n_blocks == 1
            s_next = block_scores(0, 0) if ahead else None
            for hh in heads:
                pv = None
                for jj in range(n_blocks):
                    off = pl.multiple_of((j + jj) * ATT_TILE, ATT_TILE)
                    s = s_next if ahead else block_scores(hh, jj)
                    if ahead and hh + 1 < ATT_HEADS:
                        s_next = block_scores(hh + 1, 0)
                    d = _dot(jnp.exp2(s).astype(BF16), v_ref[0, hh, pl.ds(off, ATT_TILE), :])
                    pv = d if pv is None else pv + d
                if is_first is True:
                    acc_ref[hh] = pv
                else:
                    acc_ref[hh] = pv + jnp.where(is_first, 0.0, acc_ref[hh])

        @pl.when(i == 0)
        def _():
            def zero(hh, carry):
                acc_ref[hh] = jnp.zeros((ATT_TILE, LANES), F32)
                return carry
            lax.fori_loop(0, ATT_HEADS, zero, 0)

        first = first_ref[(b * pl.num_programs(1) + group) * pl.num_programs(2) + i]
        n_live = i - first
        odd = n_live & 1

        @pl.when(odd == 1)
        def _():
            add_blocks(first, 1, True)

        def pair(p, carry):
            add_blocks(first + odd + 2 * p, 2, (odd == 0) & (p == 0))
            return carry

        lax.fori_loop(0, n_live >> 1, pair, 0)

        def past(hh, rows):
            return jnp.where(n_live > 0, acc_ref[hh, rows], 0.0)

        top_rows, bot_rows = slice(0, half), slice(half, ATT_TILE)
        units = ([(hh, bot_rows, ATT_TILE, mask_bot) for hh in heads]
                 + [(hh, top_rows, half, mask_top) for hh in heads])

        def scores(hh, rows, n_keys, mask):
            s = _dot_nt(q_ref[0, hh, rows, :], k_ref[0, hh, pl.ds(start, n_keys), :])
            return jnp.where(mask(), s, -jnp.inf)

        accs, s_next = [], scores(*units[0])
        for n, (hh, rows, n_keys, _) in enumerate(units):
            s = s_next
            if n + 1 < len(units):
                s_next = scores(*units[n + 1])
            accs.append(past(hh, rows) + _dot(jnp.exp2(s).astype(BF16),
                                              v_ref[0, hh, pl.ds(start, n_keys), :]))
        finish(accs[:ATT_HEADS], bot_rows)
        finish(accs[ATT_HEADS:], top_rows)

    @pl.when(flag_ref[0] != 1)
    def _online():
        acc_ref[...] = jnp.zeros_like(acc_ref)
        m_ref[...] = jnp.full_like(m_ref, -jnp.inf)

        def update(s, m_old, acc_old, v):
            m_new = jnp.maximum(m_old, jnp.max(s, axis=-1, keepdims=True))
            pv = _dot(jnp.exp2(s - m_new).astype(BF16), v)
            return m_new, jnp.exp2(m_old - m_new) * acc_old + pv

        def body(j, carry):
            off = pl.multiple_of(j * ATT_TILE, ATT_TILE)

            def head(hh, c):
                s = _dot_nt(q_ref[0, hh], k_ref[0, hh, pl.ds(off, ATT_TILE), :])
                m_ref[hh], acc_ref[hh] = update(s, m_ref[hh], acc_ref[hh],
                                                v_ref[0, hh, pl.ds(off, ATT_TILE), :])
                return c

            return lax.fori_loop(0, ATT_HEADS, head, carry)

        lax.fori_loop(0, i, body, 0)

        def diag_head(hh, c):
            vd = v_ref[0, hh, pl.ds(start, ATT_TILE), :]
            s_top, s_bot = diag_scores(hh)
            _, top = update(s_top, m_ref[hh, 0:half], acc_ref[hh, 0:half], vd[0:half])
            _, bot = update(s_bot, m_ref[hh, half:ATT_TILE], acc_ref[hh, half:ATT_TILE], vd)
            acc_ref[hh, 0:half] = top
            acc_ref[hh, half:ATT_TILE] = bot
            return c

        lax.fori_loop(0, ATT_HEADS, diag_head, 0)
        finish([acc_ref[hh] for hh in heads])


def _out_kernel(y_ref, gate_ref, x_ref, w_ref, o_ref):
    wf_ref, wm_ref = w_ref.at[0:WIDTH], w_ref.at[WIDTH:2 * WIDTH]
    wo_ref = w_ref.at[2 * WIDTH:2 * WIDTH + D_MODEL]
    slabs = [slice(c, c + MXU_COLS) for c in range(0, D_MODEL, MXU_COLS)]
    yf, ym = y_ref[:, 0:WIDTH], y_ref[:, WIDTH:2 * WIDTH]
    merged = []
    for cols in slabs:
        pf = _dot(yf, wf_ref[:, cols])
        pm = _dot(ym, wm_ref[:, cols])
        ga = gate_ref[:, cols].astype(F32)
        gb = gate_ref[:, D_MODEL + cols.start:D_MODEL + cols.stop].astype(F32)
        merged.append((ga * pf + gb * pm).astype(BF16))
    merged = jnp.concatenate(merged, axis=1)
    for cols in slabs:
        o_ref[:, cols] = x_ref[:, cols] + _dot(merged, wo_ref[:, cols])


def _first_live_blocks(cend):
    sub = ATT_TILE // TILE
    c_k = cend[:, sub - 1::sub]
    c_q = jnp.concatenate([cend[:, :1], cend[:, sub - 1:-1:sub]], axis=1)
    n = c_k.shape[1]
    past = jnp.arange(n)[None, :] < jnp.arange(n)[:, None]
    dead = past[None, :, :, None] & (c_q[:, :, None, :] - c_k[:, None, :, :] < EXP2_ZERO_BELOW)
    return jnp.min(jnp.sum(dead.astype(jnp.int32), axis=2), axis=-1)


def _const_spec(shape):
    return pl.BlockSpec(shape, lambda *_: (0,) * len(shape))


def _layer(x2, batch, seq, norm_g, w_in, b_f, b_gate, fox_q_g, fox_k_g, moba_q_g, moba_k_g,
           w_fox, w_moba, w_out):
    rows = batch * seq
    tiles_per_seq = seq // TILE
    n_tiles = rows // TILE
    scale = HEAD_DIM ** -0.5

    w = w_in.astype(BF16)
    w_fl = jnp.pad(w_in[:, W_COLS:], ((0, 0), (0, LANES - N_HEADS))).astype(BF16)

    gains = jnp.stack([fox_q_g, fox_k_g, moba_q_g, moba_k_g]).astype(F32)
    gain_max = jnp.max(jnp.abs(gains), axis=1)
    b_fox = 8.0 * LOG2E * BOUND_MARGIN * gain_max[0] * gain_max[1]
    b_moba = 8.0 * LOG2E * BOUND_MARGIN * gain_max[2] * gain_max[3]
    fixed_ok = jnp.maximum(b_fox, b_moba) <= FIXED_STABILIZER_MAX_BOUND

    def stabilizer(b):
        u = -(b * (1.0 + 2.0 ** -7)).astype(BF16).astype(F32)
        return jnp.where(fixed_ok, u, 0.0)

    big = jnp.exp2(jnp.ceil(jnp.log2(2.1 * b_moba + 160.0)))
    flag = fixed_ok.astype(jnp.int32).reshape(1)

    def row(*pieces):
        r = jnp.concatenate([jnp.ravel(p).astype(F32) for p in pieces])
        return jnp.pad(r, (0, 2 * D_MODEL - r.shape[0]))

    def lanes(v):
        return jnp.full((LANES,), v, F32)

    q_scale = scale * LOG2E
    params = jnp.stack([
        row(norm_g),
        row(b_gate),
        row(jnp.tile(gains * jnp.array([q_scale, 1.0, q_scale, 1.0], F32)[:, None], (1, N_HEADS))),
        row(b_f, jnp.zeros(LANES - N_HEADS), lanes(big), lanes(stabilizer(b_fox)),
            lanes(stabilizer(b_moba))),
    ] + [jnp.zeros(2 * D_MODEL, F32)] * (SUBLANES - 4))

    sel_fq, sel_fk, sel_mq, tri = _routing_constants()
    rc, rsp, rsm = _rope_tables(seq)

    rope_spec = pl.BlockSpec((TILE, LANES), lambda g: (g % tiles_per_seq, 0))
    head_spec = pl.BlockSpec((1, N_AUG_HEADS, TILE, LANES),
                             lambda g: (g // tiles_per_seq, 0, g % tiles_per_seq, 0))
    qkv_shape = jax.ShapeDtypeStruct((batch, N_AUG_HEADS, seq, LANES), BF16)

    q_all, k_all, v_all, gz, gates, cend = pl.pallas_call(
        functools.partial(_proj_kernel, tiles_per_seq=tiles_per_seq),
        grid=(n_tiles,),
        in_specs=[
            _const_spec((TILE, D_MODEL)),
            pl.BlockSpec((TILE, D_MODEL), lambda g: (jnp.minimum(g + 1, n_tiles - 1), 0)),
            _const_spec(w_in.shape), _const_spec((D_MODEL, LANES)),
            _const_spec((SUBLANES, 2 * D_MODEL)),
            rope_spec, rope_spec, rope_spec,
            _const_spec((TILE, TILE)),
            _const_spec((LANES, WIDTH)), _const_spec((LANES, WIDTH)), _const_spec((LANES, WIDTH)),
        ],
        out_specs=[
            head_spec, head_spec, head_spec,
            pl.BlockSpec((TILE, 2 * WIDTH), lambda g: (g, 0)),
            pl.BlockSpec((TILE, 2 * D_MODEL), lambda g: (g, 0)),
            pl.BlockSpec((1, SUBLANES, LANES), lambda g: (g, 0, 0)),
        ],
        out_shape=[
            qkv_shape, qkv_shape, qkv_shape,
            jax.ShapeDtypeStruct((rows, 2 * WIDTH), BF16),
            jax.ShapeDtypeStruct((rows, 2 * D_MODEL), BF16),
            jax.ShapeDtypeStruct((n_tiles, SUBLANES, LANES), F32),
        ],
        scratch_shapes=[pltpu.VMEM((1, LANES), F32),
                        pltpu.VMEM((LANES, WIDTH), F32),
                        pltpu.VMEM((2, TILE, D_MODEL), BF16)],
        compiler_params=pltpu.CompilerParams(
            dimension_semantics=("arbitrary",), vmem_limit_bytes=VMEM_LIMIT),
        name="proj_epilogue",
    )(x2, x2, w, w_fl, params,
      jnp.asarray(rc), jnp.asarray(rsp), jnp.asarray(rsm),
      jnp.asarray(tri, BF16), jnp.asarray(sel_fq, BF16), jnp.asarray(sel_fk, BF16),
      jnp.asarray(sel_mq, BF16))

    n_groups = N_AUG_HEADS // ATT_HEADS
    group_cols = ATT_HEADS // 2 * LANES
    att_tiles = seq // ATT_TILE
    cend = cend[:, 0, :N_HEADS].reshape(batch, tiles_per_seq, N_HEADS) * LOG2E
    first = jnp.stack([_first_live_blocks(cend), jnp.zeros((batch, att_tiles), jnp.int32)], axis=1)
    kv_spec = pl.BlockSpec((1, ATT_HEADS, seq, LANES), lambda b, g, i, *_: (b, g, 0, 0))
    row_spec = pl.BlockSpec((ATT_TILE, group_cols), lambda b, g, i, *_: (b * att_tiles + i, g))
    y = pl.pallas_call(
        _attn_kernel,
        grid_spec=pltpu.PrefetchScalarGridSpec(
            num_scalar_prefetch=2,
            grid=(batch, n_groups, att_tiles),
            in_specs=[
                pl.BlockSpec((1, ATT_HEADS, ATT_TILE, LANES), lambda b, g, i, *_: (b, g, i, 0)),
                kv_spec, kv_spec, row_spec,
            ],
            out_specs=row_spec,
            scratch_shapes=[pltpu.VMEM((ATT_HEADS, ATT_TILE, LANES), F32),
                            pltpu.VMEM((ATT_HEADS, ATT_TILE, 1), F32)],
        ),
        out_shape=jax.ShapeDtypeStruct((rows, 2 * WIDTH), BF16),
        compiler_params=pltpu.CompilerParams(
            dimension_semantics=("arbitrary", "arbitrary", "arbitrary"),
            vmem_limit_bytes=VMEM_LIMIT),
        name="flash_attn",
    )(flag, first.reshape(-1), q_all, k_all, v_all, gz)

    out = pl.pallas_call(
        _out_kernel,
        grid=(rows // OUT_TILE,),
        in_specs=[
            pl.BlockSpec((OUT_TILE, 2 * WIDTH), lambda g: (g, 0)),
            pl.BlockSpec((OUT_TILE, 2 * D_MODEL), lambda g: (g, 0)),
            pl.BlockSpec((OUT_TILE, D_MODEL), lambda g: (g, 0)),
            _const_spec((2 * WIDTH + D_MODEL, D_MODEL)),
        ],
        out_specs=pl.BlockSpec((OUT_TILE, D_MODEL), lambda g: (g, 0)),
        out_shape=jax.ShapeDtypeStruct((rows, D_MODEL), F32),
        compiler_params=pltpu.CompilerParams(
            dimension_semantics=("arbitrary",), vmem_limit_bytes=VMEM_LIMIT),
        name="merge_out",
    )(y, gates, x2, jnp.concatenate([w_fox, w_moba, w_out], axis=0).astype(BF16))
    return out


def kernel(x, norm_g, w_in, b_f, b_gate, fox_q_g, fox_k_g, moba_q_g, moba_k_g, w_fox, w_moba, w_out):
    batch, seq, d_model = x.shape
    assert d_model == D_MODEL and seq % ATT_TILE == 0 and seq // MOBA_BLOCK <= GROUP
    assert (batch * seq) % OUT_TILE == 0
    x2 = x.reshape(batch * seq, D_MODEL)
    for layer in range(norm_g.shape[0]):
        x2 = _layer(x2, batch, seq, norm_g[layer], w_in[layer], b_f[layer], b_gate[layer],
                    fox_q_g[layer], fox_k_g[layer], moba_q_g[layer], moba_k_g[layer],
                    w_fox[layer], w_moba[layer], w_out[layer])
    return x2.reshape(batch, seq, D_MODEL)
```
